```python
import math
import jax
import jax.numpy as jnp
from jax import lax
import numpy as np

D_MODEL = 2048
BATCH = 2
SEQ = 4096
DEPTH = 1
DEC_BATCH = 128
DEC_SEQ = 4
PAST_LEN = 8192
PAGE_SIZE = 128

HEAD_DIM = 64
N_HEADS = D_MODEL // HEAD_DIM
N_KV_HEADS = 4
GQA_GROUP = N_HEADS // N_KV_HEADS
WINDOW = 128
ATTN_BLOCK = WINDOW
CONV_CHANNELS = D_MODEL // 2
CONV_WIDTH = 31
N_GROUPS = 4
EXPERTS_PER_GROUP = 8
N_EXPERTS = N_GROUPS * EXPERTS_PER_GROUP
TOP_K = 2
D_EXPERT = D_MODEL // 2
MOE_BLOCK = 128
EPS = 1e-6
NEG_INF = -1e30
Q_W = N_HEADS * HEAD_DIM
KV_W = N_KV_HEADS * HEAD_DIM
PROJ_COLS = Q_W + 2 * KV_W + 2 * CONV_CHANNELS + 2 * D_MODEL

kernel_name = 'hybrid_swa_conformer_hiermoe_step'


def rms_norm(x, w):
    xf = x.astype(jnp.float32)
    y = xf * lax.rsqrt(jnp.mean(xf * xf, axis=-1, keepdims=True) + EPS)
    return (y * w.astype(jnp.float32)).astype(x.dtype)


def layer_norm(x, w, b):
    xf = x.astype(jnp.float32)
    mu = jnp.mean(xf, axis=-1, keepdims=True)
    var = jnp.mean(jnp.square(xf - mu), axis=-1, keepdims=True)
    y = (xf - mu) * lax.rsqrt(var + EPS)
    return (y * w.astype(jnp.float32) + b.astype(jnp.float32)).astype(x.dtype)


def project(x, attn_norm_w, w_in, b_gate, q_norm_w, k_norm_w):
    bsz, slen, _ = x.shape
    z = rms_norm(x, attn_norm_w) @ w_in
    q, k, v, u, g = jnp.split(
        z, [Q_W, Q_W + KV_W, Q_W + 2 * KV_W, Q_W + 2 * KV_W + 2 * CONV_CHANNELS], axis=-1)
    q = rms_norm(q.reshape(bsz, slen, N_HEADS, HEAD_DIM), q_norm_w)
    k = rms_norm(k.reshape(bsz, slen, N_KV_HEADS, HEAD_DIM), k_norm_w)
    v = v.reshape(bsz, slen, N_KV_HEADS, HEAD_DIM)
    u_a, u_b = jnp.split(u, 2, axis=-1)
    u = u_a * jax.nn.sigmoid(u_b)
    g_attn, g_conv = jnp.split(jax.nn.sigmoid(g + b_gate), 2, axis=-1)
    return q, k, v, u, g_attn, g_conv


def sink_attention(q, k, v, mask, sinks):
    s = jnp.einsum('bnqkgd,bnjkd->bnkgqj', q, k).astype(jnp.float32) * (HEAD_DIM ** -0.5)
    s = jnp.where(mask[None, :, None, None], s, NEG_INF)
    sink = sinks.astype(jnp.float32).reshape(N_KV_HEADS, GQA_GROUP)[None, None, :, :, None, None]
    m = jnp.maximum(jnp.max(s, axis=-1, keepdims=True), sink)
    p = jnp.exp(s - m)
    p = p / (jnp.sum(p, axis=-1, keepdims=True) + jnp.exp(sink - m))
    return jnp.einsum('bnkgqj,bnjkd->bnqkgd', p.astype(v.dtype), v)


def prompt_attention(q, k, v, sinks):
    bsz, slen = q.shape[:2]
    nb = slen // ATTN_BLOCK
    qb = q.reshape(bsz, nb, ATTN_BLOCK, N_KV_HEADS, GQA_GROUP, HEAD_DIM)

    def banded(t):
        tb = t.reshape(bsz, nb, ATTN_BLOCK, N_KV_HEADS, HEAD_DIM)
        prev = jnp.pad(tb, ((0, 0), (1, 0), (0, 0), (0, 0), (0, 0)))[:, :-1]
        return jnp.concatenate([prev, tb], axis=2)

    q_pos = jnp.arange(ATTN_BLOCK)
    k_pos = jnp.arange(2 * ATTN_BLOCK) - ATTN_BLOCK
    rel = q_pos[:, None] - k_pos[None, :]
    band = (rel >= 0) & (rel <= WINDOW)
    first = jnp.arange(nb)[:, None, None] == 0
    mask = band[None] & ~(first & (k_pos < 0)[None, None, :])
    o = sink_attention(qb, banded(k), banded(v), mask, sinks)
    return o.reshape(bsz, slen, Q_W)


def sample_attention(q, k_all, v_all, sinks):
    bsz, qlen = q.shape[:2]
    qb = q.reshape(bsz, 1, qlen, N_KV_HEADS, GQA_GROUP, HEAD_DIM)
    q_pos = jnp.arange(qlen)
    k_pos = jnp.concatenate([jnp.arange(WINDOW) - WINDOW, jnp.arange(qlen)])
    rel = q_pos[:, None] - k_pos[None, :]
    mask = ((rel >= 0) & (rel <= WINDOW))[None]
    o = sink_attention(qb, k_all[:, None], v_all[:, None], mask, sinks)
    return o.reshape(bsz, qlen, Q_W)


def conv_module(u_ext, conv_dw_w, conv_dw_b, conv_ln_w, conv_ln_b, w_conv_out):
    y = lax.conv_general_dilated(
        u_ext, conv_dw_w[:, None, :], window_strides=(1,), padding='VALID',
        dimension_numbers=('NWC', 'WIO', 'NWC'), feature_group_count=CONV_CHANNELS)
    y = jax.nn.silu(layer_norm(y + conv_dw_b, conv_ln_w, conv_ln_b))
    return y @ w_conv_out


def hier_moe(xf, router_group, router_expert, w_gate, w_up, w_down):
    t = xf.shape[0]
    pg = jax.nn.softmax((xf @ router_group).astype(jnp.float32), axis=-1)
    g_star = jnp.argmax(pg, axis=-1)
    pg_star = jnp.take_along_axis(pg, g_star[:, None], axis=-1)
    fine = (xf @ router_expert).astype(jnp.float32).reshape(t, N_GROUPS, EXPERTS_PER_GROUP)
    fine_sel = jnp.take_along_axis(fine, g_star[:, None, None], axis=1)[:, 0]
    top_v, top_i = lax.top_k(fine_sel, TOP_K)
    w = pg_star * jax.nn.softmax(top_v, axis=-1)
    e_idx = g_star[:, None].astype(jnp.int32) * EXPERTS_PER_GROUP + top_i.astype(jnp.int32)

    a = t * TOP_K
    flat_e = e_idx.reshape(a)
    flat_tok = jnp.arange(a, dtype=jnp.int32) // TOP_K
    flat_w = w.reshape(a)
    order = jnp.argsort(flat_e, stable=True)
    se = flat_e[order]
    counts = jnp.bincount(flat_e, length=N_EXPERTS).astype(jnp.int32)
    padded = (counts + MOE_BLOCK - 1) // MOE_BLOCK * MOE_BLOCK
    pad_end = jnp.cumsum(padded)
    pad_start = pad_end - padded
    start = jnp.cumsum(counts) - counts
    dest = pad_start[se] + (jnp.arange(a, dtype=jnp.int32) - start[se])
    n_blocks = -(-a // MOE_BLOCK) + N_EXPERTS
    rows = n_blocks * MOE_BLOCK
    row_tok = jnp.zeros((rows,), jnp.int32).at[dest].set(flat_tok[order])
    row_w = jnp.zeros((rows,), xf.dtype).at[dest].set(flat_w[order].astype(xf.dtype))
    blk_exp = jnp.minimum(
        jnp.searchsorted(pad_end, jnp.arange(n_blocks, dtype=jnp.int32) * MOE_BLOCK, side='right'),
        N_EXPERTS - 1)
    xb = xf[row_tok].reshape(n_blocks, MOE_BLOCK, xf.shape[-1])

    def expert_block(args):
        xblk, e = args
        h = jax.nn.silu(xblk @ w_gate[e]) * (xblk @ w_up[e])
        return h @ w_down[e]

    yb = lax.map(expert_block, (xb, blk_exp)).reshape(rows, xf.shape[-1])
    return jax.ops.segment_sum(yb * row_w[:, None], row_tok, num_segments=t)


def merge_and_ffn(x, attn_o, conv_o, g_attn, g_conv, w_out, ffn_norm_w,
                  router_group, router_expert, w_gate, w_up, w_down):
    h = x + (g_attn * attn_o + g_conv * conv_o) @ w_out
    bsz, slen, d = h.shape
    f = hier_moe(rms_norm(h, ffn_norm_w).reshape(bsz * slen, d),
                 router_group, router_expert, w_gate, w_up, w_down)
    return h + f.reshape(bsz, slen, d)


def decoder_layer(x_p, x_s, ck, cv, sc, attn_norm_w, w_in, b_gate, q_norm_w, k_norm_w, sinks,
                  conv_dw_w, conv_dw_b, conv_ln_w, conv_ln_b, w_conv_out, w_out, ffn_norm_w,
                  router_group, router_expert, w_gate, w_up, w_down):
    mix = (attn_norm_w, w_in, b_gate, q_norm_w, k_norm_w)
    conv = (conv_dw_w, conv_dw_b, conv_ln_w, conv_ln_b, w_conv_out)
    ffn = (w_out, ffn_norm_w, router_group, router_expert, w_gate, w_up, w_down)
    q, k, v, u, ga, gc = project(x_p, *mix)
    a = prompt_attention(q, k, v, sinks)
    u_ext = jnp.pad(u, ((0, 0), (CONV_WIDTH - 1, 0), (0, 0)))
    c = conv_module(u_ext, *conv)
    y_p = merge_and_ffn(x_p, a, c, ga, gc, *ffn)
    new_p = (k[:, -WINDOW:], v[:, -WINDOW:], u[:, -(CONV_WIDTH - 1):])
    q, k, v, u, ga, gc = project(x_s, *mix)
    k_all = jnp.concatenate([ck, k], axis=1)
    v_all = jnp.concatenate([cv, v], axis=1)
    a = sample_attention(q, k_all, v_all, sinks)
    u_ext = jnp.concatenate([sc, u], axis=1)
    c = conv_module(u_ext, *conv)
    y_s = merge_and_ffn(x_s, a, c, ga, gc, *ffn)
    new_s = (k_all[:, -WINDOW:], v_all[:, -WINDOW:], u_ext[:, -(CONV_WIDTH - 1):])
    return y_p, y_s, new_p, new_s


def setup_inputs(seed: int = 0) -> dict:
    key = jax.random.key(seed)
    ks = jax.random.split(key, 24)
    f32 = jnp.float32

    def nrm(k, shape, scale):
        return jax.random.normal(k, shape, f32) * scale

    return {
        'x_prompt': nrm(ks[0], (BATCH, SEQ, D_MODEL), 1.0),
        'x_sample': nrm(ks[1], (DEC_BATCH, DEC_SEQ, D_MODEL), 1.0),
        'cache_k': nrm(ks[2], (DEPTH, DEC_BATCH, WINDOW, N_KV_HEADS, HEAD_DIM), 1.0),
        'cache_v': nrm(ks[3], (DEPTH, DEC_BATCH, WINDOW, N_KV_HEADS, HEAD_DIM), 1.0),
        'state_conv': nrm(ks[4], (DEPTH, DEC_BATCH, CONV_WIDTH - 1, CONV_CHANNELS), 0.5),
        'attn_norm_w': 1.0 + nrm(ks[5], (DEPTH, D_MODEL), 0.05),
        'w_in': nrm(ks[6], (DEPTH, D_MODEL, PROJ_COLS), D_MODEL ** -0.5),
        'b_gate': nrm(ks[7], (DEPTH, 2 * D_MODEL), 0.1),
        'q_norm_w': 1.0 + nrm(ks[8], (DEPTH, HEAD_DIM), 0.05),
        'k_norm_w': 1.0 + nrm(ks[9], (DEPTH, HEAD_DIM), 0.05),
        'sinks': nrm(ks[10], (DEPTH, N_HEADS), 0.5),
        'conv_dw_w': nrm(ks[11], (DEPTH, CONV_WIDTH, CONV_CHANNELS), CONV_WIDTH ** -0.5),
        'conv_dw_b': nrm(ks[12], (DEPTH, CONV_CHANNELS), 0.05),
        'conv_ln_w': 1.0 + nrm(ks[13], (DEPTH, CONV_CHANNELS), 0.05),
        'conv_ln_b': nrm(ks[14], (DEPTH, CONV_CHANNELS), 0.05),
        'w_conv_out': nrm(ks[15], (DEPTH, CONV_CHANNELS, D_MODEL), CONV_CHANNELS ** -0.5),
        'w_out': nrm(ks[16], (DEPTH, D_MODEL, D_MODEL), D_MODEL ** -0.5),
        'ffn_norm_w': 1.0 + nrm(ks[17], (DEPTH, D_MODEL), 0.05),
        'router_group': nrm(ks[18], (DEPTH, D_MODEL, N_GROUPS), D_MODEL ** -0.5),
        'router_expert': nrm(ks[19], (DEPTH, D_MODEL, N_EXPERTS), D_MODEL ** -0.5),
        'w_gate': nrm(ks[20], (DEPTH, N_EXPERTS, D_MODEL, D_EXPERT), D_MODEL ** -0.5),
        'w_up': nrm(ks[21], (DEPTH, N_EXPERTS, D_MODEL, D_EXPERT), D_MODEL ** -0.5),
        'w_down': nrm(ks[22], (DEPTH, N_EXPERTS, D_EXPERT, D_MODEL), D_EXPERT ** -0.5),
    }


def reference(x_prompt, x_sample, cache_k, cache_v, state_conv, attn_norm_w, w_in, b_gate,
              q_norm_w, k_norm_w, sinks, conv_dw_w, conv_dw_b, conv_ln_w, conv_ln_b,
              w_conv_out, w_out, ffn_norm_w, router_group, router_expert, w_gate, w_up, w_down):
    x_p, x_s = x_prompt, x_sample
    kp, vp, cp, ks_, vs_, cs_ = [], [], [], [], [], []
    for l in range(DEPTH):
        x_p, x_s, new_p, new_s = decoder_layer(
            x_p, x_s, cache_k[l], cache_v[l], state_conv[l], attn_norm_w[l], w_in[l], b_gate[l],
            q_norm_w[l], k_norm_w[l], sinks[l], conv_dw_w[l], conv_dw_b[l], conv_ln_w[l],
            conv_ln_b[l], w_conv_out[l], w_out[l], ffn_norm_w[l], router_group[l],
            router_expert[l], w_gate[l], w_up[l], w_down[l])
        kp.append(new_p[0]); vp.append(new_p[1]); cp.append(new_p[2])
        ks_.append(new_s[0]); vs_.append(new_s[1]); cs_.append(new_s[2])
    return (x_p, x_s, jnp.stack(kp), jnp.stack(vp), jnp.stack(cp),
            jnp.stack(ks_), jnp.stack(vs_), jnp.stack(cs_))
```

```python
import functools

import jax
import jax.numpy as jnp
from jax import lax
from jax.experimental import pallas as pl
from jax.experimental.pallas import tpu as pltpu

F32 = jnp.float32
BF16 = jnp.bfloat16
EPS = 1e-6
NEG_INF = -1e30

HEAD_DIM = 64
N_KV_HEADS = 4
WINDOW = 128
CONV_WIDTH = 31
N_GROUPS = 4
EXPERTS_PER_GROUP = 8
N_EXPERTS = N_GROUPS * EXPERTS_PER_GROUP
TOP_K = 2

LANES = 128
SUBLANES = 8
MXU_DIM = 256
VMEM_LIMIT = 56 * 1024 * 1024

TOKEN_BLOCK = 512
CHUNK_ROWS = SUBLANES
ROW_PAD = 128
EXPERT_ROWS = 1024
D_CHUNK = 256
META_FIELDS = 8


def _cparams(sem):
    return pltpu.CompilerParams(dimension_semantics=sem, vmem_limit_bytes=VMEM_LIMIT)


def _log2(n):
    assert n > 0 and n & (n - 1) == 0, n
    return n.bit_length() - 1


def _div_pow2(x, n):
    return lax.shift_right_logical(x, jnp.int32(_log2(n)))


def _mod_pow2(x, n):
    _log2(n)
    return x & jnp.int32(n - 1)


def _round_up(x, m):
    return (x + m - 1) // m * m


def _prenorm_kernel(xp_ref, xs_ref, w_ref, o_ref, *, n_p):
    i = pl.program_id(0)

    def body(x_ref):
        x = x_ref[...]
        ms = jnp.mean(x * x, axis=-1, keepdims=True)
        o_ref[...] = (x * lax.rsqrt(ms + EPS) * w_ref[...]).astype(o_ref.dtype)

    @pl.when(i < n_p)
    def _():
        body(xp_ref)

    @pl.when(i >= n_p)
    def _():
        body(xs_ref)


def _prenorm(xp, xs, w, tile=256):
    tp, d = xp.shape
    ts = xs.shape[0]
    n_p, n_s = tp // tile, ts // tile
    return pl.pallas_call(
        functools.partial(_prenorm_kernel, n_p=n_p),
        grid=(n_p + n_s,),
        in_specs=[
            pl.BlockSpec((tile, d), lambda i: (jnp.minimum(i, n_p - 1), 0)),
            pl.BlockSpec((tile, d), lambda i: (jnp.maximum(i - n_p, 0), 0)),
            pl.BlockSpec((1, d), lambda i: (0, 0)),
        ],
        out_specs=pl.BlockSpec((tile, d), lambda i: (i, 0)),
        out_shape=jax.ShapeDtypeStruct((tp + ts, d), BF16),
        compiler_params=_cparams(("arbitrary",)),
        name="prenorm",
    )(xp, xs, w.reshape(1, d))


def _head_rms(z, hw, bd_ref):
    ss = z * z
    hi = ss.astype(BF16)
    lo = (ss - hi.astype(F32)).astype(BF16)
    bd = bd_ref[...]
    tot = jnp.dot(hi, bd, preferred_element_type=F32) + jnp.dot(lo, bd, preferred_element_type=F32)
    return z * lax.rsqrt(tot * (1.0 / HEAD_DIM) + EPS) * hw


def _load_w(i, w_ref, wb_ref):
    @pl.when(i == 0)
    def _():
        wb_ref[...] = w_ref[...].astype(BF16)


def _proj_q_kernel(x_ref, w_ref, hw_ref, bd_ref, o_ref, wb_ref):
    _load_w(pl.program_id(1), w_ref, wb_ref)
    z = jnp.dot(x_ref[...], wb_ref[...], preferred_element_type=F32)
    for c in range(z.shape[1] // MXU_DIM):
        sl = slice(c * MXU_DIM, (c + 1) * MXU_DIM)
        o_ref[:, sl] = _head_rms(z[:, sl], hw_ref[...], bd_ref).astype(o_ref.dtype)


def _proj_kv_kernel(x_ref, w_ref, hw_ref, bd_ref, k_ref, v_ref, wb_ref):
    _load_w(pl.program_id(1), w_ref, wb_ref)
    z = jnp.dot(x_ref[...], wb_ref[...], preferred_element_type=F32)
    kw = k_ref.shape[1]
    k_ref[...] = _head_rms(z[:, :kw], hw_ref[...], bd_ref)
    v_ref[...] = z[:, kw:]


def _proj_glu_kernel(x_ref, wa_ref, wg_ref, o_ref, wab_ref, wgb_ref):
    i = pl.program_id(1)
    _load_w(i, wa_ref, wab_ref)
    _load_w(i, wg_ref, wgb_ref)
    x = x_ref[...]
    a = jnp.dot(x, wab_ref[...], preferred_element_type=F32)
    g = jnp.dot(x, wgb_ref[...], preferred_element_type=F32)
    o_ref[...] = a * jax.nn.sigmoid(g)


def _proj_gate_kernel(x_ref, w_ref, b_ref, o_ref, wb_ref):
    _load_w(pl.program_id(1), w_ref, wb_ref)
    z = jnp.dot(x_ref[...], wb_ref[...], preferred_element_type=F32)
    o_ref[...] = jax.nn.sigmoid(z + b_ref[...]).astype(o_ref.dtype)


def _row_tile(t, cap=1152):
    best = 16
    for m in range(16, cap + 1, 16):
        if t % m == 0:
            best = m
    return best


def _project(xn, w_in, b_gate, q_norm_w, k_norm_w, d_model, conv_ch):
    t, d = xn.shape
    q_w = d_model
    kv_w = N_KV_HEADS * HEAD_DIM
    tn = 2 * kv_w
    tm = _row_tile(t)
    ni = t // tm
    reps = MXU_DIM // HEAD_DIM
    hq = jnp.tile(q_norm_w.astype(F32), reps).reshape(1, MXU_DIM)
    hk = jnp.tile(k_norm_w.astype(F32), reps).reshape(1, MXU_DIM)
    gid = jnp.arange(MXU_DIM) // HEAD_DIM
    bd = (gid[:, None] == gid[None, :]).astype(BF16)

    x_spec = pl.BlockSpec((tm, d), lambda s, i: (i, 0))
    small = lambda shape: pl.BlockSpec(shape, lambda s, i: (0, 0))

    def w_spec(c0):
        return pl.BlockSpec((d, tn), lambda s, i: (0, c0 + s))

    wscr = pltpu.VMEM((d, tn), BF16)
    cp = _cparams(("arbitrary", "arbitrary"))

    q = pl.pallas_call(
        _proj_q_kernel, grid=(q_w // tn, ni),
        in_specs=[x_spec, w_spec(0), small((1, MXU_DIM)), small((MXU_DIM, MXU_DIM))],
        out_specs=pl.BlockSpec((tm, tn), lambda s, i: (i, s)),
        out_shape=jax.ShapeDtypeStruct((t, q_w), BF16),
        scratch_shapes=[wscr], compiler_params=cp, name="proj_q",
    )(xn, w_in, hq, bd)

    c_kv = q_w // tn
    k, v = pl.pallas_call(
        _proj_kv_kernel, grid=(1, ni),
        in_specs=[x_spec, w_spec(c_kv), small((1, MXU_DIM)), small((MXU_DIM, MXU_DIM))],
        out_specs=[pl.BlockSpec((tm, kv_w), lambda s, i: (i, 0))] * 2,
        out_shape=[jax.ShapeDtypeStruct((t, kv_w), F32)] * 2,
        scratch_shapes=[wscr], compiler_params=cp, name="proj_kv",
    )(xn, w_in, hk, bd)

    c_a = c_kv + 1
    n_glu = conv_ch // tn
    u = pl.pallas_call(
        _proj_glu_kernel, grid=(n_glu, ni),
        in_specs=[x_spec, w_spec(c_a), w_spec(c_a + n_glu)],
        out_specs=pl.BlockSpec((tm, tn), lambda s, i: (i, s)),
        out_shape=jax.ShapeDtypeStruct((t, conv_ch), F32),
        scratch_shapes=[wscr, wscr], compiler_params=cp, name="proj_glu",
    )(xn, w_in, w_in)

    c_g = c_a + 2 * n_glu
    n_gate = 2 * d_model // tn
    gates = pl.pallas_call(
        _proj_gate_kernel, grid=(n_gate, ni),
        in_specs=[x_spec, w_spec(c_g), pl.BlockSpec((1, tn), lambda s, i: (0, s))],
        out_specs=pl.BlockSpec((tm, tn), lambda s, i: (i, s)),
        out_shape=jax.ShapeDtypeStruct((t, 2 * d_model), BF16),
        scratch_shapes=[wscr], compiler_params=cp, name="proj_gate",
    )(xn, w_in, b_gate.reshape(1, -1))
    return q, k, v, u, gates


def _attn_prompt_kernel(sink_ref, q_ref, kc_ref, kp_ref, vc_ref, vp_ref, o_ref):
    blk = q_ref.shape[0]
    n = pl.program_id(1)
    rows = 2 * blk
    r_iota = lax.broadcasted_iota(jnp.int32, (rows, rows), 0)
    c_iota = lax.broadcasted_iota(jnp.int32, (rows, rows), 1)
    qi = jnp.where(r_iota >= blk, r_iota - blk, r_iota)
    mask = (c_iota >= qi) & (c_iota <= qi + WINDOW) & ((c_iota >= blk) | (n > 0))
    lo_lane = lax.broadcasted_iota(jnp.int32, (blk, LANES), 1) < HEAD_DIM
    top_row = lax.broadcasted_iota(jnp.int32, (rows, 1), 0) < blk
    zero = jnp.zeros((blk, LANES), q_ref.dtype)
    scale = HEAD_DIM ** -0.5
    group = q_ref.shape[1] // (N_KV_HEADS * LANES)
    for g in range(N_KV_HEADS):
        hs = slice(g * HEAD_DIM, (g + 1) * HEAD_DIM)
        kg = jnp.concatenate([kp_ref[:, hs], kc_ref[:, hs]], axis=0)
        vg = jnp.concatenate([vp_ref[:, hs], vc_ref[:, hs]], axis=0)
        kdup = jnp.concatenate([kg, kg], axis=1).astype(BF16)
        vdup = jnp.concatenate([vg, vg], axis=1).astype(BF16)
        for p in range(group):
            pair = g * group + p
            ls = slice(pair * LANES, (pair + 1) * LANES)
            qp = q_ref[:, ls]
            lhs = jnp.concatenate([jnp.where(lo_lane, qp, zero), jnp.where(lo_lane, zero, qp)], axis=0)
            s = lax.dot_general(lhs, kdup, (((1,), (1,)), ((), ())), preferred_element_type=F32) * scale
            s = jnp.where(mask, s, NEG_INF)
            sink = jnp.where(top_row, sink_ref[2 * pair], sink_ref[2 * pair + 1])
            m = jnp.maximum(jnp.max(s, axis=-1, keepdims=True), sink)
            e = jnp.exp(s - m)
            denom = jnp.sum(e, axis=-1, keepdims=True) + jnp.exp(sink - m)
            o2 = jnp.dot(e.astype(BF16), vdup, preferred_element_type=F32) / denom
            o_ref[:, ls] = jnp.where(lo_lane, o2[:blk], o2[blk:]).astype(o_ref.dtype)


def _attn_prompt(q, k, v, sinks, bsz, slen):
    blk = WINDOW
    nb = slen // blk
    qw, kw = q.shape[1], k.shape[1]
    cur = lambda b, n: (b * nb + n, 0)
    prev = lambda b, n: (b * nb + jnp.maximum(n - 1, 0), 0)
    return pl.pallas_call(
        _attn_prompt_kernel,
        grid=(bsz, nb),
        in_specs=[
            pl.BlockSpec(memory_space=pltpu.SMEM),
            pl.BlockSpec((blk, qw), cur),
            pl.BlockSpec((blk, kw), cur), pl.BlockSpec((blk, kw), prev),
            pl.BlockSpec((blk, kw), cur), pl.BlockSpec((blk, kw), prev),
        ],
        out_specs=pl.BlockSpec((blk, qw), cur),
        out_shape=jax.ShapeDtypeStruct((bsz * slen, qw), BF16),
        compiler_params=_cparams(("arbitrary", "arbitrary")),
        name="attn_prompt",
    )(sinks.astype(F32), q, k, k, v, v)


def _attn_sample_kernel(sink_ref, q_ref, k_ref, v_ref, o_ref, *, qlen):
    nseq, rows, _ = q_ref.shape
    nkeys, kvw = k_ref.shape[1], k_ref.shape[2]
    per_kv = rows // N_KV_HEADS
    r_lane = _div_pow2(lax.broadcasted_iota(jnp.int32, (rows, kvw), 0), per_kv)
    c_lane = _div_pow2(lax.broadcasted_iota(jnp.int32, (rows, kvw), 1), HEAD_DIM)
    own = r_lane == c_lane
    qi = _mod_pow2(lax.broadcasted_iota(jnp.int32, (rows, nkeys), 0), qlen)
    kc = lax.broadcasted_iota(jnp.int32, (rows, nkeys), 1)
    mask = ((kc < WINDOW) & (kc >= qi)) | ((kc >= WINDOW) & (kc - WINDOW <= qi))
    sink = sink_ref[...]
    scale = HEAD_DIM ** -0.5
    for b in range(nseq):
        q = q_ref[b]
        qe = jnp.where(own, jnp.concatenate([q] * N_KV_HEADS, axis=1), jnp.zeros((), q.dtype))
        s = lax.dot_general(qe, k_ref[b].astype(BF16), (((1,), (1,)), ((), ())),
                            preferred_element_type=F32) * scale
        s = jnp.where(mask, s, NEG_INF)
        m = jnp.maximum(jnp.max(s, axis=-1, keepdims=True), sink)
        e = jnp.exp(s - m)
        denom = jnp.sum(e, axis=-1, keepdims=True) + jnp.exp(sink - m)
        o = jnp.dot(e.astype(BF16), v_ref[b].astype(BF16), preferred_element_type=F32)
        o = jnp.where(own, o, 0.0)
        acc = o[:, :HEAD_DIM]
        for g in range(1, N_KV_HEADS):
            acc = acc + o[:, g * HEAD_DIM:(g + 1) * HEAD_DIM]
        o_ref[b] = (acc / denom).astype(o_ref.dtype)


def _attn_sample(q, k_all, v_all, sinks, qlen, seq_per_step=8):
    bsz, nkeys, kvw = k_all.shape
    n_heads = q.shape[1] // HEAD_DIM
    group = n_heads // N_KV_HEADS
    rows = n_heads * qlen
    qr = q.reshape(bsz, qlen, N_KV_HEADS, group, HEAD_DIM).transpose(0, 2, 3, 1, 4).reshape(bsz, rows, HEAD_DIM)
    sink_rows = jnp.repeat(sinks.astype(F32), qlen).reshape(rows, 1)
    seq = lambda i: (i, 0, 0)
    o = pl.pallas_call(
        functools.partial(_attn_sample_kernel, qlen=qlen),
        grid=(bsz // seq_per_step,),
        in_specs=[
            pl.BlockSpec((rows, 1), lambda i: (0, 0)),
            pl.BlockSpec((seq_per_step, rows, HEAD_DIM), seq),
            pl.BlockSpec((seq_per_step, nkeys, kvw), seq),
            pl.BlockSpec((seq_per_step, nkeys, kvw), seq),
        ],
        out_specs=pl.BlockSpec((seq_per_step, rows, HEAD_DIM), seq),
        out_shape=jax.ShapeDtypeStruct((bsz, rows, HEAD_DIM), BF16),
        compiler_params=_cparams(("arbitrary",)),
        name="attn_sample",
    )(sink_rows, qr, k_all, v_all)
    return o.reshape(bsz, N_KV_HEADS, group, qlen, HEAD_DIM).transpose(0, 3, 1, 2, 4).reshape(bsz * qlen, -1)


CONV_HALO = 32
CONV_ROWS = 32
CONV_COLS = 256


def _conv_prompt_kernel(uc_ref, up_ref, w_ref, b_ref, y_ref, sh_ref):
    tile = uc_ref.shape[0]
    n = pl.program_id(1)
    halo = jnp.where(n > 0, up_ref[...], 0.0)
    sh_ref[0, :CONV_HALO, :] = halo
    sh_ref[0, CONV_HALO:, :] = uc_ref[...]
    keep = tile + CONV_HALO - SUBLANES
    for r in range(1, SUBLANES):
        sh_ref[r, :keep, :] = sh_ref[0, r:r + keep, :]
    first = CONV_HALO - (CONV_WIDTH - 1)
    ch = uc_ref.shape[1]

    def rows_step(rc, carry):
        r0 = pl.multiple_of(rc * CONV_ROWS, CONV_ROWS)
        for cc in range(ch // CONV_COLS):
            cs = slice(cc * CONV_COLS, (cc + 1) * CONV_COLS)
            acc = jnp.zeros((CONV_ROWS, CONV_COLS), F32)
            for j in range(CONV_WIDTH):
                off = first + j
                a, r = off // SUBLANES, off % SUBLANES
                acc = acc + sh_ref[r, pl.ds(r0 + a * SUBLANES, CONV_ROWS), cs] * w_ref[j:j + 1, cs]
            y_ref[pl.ds(r0, CONV_ROWS), cs] = acc + b_ref[:, cs]
        return carry

    lax.fori_loop(0, tile // CONV_ROWS, rows_step, 0)


def _conv_prompt(u, w, b, bsz, slen, tile=256):
    ch = u.shape[1]
    nt = slen // tile
    per = tile // CONV_HALO
    return pl.pallas_call(
        _conv_prompt_kernel,
        grid=(bsz, nt),
        in_specs=[
            pl.BlockSpec((tile, ch), lambda bb, n: (bb * nt + n, 0)),
            pl.BlockSpec((CONV_HALO, ch), lambda bb, n: (jnp.maximum((bb * nt + n) * per - 1, 0), 0)),
            pl.BlockSpec((CONV_WIDTH, ch), lambda bb, n: (0, 0)),
            pl.BlockSpec((1, ch), lambda bb, n: (0, 0)),
        ],
        out_specs=pl.BlockSpec((tile, ch), lambda bb, n: (bb * nt + n, 0)),
        out_shape=jax.ShapeDtypeStruct((bsz * slen, ch), F32),
        scratch_shapes=[pltpu.VMEM((SUBLANES, tile + CONV_HALO, ch), F32)],
        compiler_params=_cparams(("arbitrary", "arbitrary")),
        name="conv_prompt",
    )(u, u, w, b.reshape(1, ch))


def _conv_sample_kernel(u_ref, w_ref, b_ref, y_ref, *, qlen):
    ch = w_ref.shape[1]
    for i in range(qlen):
        acc = jnp.zeros((u_ref.shape[0], ch), F32)
        for j in range(CONV_WIDTH):
            acc = acc + u_ref[:, (i + j) * ch:(i + j + 1) * ch] * w_ref[j:j + 1, :]
        y_ref[:, i * ch:(i + 1) * ch] = acc + b_ref[...]


def _conv_sample(u_ext, w, b, seq_per_step=16):
    bsz, ext, ch = u_ext.shape
    qlen = ext - (CONV_WIDTH - 1)
    y = pl.pallas_call(
        functools.partial(_conv_sample_kernel, qlen=qlen),
        grid=(bsz // seq_per_step,),
        in_specs=[
            pl.BlockSpec((seq_per_step, ext * ch), lambda i: (i, 0)),
            pl.BlockSpec((CONV_WIDTH, ch), lambda i: (0, 0)),
            pl.BlockSpec((1, ch), lambda i: (0, 0)),
        ],
        out_specs=pl.BlockSpec((seq_per_step, qlen * ch), lambda i: (i, 0)),
        out_shape=jax.ShapeDtypeStruct((bsz, qlen * ch), F32),
        compiler_params=_cparams(("arbitrary",)),
        name="conv_sample",
    )(u_ext.reshape(bsz, ext * ch), w, b.reshape(1, ch))
    return y.reshape(bsz * qlen, ch)


def _merge_kernel(y_ref, ao_ref, ga_ref, gc_ref, x_ref, lnw_ref, lnb_ref, wco_ref, wo_ref, fw_ref, wr_ref,
                  h_ref, xn_ref, lg_ref):
    y = y_ref[...]
    mu = jnp.mean(y, axis=-1, keepdims=True)
    yc = y - mu
    var = jnp.mean(yc * yc, axis=-1, keepdims=True)
    yl = yc * lax.rsqrt(var + EPS) * lnw_ref[...] + lnb_ref[...]
    act = yl * jax.nn.sigmoid(yl)
    conv_o = jnp.dot(act.astype(BF16), wco_ref[...], preferred_element_type=F32)
    mix = ga_ref[...].astype(F32) * ao_ref[...].astype(F32) + gc_ref[...].astype(F32) * conv_o
    h = x_ref[...] + jnp.dot(mix.astype(BF16), wo_ref[...], preferred_element_type=F32)
    h_ref[...] = h
    ms = jnp.mean(h * h, axis=-1, keepdims=True)
    xn = (h * lax.rsqrt(ms + EPS) * fw_ref[...]).astype(BF16)
    xn_ref[...] = xn
    lg_ref[...] = jnp.dot(xn, wr_ref[...], preferred_element_type=F32)


def _merge(y, ao, gates, row0, x, ln_w, ln_b, wco, wo, fw, wr, tile=256):
    t, d = x.shape
    ch = y.shape[1]
    off = row0 // tile
    row = lambda i: (i, 0)
    const = lambda shape: pl.BlockSpec(shape, lambda i: (0, 0), pipeline_mode=pl.Buffered(1))
    return pl.pallas_call(
        _merge_kernel,
        grid=(t // tile,),
        in_specs=[
            pl.BlockSpec((tile, ch), row),
            pl.BlockSpec((tile, d), row),
            pl.BlockSpec((tile, d), lambda i: (i + off, 0)),
            pl.BlockSpec((tile, d), lambda i: (i + off, 1)),
            pl.BlockSpec((tile, d), row),
            const((1, ch)), const((1, ch)), const((ch, d)), const((d, d)), const((1, d)),
            const((d, LANES)),
        ],
        out_specs=[pl.BlockSpec((tile, d), row), pl.BlockSpec((tile, d), row), pl.BlockSpec((tile, LANES), row)],
        out_shape=[jax.ShapeDtypeStruct((t, d), F32), jax.ShapeDtypeStruct((t, d), BF16),
                   jax.ShapeDtypeStruct((t, LANES), F32)],
        compiler_params=_cparams(("arbitrary",)),
        name="merge",
    )(y, ao, gates, gates, x, ln_w.reshape(1, ch), ln_b.reshape(1, ch), wco, wo, fw.reshape(1, d), wr)


def _route_kernel(lp_ref, ls_ref, tri_ref, upper_ref, meta_ref, metat_ref, cnt_ref, *, n_p):
    i = pl.program_id(0)

    def body(l_ref):
        lg = l_ref[...]
        shape = lg.shape
        lane = lax.broadcasted_iota(jnp.int32, shape, 1)
        big = jnp.int32(LANES)
        is_g = (lane >= N_EXPERTS) & (lane < N_EXPERTS + N_GROUPS)
        gl = jnp.where(is_g, lg, -jnp.inf)
        gmax = jnp.max(gl, axis=-1, keepdims=True)
        g_star = jnp.min(jnp.where(gl == gmax, lane - N_EXPERTS, big), axis=-1, keepdims=True)
        pg_star = 1.0 / jnp.sum(jnp.exp(gl - gmax), axis=-1, keepdims=True)
        in_g = (lane < N_EXPERTS) & (_div_pow2(lane, EXPERTS_PER_GROUP) == g_star)
        el = jnp.where(in_g, lg, -jnp.inf)
        v1 = jnp.max(el, axis=-1, keepdims=True)
        i1 = jnp.min(jnp.where(el == v1, lane, big), axis=-1, keepdims=True)
        el2 = jnp.where(lane == i1, -jnp.inf, el)
        v2 = jnp.max(el2, axis=-1, keepdims=True)
        i2 = jnp.min(jnp.where(el2 == v2, lane, big), axis=-1, keepdims=True)
        e2 = jnp.exp(v2 - v1)
        w1 = pg_star * (1.0 / (1.0 + e2))
        w2 = pg_star * (e2 / (1.0 + e2))
        oh1 = lane == i1
        oh2 = lane == i2
        cnt = jnp.where(oh1 | oh2, 1.0, 0.0)
        before = jnp.dot(tri_ref[...], cnt.astype(BF16), preferred_element_type=F32)
        tot = jnp.sum(cnt, axis=0, keepdims=True)
        runs = jnp.floor((tot + (CHUNK_ROWS - 1)) * (1.0 / CHUNK_ROWS))
        runs8 = jnp.broadcast_to(runs, (SUBLANES, LANES)).astype(BF16)
        start = jnp.dot(runs8, upper_ref[...], preferred_element_type=F32)[0:1] * CHUNK_ROWS
        slot = before + start
        s1 = jnp.sum(jnp.where(oh1, slot, 0.0), axis=-1, keepdims=True)
        s2 = jnp.sum(jnp.where(oh2, slot, 0.0), axis=-1, keepdims=True)
        cols = [i1.astype(F32), i2.astype(F32), w1, w2, s1, s2]
        meta = jnp.zeros(shape, F32)
        for c, val in enumerate(cols):
            meta = jnp.where(lane == c, val, meta)
        meta_ref[...] = meta
        metat_ref[...] = meta.T[:META_FIELDS]
        cnt_ref[...] = tot

    @pl.when(i < n_p)
    def _():
        body(lp_ref)

    @pl.when(i >= n_p)
    def _():
        body(ls_ref)


def _route(lg_p, lg_s):
    tile = TOKEN_BLOCK
    tp, ts = lg_p.shape[0], lg_s.shape[0]
    n_p, n_s = tp // tile, ts // tile
    nb = n_p + n_s
    tri = (jnp.arange(tile)[:, None] > jnp.arange(tile)[None, :]).astype(BF16)
    upper = (jnp.arange(LANES)[:, None] < jnp.arange(LANES)[None, :]).astype(BF16)
    return pl.pallas_call(
        functools.partial(_route_kernel, n_p=n_p),
        grid=(nb,),
        in_specs=[
            pl.BlockSpec((tile, LANES), lambda i: (jnp.minimum(i, n_p - 1), 0)),
            pl.BlockSpec((tile, LANES), lambda i: (jnp.maximum(i - n_p, 0), 0)),
            pl.BlockSpec((tile, tile), lambda i: (0, 0)),
            pl.BlockSpec((LANES, LANES), lambda i: (0, 0)),
        ],
        out_specs=[pl.BlockSpec((tile, LANES), lambda i: (i, 0)),
                   pl.BlockSpec((None, META_FIELDS, tile), lambda i: (i, 0, 0)),
                   pl.BlockSpec((None, 1, LANES), lambda i: (i, 0, 0))],
        out_shape=[jax.ShapeDtypeStruct((tp + ts, LANES), F32),
                   jax.ShapeDtypeStruct((nb, META_FIELDS, tile), F32),
                   jax.ShapeDtypeStruct((nb, 1, LANES), F32)],
        compiler_params=_cparams(("arbitrary",)),
        name="route",
    )(lg_p, lg_s, tri, upper)


def _plan_sizes(n_tok):
    nb = n_tok // TOKEN_BLOCK
    pad_per_block = N_EXPERTS * (CHUNK_ROWS - 1)
    max_chunks = _round_up(-(-(TOKEN_BLOCK * TOP_K + pad_per_block) // CHUNK_ROWS), SUBLANES)
    n_rows = _round_up(n_tok * TOP_K + nb * pad_per_block + N_EXPERTS * (ROW_PAD - 1), ROW_PAD)
    n_items = N_EXPERTS + n_rows // EXPERT_ROWS + 1
    max_zero = N_EXPERTS * (ROW_PAD // CHUNK_ROWS)
    return nb, max_chunks, n_rows, n_items, max_zero


def _dispatch_plan(counts, max_chunks, n_items, max_zero):
    i32 = jnp.int32
    c = counts[:, 0, :N_EXPERTS].astype(i32)
    run = (c + CHUNK_ROWS - 1) // CHUNK_ROWS
    run_end = jnp.cumsum(run, axis=1)
    src_start = run_end - run
    nchunk = run_end[:, -1]
    seg = jnp.sum(run, axis=0) * CHUNK_ROWS
    padded = (seg + ROW_PAD - 1) // ROW_PAD * ROW_PAD
    pad_end = jnp.cumsum(padded)
    pad_start = pad_end - padded
    dst_start = pad_start[None, :] // CHUNK_ROWS + (jnp.cumsum(run, axis=0) - run)
    jj = jnp.arange(max_chunks, dtype=i32)
    e_of = jnp.minimum(jnp.sum(run_end[:, None, :] <= jj[None, :, None], axis=-1), N_EXPERTS - 1)
    chunk_dst = (jnp.take_along_axis(dst_start, e_of, axis=1)
                 + (jj[None, :] - jnp.take_along_axis(src_start, e_of, axis=1)))
    chunk_dst = jnp.where(jj[None, :] < nchunk[:, None], chunk_dst, 0) * CHUNK_ROWS
    tail = (padded - seg) // CHUNK_ROWS
    tail_end = jnp.cumsum(tail)
    z = jnp.arange(max_zero, dtype=i32)
    ez = jnp.minimum(jnp.sum(tail_end[None, :] <= z[:, None], axis=-1), N_EXPERTS - 1)
    zero_dst = (pad_start[ez] + seg[ez]) // CHUNK_ROWS + (z - (tail_end[ez] - tail[ez]))
    nzero = tail_end[-1]
    zero_dst = jnp.where(z < nzero, zero_dst, 0) * CHUNK_ROWS
    items_e = (padded + EXPERT_ROWS - 1) // EXPERT_ROWS
    item_end = jnp.cumsum(items_e)
    item_start = item_end - items_e
    w = jnp.arange(n_items, dtype=i32)
    valid = w < item_end[-1]
    item_e = jnp.minimum(jnp.sum(item_end[None, :] <= w[:, None], axis=-1), N_EXPERTS - 1)
    last_e = jnp.minimum(jnp.sum(item_end <= item_end[-1] - 1), N_EXPERTS - 1)
    item_e = jnp.where(valid, item_e, last_e)
    k = w - item_start[item_e]
    item_row0 = jnp.where(valid, pad_start[item_e] + k * EXPERT_ROWS, 0)
    item_rows = jnp.where(valid, jnp.clip(padded[item_e] - k * EXPERT_ROWS, 0, EXPERT_ROWS), 0)
    item_nsub = item_rows // ROW_PAD
    return (chunk_dst.reshape(-1).astype(i32), nchunk.astype(i32), zero_dst.astype(i32),
            nzero.reshape(1).astype(i32), item_e.astype(i32), item_row0.astype(i32), item_nsub.astype(i32))


def _scatter_kernel(cd_ref, nc_ref, zd_ref, nz_ref, mt_ref, xp_ref, xs_ref, out_hbm, loc_ref, zero_ref, sem,
                    *, n_p, max_chunks):
    i = pl.program_id(0)
    n_slots = loc_ref.shape[0]
    d = xp_ref.shape[1]
    mt = mt_ref[...]
    w1, w2 = mt[2:3, :], mt[3:4, :]
    s1, s2 = mt[4:5, :].astype(jnp.int32), mt[5:6, :].astype(jnp.int32)
    slot = lax.broadcasted_iota(jnp.int32, (n_slots, mt.shape[1]), 0)
    p1 = slot == s1
    p2 = slot == s2
    perm = jnp.where(p1 | p2, 1.0, 0.0).astype(BF16)
    wcol = jnp.sum(jnp.where(p1, w1, 0.0) + jnp.where(p2, w2, 0.0), axis=-1, keepdims=True)
    loc_ref[:, d:] = jnp.broadcast_to(wcol, (n_slots, LANES))

    @pl.when(i < n_p)
    def _():
        loc_ref[:, :d] = jnp.dot(perm, xp_ref[...], preferred_element_type=F32)

    @pl.when(i >= n_p)
    def _():
        loc_ref[:, :d] = jnp.dot(perm, xs_ref[...], preferred_element_type=F32)

    def rows_at(r):
        return pl.ds(pl.multiple_of(r, CHUNK_ROWS), CHUNK_ROWS)

    def chunk_copy(j):
        return pltpu.make_async_copy(loc_ref.at[rows_at(j * CHUNK_ROWS)],
                                     out_hbm.at[rows_at(cd_ref[i * max_chunks + j])], sem)

    def zero_copy(z):
        return pltpu.make_async_copy(zero_ref, out_hbm.at[rows_at(zd_ref[z])], sem)

    def loop(n, fn):
        def body(j, carry):
            fn(j)
            return carry
        lax.fori_loop(0, n, body, 0)

    n = nc_ref[i]
    last = i == pl.num_programs(0) - 1
    loop(n, lambda j: chunk_copy(j).start())

    @pl.when(last)
    def _():
        zero_ref[...] = jnp.zeros_like(zero_ref)
        loop(nz_ref[0], lambda z: zero_copy(z).start())

    loop(n, lambda j: chunk_copy(j).wait())

    @pl.when(last)
    def _():
        loop(nz_ref[0], lambda z: zero_copy(z).wait())


def _scatter_rows(xn_p, xn_s, metat, chunk_dst, nchunk, zero_dst, nzero, max_chunks, n_rows):
    tile = TOKEN_BLOCK
    tp, d = xn_p.shape
    ts = xn_s.shape[0]
    n_p, n_s = tp // tile, ts // tile
    n_slots = max_chunks * CHUNK_ROWS
    width = d + LANES
    return pl.pallas_call(
        functools.partial(_scatter_kernel, n_p=n_p, max_chunks=max_chunks),
        grid_spec=pltpu.PrefetchScalarGridSpec(
            num_scalar_prefetch=4,
            grid=(n_p + n_s,),
            in_specs=[
                pl.BlockSpec((None, META_FIELDS, tile), lambda i, *_: (i, 0, 0)),
                pl.BlockSpec((tile, d), lambda i, *_: (jnp.minimum(i, n_p - 1), 0)),
                pl.BlockSpec((tile, d), lambda i, *_: (jnp.maximum(i - n_p, 0), 0)),
            ],
            out_specs=pl.BlockSpec(memory_space=pl.ANY),
            scratch_shapes=[pltpu.VMEM((n_slots, width), F32), pltpu.VMEM((CHUNK_ROWS, width), F32),
                            pltpu.SemaphoreType.DMA(())],
        ),
        out_shape=jax.ShapeDtypeStruct((n_rows, width), F32),
        compiler_params=_cparams(("arbitrary",)),
        name="scatter_rows",
    )(chunk_dst, nchunk, zero_dst, nzero, metat, xn_p, xn_s)


def _expert_kernel(e_ref, row0_ref, nsub_ref, xs_hbm, wg_ref, wu_ref, wd_ref, out_hbm,
                   xf_ref, xb_ref, acc_ref, wgb_ref, wub_ref, wdb_ref, sem):
    w = pl.program_id(0)
    c = pl.program_id(1)
    nsub = nsub_ref[w]
    row0 = pl.multiple_of(row0_ref[w], ROW_PAD)
    d = acc_ref.shape[1]

    def sub_rows(s):
        return pl.ds(pl.multiple_of(s * ROW_PAD, ROW_PAD), ROW_PAD)

    def hbm_rows(s):
        return pl.ds(pl.multiple_of(row0 + s * ROW_PAD, ROW_PAD), ROW_PAD)

    def in_copy(s):
        return pltpu.make_async_copy(xs_hbm.at[hbm_rows(s)], xf_ref.at[sub_rows(s)], sem)

    def out_copy(s):
        return pltpu.make_async_copy(acc_ref.at[sub_rows(s)], out_hbm.at[hbm_rows(s)], sem)

    def for_subs(fn):
        def body(s, carry):
            fn(s)
            return carry
        lax.fori_loop(0, nsub, body, 0)

    @pl.when((c == 0) & (nsub > 0))
    def _():
        for_subs(lambda s: in_copy(s).start())
        for_subs(lambda s: in_copy(s).wait())

        def prep(s):
            xb_ref[sub_rows(s), :] = xf_ref[sub_rows(s), :d].astype(BF16)
            acc_ref[sub_rows(s), :] = jnp.zeros((ROW_PAD, d), F32)
        for_subs(prep)

    @pl.when(nsub > 0)
    def _():
        wgb_ref[...] = wg_ref[...].astype(BF16)
        wub_ref[...] = wu_ref[...].astype(BF16)
        wdb_ref[...] = wd_ref[...].astype(BF16)

        def ffn(s):
            x = xb_ref[sub_rows(s), :]
            g = jnp.dot(x, wgb_ref[...], preferred_element_type=F32)
            u = jnp.dot(x, wub_ref[...], preferred_element_type=F32)
            hmid = (g * jax.nn.sigmoid(g) * u).astype(BF16)
            acc_ref[sub_rows(s), :] += jnp.dot(hmid, wdb_ref[...], preferred_element_type=F32)
        for_subs(ffn)

    @pl.when((c == pl.num_programs(1) - 1) & (nsub > 0))
    def _():
        def weigh(s):
            acc_ref[sub_rows(s), :] = acc_ref[sub_rows(s), :] * xf_ref[sub_rows(s), d:d + 1]
        for_subs(weigh)
        for_subs(lambda s: out_copy(s).start())
        for_subs(lambda s: out_copy(s).wait())


def _experts(xs, item_e, item_row0, item_nsub, w_gate, w_up, w_down):
    n_rows, width = xs.shape
    d = width - LANES
    d_exp = w_gate.shape[2]
    nc = d_exp // D_CHUNK
    n_items = item_e.shape[0]

    def chunk_of(w, c, nsub_ref):
        return jnp.where(nsub_ref[w] > 0, c, nc - 1)

    return pl.pallas_call(
        _expert_kernel,
        grid_spec=pltpu.PrefetchScalarGridSpec(
            num_scalar_prefetch=3,
            grid=(n_items, nc),
            in_specs=[
                pl.BlockSpec(memory_space=pl.ANY),
                pl.BlockSpec((None, d, D_CHUNK), lambda w, c, e, r, n: (e[w], 0, chunk_of(w, c, n))),
                pl.BlockSpec((None, d, D_CHUNK), lambda w, c, e, r, n: (e[w], 0, chunk_of(w, c, n))),
                pl.BlockSpec((None, D_CHUNK, d), lambda w, c, e, r, n: (e[w], chunk_of(w, c, n), 0)),
            ],
            out_specs=pl.BlockSpec(memory_space=pl.ANY),
            scratch_shapes=[
                pltpu.VMEM((EXPERT_ROWS, width), F32),
                pltpu.VMEM((EXPERT_ROWS, d), BF16),
                pltpu.VMEM((EXPERT_ROWS, d), F32),
                pltpu.VMEM((d, D_CHUNK), BF16),
                pltpu.VMEM((d, D_CHUNK), BF16),
                pltpu.VMEM((D_CHUNK, d), BF16),
                pltpu.SemaphoreType.DMA(()),
            ],
        ),
        out_shape=jax.ShapeDtypeStruct((n_rows, d), F32),
        compiler_params=_cparams(("arbitrary", "arbitrary")),
        name="experts",
    )(item_e, item_row0, item_nsub, xs, w_gate, w_up, w_down)


COMBINE_SPLIT = 2
SLOT_STEP = 256


def _combine_kernel(cd_ref, nc_ref, meta_ref, hp_ref, hs_ref, yb_hbm, yp_ref, ys_ref, loc_ref, sem,
                    *, n_p, max_chunks):
    i = pl.program_id(0)
    j = pl.program_id(1)
    n_slots = loc_ref.shape[0]

    def rows_at(r):
        return pl.ds(pl.multiple_of(r, CHUNK_ROWS), CHUNK_ROWS)

    def chunk_copy(c):
        return pltpu.make_async_copy(yb_hbm.at[rows_at(cd_ref[i * max_chunks + c])],
                                     loc_ref.at[rows_at(c * CHUNK_ROWS)], sem)

    @pl.when(j == 0)
    def _():
        n = nc_ref[i]

        def start(c, carry):
            chunk_copy(c).start()
            return carry
        lax.fori_loop(0, n, start, 0)

        def clear(c, carry):
            loc_ref[rows_at(c * CHUNK_ROWS), :] = jnp.zeros((CHUNK_ROWS, loc_ref.shape[1]), F32)
            return carry
        lax.fori_loop(n, max_chunks, clear, 0)

        def wait(c, carry):
            chunk_copy(c).wait()
            return carry
        lax.fori_loop(0, n, wait, 0)

    meta = meta_ref[...]
    s1 = meta[:, 4:5].astype(jnp.int32)
    s2 = meta[:, 5:6].astype(jnp.int32)
    rows = meta.shape[0]
    f = jnp.zeros((rows, loc_ref.shape[1]), F32)
    for r0 in range(0, n_slots, SLOT_STEP):
        slot = lax.broadcasted_iota(jnp.int32, (rows, SLOT_STEP), 1) + r0
        pt = jnp.where((slot == s1) | (slot == s2), 1.0, 0.0).astype(BF16)
        f = f + jnp.dot(pt, loc_ref[r0:r0 + SLOT_STEP, :].astype(BF16), preferred_element_type=F32)
    g = i * pl.num_programs(1) + j

    @pl.when(g < n_p)
    def _():
        yp_ref[...] = hp_ref[...] + f

    @pl.when(g >= n_p)
    def _():
        ys_ref[...] = hs_ref[...] + f


def _combine(h_p, h_s, meta, chunk_dst, nchunk, yb, max_chunks):
    tp, d = h_p.shape
    ts = h_s.shape[0]
    tile = TOKEN_BLOCK // COMBINE_SPLIT
    n_p, n_s = tp // tile, ts // tile
    nb = (tp + ts) // TOKEN_BLOCK
    n_slots = max_chunks * CHUNK_ROWS
    assert n_slots % SLOT_STEP == 0
    blk = lambda i, j: i * COMBINE_SPLIT + j
    p_idx = lambda i, j, *_: (jnp.minimum(blk(i, j), n_p - 1), 0)
    s_idx = lambda i, j, *_: (jnp.maximum(blk(i, j) - n_p, 0), 0)
    return pl.pallas_call(
        functools.partial(_combine_kernel, n_p=n_p, max_chunks=max_chunks),
        grid_spec=pltpu.PrefetchScalarGridSpec(
            num_scalar_prefetch=2,
            grid=(nb, COMBINE_SPLIT),
            in_specs=[
                pl.BlockSpec((tile, LANES), lambda i, j, *_: (blk(i, j), 0)),
                pl.BlockSpec((tile, d), p_idx),
                pl.BlockSpec((tile, d), s_idx),
                pl.BlockSpec(memory_space=pl.ANY),
            ],
            out_specs=[pl.BlockSpec((tile, d), p_idx), pl.BlockSpec((tile, d), s_idx)],
            scratch_shapes=[pltpu.VMEM((n_slots, d), F32), pltpu.SemaphoreType.DMA(())],
        ),
        out_shape=[jax.ShapeDtypeStruct((tp, d), F32), jax.ShapeDtypeStruct((ts, d), F32)],
        compiler_params=_cparams(("arbitrary", "arbitrary")),
        name="combine",
    )(chunk_dst, nchunk, meta, h_p, h_s, yb)


def _layer(x_p, x_s, ck, cv, sc, attn_norm_w, w_in, b_gate, q_norm_w, k_norm_w, sinks, conv_dw_w, conv_dw_b,
           conv_ln_w, conv_ln_b, w_conv_out, w_out, ffn_norm_w, router_group, router_expert, w_gate, w_up,
           w_down):
    bsz, slen, d = x_p.shape
    dbs, dlen, _ = x_s.shape
    tp, ts = bsz * slen, dbs * dlen
    conv_ch = conv_dw_w.shape[1]
    kvw = N_KV_HEADS * HEAD_DIM
    xp2, xs2 = x_p.reshape(tp, d), x_s.reshape(ts, d)

    xn = _prenorm(xp2, xs2, attn_norm_w)
    q, k, v, u, gates = _project(xn, w_in, b_gate, q_norm_w, k_norm_w, d, conv_ch)

    ao_p = _attn_prompt(q, k, v, sinks, bsz, slen)
    k_s = k[tp:].reshape(dbs, dlen, kvw)
    v_s = v[tp:].reshape(dbs, dlen, kvw)
    k_all = jnp.concatenate([ck.reshape(dbs, WINDOW, kvw), k_s], axis=1)
    v_all = jnp.concatenate([cv.reshape(dbs, WINDOW, kvw), v_s], axis=1)
    ao_s = _attn_sample(q[tp:], k_all, v_all, sinks, dlen)

    y_p = _conv_prompt(u, conv_dw_w, conv_dw_b, bsz, slen)
    u_ext = jnp.concatenate([sc, u[tp:].reshape(dbs, dlen, conv_ch)], axis=1)
    y_s = _conv_sample(u_ext, conv_dw_w, conv_dw_b)

    wco = w_conv_out.astype(BF16)
    wo = w_out.astype(BF16)
    wr = jnp.concatenate([router_expert, router_group,
                          jnp.zeros((d, LANES - N_EXPERTS - N_GROUPS), router_expert.dtype)], axis=1).astype(BF16)
    margs = (conv_ln_w, conv_ln_b, wco, wo, ffn_norm_w, wr)
    h_p, xn_p, lg_p = _merge(y_p, ao_p, gates, 0, xp2, *margs)
    h_s, xn_s, lg_s = _merge(y_s, ao_s, gates, tp, xs2, *margs)

    meta, metat, counts = _route(lg_p, lg_s)
    _, max_chunks, n_rows, n_items, max_zero = _plan_sizes(tp + ts)
    chunk_dst, nchunk, zero_dst, nzero, item_e, item_row0, item_nsub = _dispatch_plan(
        counts, max_chunks, n_items, max_zero)
    xs_rows = _scatter_rows(xn_p, xn_s, metat, chunk_dst, nchunk, zero_dst, nzero, max_chunks, n_rows)
    yb = _experts(xs_rows, item_e, item_row0, item_nsub, w_gate, w_up, w_down)
    out_p, out_s = _combine(h_p, h_s, meta, chunk_dst, nchunk, yb, max_chunks)

    new_p = (k[:tp].reshape(bsz, slen, N_KV_HEADS, HEAD_DIM)[:, -WINDOW:],
             v[:tp].reshape(bsz, slen, N_KV_HEADS, HEAD_DIM)[:, -WINDOW:],
             u[:tp].reshape(bsz, slen, conv_ch)[:, -(CONV_WIDTH - 1):])
    new_s = (k_all[:, -WINDOW:].reshape(dbs, WINDOW, N_KV_HEADS, HEAD_DIM),
             v_all[:, -WINDOW:].reshape(dbs, WINDOW, N_KV_HEADS, HEAD_DIM),
             u_ext[:, -(CONV_WIDTH - 1):])
    return out_p.reshape(bsz, slen, d), out_s.reshape(dbs, dlen, d), new_p, new_s


def kernel(x_prompt, x_sample, cache_k, cache_v, state_conv, attn_norm_w, w_in, b_gate, q_norm_w, k_norm_w,
           sinks, conv_dw_w, conv_dw_b, conv_ln_w, conv_ln_b, w_conv_out, w_out, ffn_norm_w, router_group,
           router_expert, w_gate, w_up, w_down):
    x_p, x_s = x_prompt, x_sample
    kp, vp, cp, ks_, vs_, cs_ = [], [], [], [], [], []
    for l in range(cache_k.shape[0]):
        x_p, x_s, new_p, new_s = _layer(
            x_p, x_s, cache_k[l], cache_v[l], state_conv[l], attn_norm_w[l], w_in[l], b_gate[l], q_norm_w[l],
            k_norm_w[l], sinks[l], conv_dw_w[l], conv_dw_b[l], conv_ln_w[l], conv_ln_b[l], w_conv_out[l],
            w_out[l], ffn_norm_w[l], router_group[l], router_expert[l], w_gate[l], w_up[l], w_down[l])
        kp.append(new_p[0]); vp.append(new_p[1]); cp.append(new_p[2])
        ks_.append(new_s[0]); vs_.append(new_s[1]); cs_.append(new_s[2])
    return (x_p, x_s, jnp.stack(kp), jnp.stack(vp), jnp.stack(cp),
            jnp.stack(ks_), jnp.stack(vs_), jnp.stack(cs_))
```

```python
import functools

import jax
import jax.numpy as jnp
from jax import lax
from jax.experimental import pallas as pl
from jax.experimental.pallas import tpu as pltpu

F32 = jnp.float32
BF16 = jnp.bfloat16
EPS = 1e-6
NEG_INF = -1e30

HEAD_DIM = 64
N_KV_HEADS = 4
WINDOW = 128
CONV_WIDTH = 31
N_GROUPS = 4
EXPERTS_PER_GROUP = 8
N_EXPERTS = N_GROUPS * EXPERTS_PER_GROUP
TOP_K = 2

LANES = 128
SUBLANES = 8
MXU_DIM = 256
VMEM_LIMIT = 56 * 1024 * 1024

TOKEN_BLOCK = 512
CHUNK_ROWS = SUBLANES
ROW_PAD = 128
EXPERT_ROWS = 768
D_CHUNK = 256
META_FIELDS = 8


def _cparams(sem):
    return pltpu.CompilerParams(dimension_semantics=sem, vmem_limit_bytes=VMEM_LIMIT)


def _log2(n):
    assert n > 0 and n & (n - 1) == 0, n
    return n.bit_length() - 1


def _div_pow2(x, n):
    return lax.shift_right_logical(x, jnp.int32(_log2(n)))


def _mod_pow2(x, n):
    _log2(n)
    return x & jnp.int32(n - 1)


def _round_up(x, m):
    return (x + m - 1) // m * m


def _prenorm_kernel(xp_ref, xs_ref, w_ref, o_ref, *, n_p):
    i = pl.program_id(0)

    def body(x_ref):
        x = x_ref[...]
        ms = jnp.mean(x * x, axis=-1, keepdims=True)
        o_ref[...] = (x * lax.rsqrt(ms + EPS) * w_ref[...]).astype(o_ref.dtype)

    @pl.when(i < n_p)
    def _():
        body(xp_ref)

    @pl.when(i >= n_p)
    def _():
        body(xs_ref)


def _prenorm(xp, xs, w, tile=512):
    tp, d = xp.shape
    ts = xs.shape[0]
    n_p, n_s = tp // tile, ts // tile
    return pl.pallas_call(
        functools.partial(_prenorm_kernel, n_p=n_p),
        grid=(n_p + n_s,),
        in_specs=[
            pl.BlockSpec((tile, d), lambda i: (jnp.minimum(i, n_p - 1), 0)),
            pl.BlockSpec((tile, d), lambda i: (jnp.maximum(i - n_p, 0), 0)),
            pl.BlockSpec((1, d), lambda i: (0, 0)),
        ],
        out_specs=pl.BlockSpec((tile, d), lambda i: (i, 0)),
        out_shape=jax.ShapeDtypeStruct((tp + ts, d), BF16),
        compiler_params=_cparams(("arbitrary",)),
        name="prenorm",
    )(xp, xs, w.reshape(1, d))


def _head_rms(z, hw, bd_ref):
    ss = z * z
    hi = ss.astype(BF16)
    lo = (ss - hi.astype(F32)).astype(BF16)
    bd = bd_ref[...]
    tot = jnp.dot(hi, bd, preferred_element_type=F32) + jnp.dot(lo, bd, preferred_element_type=F32)
    return z * lax.rsqrt(tot * (1.0 / HEAD_DIM) + EPS) * hw


def _load_w(i, w_ref, wb_ref):
    @pl.when(i == 0)
    def _():
        wb_ref[...] = w_ref[...].astype(BF16)


def _proj_q_kernel(x_ref, w_ref, hw_ref, bd_ref, o_ref, wb_ref):
    _load_w(pl.program_id(1), w_ref, wb_ref)
    z = jnp.dot(x_ref[...], wb_ref[...], preferred_element_type=F32)
    for c in range(z.shape[1] // MXU_DIM):
        sl = slice(c * MXU_DIM, (c + 1) * MXU_DIM)
        o_ref[:, sl] = _head_rms(z[:, sl], hw_ref[...], bd_ref).astype(o_ref.dtype)


def _proj_kv_kernel(x_ref, w_ref, hw_ref, bd_ref, k_ref, v_ref, wb_ref):
    _load_w(pl.program_id(1), w_ref, wb_ref)
    z = jnp.dot(x_ref[...], wb_ref[...], preferred_element_type=F32)
    kw = k_ref.shape[1]
    k_ref[...] = _head_rms(z[:, :kw], hw_ref[...], bd_ref)
    v_ref[...] = z[:, kw:]


def _proj_glu_kernel(x_ref, wa_ref, wg_ref, o_ref, wab_ref, wgb_ref):
    i = pl.program_id(1)
    _load_w(i, wa_ref, wab_ref)
    _load_w(i, wg_ref, wgb_ref)
    x = x_ref[...]
    a = jnp.dot(x, wab_ref[...], preferred_element_type=F32)
    g = jnp.dot(x, wgb_ref[...], preferred_element_type=F32)
    o_ref[...] = a * jax.nn.sigmoid(g)


def _proj_gate_kernel(x_ref, w_ref, b_ref, o_ref, wb_ref):
    _load_w(pl.program_id(1), w_ref, wb_ref)
    z = jnp.dot(x_ref[...], wb_ref[...], preferred_element_type=F32)
    o_ref[...] = jax.nn.sigmoid(z + b_ref[...]).astype(o_ref.dtype)


def _row_tile(t, cap=1152):
    best = 16
    for m in range(16, cap + 1, 16):
        if t % m == 0:
            best = m
    return best


def _project(xn, w_in, b_gate, q_norm_w, k_norm_w, d_model, conv_ch):
    t, d = xn.shape
    q_w = d_model
    kv_w = N_KV_HEADS * HEAD_DIM
    tn = 2 * kv_w
    tm = _row_tile(t)
    ni = t // tm
    reps = MXU_DIM // HEAD_DIM
    hq = jnp.tile(q_norm_w.astype(F32), reps).reshape(1, MXU_DIM)
    hk = jnp.tile(k_norm_w.astype(F32), reps).reshape(1, MXU_DIM)
    gid = jnp.arange(MXU_DIM) // HEAD_DIM
    bd = (gid[:, None] == gid[None, :]).astype(BF16)

    x_spec = pl.BlockSpec((tm, d), lambda s, i: (i, 0))
    small = lambda shape: pl.BlockSpec(shape, lambda s, i: (0, 0))

    def w_spec(c0):
        return pl.BlockSpec((d, tn), lambda s, i: (0, c0 + s))

    wscr = pltpu.VMEM((d, tn), BF16)
    cp = _cparams(("arbitrary", "arbitrary"))

    q = pl.pallas_call(
        _proj_q_kernel, grid=(q_w // tn, ni),
        in_specs=[x_spec, w_spec(0), small((1, MXU_DIM)), small((MXU_DIM, MXU_DIM))],
        out_specs=pl.BlockSpec((tm, tn), lambda s, i: (i, s)),
        out_shape=jax.ShapeDtypeStruct((t, q_w), BF16),
        scratch_shapes=[wscr], compiler_params=cp, name="proj_q",
    )(xn, w_in, hq, bd)

    c_kv = q_w // tn
    k, v = pl.pallas_call(
        _proj_kv_kernel, grid=(1, ni),
        in_specs=[x_spec, w_spec(c_kv), small((1, MXU_DIM)), small((MXU_DIM, MXU_DIM))],
        out_specs=[pl.BlockSpec((tm, kv_w), lambda s, i: (i, 0))] * 2,
        out_shape=[jax.ShapeDtypeStruct((t, kv_w), F32)] * 2,
        scratch_shapes=[wscr], compiler_params=cp, name="proj_kv",
    )(xn, w_in, hk, bd)

    c_a = c_kv + 1
    n_glu = conv_ch // tn
    u = pl.pallas_call(
        _proj_glu_kernel, grid=(n_glu, ni),
        in_specs=[x_spec, w_spec(c_a), w_spec(c_a + n_glu)],
        out_specs=pl.BlockSpec((tm, tn), lambda s, i: (i, s)),
        out_shape=jax.ShapeDtypeStruct((t, conv_ch), F32),
        scratch_shapes=[wscr, wscr], compiler_params=cp, name="proj_glu",
    )(xn, w_in, w_in)

    c_g = c_a + 2 * n_glu
    n_gate = 2 * d_model // tn
    gates = pl.pallas_call(
        _proj_gate_kernel, grid=(n_gate, ni),
        in_specs=[x_spec, w_spec(c_g), pl.BlockSpec((1, tn), lambda s, i: (0, s))],
        out_specs=pl.BlockSpec((tm, tn), lambda s, i: (i, s)),
        out_shape=jax.ShapeDtypeStruct((t, 2 * d_model), BF16),
        scratch_shapes=[wscr], compiler_params=cp, name="proj_gate",
    )(xn, w_in, b_gate.reshape(1, -1))
    return q, k, v, u, gates


def _attn_prompt_kernel(sink_ref, q_ref, kc_ref, kp_ref, vc_ref, vp_ref, o_ref):
    blk = q_ref.shape[0]
    n = pl.program_id(1)
    rows = 2 * blk
    r_iota = lax.broadcasted_iota(jnp.int32, (rows, rows), 0)
    c_iota = lax.broadcasted_iota(jnp.int32, (rows, rows), 1)
    qi = jnp.where(r_iota >= blk, r_iota - blk, r_iota)
    mask = (c_iota >= qi) & (c_iota <= qi + WINDOW) & ((c_iota >= blk) | (n > 0))
    lo_lane = lax.broadcasted_iota(jnp.int32, (blk, LANES), 1) < HEAD_DIM
    top_row = lax.broadcasted_iota(jnp.int32, (rows, 1), 0) < blk
    zero = jnp.zeros((blk, LANES), q_ref.dtype)
    scale = HEAD_DIM ** -0.5
    group = q_ref.shape[1] // (N_KV_HEADS * LANES)
    for g in range(N_KV_HEADS):
        hs = slice(g * HEAD_DIM, (g + 1) * HEAD_DIM)
        kg = jnp.concatenate([kp_ref[:, hs], kc_ref[:, hs]], axis=0)
        vg = jnp.concatenate([vp_ref[:, hs], vc_ref[:, hs]], axis=0)
        kdup = jnp.concatenate([kg, kg], axis=1).astype(BF16)
        vdup = jnp.concatenate([vg, vg], axis=1).astype(BF16)
        for p in range(group):
            pair = g * group + p
            ls = slice(pair * LANES, (pair + 1) * LANES)
            qp = q_ref[:, ls]
            lhs = jnp.concatenate([jnp.where(lo_lane, qp, zero), jnp.where(lo_lane, zero, qp)], axis=0)
            s = lax.dot_general(lhs, kdup, (((1,), (1,)), ((), ())), preferred_element_type=F32) * scale
            s = jnp.where(mask, s, NEG_INF)
            sink = jnp.where(top_row, sink_ref[2 * pair], sink_ref[2 * pair + 1])
            m = jnp.maximum(jnp.max(s, axis=-1, keepdims=True), sink)
            e = jnp.exp(s - m)
            denom = jnp.sum(e, axis=-1, keepdims=True) + jnp.exp(sink - m)
            o2 = jnp.dot(e.astype(BF16), vdup, preferred_element_type=F32) / denom
            o_ref[:, ls] = jnp.where(lo_lane, o2[:blk], o2[blk:]).astype(o_ref.dtype)


def _attn_prompt(q, k, v, sinks, bsz, slen):
    blk = WINDOW
    nb = slen // blk
    qw, kw = q.shape[1], k.shape[1]
    cur = lambda b, n: (b * nb + n, 0)
    prev = lambda b, n: (b * nb + jnp.maximum(n - 1, 0), 0)
    return pl.pallas_call(
        _attn_prompt_kernel,
        grid=(bsz, nb),
        in_specs=[
            pl.BlockSpec(memory_space=pltpu.SMEM),
            pl.BlockSpec((blk, qw), cur),
            pl.BlockSpec((blk, kw), cur), pl.BlockSpec((blk, kw), prev),
            pl.BlockSpec((blk, kw), cur), pl.BlockSpec((blk, kw), prev),
        ],
        out_specs=pl.BlockSpec((blk, qw), cur),
        out_shape=jax.ShapeDtypeStruct((bsz * slen, qw), BF16),
        compiler_params=_cparams(("arbitrary", "arbitrary")),
        name="attn_prompt",
    )(sinks.astype(F32), q, k, k, v, v)


def _attn_sample_kernel(sink_ref, q_ref, kc_ref, kn_ref, vc_ref, vn_ref, o_ref):
    nseq, rows, _ = q_ref.shape
    ncache, kvw = kc_ref.shape[1], kc_ref.shape[2]
    qlen = kn_ref.shape[1]
    per_kv = rows // N_KV_HEADS
    r_lane = _div_pow2(lax.broadcasted_iota(jnp.int32, (rows, kvw), 0), per_kv)
    c_lane = _div_pow2(lax.broadcasted_iota(jnp.int32, (rows, kvw), 1), HEAD_DIM)
    own = r_lane == c_lane
    mask_c = (lax.broadcasted_iota(jnp.int32, (rows, ncache), 1)
              >= _mod_pow2(lax.broadcasted_iota(jnp.int32, (rows, ncache), 0), qlen))
    mask_n = (lax.broadcasted_iota(jnp.int32, (rows, qlen), 1)
              <= _mod_pow2(lax.broadcasted_iota(jnp.int32, (rows, qlen), 0), qlen))
    sink = sink_ref[...]
    scale = HEAD_DIM ** -0.5
    contract_last = (((1,), (1,)), ((), ()))
    for b in range(nseq):
        q = q_ref[b]
        qe = jnp.where(own, jnp.concatenate([q] * N_KV_HEADS, axis=1), jnp.zeros((), q.dtype))
        sc = lax.dot_general(qe, kc_ref[b].astype(BF16), contract_last, preferred_element_type=F32) * scale
        sn = lax.dot_general(qe, kn_ref[b].astype(BF16), contract_last, preferred_element_type=F32) * scale
        sc = jnp.where(mask_c, sc, NEG_INF)
        sn = jnp.where(mask_n, sn, NEG_INF)
        m = jnp.maximum(jnp.maximum(jnp.max(sc, axis=-1, keepdims=True), jnp.max(sn, axis=-1, keepdims=True)), sink)
        ec = jnp.exp(sc - m)
        en = jnp.exp(sn - m)
        denom = jnp.sum(ec, axis=-1, keepdims=True) + jnp.sum(en, axis=-1, keepdims=True) + jnp.exp(sink - m)
        o = (jnp.dot(ec.astype(BF16), vc_ref[b].astype(BF16), preferred_element_type=F32)
             + jnp.dot(en.astype(BF16), vn_ref[b].astype(BF16), preferred_element_type=F32))
        o = jnp.where(own, o, 0.0)
        acc = o[:, :HEAD_DIM]
        for g in range(1, N_KV_HEADS):
            acc = acc + o[:, g * HEAD_DIM:(g + 1) * HEAD_DIM]
        o_ref[b] = (acc / denom).astype(o_ref.dtype)


def _attn_sample(q, k_cache, k_new, v_cache, v_new, sinks, seq_per_step=8):
    bsz, ncache, kvw = k_cache.shape
    qlen = k_new.shape[1]
    n_heads = q.shape[1] // HEAD_DIM
    group = n_heads // N_KV_HEADS
    rows = n_heads * qlen
    qr = q.reshape(bsz, qlen, N_KV_HEADS, group, HEAD_DIM).transpose(0, 2, 3, 1, 4).reshape(bsz, rows, HEAD_DIM)
    sink_rows = jnp.repeat(sinks.astype(F32), qlen).reshape(rows, 1)
    seq = lambda i: (i, 0, 0)
    cache_spec = pl.BlockSpec((seq_per_step, ncache, kvw), seq)
    new_spec = pl.BlockSpec((seq_per_step, qlen, kvw), seq)
    o = pl.pallas_call(
        _attn_sample_kernel,
        grid=(bsz // seq_per_step,),
        in_specs=[
            pl.BlockSpec((rows, 1), lambda i: (0, 0)),
            pl.BlockSpec((seq_per_step, rows, HEAD_DIM), seq),
            cache_spec, new_spec, cache_spec, new_spec,
        ],
        out_specs=pl.BlockSpec((seq_per_step, rows, HEAD_DIM), seq),
        out_shape=jax.ShapeDtypeStruct((bsz, rows, HEAD_DIM), BF16),
        compiler_params=_cparams(("arbitrary",)),
        name="attn_sample",
    )(sink_rows, qr, k_cache, k_new, v_cache, v_new)
    return o.reshape(bsz, N_KV_HEADS, group, qlen, HEAD_DIM).transpose(0, 3, 1, 2, 4).reshape(bsz * qlen, -1)


CONV_HALO = 32
CONV_ROWS = 32
CONV_COLS = 256


def _conv_prompt_kernel(uc_ref, up_ref, w_ref, b_ref, y_ref, sh_ref):
    tile = uc_ref.shape[0]
    n = pl.program_id(1)
    halo = jnp.where(n > 0, up_ref[...], 0.0)
    sh_ref[0, :CONV_HALO, :] = halo
    sh_ref[0, CONV_HALO:, :] = uc_ref[...]
    keep = tile + CONV_HALO - SUBLANES
    for r in range(1, SUBLANES):
        sh_ref[r, :keep, :] = sh_ref[0, r:r + keep, :]
    first = CONV_HALO - (CONV_WIDTH - 1)
    ch = uc_ref.shape[1]

    def rows_step(rc, carry):
        r0 = pl.multiple_of(rc * CONV_ROWS, CONV_ROWS)
        for cc in range(ch // CONV_COLS):
            cs = slice(cc * CONV_COLS, (cc + 1) * CONV_COLS)
            acc = jnp.zeros((CONV_ROWS, CONV_COLS), F32)
            for j in range(CONV_WIDTH):
                off = first + j
                a, r = off // SUBLANES, off % SUBLANES
                acc = acc + sh_ref[r, pl.ds(r0 + a * SUBLANES, CONV_ROWS), cs] * w_ref[j:j + 1, cs]
            y_ref[pl.ds(r0, CONV_ROWS), cs] = acc + b_ref[:, cs]
        return carry

    lax.fori_loop(0, tile // CONV_ROWS, rows_step, 0)


def _conv_prompt(u, w, b, bsz, slen, tile=256):
    ch = u.shape[1]
    nt = slen // tile
    per = tile // CONV_HALO
    return pl.pallas_call(
        _conv_prompt_kernel,
        grid=(bsz, nt),
        in_specs=[
            pl.BlockSpec((tile, ch), lambda bb, n: (bb * nt + n, 0)),
            pl.BlockSpec((CONV_HALO, ch), lambda bb, n: (jnp.maximum((bb * nt + n) * per - 1, 0), 0)),
            pl.BlockSpec((CONV_WIDTH, ch), lambda bb, n: (0, 0)),
            pl.BlockSpec((1, ch), lambda bb, n: (0, 0)),
        ],
        out_specs=pl.BlockSpec((tile, ch), lambda bb, n: (bb * nt + n, 0)),
        out_shape=jax.ShapeDtypeStruct((bsz * slen, ch), F32),
        scratch_shapes=[pltpu.VMEM((SUBLANES, tile + CONV_HALO, ch), F32)],
        compiler_params=_cparams(("arbitrary", "arbitrary")),
        name="conv_prompt",
    )(u, u, w, b.reshape(1, ch))


def _conv_sample_kernel(sc_ref, un_ref, w_ref, b_ref, y_ref):
    nctx = sc_ref.shape[1]
    qlen = un_ref.shape[1]
    ch = w_ref.shape[1]

    def row(t):
        return sc_ref[:, t, :] if t < nctx else un_ref[:, t - nctx, :]

    for i in range(qlen):
        acc = jnp.zeros((sc_ref.shape[0], ch), F32)
        for j in range(CONV_WIDTH):
            acc = acc + row(i + j) * w_ref[j:j + 1, :]
        y_ref[:, i, :] = acc + b_ref[...]


def _conv_sample(state, u_new, w, b, seq_per_step=16):
    bsz, nctx, ch = state.shape
    qlen = u_new.shape[1]
    seq = lambda i: (i, 0, 0)
    y = pl.pallas_call(
        _conv_sample_kernel,
        grid=(bsz // seq_per_step,),
        in_specs=[
            pl.BlockSpec((seq_per_step, nctx, ch), seq),
            pl.BlockSpec((seq_per_step, qlen, ch), seq),
            pl.BlockSpec((CONV_WIDTH, ch), lambda i: (0, 0)),
            pl.BlockSpec((1, ch), lambda i: (0, 0)),
        ],
        out_specs=pl.BlockSpec((seq_per_step, qlen, ch), seq),
        out_shape=jax.ShapeDtypeStruct((bsz, qlen, ch), F32),
        compiler_params=_cparams(("arbitrary",)),
        name="conv_sample",
    )(state, u_new, w, b.reshape(1, ch))
    return y.reshape(bsz * qlen, ch)


def _merge_kernel(y_ref, ao_ref, ga_ref, gc_ref, x_ref, lnw_ref, lnb_ref, wco_ref, wo_ref, fw_ref, wr_ref,
                  h_ref, xn_ref, lg_ref):
    y = y_ref[...]
    mu = jnp.mean(y, axis=-1, keepdims=True)
    yc = y - mu
    var = jnp.mean(yc * yc, axis=-1, keepdims=True)
    yl = yc * lax.rsqrt(var + EPS) * lnw_ref[...] + lnb_ref[...]
    act = yl * jax.nn.sigmoid(yl)
    conv_o = jnp.dot(act.astype(BF16), wco_ref[...], preferred_element_type=F32)
    mix = ga_ref[...].astype(F32) * ao_ref[...].astype(F32) + gc_ref[...].astype(F32) * conv_o
    h = x_ref[...] + jnp.dot(mix.astype(BF16), wo_ref[...], preferred_element_type=F32)
    h_ref[...] = h
    ms = jnp.mean(h * h, axis=-1, keepdims=True)
    xn = (h * lax.rsqrt(ms + EPS) * fw_ref[...]).astype(BF16)
    xn_ref[...] = xn
    lg_ref[...] = jnp.dot(xn, wr_ref[...], preferred_element_type=F32)


def _merge(y, ao, gates, row0, x, ln_w, ln_b, wco, wo, fw, wr, tile=256):
    t, d = x.shape
    ch = y.shape[1]
    off = row0 // tile
    row = lambda i: (i, 0)
    const = lambda shape: pl.BlockSpec(shape, lambda i: (0, 0), pipeline_mode=pl.Buffered(1))
    return pl.pallas_call(
        _merge_kernel,
        grid=(t // tile,),
        in_specs=[
            pl.BlockSpec((tile, ch), row),
            pl.BlockSpec((tile, d), row),
            pl.BlockSpec((tile, d), lambda i: (i + off, 0)),
            pl.BlockSpec((tile, d), lambda i: (i + off, 1)),
            pl.BlockSpec((tile, d), row),
            const((1, ch)), const((1, ch)), const((ch, d)), const((d, d)), const((1, d)),
            const((d, LANES)),
        ],
        out_specs=[pl.BlockSpec((tile, d), row), pl.BlockSpec((tile, d), row), pl.BlockSpec((tile, LANES), row)],
        out_shape=[jax.ShapeDtypeStruct((t, d), F32), jax.ShapeDtypeStruct((t, d), BF16),
                   jax.ShapeDtypeStruct((t, LANES), F32)],
        compiler_params=_cparams(("arbitrary",)),
        name="merge",
    )(y, ao, gates, gates, x, ln_w.reshape(1, ch), ln_b.reshape(1, ch), wco, wo, fw.reshape(1, d), wr)


def _route_kernel(lp_ref, ls_ref, tri_ref, upper_ref, meta_ref, metat_ref, cnt_ref, *, n_p):
    i = pl.program_id(0)

    def body(l_ref):
        lg = l_ref[...]
        shape = lg.shape
        lane = lax.broadcasted_iota(jnp.int32, shape, 1)
        big = jnp.int32(LANES)
        is_g = (lane >= N_EXPERTS) & (lane < N_EXPERTS + N_GROUPS)
        gl = jnp.where(is_g, lg, -jnp.inf)
        gmax = jnp.max(gl, axis=-1, keepdims=True)
        g_star = jnp.min(jnp.where(gl == gmax, lane - N_EXPERTS, big), axis=-1, keepdims=True)
        pg_star = 1.0 / jnp.sum(jnp.exp(gl - gmax), axis=-1, keepdims=True)
        in_g = (lane < N_EXPERTS) & (_div_pow2(lane, EXPERTS_PER_GROUP) == g_star)
        el = jnp.where(in_g, lg, -jnp.inf)
        v1 = jnp.max(el, axis=-1, keepdims=True)
        i1 = jnp.min(jnp.where(el == v1, lane, big), axis=-1, keepdims=True)
        el2 = jnp.where(lane == i1, -jnp.inf, el)
        v2 = jnp.max(el2, axis=-1, keepdims=True)
        i2 = jnp.min(jnp.where(el2 == v2, lane, big), axis=-1, keepdims=True)
        e2 = jnp.exp(v2 - v1)
        w1 = pg_star * (1.0 / (1.0 + e2))
        w2 = pg_star * (e2 / (1.0 + e2))
        oh1 = lane == i1
        oh2 = lane == i2
        cnt = jnp.where(oh1 | oh2, 1.0, 0.0)
        before = jnp.dot(tri_ref[...], cnt.astype(BF16), preferred_element_type=F32)
        tot = jnp.sum(cnt, axis=0, keepdims=True)
        runs = jnp.floor((tot + (CHUNK_ROWS - 1)) * (1.0 / CHUNK_ROWS))
        runs8 = jnp.broadcast_to(runs, (SUBLANES, LANES)).astype(BF16)
        start = jnp.dot(runs8, upper_ref[...], preferred_element_type=F32)[0:1] * CHUNK_ROWS
        slot = before + start
        s1 = jnp.sum(jnp.where(oh1, slot, 0.0), axis=-1, keepdims=True)
        s2 = jnp.sum(jnp.where(oh2, slot, 0.0), axis=-1, keepdims=True)
        cols = [i1.astype(F32), i2.astype(F32), w1, w2, s1, s2]
        meta = jnp.zeros(shape, F32)
        for c, val in enumerate(cols):
            meta = jnp.where(lane == c, val, meta)
        meta_ref[...] = meta
        metat_ref[...] = meta.T[:META_FIELDS]
        cnt_ref[...] = tot

    @pl.when(i < n_p)
    def _():
        body(lp_ref)

    @pl.when(i >= n_p)
    def _():
        body(ls_ref)


def _route(lg_p, lg_s):
    tile = TOKEN_BLOCK
    tp, ts = lg_p.shape[0], lg_s.shape[0]
    n_p, n_s = tp // tile, ts // tile
    nb = n_p + n_s
    tri = (jnp.arange(tile)[:, None] > jnp.arange(tile)[None, :]).astype(BF16)
    upper = (jnp.arange(LANES)[:, None] < jnp.arange(LANES)[None, :]).astype(BF16)
    return pl.pallas_call(
        functools.partial(_route_kernel, n_p=n_p),
        grid=(nb,),
        in_specs=[
            pl.BlockSpec((tile, LANES), lambda i: (jnp.minimum(i, n_p - 1), 0)),
            pl.BlockSpec((tile, LANES), lambda i: (jnp.maximum(i - n_p, 0), 0)),
            pl.BlockSpec((tile, tile), lambda i: (0, 0)),
            pl.BlockSpec((LANES, LANES), lambda i: (0, 0)),
        ],
        out_specs=[pl.BlockSpec((tile, LANES), lambda i: (i, 0)),
                   pl.BlockSpec((None, META_FIELDS, tile), lambda i: (i, 0, 0)),
                   pl.BlockSpec((None, 1, LANES), lambda i: (i, 0, 0))],
        out_shape=[jax.ShapeDtypeStruct((tp + ts, LANES), F32),
                   jax.ShapeDtypeStruct((nb, META_FIELDS, tile), F32),
                   jax.ShapeDtypeStruct((nb, 1, LANES), F32)],
        compiler_params=_cparams(("arbitrary",)),
        name="route",
    )(lg_p, lg_s, tri, upper)


def _plan_sizes(n_tok):
    nb = n_tok // TOKEN_BLOCK
    pad_per_block = N_EXPERTS * (CHUNK_ROWS - 1)
    max_chunks = _round_up(-(-(TOKEN_BLOCK * TOP_K + pad_per_block) // CHUNK_ROWS), SUBLANES)
    n_rows = _round_up(n_tok * TOP_K + nb * pad_per_block + N_EXPERTS * (ROW_PAD - 1), ROW_PAD)
    n_items = N_EXPERTS + n_rows // EXPERT_ROWS + 1
    max_zero = N_EXPERTS * (ROW_PAD // CHUNK_ROWS)
    return nb, max_chunks, n_rows, n_items, max_zero


def _dispatch_plan(counts, max_chunks, n_items, max_zero):
    i32 = jnp.int32
    c = counts[:, 0, :N_EXPERTS].astype(i32)
    run = (c + CHUNK_ROWS - 1) // CHUNK_ROWS
    run_end = jnp.cumsum(run, axis=1)
    src_start = run_end - run
    nchunk = run_end[:, -1]
    seg = jnp.sum(run, axis=0) * CHUNK_ROWS
    padded = (seg + ROW_PAD - 1) // ROW_PAD * ROW_PAD
    pad_end = jnp.cumsum(padded)
    pad_start = pad_end - padded
    dst_start = pad_start[None, :] // CHUNK_ROWS + (jnp.cumsum(run, axis=0) - run)
    jj = jnp.arange(max_chunks, dtype=i32)[None, :, None]
    in_run = (src_start[:, None, :] <= jj) & (jj < run_end[:, None, :])
    chunk_dst = jnp.sum(jnp.where(in_run, (dst_start - src_start)[:, None, :] + jj, 0), axis=-1) * CHUNK_ROWS
    tail = (padded - seg) // CHUNK_ROWS
    tail_end = jnp.cumsum(tail)
    tail_start = tail_end - tail
    z = jnp.arange(max_zero, dtype=i32)[:, None]
    in_tail = (tail_start[None, :] <= z) & (z < tail_end[None, :])
    zero_base = (pad_start + seg) // CHUNK_ROWS - tail_start
    zero_dst = jnp.sum(jnp.where(in_tail, zero_base[None, :] + z, 0), axis=-1) * CHUNK_ROWS
    nzero = tail_end[-1]
    items_e = (padded + EXPERT_ROWS - 1) // EXPERT_ROWS
    item_end = jnp.cumsum(items_e)
    item_start = item_end - items_e
    w = jnp.arange(n_items, dtype=i32)[:, None]
    in_item = (item_start[None, :] <= w) & (w < item_end[None, :])
    k = w - item_start[None, :]
    e_ids = jnp.arange(N_EXPERTS, dtype=i32)
    last_e = jnp.max(jnp.where(items_e > 0, e_ids, 0))
    item_e = jnp.where(w[:, 0] < item_end[-1], jnp.sum(jnp.where(in_item, e_ids[None, :], 0), axis=-1), last_e)
    item_row0 = jnp.sum(jnp.where(in_item, pad_start[None, :] + k * EXPERT_ROWS, 0), axis=-1)
    item_rows = jnp.sum(jnp.where(in_item, jnp.clip(padded[None, :] - k * EXPERT_ROWS, 0, EXPERT_ROWS), 0), axis=-1)
    item_nsub = item_rows // ROW_PAD
    return (chunk_dst.reshape(-1).astype(i32), nchunk.astype(i32), zero_dst.astype(i32),
            nzero.reshape(1).astype(i32), item_e.astype(i32), item_row0.astype(i32), item_nsub.astype(i32))


def _scatter_kernel(cd_ref, nc_ref, zd_ref, nz_ref, mt_ref, xp_ref, xs_ref, out_hbm, loc_ref, zero_ref, sem,
                    zero_sem, *, n_p, max_chunks):
    i = pl.program_id(0)
    buf = i % 2
    n_slots = loc_ref.shape[1]
    d = xp_ref.shape[1]
    mt = mt_ref[...]
    w1, w2 = mt[2:3, :], mt[3:4, :]
    s1, s2 = mt[4:5, :].astype(jnp.int32), mt[5:6, :].astype(jnp.int32)
    slot = lax.broadcasted_iota(jnp.int32, (n_slots, mt.shape[1]), 0)
    p1 = slot == s1
    p2 = slot == s2
    perm = jnp.where(p1 | p2, 1.0, 0.0).astype(BF16)
    wcol = jnp.sum(jnp.where(p1, w1, 0.0) + jnp.where(p2, w2, 0.0), axis=-1, keepdims=True)
    loc_ref[buf, :, d:] = jnp.broadcast_to(wcol, (n_slots, LANES))

    @pl.when(i < n_p)
    def _():
        loc_ref[buf, :, :d] = jnp.dot(perm, xp_ref[...], preferred_element_type=F32)

    @pl.when(i >= n_p)
    def _():
        loc_ref[buf, :, :d] = jnp.dot(perm, xs_ref[...], preferred_element_type=F32)

    def rows_at(r):
        return pl.ds(pl.multiple_of(r, CHUNK_ROWS), CHUNK_ROWS)

    def chunk_copy(blk, j):
        return pltpu.make_async_copy(loc_ref.at[blk % 2, rows_at(j * CHUNK_ROWS)],
                                     out_hbm.at[rows_at(cd_ref[blk * max_chunks + j])], sem.at[blk % 2])

    def zero_copy(z):
        return pltpu.make_async_copy(zero_ref, out_hbm.at[rows_at(zd_ref[z])], zero_sem)

    def loop(n, fn):
        def body(j, carry):
            fn(j)
            return carry
        lax.fori_loop(0, n, body, 0)

    @pl.when(i > 0)
    def _():
        loop(nc_ref[i - 1], lambda j: chunk_copy(i - 1, j).wait())

    loop(nc_ref[i], lambda j: chunk_copy(i, j).start())

    @pl.when(i == pl.num_programs(0) - 1)
    def _():
        zero_ref[...] = jnp.zeros_like(zero_ref)
        loop(nz_ref[0], lambda z: zero_copy(z).start())
        loop(nc_ref[i], lambda j: chunk_copy(i, j).wait())
        loop(nz_ref[0], lambda z: zero_copy(z).wait())


def _scatter_rows(xn_p, xn_s, metat, chunk_dst, nchunk, zero_dst, nzero, max_chunks, n_rows):
    tile = TOKEN_BLOCK
    tp, d = xn_p.shape
    ts = xn_s.shape[0]
    n_p, n_s = tp // tile, ts // tile
    n_slots = max_chunks * CHUNK_ROWS
    width = d + LANES
    return pl.pallas_call(
        functools.partial(_scatter_kernel, n_p=n_p, max_chunks=max_chunks),
        grid_spec=pltpu.PrefetchScalarGridSpec(
            num_scalar_prefetch=4,
            grid=(n_p + n_s,),
            in_specs=[
                pl.BlockSpec((None, META_FIELDS, tile), lambda i, *_: (i, 0, 0)),
                pl.BlockSpec((tile, d), lambda i, *_: (jnp.minimum(i, n_p - 1), 0)),
                pl.BlockSpec((tile, d), lambda i, *_: (jnp.maximum(i - n_p, 0), 0)),
            ],
            out_specs=pl.BlockSpec(memory_space=pl.ANY),
            scratch_shapes=[pltpu.VMEM((2, n_slots, width), F32), pltpu.VMEM((CHUNK_ROWS, width), F32),
                            pltpu.SemaphoreType.DMA((2,)), pltpu.SemaphoreType.DMA(())],
        ),
        out_shape=jax.ShapeDtypeStruct((n_rows, width), F32),
        compiler_params=_cparams(("arbitrary",)),
        name="scatter_rows",
    )(chunk_dst, nchunk, zero_dst, nzero, metat, xn_p, xn_s)


def _expert_kernel(e_ref, row0_ref, nsub_ref, xs_hbm, wg_ref, wu_ref, wd_ref, out_hbm,
                   xf_ref, acc_ref, ob_ref, pend_ref, in_sem, out_sem):
    w = pl.program_id(0)
    c = pl.program_id(1)
    n_items = pl.num_programs(0)
    last_c = pl.num_programs(1) - 1
    nsub = nsub_ref[w]
    d = acc_ref.shape[1]
    slot = w % 2

    def sub_rows(s):
        return pl.ds(pl.multiple_of(s * ROW_PAD, ROW_PAD), ROW_PAD)

    def hbm_rows(item, s):
        return pl.ds(pl.multiple_of(row0_ref[item] + s * ROW_PAD, ROW_PAD), ROW_PAD)

    def in_copy(item, s):
        return pltpu.make_async_copy(xs_hbm.at[hbm_rows(item, s)], xf_ref.at[item % 2, sub_rows(s)],
                                     in_sem.at[item % 2])

    def out_copy(item, s):
        return pltpu.make_async_copy(ob_ref.at[sub_rows(s)], out_hbm.at[hbm_rows(item, s)], out_sem)

    def loop(n, fn):
        def body(s, carry):
            fn(s)
            return carry
        lax.fori_loop(0, n, body, 0)

    @pl.when(c == 0)
    def _():
        @pl.when(w == 0)
        def _():
            pend_ref[0] = 0
            pend_ref[1] = 0
            loop(nsub, lambda s: in_copy(w, s).start())

        loop(nsub, lambda s: in_copy(w, s).wait())
        nxt = jnp.minimum(w + 1, n_items - 1)

        @pl.when(w + 1 < n_items)
        def _():
            loop(nsub_ref[nxt], lambda s: in_copy(nxt, s).start())

    def drain_out():
        prev = pend_ref[1]
        loop(pend_ref[0], lambda s: out_copy(prev, s).wait())
        pend_ref[0] = 0

    for n in range(1, EXPERT_ROWS // ROW_PAD + 1):
        m = n * ROW_PAD

        @pl.when(nsub == n)
        def _(m=m):
            @pl.when(c == 0)
            def _():
                acc_ref[:m, :] = jnp.zeros((m, d), F32)

            x = xf_ref[slot, :m, :d].astype(BF16)
            g = jnp.dot(x, wg_ref[...].astype(BF16), preferred_element_type=F32)
            u = jnp.dot(x, wu_ref[...].astype(BF16), preferred_element_type=F32)
            hmid = (g * jax.nn.sigmoid(g) * u).astype(BF16)
            acc_ref[:m, :] += jnp.dot(hmid, wd_ref[...].astype(BF16), preferred_element_type=F32)

            @pl.when(c == last_c)
            def _():
                drain_out()
                ob_ref[:m, :] = acc_ref[:m, :] * xf_ref[slot, :m, d:d + 1]
                loop(nsub, lambda s: out_copy(w, s).start())
                pend_ref[0] = nsub
                pend_ref[1] = w

    @pl.when((w == n_items - 1) & (c == last_c))
    def _():
        drain_out()


def _experts(xs, item_e, item_row0, item_nsub, w_gate, w_up, w_down):
    n_rows, width = xs.shape
    d = width - LANES
    d_exp = w_gate.shape[2]
    nc = d_exp // D_CHUNK
    n_items = item_e.shape[0]

    def chunk_of(w, c, nsub_ref):
        return jnp.where(nsub_ref[w] > 0, c, nc - 1)

    return pl.pallas_call(
        _expert_kernel,
        grid_spec=pltpu.PrefetchScalarGridSpec(
            num_scalar_prefetch=3,
            grid=(n_items, nc),
            in_specs=[
                pl.BlockSpec(memory_space=pl.ANY),
                pl.BlockSpec((None, d, D_CHUNK), lambda w, c, e, r, n: (e[w], 0, chunk_of(w, c, n))),
                pl.BlockSpec((None, d, D_CHUNK), lambda w, c, e, r, n: (e[w], 0, chunk_of(w, c, n))),
                pl.BlockSpec((None, D_CHUNK, d), lambda w, c, e, r, n: (e[w], chunk_of(w, c, n), 0)),
            ],
            out_specs=pl.BlockSpec(memory_space=pl.ANY),
            scratch_shapes=[
                pltpu.VMEM((2, EXPERT_ROWS, width), F32),
                pltpu.VMEM((EXPERT_ROWS, d), F32),
                pltpu.VMEM((EXPERT_ROWS, d), F32),
                pltpu.SMEM((2,), jnp.int32),
                pltpu.SemaphoreType.DMA((2,)),
                pltpu.SemaphoreType.DMA(()),
            ],
        ),
        out_shape=jax.ShapeDtypeStruct((n_rows, d), F32),
        compiler_params=_cparams(("arbitrary", "arbitrary")),
        name="experts",
    )(item_e, item_row0, item_nsub, xs, w_gate, w_up, w_down)


COMBINE_SPLIT = 2
SLOT_STEP = 256


def _combine_kernel(cd_ref, nc_ref, meta_ref, hp_ref, hs_ref, yb_hbm, yp_ref, ys_ref, loc_ref, sem,
                    *, n_p, max_chunks):
    i = pl.program_id(0)
    j = pl.program_id(1)
    nb = pl.num_programs(0)
    buf = i % 2
    n_slots, d = loc_ref.shape[1], loc_ref.shape[2]

    def rows_at(r):
        return pl.ds(pl.multiple_of(r, CHUNK_ROWS), CHUNK_ROWS)

    def chunk_copy(blk, c):
        return pltpu.make_async_copy(yb_hbm.at[rows_at(cd_ref[blk * max_chunks + c])],
                                     loc_ref.at[blk % 2, rows_at(c * CHUNK_ROWS)], sem.at[blk % 2])

    def fetch(blk):
        n = nc_ref[blk]

        def start(c, carry):
            chunk_copy(blk, c).start()
            return carry
        lax.fori_loop(0, n, start, 0)

        def clear(c, carry):
            loc_ref[blk % 2, rows_at(c * CHUNK_ROWS), :] = jnp.zeros((CHUNK_ROWS, d), F32)
            return carry
        lax.fori_loop(n, max_chunks, clear, 0)

    @pl.when(j == 0)
    def _():
        @pl.when(i == 0)
        def _():
            fetch(i)

        def wait(c, carry):
            chunk_copy(i, c).wait()
            return carry
        lax.fori_loop(0, nc_ref[i], wait, 0)

        @pl.when(i + 1 < nb)
        def _():
            fetch(jnp.minimum(i + 1, nb - 1))

    meta = meta_ref[...]
    s1 = meta[:, 4:5].astype(jnp.int32)
    s2 = meta[:, 5:6].astype(jnp.int32)
    rows = meta.shape[0]
    f = jnp.zeros((rows, d), F32)
    for r0 in range(0, n_slots, SLOT_STEP):
        slot = lax.broadcasted_iota(jnp.int32, (rows, SLOT_STEP), 1) + r0
        pt = jnp.where((slot == s1) | (slot == s2), 1.0, 0.0).astype(BF16)
        f = f + jnp.dot(pt, loc_ref[buf, r0:r0 + SLOT_STEP, :].astype(BF16), preferred_element_type=F32)
    g = i * pl.num_programs(1) + j

    @pl.when(g < n_p)
    def _():
        yp_ref[...] = hp_ref[...] + f

    @pl.when(g >= n_p)
    def _():
        ys_ref[...] = hs_ref[...] + f


def _combine(h_p, h_s, meta, chunk_dst, nchunk, yb, max_chunks):
    tp, d = h_p.shape
    ts = h_s.shape[0]
    tile = TOKEN_BLOCK // COMBINE_SPLIT
    n_p, n_s = tp // tile, ts // tile
    nb = (tp + ts) // TOKEN_BLOCK
    n_slots = max_chunks * CHUNK_ROWS
    assert n_slots % SLOT_STEP == 0
    blk = lambda i, j: i * COMBINE_SPLIT + j
    p_idx = lambda i, j, *_: (jnp.minimum(blk(i, j), n_p - 1), 0)
    s_idx = lambda i, j, *_: (jnp.maximum(blk(i, j) - n_p, 0), 0)
    return pl.pallas_call(
        functools.partial(_combine_kernel, n_p=n_p, max_chunks=max_chunks),
        grid_spec=pltpu.PrefetchScalarGridSpec(
            num_scalar_prefetch=2,
            grid=(nb, COMBINE_SPLIT),
            in_specs=[
                pl.BlockSpec((tile, LANES), lambda i, j, *_: (blk(i, j), 0)),
                pl.BlockSpec((tile, d), p_idx),
                pl.BlockSpec((tile, d), s_idx),
                pl.BlockSpec(memory_space=pl.ANY),
            ],
            out_specs=[pl.BlockSpec((tile, d), p_idx), pl.BlockSpec((tile, d), s_idx)],
            scratch_shapes=[pltpu.VMEM((2, n_slots, d), F32), pltpu.SemaphoreType.DMA((2,))],
        ),
        out_shape=[jax.ShapeDtypeStruct((tp, d), F32), jax.ShapeDtypeStruct((ts, d), F32)],
        compiler_params=_cparams(("arbitrary", "arbitrary")),
        name="combine",
    )(chunk_dst, nchunk, meta, h_p, h_s, yb)


def _layer(x_p, x_s, ck, cv, sc, attn_norm_w, w_in, b_gate, q_norm_w, k_norm_w, sinks, conv_dw_w, conv_dw_b,
           conv_ln_w, conv_ln_b, w_conv_out, w_out, ffn_norm_w, router_group, router_expert, w_gate, w_up,
           w_down):
    bsz, slen, d = x_p.shape
    dbs, dlen, _ = x_s.shape
    tp, ts = bsz * slen, dbs * dlen
    conv_ch = conv_dw_w.shape[1]
    kvw = N_KV_HEADS * HEAD_DIM
    xp2, xs2 = x_p.reshape(tp, d), x_s.reshape(ts, d)

    xn = _prenorm(xp2, xs2, attn_norm_w)
    q, k, v, u, gates = _project(xn, w_in, b_gate, q_norm_w, k_norm_w, d, conv_ch)

    ao_p = _attn_prompt(q, k, v, sinks, bsz, slen)
    k_s = k[tp:].reshape(dbs, dlen, kvw)
    v_s = v[tp:].reshape(dbs, dlen, kvw)
    ck2 = ck.reshape(dbs, WINDOW, kvw)
    cv2 = cv.reshape(dbs, WINDOW, kvw)
    ao_s = _attn_sample(q[tp:], ck2, k_s, cv2, v_s, sinks)

    y_p = _conv_prompt(u, conv_dw_w, conv_dw_b, bsz, slen)
    u_s = u[tp:].reshape(dbs, dlen, conv_ch)
    y_s = _conv_sample(sc, u_s, conv_dw_w, conv_dw_b)

    wco = w_conv_out.astype(BF16)
    wo = w_out.astype(BF16)
    wr = jnp.concatenate([router_expert, router_group,
                          jnp.zeros((d, LANES - N_EXPERTS - N_GROUPS), router_expert.dtype)], axis=1).astype(BF16)
    margs = (conv_ln_w, conv_ln_b, wco, wo, ffn_norm_w, wr)
    h_p, xn_p, lg_p = _merge(y_p, ao_p, gates, 0, xp2, *margs)
    h_s, xn_s, lg_s = _merge(y_s, ao_s, gates, tp, xs2, *margs)

    meta, metat, counts = _route(lg_p, lg_s)
    _, max_chunks, n_rows, n_items, max_zero = _plan_sizes(tp + ts)
    chunk_dst, nchunk, zero_dst, nzero, item_e, item_row0, item_nsub = _dispatch_plan(
        counts, max_chunks, n_items, max_zero)
    xs_rows = _scatter_rows(xn_p, xn_s, metat, chunk_dst, nchunk, zero_dst, nzero, max_chunks, n_rows)
    yb = _experts(xs_rows, item_e, item_row0, item_nsub, w_gate, w_up, w_down)
    out_p, out_s = _combine(h_p, h_s, meta, chunk_dst, nchunk, yb, max_chunks)

    def seq_tails(a, n):
        return jnp.stack([lax.slice_in_dim(a, (b + 1) * slen - n, (b + 1) * slen, axis=0) for b in range(bsz)])

    def shifted(old, new):
        return jnp.concatenate([old[:, new.shape[1]:], new], axis=1)

    new_p = (seq_tails(k, WINDOW).reshape(bsz, WINDOW, N_KV_HEADS, HEAD_DIM),
             seq_tails(v, WINDOW).reshape(bsz, WINDOW, N_KV_HEADS, HEAD_DIM),
             seq_tails(u, CONV_WIDTH - 1))
    new_s = (shifted(ck, k_s.reshape(dbs, dlen, N_KV_HEADS, HEAD_DIM)),
             shifted(cv, v_s.reshape(dbs, dlen, N_KV_HEADS, HEAD_DIM)),
             shifted(sc, u_s))
    return out_p.reshape(bsz, slen, d), out_s.reshape(dbs, dlen, d), new_p, new_s


def kernel(x_prompt, x_sample, cache_k, cache_v, state_conv, attn_norm_w, w_in, b_gate, q_norm_w, k_norm_w,
           sinks, conv_dw_w, conv_dw_b, conv_ln_w, conv_ln_b, w_conv_out, w_out, ffn_norm_w, router_group,
           router_expert, w_gate, w_up, w_down):
    x_p, x_s = x_prompt, x_sample
    kp, vp, cp, ks_, vs_, cs_ = [], [], [], [], [], []
    for l in range(cache_k.shape[0]):
        x_p, x_s, new_p, new_s = _layer(
            x_p, x_s, cache_k[l], cache_v[l], state_conv[l], attn_norm_w[l], w_in[l], b_gate[l], q_norm_w[l],
            k_norm_w[l], sinks[l], conv_dw_w[l], conv_dw_b[l], conv_ln_w[l], conv_ln_b[l], w_conv_out[l],
            w_out[l], ffn_norm_w[l], router_group[l], router_expert[l], w_gate[l], w_up[l], w_down[l])
        kp.append(new_p[0]); vp.append(new_p[1]); cp.append(new_p[2])
        ks_.append(new_s[0]); vs_.append(new_s[1]); cs_.append(new_s[2])
    return (x_p, x_s, jnp.stack(kp), jnp.stack(vp), jnp.stack(cp),
            jnp.stack(ks_), jnp.stack(vs_), jnp.stack(cs_))
```

```python
import functools

import jax
import jax.numpy as jnp
from jax import lax
from jax.experimental import pallas as pl
from jax.experimental.pallas import tpu as pltpu

F32 = jnp.float32
BF16 = jnp.bfloat16
EPS = 1e-6
NEG_INF = -1e30

HEAD_DIM = 64
N_KV_HEADS = 4
WINDOW = 128
CONV_WIDTH = 31
N_GROUPS = 4
EXPERTS_PER_GROUP = 8
N_EXPERTS = N_GROUPS * EXPERTS_PER_GROUP
TOP_K = 2

LANES = 128
SUBLANES = 8
MXU_DIM = 256
VMEM_LIMIT = 56 * 1024 * 1024

TOKEN_BLOCK = 512
CHUNK_ROWS = SUBLANES
ROW_PAD = 128
EXPERT_ROWS = 768
D_CHUNK = 256
META_FIELDS = 8


def _cparams(sem):
    return pltpu.CompilerParams(dimension_semantics=sem, vmem_limit_bytes=VMEM_LIMIT)


def _log2(n):
    assert n > 0 and n & (n - 1) == 0, n
    return n.bit_length() - 1


def _div_pow2(x, n):
    return lax.shift_right_logical(x, jnp.int32(_log2(n)))


def _mod_pow2(x, n):
    _log2(n)
    return x & jnp.int32(n - 1)


def _round_up(x, m):
    return (x + m - 1) // m * m


HIGH_HALF = 0xFFFF0000


def _pack_pairs(a, b):
    ua = pltpu.bitcast(a, jnp.uint32) & jnp.uint32(HIGH_HALF)
    return ua | (pltpu.bitcast(b, jnp.uint32) >> 16)


def _unpack_pairs(w):
    return (pltpu.bitcast(w & jnp.uint32(HIGH_HALF), F32), pltpu.bitcast(w << 16, F32))


def _prenorm_kernel(xp_ref, xs_ref, w_ref, o_ref, *, n_p):
    i = pl.program_id(0)

    def body(x_ref):
        x = x_ref[...]
        ms = jnp.mean(x * x, axis=-1, keepdims=True)
        o_ref[...] = (x * lax.rsqrt(ms + EPS) * w_ref[...]).astype(o_ref.dtype)

    @pl.when(i < n_p)
    def _():
        body(xp_ref)

    @pl.when(i >= n_p)
    def _():
        body(xs_ref)


def _prenorm(xp, xs, w, tile=512):
    tp, d = xp.shape
    ts = xs.shape[0]
    n_p, n_s = tp // tile, ts // tile
    return pl.pallas_call(
        functools.partial(_prenorm_kernel, n_p=n_p),
        grid=(n_p + n_s,),
        in_specs=[
            pl.BlockSpec((tile, d), lambda i: (jnp.minimum(i, n_p - 1), 0)),
            pl.BlockSpec((tile, d), lambda i: (jnp.maximum(i - n_p, 0), 0)),
            pl.BlockSpec((1, d), lambda i: (0, 0)),
        ],
        out_specs=pl.BlockSpec((tile, d), lambda i: (i, 0)),
        out_shape=jax.ShapeDtypeStruct((tp + ts, d), BF16),
        compiler_params=_cparams(("arbitrary",)),
        name="prenorm",
    )(xp, xs, w.reshape(1, d))


def _head_rms(z, hw, bd_ref):
    ss = z * z
    hi = ss.astype(BF16)
    lo = (ss - hi.astype(F32)).astype(BF16)
    bd = bd_ref[...]
    tot = jnp.dot(hi, bd, preferred_element_type=F32) + jnp.dot(lo, bd, preferred_element_type=F32)
    return z * lax.rsqrt(tot * (1.0 / HEAD_DIM) + EPS) * hw


def _load_w(i, w_ref, wb_ref):
    @pl.when(i == 0)
    def _():
        wb_ref[...] = w_ref[...].astype(BF16)


def _proj_q_kernel(x_ref, w_ref, hw_ref, bd_ref, o_ref, wb_ref):
    _load_w(pl.program_id(1), w_ref, wb_ref)
    z = jnp.dot(x_ref[...], wb_ref[...], preferred_element_type=F32)
    for c in range(z.shape[1] // MXU_DIM):
        sl = slice(c * MXU_DIM, (c + 1) * MXU_DIM)
        o_ref[:, sl] = _head_rms(z[:, sl], hw_ref[...], bd_ref).astype(o_ref.dtype)


def _proj_kv_kernel(x_ref, w_ref, hw_ref, bd_ref, k_ref, v_ref, wb_ref):
    _load_w(pl.program_id(1), w_ref, wb_ref)
    z = jnp.dot(x_ref[...], wb_ref[...], preferred_element_type=F32)
    kw = k_ref.shape[1]
    k_ref[...] = _head_rms(z[:, :kw], hw_ref[...], bd_ref)
    v_ref[...] = z[:, kw:]


def _proj_glu_kernel(x_ref, wa_ref, wg_ref, o_ref, wab_ref, wgb_ref):
    i = pl.program_id(1)
    _load_w(i, wa_ref, wab_ref)
    _load_w(i, wg_ref, wgb_ref)
    x = x_ref[...]
    a = jnp.dot(x, wab_ref[...], preferred_element_type=F32)
    g = jnp.dot(x, wgb_ref[...], preferred_element_type=F32)
    o_ref[...] = a * jax.nn.sigmoid(g)


def _proj_gate_kernel(x_ref, w_ref, b_ref, o_ref, wb_ref):
    _load_w(pl.program_id(1), w_ref, wb_ref)
    z = jnp.dot(x_ref[...], wb_ref[...], preferred_element_type=F32)
    o_ref[...] = jax.nn.sigmoid(z + b_ref[...]).astype(o_ref.dtype)


def _row_tile(t, cap=1152):
    best = 16
    for m in range(16, cap + 1, 16):
        if t % m == 0:
            best = m
    return best


def _project(xn, w_in, b_gate, q_norm_w, k_norm_w, d_model, conv_ch):
    t, d = xn.shape
    q_w = d_model
    kv_w = N_KV_HEADS * HEAD_DIM
    tn = 2 * kv_w
    tm = _row_tile(t)
    ni = t // tm
    tm_big = _row_tile(t, cap=2304)
    ni_big = t // tm_big
    reps = MXU_DIM // HEAD_DIM
    hq = jnp.tile(q_norm_w.astype(F32) * HEAD_DIM ** -0.5, reps).reshape(1, MXU_DIM)
    hk = jnp.tile(k_norm_w.astype(F32), reps).reshape(1, MXU_DIM)
    gid = jnp.arange(MXU_DIM) // HEAD_DIM
    bd = (gid[:, None] == gid[None, :]).astype(BF16)

    x_spec = pl.BlockSpec((tm, d), lambda s, i: (i, 0))
    x_big = pl.BlockSpec((tm_big, d), lambda s, i: (i, 0))
    small = lambda shape: pl.BlockSpec(shape, lambda s, i: (0, 0))

    def w_spec(c0):
        return pl.BlockSpec((d, tn), lambda s, i: (0, c0 + s))

    wscr = pltpu.VMEM((d, tn), BF16)
    cp = _cparams(("arbitrary", "arbitrary"))

    q = pl.pallas_call(
        _proj_q_kernel, grid=(q_w // tn, ni_big),
        in_specs=[x_big, w_spec(0), small((1, MXU_DIM)), small((MXU_DIM, MXU_DIM))],
        out_specs=pl.BlockSpec((tm_big, tn), lambda s, i: (i, s)),
        out_shape=jax.ShapeDtypeStruct((t, q_w), BF16),
        scratch_shapes=[wscr], compiler_params=cp, name="proj_q",
    )(xn, w_in, hq, bd)

    c_kv = q_w // tn
    k, v = pl.pallas_call(
        _proj_kv_kernel, grid=(1, ni),
        in_specs=[x_spec, w_spec(c_kv), small((1, MXU_DIM)), small((MXU_DIM, MXU_DIM))],
        out_specs=[pl.BlockSpec((tm, kv_w), lambda s, i: (i, 0))] * 2,
        out_shape=[jax.ShapeDtypeStruct((t, kv_w), F32)] * 2,
        scratch_shapes=[wscr], compiler_params=cp, name="proj_kv",
    )(xn, w_in, hk, bd)

    c_a = c_kv + 1
    n_glu = conv_ch // tn
    u = pl.pallas_call(
        _proj_glu_kernel, grid=(n_glu, ni),
        in_specs=[x_spec, w_spec(c_a), w_spec(c_a + n_glu)],
        out_specs=pl.BlockSpec((tm, tn), lambda s, i: (i, s)),
        out_shape=jax.ShapeDtypeStruct((t, conv_ch), F32),
        scratch_shapes=[wscr, wscr], compiler_params=cp, name="proj_glu",
    )(xn, w_in, w_in)

    c_g = c_a + 2 * n_glu
    n_gate = 2 * d_model // tn
    gates = pl.pallas_call(
        _proj_gate_kernel, grid=(n_gate, ni_big),
        in_specs=[x_big, w_spec(c_g), pl.BlockSpec((1, tn), lambda s, i: (0, s))],
        out_specs=pl.BlockSpec((tm_big, tn), lambda s, i: (i, s)),
        out_shape=jax.ShapeDtypeStruct((t, 2 * d_model), BF16),
        scratch_shapes=[wscr], compiler_params=cp, name="proj_gate",
    )(xn, w_in, b_gate.reshape(1, -1))
    return q, k, v, u, gates


def _attn_prompt_kernel(sink_ref, q_ref, kc_ref, kp_ref, vc_ref, vp_ref, o_ref):
    blk = q_ref.shape[0]
    n = pl.program_id(1)
    rows = 2 * blk
    r_iota = lax.broadcasted_iota(jnp.int32, (rows, rows), 0)
    c_iota = lax.broadcasted_iota(jnp.int32, (rows, rows), 1)
    qi = jnp.where(r_iota >= blk, r_iota - blk, r_iota)
    mask = (c_iota >= qi) & (c_iota <= qi + WINDOW) & ((c_iota >= blk) | (n > 0))
    lo_lane = lax.broadcasted_iota(jnp.int32, (blk, LANES), 1) < HEAD_DIM
    top_row = lax.broadcasted_iota(jnp.int32, (rows, 1), 0) < blk
    zero = jnp.zeros((blk, LANES), q_ref.dtype)
    group = q_ref.shape[1] // (N_KV_HEADS * LANES)
    for g in range(N_KV_HEADS):
        hs = slice(g * HEAD_DIM, (g + 1) * HEAD_DIM)
        kg = jnp.concatenate([kp_ref[:, hs], kc_ref[:, hs]], axis=0)
        vg = jnp.concatenate([vp_ref[:, hs], vc_ref[:, hs]], axis=0)
        kdup = jnp.concatenate([kg, kg], axis=1).astype(BF16)
        vdup = jnp.concatenate([vg, vg], axis=1).astype(BF16)
        for p in range(group):
            pair = g * group + p
            ls = slice(pair * LANES, (pair + 1) * LANES)
            qp = q_ref[:, ls]
            lhs = jnp.concatenate([jnp.where(lo_lane, qp, zero), jnp.where(lo_lane, zero, qp)], axis=0)
            s = lax.dot_general(lhs, kdup, (((1,), (1,)), ((), ())), preferred_element_type=F32)
            s = jnp.where(mask, s, NEG_INF)
            sink = jnp.where(top_row, sink_ref[2 * pair], sink_ref[2 * pair + 1])
            m = jnp.maximum(jnp.max(s, axis=-1, keepdims=True), sink)
            e = jnp.exp(s - m)
            denom = jnp.sum(e, axis=-1, keepdims=True) + jnp.exp(sink - m)
            o2 = jnp.dot(e.astype(BF16), vdup, preferred_element_type=F32) / denom
            o_ref[:, ls] = jnp.where(lo_lane, o2[:blk], o2[blk:]).astype(o_ref.dtype)


def _attn_prompt(q, k, v, sinks, bsz, slen):
    blk = WINDOW
    nb = slen // blk
    qw, kw = q.shape[1], k.shape[1]
    cur = lambda b, n: (b * nb + n, 0)
    prev = lambda b, n: (b * nb + jnp.maximum(n - 1, 0), 0)
    return pl.pallas_call(
        _attn_prompt_kernel,
        grid=(bsz, nb),
        in_specs=[
            pl.BlockSpec(memory_space=pltpu.SMEM),
            pl.BlockSpec((blk, qw), cur),
            pl.BlockSpec((blk, kw), cur), pl.BlockSpec((blk, kw), prev),
            pl.BlockSpec((blk, kw), cur), pl.BlockSpec((blk, kw), prev),
        ],
        out_specs=pl.BlockSpec((blk, qw), cur),
        out_shape=jax.ShapeDtypeStruct((bsz * slen, qw), BF16),
        compiler_params=_cparams(("arbitrary", "arbitrary")),
        name="attn_prompt",
    )(sinks.astype(F32), q, k, k, v, v)


def _split3(x):
    hi = x.astype(BF16)
    r1 = x - hi.astype(F32)
    mid = r1.astype(BF16)
    lo = (r1 - mid.astype(F32)).astype(BF16)
    return hi, mid, lo


def _attn_sample_kernel(sink_ref, sel_ref, q_ref, kt_ref, knt_ref, vt_ref, vn_ref, vnt_ref,
                        o_ref, kt_out_ref, vt_out_ref):
    nseq, rows, _ = q_ref.shape
    kvw, ncache = kt_ref.shape[1], kt_ref.shape[2]
    qlen = knt_ref.shape[2]
    per_kv = rows // N_KV_HEADS
    r_lane = _div_pow2(lax.broadcasted_iota(jnp.int32, (rows, kvw), 0), per_kv)
    c_lane = _div_pow2(lax.broadcasted_iota(jnp.int32, (rows, kvw), 1), HEAD_DIM)
    own = r_lane == c_lane
    mask_c = (lax.broadcasted_iota(jnp.int32, (rows, ncache), 1)
              >= _mod_pow2(lax.broadcasted_iota(jnp.int32, (rows, ncache), 0), qlen))
    mask_n = (lax.broadcasted_iota(jnp.int32, (rows, qlen), 1)
              <= _mod_pow2(lax.broadcasted_iota(jnp.int32, (rows, qlen), 0), qlen))
    keep_old = lax.broadcasted_iota(jnp.int32, (kvw, ncache), 1) < ncache - qlen
    sink = sink_ref[...]
    sel = sel_ref[...]
    contract_last = (((1,), (1,)), ((), ()))

    def shifted(old_t, new_t):
        placed = sum(jnp.dot(p, sel, preferred_element_type=F32) for p in _split3(new_t))
        return jnp.where(keep_old, pltpu.roll(old_t, ncache - qlen, axis=1), placed)

    for b in range(nseq):
        q = q_ref[b]
        qe = jnp.where(own, jnp.concatenate([q] * N_KV_HEADS, axis=1), jnp.zeros((), q.dtype))
        kt, knt, vt = kt_ref[b], knt_ref[b], vt_ref[b]
        sc = jnp.dot(qe, kt.astype(BF16), preferred_element_type=F32)
        sn = jnp.dot(qe, knt.astype(BF16), preferred_element_type=F32)
        sc = jnp.where(mask_c, sc, NEG_INF)
        sn = jnp.where(mask_n, sn, NEG_INF)
        m = jnp.maximum(jnp.maximum(jnp.max(sc, axis=-1, keepdims=True), jnp.max(sn, axis=-1, keepdims=True)), sink)
        ec = jnp.exp(sc - m)
        en = jnp.exp(sn - m)
        denom = jnp.sum(ec, axis=-1, keepdims=True) + jnp.sum(en, axis=-1, keepdims=True) + jnp.exp(sink - m)
        o = (lax.dot_general(ec.astype(BF16), vt.astype(BF16), contract_last, preferred_element_type=F32)
             + jnp.dot(en.astype(BF16), vn_ref[b].astype(BF16), preferred_element_type=F32))
        o = jnp.where(own, o, 0.0)
        acc = o[:, :HEAD_DIM]
        for g in range(1, N_KV_HEADS):
            acc = acc + o[:, g * HEAD_DIM:(g + 1) * HEAD_DIM]
        o_ref[b] = (acc / denom).astype(o_ref.dtype)
        kt_out_ref[b] = shifted(kt, knt)
        vt_out_ref[b] = shifted(vt, vnt_ref[b])


def _attn_sample(q, k_cache, k_new, v_cache, v_new, sinks, seq_per_step=8):
    bsz, ncache = k_cache.shape[:2]
    qlen, kvw = k_new.shape[1:]
    n_heads = q.shape[1] // HEAD_DIM
    group = n_heads // N_KV_HEADS
    rows = n_heads * qlen
    qr = q.reshape(bsz, qlen, N_KV_HEADS, group, HEAD_DIM).transpose(0, 2, 3, 1, 4).reshape(bsz, rows, HEAD_DIM)
    sink_rows = jnp.repeat(sinks.astype(F32), qlen).reshape(rows, 1)
    sel = (jnp.arange(ncache)[None, :] == jnp.arange(qlen)[:, None] + (ncache - qlen)).astype(BF16)
    feature_major = lambda c: c.transpose(0, 2, 3, 1).reshape(bsz, kvw, ncache)
    seq = lambda i: (i, 0, 0)
    cache_spec = pl.BlockSpec((seq_per_step, kvw, ncache), seq)
    newt_spec = pl.BlockSpec((seq_per_step, kvw, qlen), seq)
    o, kt_next, vt_next = pl.pallas_call(
        _attn_sample_kernel,
        grid=(bsz // seq_per_step,),
        in_specs=[
            pl.BlockSpec((rows, 1), lambda i: (0, 0)),
            pl.BlockSpec((qlen, ncache), lambda i: (0, 0)),
            pl.BlockSpec((seq_per_step, rows, HEAD_DIM), seq),
            cache_spec, newt_spec, cache_spec, pl.BlockSpec((seq_per_step, qlen, kvw), seq), newt_spec,
        ],
        out_specs=[pl.BlockSpec((seq_per_step, rows, HEAD_DIM), seq), cache_spec, cache_spec],
        out_shape=[jax.ShapeDtypeStruct((bsz, rows, HEAD_DIM), BF16),
                   jax.ShapeDtypeStruct((bsz, kvw, ncache), F32), jax.ShapeDtypeStruct((bsz, kvw, ncache), F32)],
        compiler_params=_cparams(("arbitrary",)),
        name="attn_sample",
    )(sink_rows, sel, qr, feature_major(k_cache), k_new.transpose(0, 2, 1), feature_major(v_cache), v_new,
      v_new.transpose(0, 2, 1))
    position_major = lambda t: t.reshape(bsz, N_KV_HEADS, HEAD_DIM, ncache).transpose(0, 3, 1, 2)
    ao = o.reshape(bsz, N_KV_HEADS, group, qlen, HEAD_DIM).transpose(0, 3, 1, 2, 4).reshape(bsz * qlen, -1)
    return ao, position_major(kt_next), position_major(vt_next)


CONV_HALO = 32
CONV_ROWS = 32
CONV_COLS = 256


def _conv_prompt_kernel(uc_ref, up_ref, w_ref, b_ref, y_ref, sh_ref):
    tile = uc_ref.shape[0]
    n = pl.program_id(1)
    halo = jnp.where(n > 0, up_ref[...], 0.0)
    sh_ref[0, :CONV_HALO, :] = halo
    sh_ref[0, CONV_HALO:, :] = uc_ref[...]
    keep = tile + CONV_HALO - SUBLANES
    for r in range(1, SUBLANES):
        sh_ref[r, :keep, :] = sh_ref[0, r:r + keep, :]
    first = CONV_HALO - (CONV_WIDTH - 1)
    ch = uc_ref.shape[1]

    def rows_step(rc, carry):
        r0 = pl.multiple_of(rc * CONV_ROWS, CONV_ROWS)
        for cc in range(ch // CONV_COLS):
            cs = slice(cc * CONV_COLS, (cc + 1) * CONV_COLS)
            acc = jnp.zeros((CONV_ROWS, CONV_COLS), F32)
            for j in range(CONV_WIDTH):
                off = first + j
                a, r = off // SUBLANES, off % SUBLANES
                acc = acc + sh_ref[r, pl.ds(r0 + a * SUBLANES, CONV_ROWS), cs] * w_ref[j:j + 1, cs]
            y_ref[pl.ds(r0, CONV_ROWS), cs] = acc + b_ref[:, cs]
        return carry

    lax.fori_loop(0, tile // CONV_ROWS, rows_step, 0)


def _conv_prompt(u, w, b, bsz, slen, tile=256):
    ch = u.shape[1]
    nt = slen // tile
    per = tile // CONV_HALO
    return pl.pallas_call(
        _conv_prompt_kernel,
        grid=(bsz, nt),
        in_specs=[
            pl.BlockSpec((tile, ch), lambda bb, n: (bb * nt + n, 0)),
            pl.BlockSpec((CONV_HALO, ch), lambda bb, n: (jnp.maximum((bb * nt + n) * per - 1, 0), 0)),
            pl.BlockSpec((CONV_WIDTH, ch), lambda bb, n: (0, 0)),
            pl.BlockSpec((1, ch), lambda bb, n: (0, 0)),
        ],
        out_specs=pl.BlockSpec((tile, ch), lambda bb, n: (bb * nt + n, 0)),
        out_shape=jax.ShapeDtypeStruct((bsz * slen, ch), F32),
        scratch_shapes=[pltpu.VMEM((SUBLANES, tile + CONV_HALO, ch), F32)],
        compiler_params=_cparams(("arbitrary", "arbitrary")),
        name="conv_prompt",
    )(u, u, w, b.reshape(1, ch))


def _conv_sample_kernel(sc_ref, un_ref, w_ref, b_ref, y_ref):
    nctx = sc_ref.shape[0]
    qlen = un_ref.shape[0]
    ch = w_ref.shape[1]

    def row(t):
        return sc_ref[t] if t < nctx else un_ref[t - nctx]

    for i in range(qlen):
        acc = jnp.zeros((sc_ref.shape[1], ch), F32)
        for j in range(CONV_WIDTH):
            acc = acc + row(i + j) * w_ref[j:j + 1, :]
        y_ref[i] = acc + b_ref[...]


def _conv_sample(state_t, u_new_t, w, b, seq_per_step=16):
    nctx, bsz, ch = state_t.shape
    qlen = u_new_t.shape[0]
    seq = lambda i: (0, i, 0)
    return pl.pallas_call(
        _conv_sample_kernel,
        grid=(bsz // seq_per_step,),
        in_specs=[
            pl.BlockSpec((nctx, seq_per_step, ch), seq),
            pl.BlockSpec((qlen, seq_per_step, ch), seq),
            pl.BlockSpec((CONV_WIDTH, ch), lambda i: (0, 0)),
            pl.BlockSpec((1, ch), lambda i: (0, 0)),
        ],
        out_specs=pl.BlockSpec((qlen, seq_per_step, ch), seq),
        out_shape=jax.ShapeDtypeStruct((qlen, bsz, ch), F32),
        compiler_params=_cparams(("arbitrary",)),
        name="conv_sample",
    )(state_t, u_new_t, w, b.reshape(1, ch))


def _merge_kernel(y_ref, ao_ref, ga_ref, gc_ref, x_ref, lnw_ref, lnb_ref, wco_ref, wo_ref, fw_ref, wr_ref,
                  h_ref, xn_ref, lg_ref):
    y = y_ref[...]
    mu = jnp.mean(y, axis=-1, keepdims=True)
    yc = y - mu
    var = jnp.mean(yc * yc, axis=-1, keepdims=True)
    yl = yc * lax.rsqrt(var + EPS) * lnw_ref[...] + lnb_ref[...]
    act = yl * jax.nn.sigmoid(yl)
    conv_o = jnp.dot(act.astype(BF16), wco_ref[...], preferred_element_type=F32)
    mix = ga_ref[...].astype(F32) * ao_ref[...].astype(F32) + gc_ref[...].astype(F32) * conv_o
    h = x_ref[...] + jnp.dot(mix.astype(BF16), wo_ref[...], preferred_element_type=F32)
    h_ref[...] = h
    ms = jnp.mean(h * h, axis=-1, keepdims=True)
    xn = (h * lax.rsqrt(ms + EPS) * fw_ref[...]).astype(BF16)
    xn_ref[...] = xn
    lg_ref[...] = jnp.dot(xn, wr_ref[...], preferred_element_type=F32)


def _merge(y, ao, gates, row0, x, ln_w, ln_b, wco, wo, fw, wr, tile=256):
    t, d = x.shape
    ch = y.shape[1]
    off = row0 // tile
    row = lambda i: (i, 0)
    const = lambda shape: pl.BlockSpec(shape, lambda i: (0, 0), pipeline_mode=pl.Buffered(1))
    return pl.pallas_call(
        _merge_kernel,
        grid=(t // tile,),
        in_specs=[
            pl.BlockSpec((tile, ch), row),
            pl.BlockSpec((tile, d), row),
            pl.BlockSpec((tile, d), lambda i: (i + off, 0)),
            pl.BlockSpec((tile, d), lambda i: (i + off, 1)),
            pl.BlockSpec((tile, d), row),
            const((1, ch)), const((1, ch)), const((ch, d)), const((d, d)), const((1, d)),
            const((d, LANES)),
        ],
        out_specs=[pl.BlockSpec((tile, d), row), pl.BlockSpec((tile, d), row), pl.BlockSpec((tile, LANES), row)],
        out_shape=[jax.ShapeDtypeStruct((t, d), F32), jax.ShapeDtypeStruct((t, d), BF16),
                   jax.ShapeDtypeStruct((t, LANES), F32)],
        compiler_params=_cparams(("arbitrary",)),
        name="merge",
    )(y, ao, gates, gates, x, ln_w.reshape(1, ch), ln_b.reshape(1, ch), wco, wo, fw.reshape(1, d), wr)


def _route_kernel(lp_ref, ls_ref, tri_ref, upper_ref, meta_ref, metat_ref, cnt_ref, *, n_p):
    i = pl.program_id(0)

    def body(l_ref):
        lg = l_ref[...]
        shape = lg.shape
        lane = lax.broadcasted_iota(jnp.int32, shape, 1)
        big = jnp.int32(LANES)
        is_g = (lane >= N_EXPERTS) & (lane < N_EXPERTS + N_GROUPS)
        gl = jnp.where(is_g, lg, -jnp.inf)
        gmax = jnp.max(gl, axis=-1, keepdims=True)
        g_star = jnp.min(jnp.where(gl == gmax, lane - N_EXPERTS, big), axis=-1, keepdims=True)
        pg_star = 1.0 / jnp.sum(jnp.exp(gl - gmax), axis=-1, keepdims=True)
        in_g = (lane < N_EXPERTS) & (_div_pow2(lane, EXPERTS_PER_GROUP) == g_star)
        el = jnp.where(in_g, lg, -jnp.inf)
        v1 = jnp.max(el, axis=-1, keepdims=True)
        i1 = jnp.min(jnp.where(el == v1, lane, big), axis=-1, keepdims=True)
        el2 = jnp.where(lane == i1, -jnp.inf, el)
        v2 = jnp.max(el2, axis=-1, keepdims=True)
        i2 = jnp.min(jnp.where(el2 == v2, lane, big), axis=-1, keepdims=True)
        e2 = jnp.exp(v2 - v1)
        w1 = pg_star * (1.0 / (1.0 + e2))
        w2 = pg_star * (e2 / (1.0 + e2))
        oh1 = lane == i1
        oh2 = lane == i2
        cnt = jnp.where(oh1 | oh2, 1.0, 0.0)
        before = jnp.dot(tri_ref[...], cnt.astype(BF16), preferred_element_type=F32)
        tot = jnp.sum(cnt, axis=0, keepdims=True)
        runs = jnp.floor((tot + (CHUNK_ROWS - 1)) * (1.0 / CHUNK_ROWS))
        runs8 = jnp.broadcast_to(runs, (SUBLANES, LANES)).astype(BF16)
        start = jnp.dot(runs8, upper_ref[...], preferred_element_type=F32)[0:1] * CHUNK_ROWS
        slot = before + start
        s1 = jnp.sum(jnp.where(oh1, slot, 0.0), axis=-1, keepdims=True)
        s2 = jnp.sum(jnp.where(oh2, slot, 0.0), axis=-1, keepdims=True)
        cols = [i1.astype(F32), i2.astype(F32), w1, w2, s1, s2]
        meta = jnp.zeros(shape, F32)
        for c, val in enumerate(cols):
            meta = jnp.where(lane == c, val, meta)
        meta_ref[...] = meta
        metat_ref[...] = meta.T[:META_FIELDS]
        cnt_ref[...] = tot

    @pl.when(i < n_p)
    def _():
        body(lp_ref)

    @pl.when(i >= n_p)
    def _():
        body(ls_ref)


def _route(lg_p, lg_s):
    tile = TOKEN_BLOCK
    tp, ts = lg_p.shape[0], lg_s.shape[0]
    n_p, n_s = tp // tile, ts // tile
    nb = n_p + n_s
    tri = (jnp.arange(tile)[:, None] > jnp.arange(tile)[None, :]).astype(BF16)
    upper = (jnp.arange(LANES)[:, None] < jnp.arange(LANES)[None, :]).astype(BF16)
    return pl.pallas_call(
        functools.partial(_route_kernel, n_p=n_p),
        grid=(nb,),
        in_specs=[
            pl.BlockSpec((tile, LANES), lambda i: (jnp.minimum(i, n_p - 1), 0)),
            pl.BlockSpec((tile, LANES), lambda i: (jnp.maximum(i - n_p, 0), 0)),
            pl.BlockSpec((tile, tile), lambda i: (0, 0)),
            pl.BlockSpec((LANES, LANES), lambda i: (0, 0)),
        ],
        out_specs=[pl.BlockSpec((tile, LANES), lambda i: (i, 0)),
                   pl.BlockSpec((None, META_FIELDS, tile), lambda i: (i, 0, 0)),
                   pl.BlockSpec((None, 1, LANES), lambda i: (i, 0, 0))],
        out_shape=[jax.ShapeDtypeStruct((tp + ts, LANES), F32),
                   jax.ShapeDtypeStruct((nb, META_FIELDS, tile), F32),
                   jax.ShapeDtypeStruct((nb, 1, LANES), F32)],
        compiler_params=_cparams(("arbitrary",)),
        name="route",
    )(lg_p, lg_s, tri, upper)


def _plan_sizes(n_tok):
    nb = n_tok // TOKEN_BLOCK
    pad_per_block = N_EXPERTS * (CHUNK_ROWS - 1)
    max_chunks = _round_up(-(-(TOKEN_BLOCK * TOP_K + pad_per_block) // CHUNK_ROWS), SUBLANES)
    n_rows = _round_up(n_tok * TOP_K + nb * pad_per_block + N_EXPERTS * (ROW_PAD - 1), ROW_PAD)
    n_items = N_EXPERTS + n_rows // EXPERT_ROWS + 1
    max_zero = N_EXPERTS * (ROW_PAD // CHUNK_ROWS)
    return nb, max_chunks, n_rows, n_items, max_zero


def _dispatch_plan(counts, max_chunks, n_items, max_zero):
    i32 = jnp.int32
    c = counts[:, 0, :N_EXPERTS].astype(i32)
    run = (c + CHUNK_ROWS - 1) // CHUNK_ROWS
    run_end = jnp.cumsum(run, axis=1)
    src_start = run_end - run
    nchunk = run_end[:, -1]
    seg = jnp.sum(run, axis=0) * CHUNK_ROWS
    padded = (seg + ROW_PAD - 1) // ROW_PAD * ROW_PAD
    pad_end = jnp.cumsum(padded)
    pad_start = pad_end - padded
    dst_start = pad_start[None, :] // CHUNK_ROWS + (jnp.cumsum(run, axis=0) - run)
    jj = jnp.arange(max_chunks, dtype=i32)[None, :, None]
    in_run = (src_start[:, None, :] <= jj) & (jj < run_end[:, None, :])
    chunk_dst = jnp.sum(jnp.where(in_run, (dst_start - src_start)[:, None, :] + jj, 0), axis=-1) * CHUNK_ROWS
    tail = (padded - seg) // CHUNK_ROWS
    tail_end = jnp.cumsum(tail)
    tail_start = tail_end - tail
    z = jnp.arange(max_zero, dtype=i32)[:, None]
    in_tail = (tail_start[None, :] <= z) & (z < tail_end[None, :])
    zero_base = (pad_start + seg) // CHUNK_ROWS - tail_start
    zero_dst = jnp.sum(jnp.where(in_tail, zero_base[None, :] + z, 0), axis=-1) * CHUNK_ROWS
    nzero = tail_end[-1]
    items_e = (padded + EXPERT_ROWS - 1) // EXPERT_ROWS
    item_end = jnp.cumsum(items_e)
    item_start = item_end - items_e
    w = jnp.arange(n_items, dtype=i32)[:, None]
    in_item = (item_start[None, :] <= w) & (w < item_end[None, :])
    k = w - item_start[None, :]
    e_ids = jnp.arange(N_EXPERTS, dtype=i32)
    last_e = jnp.max(jnp.where(items_e > 0, e_ids, 0))
    item_e = jnp.where(w[:, 0] < item_end[-1], jnp.sum(jnp.where(in_item, e_ids[None, :], 0), axis=-1), last_e)
    item_row0 = jnp.sum(jnp.where(in_item, pad_start[None, :] + k * EXPERT_ROWS, 0), axis=-1)
    item_rows = jnp.sum(jnp.where(in_item, jnp.clip(padded[None, :] - k * EXPERT_ROWS, 0, EXPERT_ROWS), 0), axis=-1)
    item_nsub = item_rows // ROW_PAD
    return (chunk_dst.reshape(-1).astype(i32), nchunk.astype(i32), zero_dst.astype(i32),
            nzero.reshape(1).astype(i32), item_e.astype(i32), item_row0.astype(i32), item_nsub.astype(i32))


def _scatter_kernel(cd_ref, nc_ref, zd_ref, nz_ref, mt_ref, xp_ref, xs_ref, out_hbm, loc_ref, zero_ref, sem,
                    zero_sem, *, n_p, max_chunks):
    i = pl.program_id(0)
    buf = i % 2
    n_slots = loc_ref.shape[1]
    d = xp_ref.shape[1]
    mt = mt_ref[...]
    w1, w2 = mt[2:3, :], mt[3:4, :]
    s1, s2 = mt[4:5, :].astype(jnp.int32), mt[5:6, :].astype(jnp.int32)
    slot = lax.broadcasted_iota(jnp.int32, (n_slots, mt.shape[1]), 0)
    p1 = slot == s1
    p2 = slot == s2
    perm = jnp.where(p1 | p2, 1.0, 0.0).astype(BF16)
    wcol = jnp.sum(jnp.where(p1, w1, 0.0) + jnp.where(p2, w2, 0.0), axis=-1, keepdims=True)
    half = d // 2
    loc_ref[buf, :, half:] = pltpu.bitcast(jnp.broadcast_to(wcol, (n_slots, LANES)), jnp.uint32)

    def sort_rows(x_ref):
        rows = jnp.dot(perm, x_ref[...], preferred_element_type=F32)
        loc_ref[buf, :, :half] = _pack_pairs(rows[:, :half], rows[:, half:])

    @pl.when(i < n_p)
    def _():
        sort_rows(xp_ref)

    @pl.when(i >= n_p)
    def _():
        sort_rows(xs_ref)

    def rows_at(r):
        return pl.ds(pl.multiple_of(r, CHUNK_ROWS), CHUNK_ROWS)

    def chunk_copy(blk, j):
        return pltpu.make_async_copy(loc_ref.at[blk % 2, rows_at(j * CHUNK_ROWS)],
                                     out_hbm.at[rows_at(cd_ref[blk * max_chunks + j])], sem.at[blk % 2])

    def zero_copy(z):
        return pltpu.make_async_copy(zero_ref, out_hbm.at[rows_at(zd_ref[z])], zero_sem)

    def loop(n, fn):
        def body(j, carry):
            fn(j)
            return carry
        lax.fori_loop(0, n, body, 0)

    @pl.when(i > 0)
    def _():
        loop(nc_ref[i - 1], lambda j: chunk_copy(i - 1, j).wait())

    loop(nc_ref[i], lambda j: chunk_copy(i, j).start())

    @pl.when(i == pl.num_programs(0) - 1)
    def _():
        zero_ref[...] = jnp.zeros_like(zero_ref)
        loop(nz_ref[0], lambda z: zero_copy(z).start())
        loop(nc_ref[i], lambda j: chunk_copy(i, j).wait())
        loop(nz_ref[0], lambda z: zero_copy(z).wait())


def _scatter_rows(xn_p, xn_s, metat, chunk_dst, nchunk, zero_dst, nzero, max_chunks, n_rows):
    tile = TOKEN_BLOCK
    tp, d = xn_p.shape
    ts = xn_s.shape[0]
    n_p, n_s = tp // tile, ts // tile
    n_slots = max_chunks * CHUNK_ROWS
    width = d // 2 + LANES
    u32 = jnp.uint32
    return pl.pallas_call(
        functools.partial(_scatter_kernel, n_p=n_p, max_chunks=max_chunks),
        grid_spec=pltpu.PrefetchScalarGridSpec(
            num_scalar_prefetch=4,
            grid=(n_p + n_s,),
            in_specs=[
                pl.BlockSpec((None, META_FIELDS, tile), lambda i, *_: (i, 0, 0)),
                pl.BlockSpec((tile, d), lambda i, *_: (jnp.minimum(i, n_p - 1), 0)),
                pl.BlockSpec((tile, d), lambda i, *_: (jnp.maximum(i - n_p, 0), 0)),
            ],
            out_specs=pl.BlockSpec(memory_space=pl.ANY),
            scratch_shapes=[pltpu.VMEM((2, n_slots, width), u32), pltpu.VMEM((CHUNK_ROWS, width), u32),
                            pltpu.SemaphoreType.DMA((2,)), pltpu.SemaphoreType.DMA(())],
        ),
        out_shape=jax.ShapeDtypeStruct((n_rows, width), u32),
        compiler_params=_cparams(("arbitrary",)),
        name="scatter_rows",
    )(chunk_dst, nchunk, zero_dst, nzero, metat, xn_p, xn_s)


def _expert_kernel(e_ref, row0_ref, nsub_ref, xs_hbm, wg_ref, wu_ref, wd_ref, out_hbm,
                   xf_ref, xb_ref, acc_ref, ob_ref, pend_ref, in_sem, out_sem):
    w = pl.program_id(0)
    c = pl.program_id(1)
    n_items = pl.num_programs(0)
    last_c = pl.num_programs(1) - 1
    nsub = nsub_ref[w]
    d = acc_ref.shape[1]
    half = d // 2
    slot = w % 2

    def sub_rows(s):
        return pl.ds(pl.multiple_of(s * ROW_PAD, ROW_PAD), ROW_PAD)

    def hbm_rows(item, s):
        return pl.ds(pl.multiple_of(row0_ref[item] + s * ROW_PAD, ROW_PAD), ROW_PAD)

    def in_copy(item, s):
        return pltpu.make_async_copy(xs_hbm.at[hbm_rows(item, s)], xf_ref.at[item % 2, sub_rows(s)],
                                     in_sem.at[item % 2])

    def out_copy(item, s):
        return pltpu.make_async_copy(ob_ref.at[sub_rows(s)], out_hbm.at[hbm_rows(item, s)], out_sem)

    def loop(n, fn):
        def body(s, carry):
            fn(s)
            return carry
        lax.fori_loop(0, n, body, 0)

    @pl.when(c == 0)
    def _():
        @pl.when(w == 0)
        def _():
            pend_ref[0] = 0
            pend_ref[1] = 0
            loop(nsub, lambda s: in_copy(w, s).start())

        loop(nsub, lambda s: in_copy(w, s).wait())
        nxt = jnp.minimum(w + 1, n_items - 1)

        @pl.when(w + 1 < n_items)
        def _():
            loop(nsub_ref[nxt], lambda s: in_copy(nxt, s).start())

    def drain_out():
        prev = pend_ref[1]
        loop(pend_ref[0], lambda s: out_copy(prev, s).wait())
        pend_ref[0] = 0

    for n in range(1, EXPERT_ROWS // ROW_PAD + 1):
        m = n * ROW_PAD

        @pl.when(nsub == n)
        def _(m=m):
            @pl.when(c == 0)
            def _():
                acc_ref[:m, :] = jnp.zeros((m, d), F32)
                xa, xb = _unpack_pairs(xf_ref[slot, :m, :half])
                xb_ref[:m, :half] = xa.astype(BF16)
                xb_ref[:m, half:] = xb.astype(BF16)

            x = xb_ref[:m, :]
            g = jnp.dot(x, wg_ref[...].astype(BF16), preferred_element_type=F32)
            u = jnp.dot(x, wu_ref[...].astype(BF16), preferred_element_type=F32)
            hmid = (g * jax.nn.sigmoid(g) * u).astype(BF16)
            acc_ref[:m, :] += jnp.dot(hmid, wd_ref[...].astype(BF16), preferred_element_type=F32)

            @pl.when(c == last_c)
            def _():
                drain_out()
                mix_w = pltpu.bitcast(xf_ref[slot, :m, half:half + 1], F32)
                y = (acc_ref[:m, :] * mix_w).astype(BF16).astype(F32)
                ob_ref[:m, :] = _pack_pairs(y[:, :half], y[:, half:])
                loop(nsub, lambda s: out_copy(w, s).start())
                pend_ref[0] = nsub
                pend_ref[1] = w

    @pl.when((w == n_items - 1) & (c == last_c))
    def _():
        drain_out()


def _experts(xs, item_e, item_row0, item_nsub, w_gate, w_up, w_down):
    n_rows, width = xs.shape
    d = w_gate.shape[1]
    assert width == d // 2 + LANES
    d_exp = w_gate.shape[2]
    nc = d_exp // D_CHUNK
    n_items = item_e.shape[0]

    def chunk_of(w, c, nsub_ref):
        return jnp.where(nsub_ref[w] > 0, c, nc - 1)

    return pl.pallas_call(
        _expert_kernel,
        grid_spec=pltpu.PrefetchScalarGridSpec(
            num_scalar_prefetch=3,
            grid=(n_items, nc),
            in_specs=[
                pl.BlockSpec(memory_space=pl.ANY),
                pl.BlockSpec((None, d, D_CHUNK), lambda w, c, e, r, n: (e[w], 0, chunk_of(w, c, n))),
                pl.BlockSpec((None, d, D_CHUNK), lambda w, c, e, r, n: (e[w], 0, chunk_of(w, c, n))),
                pl.BlockSpec((None, D_CHUNK, d), lambda w, c, e, r, n: (e[w], chunk_of(w, c, n), 0)),
            ],
            out_specs=pl.BlockSpec(memory_space=pl.ANY),
            scratch_shapes=[
                pltpu.VMEM((2, EXPERT_ROWS, width), jnp.uint32),
                pltpu.VMEM((EXPERT_ROWS, d), BF16),
                pltpu.VMEM((EXPERT_ROWS, d), F32),
                pltpu.VMEM((EXPERT_ROWS, d // 2), jnp.uint32),
                pltpu.SMEM((2,), jnp.int32),
                pltpu.SemaphoreType.DMA((2,)),
                pltpu.SemaphoreType.DMA(()),
            ],
        ),
        out_shape=jax.ShapeDtypeStruct((n_rows, d // 2), jnp.uint32),
        compiler_params=_cparams(("arbitrary", "arbitrary")),
        name="experts",
    )(item_e, item_row0, item_nsub, xs, w_gate, w_up, w_down)


COMBINE_SPLIT = 2
SLOT_STEP = 256


def _combine_kernel(cd_ref, nc_ref, meta_ref, hp_ref, hs_ref, yb_hbm, yp_ref, ys_ref, loc_ref, sem,
                    *, n_p, max_chunks):
    i = pl.program_id(0)
    j = pl.program_id(1)
    nb = pl.num_programs(0)
    buf = i % 2
    n_slots, half = loc_ref.shape[1], loc_ref.shape[2]
    d = 2 * half

    def rows_at(r):
        return pl.ds(pl.multiple_of(r, CHUNK_ROWS), CHUNK_ROWS)

    def chunk_copy(blk, c):
        return pltpu.make_async_copy(yb_hbm.at[rows_at(cd_ref[blk * max_chunks + c])],
                                     loc_ref.at[blk % 2, rows_at(c * CHUNK_ROWS)], sem.at[blk % 2])

    def fetch(blk):
        n = nc_ref[blk]

        def start(c, carry):
            chunk_copy(blk, c).start()
            return carry
        lax.fori_loop(0, n, start, 0)

        def clear(c, carry):
            loc_ref[blk % 2, rows_at(c * CHUNK_ROWS), :] = jnp.zeros((CHUNK_ROWS, half), jnp.uint32)
            return carry
        lax.fori_loop(n, max_chunks, clear, 0)

    @pl.when(j == 0)
    def _():
        @pl.when(i == 0)
        def _():
            fetch(i)

        def wait(c, carry):
            chunk_copy(i, c).wait()
            return carry
        lax.fori_loop(0, nc_ref[i], wait, 0)

        @pl.when(i + 1 < nb)
        def _():
            fetch(jnp.minimum(i + 1, nb - 1))

    meta = meta_ref[...]
    s1 = meta[:, 4:5].astype(jnp.int32)
    s2 = meta[:, 5:6].astype(jnp.int32)
    rows = meta.shape[0]
    f = jnp.zeros((rows, d), F32)
    for r0 in range(0, n_slots, SLOT_STEP):
        slot = lax.broadcasted_iota(jnp.int32, (rows, SLOT_STEP), 1) + r0
        pt = jnp.where((slot == s1) | (slot == s2), 1.0, 0.0).astype(BF16)
        ya, yb = _unpack_pairs(loc_ref[buf, r0:r0 + SLOT_STEP, :])
        f = f + jnp.dot(pt, jnp.concatenate([ya, yb], axis=1).astype(BF16), preferred_element_type=F32)
    g = i * pl.num_programs(1) + j

    @pl.when(g < n_p)
    def _():
        yp_ref[...] = hp_ref[...] + f

    @pl.when(g >= n_p)
    def _():
        ys_ref[...] = hs_ref[...] + f


def _combine(h_p, h_s, meta, chunk_dst, nchunk, yb, max_chunks):
    tp, d = h_p.shape
    ts = h_s.shape[0]
    tile = TOKEN_BLOCK // COMBINE_SPLIT
    n_p, n_s = tp // tile, ts // tile
    nb = (tp + ts) // TOKEN_BLOCK
    n_slots = max_chunks * CHUNK_ROWS
    assert n_slots % SLOT_STEP == 0
    blk = lambda i, j: i * COMBINE_SPLIT + j
    p_idx = lambda i, j, *_: (jnp.minimum(blk(i, j), n_p - 1), 0)
    s_idx = lambda i, j, *_: (jnp.maximum(blk(i, j) - n_p, 0), 0)
    return pl.pallas_call(
        functools.partial(_combine_kernel, n_p=n_p, max_chunks=max_chunks),
        grid_spec=pltpu.PrefetchScalarGridSpec(
            num_scalar_prefetch=2,
            grid=(nb, COMBINE_SPLIT),
            in_specs=[
                pl.BlockSpec((tile, LANES), lambda i, j, *_: (blk(i, j), 0)),
                pl.BlockSpec((tile, d), p_idx),
                pl.BlockSpec((tile, d), s_idx),
                pl.BlockSpec(memory_space=pl.ANY),
            ],
            out_specs=[pl.BlockSpec((tile, d), p_idx), pl.BlockSpec((tile, d), s_idx)],
            scratch_shapes=[pltpu.VMEM((2, n_slots, d // 2), jnp.uint32), pltpu.SemaphoreType.DMA((2,))],
        ),
        out_shape=[jax.ShapeDtypeStruct((tp, d), F32), jax.ShapeDtypeStruct((ts, d), F32)],
        compiler_params=_cparams(("arbitrary", "arbitrary")),
        name="combine",
    )(chunk_dst, nchunk, meta, h_p, h_s, yb)


def _layer(x_p, x_s, ck, cv, sc, attn_norm_w, w_in, b_gate, q_norm_w, k_norm_w, sinks, conv_dw_w, conv_dw_b,
           conv_ln_w, conv_ln_b, w_conv_out, w_out, ffn_norm_w, router_group, router_expert, w_gate, w_up,
           w_down):
    bsz, slen, d = x_p.shape
    dbs, dlen, _ = x_s.shape
    tp, ts = bsz * slen, dbs * dlen
    conv_ch = conv_dw_w.shape[1]
    kvw = N_KV_HEADS * HEAD_DIM
    xp2, xs2 = x_p.reshape(tp, d), x_s.reshape(ts, d)

    xn = _prenorm(xp2, xs2, attn_norm_w)
    q, k, v, u, gates = _project(xn, w_in, b_gate, q_norm_w, k_norm_w, d, conv_ch)

    ao_p = _attn_prompt(q, k, v, sinks, bsz, slen)
    k_s = k[tp:].reshape(dbs, dlen, kvw)
    v_s = v[tp:].reshape(dbs, dlen, kvw)
    ao_s, ck_next, cv_next = _attn_sample(q[tp:], ck, k_s, cv, v_s, sinks)

    y_p = _conv_prompt(u, conv_dw_w, conv_dw_b, bsz, slen)
    sc_t = sc.transpose(1, 0, 2)
    u_s_t = u[tp:].reshape(dbs, dlen, conv_ch).transpose(1, 0, 2)
    y_s = _conv_sample(sc_t, u_s_t, conv_dw_w, conv_dw_b).transpose(1, 0, 2).reshape(ts, conv_ch)

    wco = w_conv_out.astype(BF16)
    wo = w_out.astype(BF16)
    wr = jnp.concatenate([router_expert, router_group,
                          jnp.zeros((d, LANES - N_EXPERTS - N_GROUPS), router_expert.dtype)], axis=1).astype(BF16)
    margs = (conv_ln_w, conv_ln_b, wco, wo, ffn_norm_w, wr)
    h_p, xn_p, lg_p = _merge(y_p, ao_p, gates, 0, xp2, *margs)
    h_s, xn_s, lg_s = _merge(y_s, ao_s, gates, tp, xs2, *margs)

    meta, metat, counts = _route(lg_p, lg_s)
    _, max_chunks, n_rows, n_items, max_zero = _plan_sizes(tp + ts)
    chunk_dst, nchunk, zero_dst, nzero, item_e, item_row0, item_nsub = _dispatch_plan(
        counts, max_chunks, n_items, max_zero)
    xs_rows = _scatter_rows(xn_p, xn_s, metat, chunk_dst, nchunk, zero_dst, nzero, max_chunks, n_rows)
    yb = _experts(xs_rows, item_e, item_row0, item_nsub, w_gate, w_up, w_down)
    out_p, out_s = _combine(h_p, h_s, meta, chunk_dst, nchunk, yb, max_chunks)

    def seq_tails(a, n):
        return jnp.stack([lax.slice_in_dim(a, (b + 1) * slen - n, (b + 1) * slen, axis=0) for b in range(bsz)])

    new_p = (seq_tails(k, WINDOW).reshape(bsz, WINDOW, N_KV_HEADS, HEAD_DIM),
             seq_tails(v, WINDOW).reshape(bsz, WINDOW, N_KV_HEADS, HEAD_DIM),
             seq_tails(u, CONV_WIDTH - 1))
    sc_next = jnp.concatenate([sc_t[dlen:], u_s_t], axis=0).transpose(1, 0, 2)
    new_s = (ck_next, cv_next, sc_next)
    return out_p.reshape(bsz, slen, d), out_s.reshape(dbs, dlen, d), new_p, new_s


def kernel(x_prompt, x_sample, cache_k, cache_v, state_conv, attn_norm_w, w_in, b_gate, q_norm_w, k_norm_w,
           sinks, conv_dw_w, conv_dw_b, conv_ln_w, conv_ln_b, w_conv_out, w_out, ffn_norm_w, router_group,
           router_expert, w_gate, w_up, w_down):
    x_p, x_s = x_prompt, x_sample
    kp, vp, cp, ks_, vs_, cs_ = [], [], [], [], [], []
    for l in range(cache_k.shape[0]):
        x_p, x_s, new_p, new_s = _layer(
            x_p, x_s, cache_k[l], cache_v[l], state_conv[l], attn_norm_w[l], w_in[l], b_gate[l], q_norm_w[l],
            k_norm_w[l], sinks[l], conv_dw_w[l], conv_dw_b[l], conv_ln_w[l], conv_ln_b[l], w_conv_out[l],
            w_out[l], ffn_norm_w[l], router_group[l], router_expert[l], w_gate[l], w_up[l], w_down[l])
        kp.append(new_p[0]); vp.append(new_p[1]); cp.append(new_p[2])
        ks_.append(new_s[0]); vs_.append(new_s[1]); cs_.append(new_s[2])
    return (x_p, x_s, jnp.stack(kp), jnp.stack(vp), jnp.stack(cp),
            jnp.stack(ks_), jnp.stack(vs_), jnp.stack(cs_))
```

```python
import functools

import jax
import jax.numpy as jnp
from jax import lax
from jax.experimental import pallas as pl
from jax.experimental.pallas import tpu as pltpu

F32 = jnp.float32
BF16 = jnp.bfloat16
EPS = 1e-6
NEG_INF = -1e30

HEAD_DIM = 64
N_KV_HEADS = 4
WINDOW = 128
CONV_WIDTH = 31
N_GROUPS = 4
EXPERTS_PER_GROUP = 8
N_EXPERTS = N_GROUPS * EXPERTS_PER_GROUP
TOP_K = 2

LANES = 128
SUBLANES = 8
MXU_DIM = 256
VMEM_LIMIT = 56 * 1024 * 1024

TOKEN_BLOCK = 512
CHUNK_ROWS = SUBLANES
ROW_PAD = 128
EXPERT_ROWS = 768
D_CHUNK = 512
META_FIELDS = 8


def _cparams(sem):
    return pltpu.CompilerParams(dimension_semantics=sem, vmem_limit_bytes=VMEM_LIMIT)


def _log2(n):
    assert n > 0 and n & (n - 1) == 0, n
    return n.bit_length() - 1


def _div_pow2(x, n):
    return lax.shift_right_logical(x, jnp.int32(_log2(n)))


def _mod_pow2(x, n):
    _log2(n)
    return x & jnp.int32(n - 1)


def _round_up(x, m):
    return (x + m - 1) // m * m


HIGH_HALF = 0xFFFF0000


def _pack_pairs(a, b):
    ua = pltpu.bitcast(a, jnp.uint32) & jnp.uint32(HIGH_HALF)
    return ua | (pltpu.bitcast(b, jnp.uint32) >> 16)


def _unpack_pairs(w):
    return (pltpu.bitcast(w & jnp.uint32(HIGH_HALF), F32), pltpu.bitcast(w << 16, F32))


def _prenorm_kernel(xp_ref, xs_ref, w_ref, o_ref, *, n_p):
    i = pl.program_id(0)

    def body(x_ref):
        x = x_ref[...]
        ms = jnp.mean(x * x, axis=-1, keepdims=True)
        o_ref[...] = (x * lax.rsqrt(ms + EPS) * w_ref[...]).astype(o_ref.dtype)

    @pl.when(i < n_p)
    def _():
        body(xp_ref)

    @pl.when(i >= n_p)
    def _():
        body(xs_ref)


def _prenorm(xp, xs, w, tile=512):
    tp, d = xp.shape
    ts = xs.shape[0]
    n_p, n_s = tp // tile, ts // tile
    return pl.pallas_call(
        functools.partial(_prenorm_kernel, n_p=n_p),
        grid=(n_p + n_s,),
        in_specs=[
            pl.BlockSpec((tile, d), lambda i: (jnp.minimum(i, n_p - 1), 0)),
            pl.BlockSpec((tile, d), lambda i: (jnp.maximum(i - n_p, 0), 0)),
            pl.BlockSpec((1, d), lambda i: (0, 0)),
        ],
        out_specs=pl.BlockSpec((tile, d), lambda i: (i, 0)),
        out_shape=jax.ShapeDtypeStruct((tp + ts, d), BF16),
        compiler_params=_cparams(("arbitrary",)),
        name="prenorm",
    )(xp, xs, w.reshape(1, d))


def _head_rms(z, hw, bd_ref):
    ss = z * z
    hi = ss.astype(BF16)
    lo = (ss - hi.astype(F32)).astype(BF16)
    bd = bd_ref[...]
    tot = jnp.dot(hi, bd, preferred_element_type=F32) + jnp.dot(lo, bd, preferred_element_type=F32)
    return z * lax.rsqrt(tot * (1.0 / HEAD_DIM) + EPS) * hw


def _load_w(i, w_ref, wb_ref):
    @pl.when(i == 0)
    def _():
        wb_ref[...] = w_ref[...].astype(BF16)


def _proj_q_kernel(x_ref, w_ref, hw_ref, bd_ref, o_ref, wb_ref):
    _load_w(pl.program_id(1), w_ref, wb_ref)
    z = jnp.dot(x_ref[...], wb_ref[...], preferred_element_type=F32)
    for c in range(z.shape[1] // MXU_DIM):
        sl = slice(c * MXU_DIM, (c + 1) * MXU_DIM)
        o_ref[:, sl] = _head_rms(z[:, sl], hw_ref[...], bd_ref).astype(o_ref.dtype)


def _proj_kv_kernel(x_ref, w_ref, hw_ref, bd_ref, k_ref, v_ref, wb_ref):
    _load_w(pl.program_id(1), w_ref, wb_ref)
    z = jnp.dot(x_ref[...], wb_ref[...], preferred_element_type=F32)
    kw = k_ref.shape[1]
    k_ref[...] = _head_rms(z[:, :kw], hw_ref[...], bd_ref)
    v_ref[...] = z[:, kw:]


def _proj_glu_kernel(x_ref, wa_ref, wg_ref, o_ref, wab_ref, wgb_ref):
    i = pl.program_id(1)
    _load_w(i, wa_ref, wab_ref)
    _load_w(i, wg_ref, wgb_ref)
    x = x_ref[...]
    a = jnp.dot(x, wab_ref[...], preferred_element_type=F32)
    g = jnp.dot(x, wgb_ref[...], preferred_element_type=F32)
    o_ref[...] = a * jax.nn.sigmoid(g)


def _proj_gate_kernel(x_ref, w_ref, b_ref, o_ref, wb_ref):
    _load_w(pl.program_id(1), w_ref, wb_ref)
    z = jnp.dot(x_ref[...], wb_ref[...], preferred_element_type=F32)
    o_ref[...] = jax.nn.sigmoid(z + b_ref[...]).astype(o_ref.dtype)


def _row_tile(t, cap=1152):
    best = 16
    for m in range(16, cap + 1, 16):
        if t % m == 0:
            best = m
    return best


def _project(xn, w_in, b_gate, q_norm_w, k_norm_w, d_model, conv_ch):
    t, d = xn.shape
    q_w = d_model
    kv_w = N_KV_HEADS * HEAD_DIM
    tn = 2 * kv_w
    tm = _row_tile(t)
    ni = t // tm
    tm_big = _row_tile(t, cap=2304)
    ni_big = t // tm_big
    reps = MXU_DIM // HEAD_DIM
    hq = jnp.tile(q_norm_w.astype(F32) * HEAD_DIM ** -0.5, reps).reshape(1, MXU_DIM)
    hk = jnp.tile(k_norm_w.astype(F32), reps).reshape(1, MXU_DIM)
    gid = jnp.arange(MXU_DIM) // HEAD_DIM
    bd = (gid[:, None] == gid[None, :]).astype(BF16)

    x_spec = pl.BlockSpec((tm, d), lambda s, i: (i, 0))
    x_big = pl.BlockSpec((tm_big, d), lambda s, i: (i, 0))
    small = lambda shape: pl.BlockSpec(shape, lambda s, i: (0, 0))

    def w_spec(c0):
        return pl.BlockSpec((d, tn), lambda s, i: (0, c0 + s))

    wscr = pltpu.VMEM((d, tn), BF16)
    cp = _cparams(("arbitrary", "arbitrary"))

    q = pl.pallas_call(
        _proj_q_kernel, grid=(q_w // tn, ni_big),
        in_specs=[x_big, w_spec(0), small((1, MXU_DIM)), small((MXU_DIM, MXU_DIM))],
        out_specs=pl.BlockSpec((tm_big, tn), lambda s, i: (i, s)),
        out_shape=jax.ShapeDtypeStruct((t, q_w), BF16),
        scratch_shapes=[wscr], compiler_params=cp, name="proj_q",
    )(xn, w_in, hq, bd)

    c_kv = q_w // tn
    k, v = pl.pallas_call(
        _proj_kv_kernel, grid=(1, ni),
        in_specs=[x_spec, w_spec(c_kv), small((1, MXU_DIM)), small((MXU_DIM, MXU_DIM))],
        out_specs=[pl.BlockSpec((tm, kv_w), lambda s, i: (i, 0))] * 2,
        out_shape=[jax.ShapeDtypeStruct((t, kv_w), F32)] * 2,
        scratch_shapes=[wscr], compiler_params=cp, name="proj_kv",
    )(xn, w_in, hk, bd)

    c_a = c_kv + 1
    n_glu = conv_ch // tn
    u = pl.pallas_call(
        _proj_glu_kernel, grid=(n_glu, ni),
        in_specs=[x_spec, w_spec(c_a), w_spec(c_a + n_glu)],
        out_specs=pl.BlockSpec((tm, tn), lambda s, i: (i, s)),
        out_shape=jax.ShapeDtypeStruct((t, conv_ch), F32),
        scratch_shapes=[wscr, wscr], compiler_params=cp, name="proj_glu",
    )(xn, w_in, w_in)

    c_g = c_a + 2 * n_glu
    n_gate = 2 * d_model // tn
    gates = pl.pallas_call(
        _proj_gate_kernel, grid=(n_gate, ni_big),
        in_specs=[x_big, w_spec(c_g), pl.BlockSpec((1, tn), lambda s, i: (0, s))],
        out_specs=pl.BlockSpec((tm_big, tn), lambda s, i: (i, s)),
        out_shape=jax.ShapeDtypeStruct((t, 2 * d_model), BF16),
        scratch_shapes=[wscr], compiler_params=cp, name="proj_gate",
    )(xn, w_in, b_gate.reshape(1, -1))
    return q, k, v, u, gates


def _attn_prompt_kernel(sink_ref, q_ref, kc_ref, kp_ref, vc_ref, vp_ref, o_ref):
    blk = q_ref.shape[0]
    n = pl.program_id(1)
    rows = 2 * blk
    r_iota = lax.broadcasted_iota(jnp.int32, (rows, rows), 0)
    c_iota = lax.broadcasted_iota(jnp.int32, (rows, rows), 1)
    qi = jnp.where(r_iota >= blk, r_iota - blk, r_iota)
    mask = (c_iota >= qi) & (c_iota <= qi + WINDOW) & ((c_iota >= blk) | (n > 0))
    lo_lane = lax.broadcasted_iota(jnp.int32, (blk, LANES), 1) < HEAD_DIM
    top_row = lax.broadcasted_iota(jnp.int32, (rows, 1), 0) < blk
    zero = jnp.zeros((blk, LANES), q_ref.dtype)
    group = q_ref.shape[1] // (N_KV_HEADS * LANES)
    for g in range(N_KV_HEADS):
        hs = slice(g * HEAD_DIM, (g + 1) * HEAD_DIM)
        kg = jnp.concatenate([kp_ref[:, hs], kc_ref[:, hs]], axis=0)
        vg = jnp.concatenate([vp_ref[:, hs], vc_ref[:, hs]], axis=0)
        kdup = jnp.concatenate([kg, kg], axis=1).astype(BF16)
        vdup = jnp.concatenate([vg, vg], axis=1).astype(BF16)
        for p in range(group):
            pair = g * group + p
            ls = slice(pair * LANES, (pair + 1) * LANES)
            qp = q_ref[:, ls]
            lhs = jnp.concatenate([jnp.where(lo_lane, qp, zero), jnp.where(lo_lane, zero, qp)], axis=0)
            s = lax.dot_general(lhs, kdup, (((1,), (1,)), ((), ())), preferred_element_type=F32)
            s = jnp.where(mask, s, NEG_INF)
            sink = jnp.where(top_row, sink_ref[2 * pair], sink_ref[2 * pair + 1])
            m = jnp.maximum(jnp.max(s, axis=-1, keepdims=True), sink)
            e = jnp.exp(s - m)
            denom = jnp.sum(e, axis=-1, keepdims=True) + jnp.exp(sink - m)
            o2 = jnp.dot(e.astype(BF16), vdup, preferred_element_type=F32) / denom
            o_ref[:, ls] = jnp.where(lo_lane, o2[:blk], o2[blk:]).astype(o_ref.dtype)


def _attn_prompt(q, k, v, sinks, bsz, slen):
    blk = WINDOW
    nb = slen // blk
    qw, kw = q.shape[1], k.shape[1]
    cur = lambda b, n: (b * nb + n, 0)
    prev = lambda b, n: (b * nb + jnp.maximum(n - 1, 0), 0)
    return pl.pallas_call(
        _attn_prompt_kernel,
        grid=(bsz, nb),
        in_specs=[
            pl.BlockSpec(memory_space=pltpu.SMEM),
            pl.BlockSpec((blk, qw), cur),
            pl.BlockSpec((blk, kw), cur), pl.BlockSpec((blk, kw), prev),
            pl.BlockSpec((blk, kw), cur), pl.BlockSpec((blk, kw), prev),
        ],
        out_specs=pl.BlockSpec((blk, qw), cur),
        out_shape=jax.ShapeDtypeStruct((bsz * slen, qw), BF16),
        compiler_params=_cparams(("arbitrary", "arbitrary")),
        name="attn_prompt",
    )(sinks.astype(F32), q, k, k, v, v)


def _attn_sample_kernel(sink_ref, q_ref, kt_ref, kn_ref, vt_ref, vn_ref, o_ref, kt_out_ref, vt_out_ref, *, qlen):
    nseq, rows, _ = q_ref.shape
    kvw, ncache = kt_ref.shape[1], kt_ref.shape[2]
    per_kv = rows // N_KV_HEADS
    r_lane = _div_pow2(lax.broadcasted_iota(jnp.int32, (rows, kvw), 0), per_kv)
    c_lane = _div_pow2(lax.broadcasted_iota(jnp.int32, (rows, kvw), 1), HEAD_DIM)
    own = r_lane == c_lane
    mask_c = (lax.broadcasted_iota(jnp.int32, (rows, ncache), 1)
              >= _mod_pow2(lax.broadcasted_iota(jnp.int32, (rows, ncache), 0), qlen))
    mask_n = (lax.broadcasted_iota(jnp.int32, (rows, qlen), 1)
              <= _mod_pow2(lax.broadcasted_iota(jnp.int32, (rows, qlen), 0), qlen))
    keep_old = lax.broadcasted_iota(jnp.int32, (kvw, ncache), 1) < ncache - qlen
    sink = sink_ref[...]
    contract_last = (((1,), (1,)), ((), ()))

    def feature_major(new_ref):
        pad = jnp.zeros((ncache - nseq * qlen, kvw), F32)
        return jnp.concatenate([new_ref[...], pad], axis=0).T

    kn_t = feature_major(kn_ref)
    vn_t = feature_major(vn_ref)

    def shifted(old_t, new_t, b):
        placed = pltpu.roll(new_t, ncache - qlen - b * qlen, axis=1)
        return jnp.where(keep_old, pltpu.roll(old_t, ncache - qlen, axis=1), placed)

    for b in range(nseq):
        q = q_ref[b]
        qe = jnp.where(own, jnp.concatenate([q] * N_KV_HEADS, axis=1), jnp.zeros((), q.dtype))
        kt, vt = kt_ref[b], vt_ref[b]
        kn = kn_ref[b * qlen:(b + 1) * qlen, :]
        vn = vn_ref[b * qlen:(b + 1) * qlen, :]
        sc = jnp.dot(qe, kt.astype(BF16), preferred_element_type=F32)
        sn = lax.dot_general(qe, kn.astype(BF16), contract_last, preferred_element_type=F32)
        sc = jnp.where(mask_c, sc, NEG_INF)
        sn = jnp.where(mask_n, sn, NEG_INF)
        m = jnp.maximum(jnp.maximum(jnp.max(sc, axis=-1, keepdims=True), jnp.max(sn, axis=-1, keepdims=True)), sink)
        ec = jnp.exp(sc - m)
        en = jnp.exp(sn - m)
        denom = jnp.sum(ec, axis=-1, keepdims=True) + jnp.sum(en, axis=-1, keepdims=True) + jnp.exp(sink - m)
        o = (lax.dot_general(ec.astype(BF16), vt.astype(BF16), contract_last, preferred_element_type=F32)
             + jnp.dot(en.astype(BF16), vn.astype(BF16), preferred_element_type=F32))
        o = jnp.where(own, o, 0.0)
        acc = o[:, :HEAD_DIM]
        for g in range(1, N_KV_HEADS):
            acc = acc + o[:, g * HEAD_DIM:(g + 1) * HEAD_DIM]
        o_ref[b] = (acc / denom).astype(o_ref.dtype)
        kt_out_ref[b] = shifted(kt, kn_t, b)
        vt_out_ref[b] = shifted(vt, vn_t, b)


def _attn_sample(q, k_cache, k_new, v_cache, v_new, sinks, seq_per_step=8):
    bsz, ncache = k_cache.shape[:2]
    kvw = k_new.shape[1]
    qlen = k_new.shape[0] // bsz
    assert seq_per_step * qlen <= ncache
    n_heads = q.shape[1] // HEAD_DIM
    group = n_heads // N_KV_HEADS
    rows = n_heads * qlen
    qr = q.reshape(bsz, qlen, N_KV_HEADS, group, HEAD_DIM).transpose(0, 2, 3, 1, 4).reshape(bsz, rows, HEAD_DIM)
    sink_rows = jnp.repeat(sinks.astype(F32), qlen).reshape(rows, 1)
    feature_major = lambda c: c.transpose(0, 2, 3, 1).reshape(bsz, kvw, ncache)
    seq = lambda i: (i, 0, 0)
    cache_spec = pl.BlockSpec((seq_per_step, kvw, ncache), seq)
    new_spec = pl.BlockSpec((seq_per_step * qlen, kvw), lambda i: (i, 0))
    o, kt_next, vt_next = pl.pallas_call(
        functools.partial(_attn_sample_kernel, qlen=qlen),
        grid=(bsz // seq_per_step,),
        in_specs=[
            pl.BlockSpec((rows, 1), lambda i: (0, 0)),
            pl.BlockSpec((seq_per_step, rows, HEAD_DIM), seq),
            cache_spec, new_spec, cache_spec, new_spec,
        ],
        out_specs=[pl.BlockSpec((seq_per_step, rows, HEAD_DIM), seq), cache_spec, cache_spec],
        out_shape=[jax.ShapeDtypeStruct((bsz, rows, HEAD_DIM), BF16),
                   jax.ShapeDtypeStruct((bsz, kvw, ncache), F32), jax.ShapeDtypeStruct((bsz, kvw, ncache), F32)],
        compiler_params=_cparams(("arbitrary",)),
        name="attn_sample",
    )(sink_rows, qr, feature_major(k_cache), k_new, feature_major(v_cache), v_new)
    position_major = lambda t: t.reshape(bsz, N_KV_HEADS, HEAD_DIM, ncache).transpose(0, 3, 1, 2)
    ao = o.reshape(bsz, N_KV_HEADS, group, qlen, HEAD_DIM).transpose(0, 3, 1, 2, 4).reshape(bsz * qlen, -1)
    return ao, position_major(kt_next), position_major(vt_next)


CONV_HALO = 32
CONV_ROWS = 64
CONV_COLS = 256


def _conv_prompt_kernel(uc_ref, up_ref, w_ref, b_ref, y_ref, sh_ref):
    tile = uc_ref.shape[0]
    n = pl.program_id(1)
    halo = jnp.where(n > 0, up_ref[...], 0.0)
    sh_ref[0, :CONV_HALO, :] = halo
    sh_ref[0, CONV_HALO:, :] = uc_ref[...]
    keep = tile + CONV_HALO - SUBLANES
    for r in range(1, SUBLANES):
        sh_ref[r, :keep, :] = sh_ref[0, r:r + keep, :]
    first = CONV_HALO - (CONV_WIDTH - 1)
    ch = uc_ref.shape[1]

    def rows_step(rc, carry):
        r0 = pl.multiple_of(rc * CONV_ROWS, CONV_ROWS)
        for cc in range(ch // CONV_COLS):
            cs = slice(cc * CONV_COLS, (cc + 1) * CONV_COLS)
            acc = jnp.zeros((CONV_ROWS, CONV_COLS), F32)
            for j in range(CONV_WIDTH):
                off = first + j
                a, r = off // SUBLANES, off % SUBLANES
                acc = acc + sh_ref[r, pl.ds(r0 + a * SUBLANES, CONV_ROWS), cs] * w_ref[j:j + 1, cs]
            y_ref[pl.ds(r0, CONV_ROWS), cs] = acc + b_ref[:, cs]
        return carry

    lax.fori_loop(0, tile // CONV_ROWS, rows_step, 0)


def _conv_prompt(u, w, b, bsz, slen, tile=256):
    ch = u.shape[1]
    nt = slen // tile
    per = tile // CONV_HALO
    return pl.pallas_call(
        _conv_prompt_kernel,
        grid=(bsz, nt),
        in_specs=[
            pl.BlockSpec((tile, ch), lambda bb, n: (bb * nt + n, 0)),
            pl.BlockSpec((CONV_HALO, ch), lambda bb, n: (jnp.maximum((bb * nt + n) * per - 1, 0), 0)),
            pl.BlockSpec((CONV_WIDTH, ch), lambda bb, n: (0, 0)),
            pl.BlockSpec((1, ch), lambda bb, n: (0, 0)),
        ],
        out_specs=pl.BlockSpec((tile, ch), lambda bb, n: (bb * nt + n, 0)),
        out_shape=jax.ShapeDtypeStruct((bsz * slen, ch), F32),
        scratch_shapes=[pltpu.VMEM((SUBLANES, tile + CONV_HALO, ch), F32)],
        compiler_params=_cparams(("arbitrary", "arbitrary")),
        name="conv_prompt",
    )(u, u, w, b.reshape(1, ch))


def _conv_sample_kernel(sc_ref, un_ref, w_ref, b_ref, y_ref):
    nctx = sc_ref.shape[0]
    qlen = un_ref.shape[0]
    ch = w_ref.shape[1]

    def row(t):
        return sc_ref[t] if t < nctx else un_ref[t - nctx]

    for i in range(qlen):
        acc = jnp.zeros((sc_ref.shape[1], ch), F32)
        for j in range(CONV_WIDTH):
            acc = acc + row(i + j) * w_ref[j:j + 1, :]
        y_ref[i] = acc + b_ref[...]


def _conv_sample(state_t, u_new_t, w, b, seq_per_step=16):
    nctx, bsz, ch = state_t.shape
    qlen = u_new_t.shape[0]
    seq = lambda i: (0, i, 0)
    return pl.pallas_call(
        _conv_sample_kernel,
        grid=(bsz // seq_per_step,),
        in_specs=[
            pl.BlockSpec((nctx, seq_per_step, ch), seq),
            pl.BlockSpec((qlen, seq_per_step, ch), seq),
            pl.BlockSpec((CONV_WIDTH, ch), lambda i: (0, 0)),
            pl.BlockSpec((1, ch), lambda i: (0, 0)),
        ],
        out_specs=pl.BlockSpec((qlen, seq_per_step, ch), seq),
        out_shape=jax.ShapeDtypeStruct((qlen, bsz, ch), F32),
        compiler_params=_cparams(("arbitrary",)),
        name="conv_sample",
    )(state_t, u_new_t, w, b.reshape(1, ch))


def _merge_kernel(y_ref, ao_ref, ga_ref, gc_ref, x_ref, lnw_ref, lnb_ref, wco_ref, wo_ref, fw_ref, wr_ref,
                  h_ref, xn_ref, lg_ref):
    y = y_ref[...]
    mu = jnp.mean(y, axis=-1, keepdims=True)
    yc = y - mu
    var = jnp.mean(yc * yc, axis=-1, keepdims=True)
    yl = yc * lax.rsqrt(var + EPS) * lnw_ref[...] + lnb_ref[...]
    act = yl * jax.nn.sigmoid(yl)
    conv_o = jnp.dot(act.astype(BF16), wco_ref[...], preferred_element_type=F32)
    mix = ga_ref[...].astype(F32) * ao_ref[...].astype(F32) + gc_ref[...].astype(F32) * conv_o
    h = x_ref[...] + jnp.dot(mix.astype(BF16), wo_ref[...], preferred_element_type=F32)
    h_ref[...] = h
    ms = jnp.mean(h * h, axis=-1, keepdims=True)
    xn = (h * lax.rsqrt(ms + EPS) * fw_ref[...]).astype(BF16)
    xn_ref[...] = xn
    lg_ref[...] = jnp.dot(xn, wr_ref[...], preferred_element_type=F32)


def _merge(y, ao, gates, row0, x, ln_w, ln_b, wco, wo, fw, wr, tile=256):
    t, d = x.shape
    ch = y.shape[1]
    off = row0 // tile
    row = lambda i: (i, 0)
    const = lambda shape: pl.BlockSpec(shape, lambda i: (0, 0), pipeline_mode=pl.Buffered(1))
    return pl.pallas_call(
        _merge_kernel,
        grid=(t // tile,),
        in_specs=[
            pl.BlockSpec((tile, ch), row),
            pl.BlockSpec((tile, d), row),
            pl.BlockSpec((tile, d), lambda i: (i + off, 0)),
            pl.BlockSpec((tile, d), lambda i: (i + off, 1)),
            pl.BlockSpec((tile, d), row),
            const((1, ch)), const((1, ch)), const((ch, d)), const((d, d)), const((1, d)),
            const((d, LANES)),
        ],
        out_specs=[pl.BlockSpec((tile, d), row), pl.BlockSpec((tile, d), row), pl.BlockSpec((tile, LANES), row)],
        out_shape=[jax.ShapeDtypeStruct((t, d), F32), jax.ShapeDtypeStruct((t, d), BF16),
                   jax.ShapeDtypeStruct((t, LANES), F32)],
        compiler_params=_cparams(("arbitrary",)),
        name="merge",
    )(y, ao, gates, gates, x, ln_w.reshape(1, ch), ln_b.reshape(1, ch), wco, wo, fw.reshape(1, d), wr)


def _route_kernel(lp_ref, ls_ref, tri_ref, upper_ref, meta_ref, metat_ref, cnt_ref, *, n_p):
    i = pl.program_id(0)

    def body(l_ref):
        lg = l_ref[...]
        shape = lg.shape
        lane = lax.broadcasted_iota(jnp.int32, shape, 1)
        big = jnp.int32(LANES)
        is_g = (lane >= N_EXPERTS) & (lane < N_EXPERTS + N_GROUPS)
        gl = jnp.where(is_g, lg, -jnp.inf)
        gmax = jnp.max(gl, axis=-1, keepdims=True)
        g_star = jnp.min(jnp.where(gl == gmax, lane - N_EXPERTS, big), axis=-1, keepdims=True)
        pg_star = 1.0 / jnp.sum(jnp.exp(gl - gmax), axis=-1, keepdims=True)
        in_g = (lane < N_EXPERTS) & (_div_pow2(lane, EXPERTS_PER_GROUP) == g_star)
        el = jnp.where(in_g, lg, -jnp.inf)
        v1 = jnp.max(el, axis=-1, keepdims=True)
        i1 = jnp.min(jnp.where(el == v1, lane, big), axis=-1, keepdims=True)
        el2 = jnp.where(lane == i1, -jnp.inf, el)
        v2 = jnp.max(el2, axis=-1, keepdims=True)
        i2 = jnp.min(jnp.where(el2 == v2, lane, big), axis=-1, keepdims=True)
        e2 = jnp.exp(v2 - v1)
        w1 = pg_star * (1.0 / (1.0 + e2))
        w2 = pg_star * (e2 / (1.0 + e2))
        oh1 = lane == i1
        oh2 = lane == i2
        cnt = jnp.where(oh1 | oh2, 1.0, 0.0)
        before = jnp.dot(tri_ref[...], cnt.astype(BF16), preferred_element_type=F32)
        tot = jnp.sum(cnt, axis=0, keepdims=True)
        runs = jnp.floor((tot + (CHUNK_ROWS - 1)) * (1.0 / CHUNK_ROWS))
        runs8 = jnp.broadcast_to(runs, (SUBLANES, LANES)).astype(BF16)
        start = jnp.dot(runs8, upper_ref[...], preferred_element_type=F32)[0:1] * CHUNK_ROWS
        slot = before + start
        s1 = jnp.sum(jnp.where(oh1, slot, 0.0), axis=-1, keepdims=True)
        s2 = jnp.sum(jnp.where(oh2, slot, 0.0), axis=-1, keepdims=True)
        cols = [i1.astype(F32), i2.astype(F32), w1, w2, s1, s2]
        meta = jnp.zeros(shape, F32)
        for c, val in enumerate(cols):
            meta = jnp.where(lane == c, val, meta)
        meta_ref[...] = meta
        metat_ref[...] = meta.T[:META_FIELDS]
        cnt_ref[...] = tot

    @pl.when(i < n_p)
    def _():
        body(lp_ref)

    @pl.when(i >= n_p)
    def _():
        body(ls_ref)


def _route(lg_p, lg_s):
    tile = TOKEN_BLOCK
    tp, ts = lg_p.shape[0], lg_s.shape[0]
    n_p, n_s = tp // tile, ts // tile
    nb = n_p + n_s
    tri = (jnp.arange(tile)[:, None] > jnp.arange(tile)[None, :]).astype(BF16)
    upper = (jnp.arange(LANES)[:, None] < jnp.arange(LANES)[None, :]).astype(BF16)
    return pl.pallas_call(
        functools.partial(_route_kernel, n_p=n_p),
        grid=(nb,),
        in_specs=[
            pl.BlockSpec((tile, LANES), lambda i: (jnp.minimum(i, n_p - 1), 0)),
            pl.BlockSpec((tile, LANES), lambda i: (jnp.maximum(i - n_p, 0), 0)),
            pl.BlockSpec((tile, tile), lambda i: (0, 0)),
            pl.BlockSpec((LANES, LANES), lambda i: (0, 0)),
        ],
        out_specs=[pl.BlockSpec((tile, LANES), lambda i: (i, 0)),
                   pl.BlockSpec((None, META_FIELDS, tile), lambda i: (i, 0, 0)),
                   pl.BlockSpec((None, 1, LANES), lambda i: (i, 0, 0))],
        out_shape=[jax.ShapeDtypeStruct((tp + ts, LANES), F32),
                   jax.ShapeDtypeStruct((nb, META_FIELDS, tile), F32),
                   jax.ShapeDtypeStruct((nb, 1, LANES), F32)],
        compiler_params=_cparams(("arbitrary",)),
        name="route",
    )(lg_p, lg_s, tri, upper)


def _plan_sizes(n_tok):
    nb = n_tok // TOKEN_BLOCK
    pad_per_block = N_EXPERTS * (CHUNK_ROWS - 1)
    max_chunks = _round_up(-(-(TOKEN_BLOCK * TOP_K + pad_per_block) // CHUNK_ROWS), SUBLANES)
    n_rows = _round_up(n_tok * TOP_K + nb * pad_per_block + N_EXPERTS * (ROW_PAD - 1), ROW_PAD)
    n_items = (n_rows + N_EXPERTS * (EXPERT_ROWS - ROW_PAD)) // EXPERT_ROWS
    max_zero = N_EXPERTS * (ROW_PAD // CHUNK_ROWS)
    return nb, max_chunks, n_rows, n_items, max_zero


def _dispatch_plan(counts, max_chunks, n_items, max_zero):
    i32 = jnp.int32
    c = counts[:, 0, :N_EXPERTS].astype(i32)
    run = (c + CHUNK_ROWS - 1) // CHUNK_ROWS
    run_end = jnp.cumsum(run, axis=1)
    src_start = run_end - run
    nchunk = run_end[:, -1]
    seg = jnp.sum(run, axis=0) * CHUNK_ROWS
    padded = (seg + ROW_PAD - 1) // ROW_PAD * ROW_PAD
    pad_end = jnp.cumsum(padded)
    pad_start = pad_end - padded
    dst_start = pad_start[None, :] // CHUNK_ROWS + (jnp.cumsum(run, axis=0) - run)
    jj = jnp.arange(max_chunks, dtype=i32)[None, :, None]
    in_run = (src_start[:, None, :] <= jj) & (jj < run_end[:, None, :])
    chunk_dst = jnp.sum(jnp.where(in_run, (dst_start - src_start)[:, None, :] + jj, 0), axis=-1) * CHUNK_ROWS
    tail = (padded - seg) // CHUNK_ROWS
    tail_end = jnp.cumsum(tail)
    tail_start = tail_end - tail
    z = jnp.arange(max_zero, dtype=i32)[:, None]
    in_tail = (tail_start[None, :] <= z) & (z < tail_end[None, :])
    zero_base = (pad_start + seg) // CHUNK_ROWS - tail_start
    zero_dst = jnp.sum(jnp.where(in_tail, zero_base[None, :] + z, 0), axis=-1) * CHUNK_ROWS
    nzero = tail_end[-1]
    items_e = (padded + EXPERT_ROWS - 1) // EXPERT_ROWS
    item_end = jnp.cumsum(items_e)
    item_start = item_end - items_e
    w = jnp.arange(n_items, dtype=i32)[:, None]
    in_item = (item_start[None, :] <= w) & (w < item_end[None, :])
    k = w - item_start[None, :]
    e_ids = jnp.arange(N_EXPERTS, dtype=i32)
    last_e = jnp.max(jnp.where(items_e > 0, e_ids, 0))
    item_e = jnp.where(w[:, 0] < item_end[-1], jnp.sum(jnp.where(in_item, e_ids[None, :], 0), axis=-1), last_e)
    item_row0 = jnp.sum(jnp.where(in_item, pad_start[None, :] + k * EXPERT_ROWS, 0), axis=-1)
    item_rows = jnp.sum(jnp.where(in_item, jnp.clip(padded[None, :] - k * EXPERT_ROWS, 0, EXPERT_ROWS), 0), axis=-1)
    item_nsub = item_rows // ROW_PAD
    return (chunk_dst.reshape(-1).astype(i32), nchunk.astype(i32), zero_dst.astype(i32),
            nzero.reshape(1).astype(i32), item_e.astype(i32), item_row0.astype(i32), item_nsub.astype(i32))


def _scatter_kernel(cd_ref, nc_ref, zd_ref, nz_ref, mt_ref, xp_ref, xs_ref, out_hbm, loc_ref, zero_ref, sem,
                    zero_sem, *, n_p, max_chunks):
    i = pl.program_id(0)
    buf = i % 2
    n_slots = loc_ref.shape[1]
    d = xp_ref.shape[1]
    mt = mt_ref[...]
    w1, w2 = mt[2:3, :], mt[3:4, :]
    s1, s2 = mt[4:5, :].astype(jnp.int32), mt[5:6, :].astype(jnp.int32)
    slot = lax.broadcasted_iota(jnp.int32, (n_slots, mt.shape[1]), 0)
    p1 = slot == s1
    p2 = slot == s2
    perm = jnp.where(p1 | p2, 1.0, 0.0).astype(BF16)
    wcol = jnp.sum(jnp.where(p1, w1, 0.0) + jnp.where(p2, w2, 0.0), axis=-1, keepdims=True)
    half = d // 2
    loc_ref[buf, :, half:] = pltpu.bitcast(jnp.broadcast_to(wcol, (n_slots, LANES)), jnp.uint32)

    def sort_rows(x_ref):
        rows = jnp.dot(perm, x_ref[...], preferred_element_type=F32)
        loc_ref[buf, :, :half] = _pack_pairs(rows[:, :half], rows[:, half:])

    @pl.when(i < n_p)
    def _():
        sort_rows(xp_ref)

    @pl.when(i >= n_p)
    def _():
        sort_rows(xs_ref)

    def rows_at(r):
        return pl.ds(pl.multiple_of(r, CHUNK_ROWS), CHUNK_ROWS)

    def chunk_copy(blk, j):
        return pltpu.make_async_copy(loc_ref.at[blk % 2, rows_at(j * CHUNK_ROWS)],
                                     out_hbm.at[rows_at(cd_ref[blk * max_chunks + j])], sem.at[blk % 2])

    def zero_copy(z):
        return pltpu.make_async_copy(zero_ref, out_hbm.at[rows_at(zd_ref[z])], zero_sem)

    def loop(n, fn):
        def body(j, carry):
            fn(j)
            return carry
        lax.fori_loop(0, n, body, 0)

    @pl.when(i > 0)
    def _():
        loop(nc_ref[i - 1], lambda j: chunk_copy(i - 1, j).wait())

    loop(nc_ref[i], lambda j: chunk_copy(i, j).start())

    @pl.when(i == pl.num_programs(0) - 1)
    def _():
        zero_ref[...] = jnp.zeros_like(zero_ref)
        loop(nz_ref[0], lambda z: zero_copy(z).start())
        loop(nc_ref[i], lambda j: chunk_copy(i, j).wait())
        loop(nz_ref[0], lambda z: zero_copy(z).wait())


def _scatter_rows(xn_p, xn_s, metat, chunk_dst, nchunk, zero_dst, nzero, max_chunks, n_rows):
    tile = TOKEN_BLOCK
    tp, d = xn_p.shape
    ts = xn_s.shape[0]
    n_p, n_s = tp // tile, ts // tile
    n_slots = max_chunks * CHUNK_ROWS
    width = d // 2 + LANES
    u32 = jnp.uint32
    return pl.pallas_call(
        functools.partial(_scatter_kernel, n_p=n_p, max_chunks=max_chunks),
        grid_spec=pltpu.PrefetchScalarGridSpec(
            num_scalar_prefetch=4,
            grid=(n_p + n_s,),
            in_specs=[
                pl.BlockSpec((None, META_FIELDS, tile), lambda i, *_: (i, 0, 0)),
                pl.BlockSpec((tile, d), lambda i, *_: (jnp.minimum(i, n_p - 1), 0)),
                pl.BlockSpec((tile, d), lambda i, *_: (jnp.maximum(i - n_p, 0), 0)),
            ],
            out_specs=pl.BlockSpec(memory_space=pl.ANY),
            scratch_shapes=[pltpu.VMEM((2, n_slots, width), u32), pltpu.VMEM((CHUNK_ROWS, width), u32),
                            pltpu.SemaphoreType.DMA((2,)), pltpu.SemaphoreType.DMA(())],
        ),
        out_shape=jax.ShapeDtypeStruct((n_rows, width), u32),
        compiler_params=_cparams(("arbitrary",)),
        name="scatter_rows",
    )(chunk_dst, nchunk, zero_dst, nzero, metat, xn_p, xn_s)


def _expert_kernel(e_ref, row0_ref, nsub_ref, xs_hbm, wg_ref, wu_ref, wd_ref, out_hbm,
                   xf_ref, xb_ref, acc_ref, ob_ref, pend_ref, in_sem, out_sem):
    w = pl.program_id(0)
    c = pl.program_id(1)
    n_items = pl.num_programs(0)
    last_c = pl.num_programs(1) - 1
    nsub = nsub_ref[w]
    d = acc_ref.shape[1]
    half = d // 2
    slot = w % 2

    def sub_rows(s):
        return pl.ds(pl.multiple_of(s * ROW_PAD, ROW_PAD), ROW_PAD)

    def hbm_rows(item, s):
        return pl.ds(pl.multiple_of(row0_ref[item] + s * ROW_PAD, ROW_PAD), ROW_PAD)

    def in_copy(item, s):
        return pltpu.make_async_copy(xs_hbm.at[hbm_rows(item, s)], xf_ref.at[item % 2, sub_rows(s)],
                                     in_sem.at[item % 2])

    def out_copy(item, s):
        return pltpu.make_async_copy(ob_ref.at[sub_rows(s)], out_hbm.at[hbm_rows(item, s)], out_sem)

    def loop(n, fn):
        def body(s, carry):
            fn(s)
            return carry
        lax.fori_loop(0, n, body, 0)

    @pl.when(c == 0)
    def _():
        @pl.when(w == 0)
        def _():
            pend_ref[0] = 0
            pend_ref[1] = 0
            loop(nsub, lambda s: in_copy(w, s).start())

        loop(nsub, lambda s: in_copy(w, s).wait())
        nxt = jnp.minimum(w + 1, n_items - 1)

        @pl.when(w + 1 < n_items)
        def _():
            loop(nsub_ref[nxt], lambda s: in_copy(nxt, s).start())

    def drain_out():
        prev = pend_ref[1]
        loop(pend_ref[0], lambda s: out_copy(prev, s).wait())
        pend_ref[0] = 0

    for n in range(1, EXPERT_ROWS // ROW_PAD + 1):
        m = n * ROW_PAD

        @pl.when(nsub == n)
        def _(m=m):
            @pl.when(c == 0)
            def _():
                acc_ref[:m, :] = jnp.zeros((m, d), F32)
                xa, xb = _unpack_pairs(xf_ref[slot, :m, :half])
                xb_ref[:m, :half] = xa.astype(BF16)
                xb_ref[:m, half:] = xb.astype(BF16)

            x = xb_ref[:m, :]
            g = jnp.dot(x, wg_ref[...].astype(BF16), preferred_element_type=F32)
            u = jnp.dot(x, wu_ref[...].astype(BF16), preferred_element_type=F32)
            hmid = (g * jax.nn.sigmoid(g) * u).astype(BF16)
            acc_ref[:m, :] += jnp.dot(hmid, wd_ref[...].astype(BF16), preferred_element_type=F32)

            @pl.when(c == last_c)
            def _():
                drain_out()
                mix_w = pltpu.bitcast(xf_ref[slot, :m, half:half + 1], F32)
                y = (acc_ref[:m, :] * mix_w).astype(BF16).astype(F32)
                ob_ref[:m, :] = _pack_pairs(y[:, :half], y[:, half:])
                loop(nsub, lambda s: out_copy(w, s).start())
                pend_ref[0] = nsub
                pend_ref[1] = w

    @pl.when((w == n_items - 1) & (c == last_c))
    def _():
        drain_out()


def _experts(xs, item_e, item_row0, item_nsub, w_gate, w_up, w_down):
    n_rows, width = xs.shape
    d = w_gate.shape[1]
    assert width == d // 2 + LANES
    d_exp = w_gate.shape[2]
    nc = d_exp // D_CHUNK
    n_items = item_e.shape[0]

    def chunk_of(w, c, nsub_ref):
        return jnp.where(nsub_ref[w] > 0, c, nc - 1)

    return pl.pallas_call(
        _expert_kernel,
        grid_spec=pltpu.PrefetchScalarGridSpec(
            num_scalar_prefetch=3,
            grid=(n_items, nc),
            in_specs=[
                pl.BlockSpec(memory_space=pl.ANY),
                pl.BlockSpec((None, d, D_CHUNK), lambda w, c, e, r, n: (e[w], 0, chunk_of(w, c, n))),
                pl.BlockSpec((None, d, D_CHUNK), lambda w, c, e, r, n: (e[w], 0, chunk_of(w, c, n))),
                pl.BlockSpec((None, D_CHUNK, d), lambda w, c, e, r, n: (e[w], chunk_of(w, c, n), 0)),
            ],
            out_specs=pl.BlockSpec(memory_space=pl.ANY),
            scratch_shapes=[
                pltpu.VMEM((2, EXPERT_ROWS, width), jnp.uint32),
                pltpu.VMEM((EXPERT_ROWS, d), BF16),
                pltpu.VMEM((EXPERT_ROWS, d), F32),
                pltpu.VMEM((EXPERT_ROWS, d // 2), jnp.uint32),
                pltpu.SMEM((2,), jnp.int32),
                pltpu.SemaphoreType.DMA((2,)),
                pltpu.SemaphoreType.DMA(()),
            ],
        ),
        out_shape=jax.ShapeDtypeStruct((n_rows, d // 2), jnp.uint32),
        compiler_params=_cparams(("arbitrary", "arbitrary")),
        name="experts",
    )(item_e, item_row0, item_nsub, xs, w_gate, w_up, w_down)


COMBINE_SPLIT = 2
SLOT_STEP = 256


def _combine_kernel(cd_ref, nc_ref, meta_ref, hp_ref, hs_ref, yb_hbm, yp_ref, ys_ref, loc_ref, sem,
                    *, n_p, max_chunks):
    i = pl.program_id(0)
    j = pl.program_id(1)
    nb = pl.num_programs(0)
    buf = i % 2
    n_slots, half = loc_ref.shape[1], loc_ref.shape[2]
    d = 2 * half

    def rows_at(r):
        return pl.ds(pl.multiple_of(r, CHUNK_ROWS), CHUNK_ROWS)

    def chunk_copy(blk, c):
        return pltpu.make_async_copy(yb_hbm.at[rows_at(cd_ref[blk * max_chunks + c])],
                                     loc_ref.at[blk % 2, rows_at(c * CHUNK_ROWS)], sem.at[blk % 2])

    def fetch(blk):
        n = nc_ref[blk]

        def start(c, carry):
            chunk_copy(blk, c).start()
            return carry
        lax.fori_loop(0, n, start, 0)

        def clear(c, carry):
            loc_ref[blk % 2, rows_at(c * CHUNK_ROWS), :] = jnp.zeros((CHUNK_ROWS, half), jnp.uint32)
            return carry
        lax.fori_loop(n, max_chunks, clear, 0)

    @pl.when(j == 0)
    def _():
        @pl.when(i == 0)
        def _():
            fetch(i)

        def wait(c, carry):
            chunk_copy(i, c).wait()
            return carry
        lax.fori_loop(0, nc_ref[i], wait, 0)

        @pl.when(i + 1 < nb)
        def _():
            fetch(jnp.minimum(i + 1, nb - 1))

    meta = meta_ref[...]
    s1 = meta[:, 4:5].astype(jnp.int32)
    s2 = meta[:, 5:6].astype(jnp.int32)
    rows = meta.shape[0]
    f = jnp.zeros((rows, d), F32)
    for r0 in range(0, n_slots, SLOT_STEP):
        slot = lax.broadcasted_iota(jnp.int32, (rows, SLOT_STEP), 1) + r0
        pt = jnp.where((slot == s1) | (slot == s2), 1.0, 0.0).astype(BF16)
        ya, yb = _unpack_pairs(loc_ref[buf, r0:r0 + SLOT_STEP, :])
        f = f + jnp.dot(pt, jnp.concatenate([ya, yb], axis=1).astype(BF16), preferred_element_type=F32)
    g = i * pl.num_programs(1) + j

    @pl.when(g < n_p)
    def _():
        yp_ref[...] = hp_ref[...] + f

    @pl.when(g >= n_p)
    def _():
        ys_ref[...] = hs_ref[...] + f


def _combine(h_p, h_s, meta, chunk_dst, nchunk, yb, max_chunks):
    tp, d = h_p.shape
    ts = h_s.shape[0]
    tile = TOKEN_BLOCK // COMBINE_SPLIT
    n_p, n_s = tp // tile, ts // tile
    nb = (tp + ts) // TOKEN_BLOCK
    n_slots = max_chunks * CHUNK_ROWS
    assert n_slots % SLOT_STEP == 0
    blk = lambda i, j: i * COMBINE_SPLIT + j
    p_idx = lambda i, j, *_: (jnp.minimum(blk(i, j), n_p - 1), 0)
    s_idx = lambda i, j, *_: (jnp.maximum(blk(i, j) - n_p, 0), 0)
    return pl.pallas_call(
        functools.partial(_combine_kernel, n_p=n_p, max_chunks=max_chunks),
        grid_spec=pltpu.PrefetchScalarGridSpec(
            num_scalar_prefetch=2,
            grid=(nb, COMBINE_SPLIT),
            in_specs=[
                pl.BlockSpec((tile, LANES), lambda i, j, *_: (blk(i, j), 0)),
                pl.BlockSpec((tile, d), p_idx),
                pl.BlockSpec((tile, d), s_idx),
                pl.BlockSpec(memory_space=pl.ANY),
            ],
            out_specs=[pl.BlockSpec((tile, d), p_idx), pl.BlockSpec((tile, d), s_idx)],
            scratch_shapes=[pltpu.VMEM((2, n_slots, d // 2), jnp.uint32), pltpu.SemaphoreType.DMA((2,))],
        ),
        out_shape=[jax.ShapeDtypeStruct((tp, d), F32), jax.ShapeDtypeStruct((ts, d), F32)],
        compiler_params=_cparams(("arbitrary", "arbitrary")),
        name="combine",
    )(chunk_dst, nchunk, meta, h_p, h_s, yb)


def _layer(x_p, x_s, ck, cv, sc, attn_norm_w, w_in, b_gate, q_norm_w, k_norm_w, sinks, conv_dw_w, conv_dw_b,
           conv_ln_w, conv_ln_b, w_conv_out, w_out, ffn_norm_w, router_group, router_expert, w_gate, w_up,
           w_down):
    bsz, slen, d = x_p.shape
    dbs, dlen, _ = x_s.shape
    tp, ts = bsz * slen, dbs * dlen
    conv_ch = conv_dw_w.shape[1]
    kvw = N_KV_HEADS * HEAD_DIM
    xp2, xs2 = x_p.reshape(tp, d), x_s.reshape(ts, d)

    xn = _prenorm(xp2, xs2, attn_norm_w)
    q, k, v, u, gates = _project(xn, w_in, b_gate, q_norm_w, k_norm_w, d, conv_ch)

    ao_p = _attn_prompt(q, k, v, sinks, bsz, slen)
    ao_s, ck_next, cv_next = _attn_sample(q[tp:], ck, k[tp:], cv, v[tp:], sinks)

    y_p = _conv_prompt(u, conv_dw_w, conv_dw_b, bsz, slen)
    sc_t = sc.transpose(1, 0, 2)
    u_s_t = u[tp:].reshape(dbs, dlen, conv_ch).transpose(1, 0, 2)
    y_s = _conv_sample(sc_t, u_s_t, conv_dw_w, conv_dw_b).transpose(1, 0, 2).reshape(ts, conv_ch)

    wco = w_conv_out.astype(BF16)
    wo = w_out.astype(BF16)
    wr = jnp.concatenate([router_expert, router_group,
                          jnp.zeros((d, LANES - N_EXPERTS - N_GROUPS), router_expert.dtype)], axis=1).astype(BF16)
    margs = (conv_ln_w, conv_ln_b, wco, wo, ffn_norm_w, wr)
    h_p, xn_p, lg_p = _merge(y_p, ao_p, gates, 0, xp2, *margs)
    h_s, xn_s, lg_s = _merge(y_s, ao_s, gates, tp, xs2, *margs)

    meta, metat, counts = _route(lg_p, lg_s)
    _, max_chunks, n_rows, n_items, max_zero = _plan_sizes(tp + ts)
    chunk_dst, nchunk, zero_dst, nzero, item_e, item_row0, item_nsub = _dispatch_plan(
        counts, max_chunks, n_items, max_zero)
    xs_rows = _scatter_rows(xn_p, xn_s, metat, chunk_dst, nchunk, zero_dst, nzero, max_chunks, n_rows)
    yb = _experts(xs_rows, item_e, item_row0, item_nsub, w_gate, w_up, w_down)
    out_p, out_s = _combine(h_p, h_s, meta, chunk_dst, nchunk, yb, max_chunks)

    def seq_tails(a, n):
        return jnp.stack([lax.slice_in_dim(a, (b + 1) * slen - n, (b + 1) * slen, axis=0) for b in range(bsz)])

    new_p = (seq_tails(k, WINDOW).reshape(bsz, WINDOW, N_KV_HEADS, HEAD_DIM),
             seq_tails(v, WINDOW).reshape(bsz, WINDOW, N_KV_HEADS, HEAD_DIM),
             seq_tails(u, CONV_WIDTH - 1))
    sc_next = jnp.concatenate([sc_t[dlen:], u_s_t], axis=0).transpose(1, 0, 2)
    new_s = (ck_next, cv_next, sc_next)
    return out_p.reshape(bsz, slen, d), out_s.reshape(dbs, dlen, d), new_p, new_s


def kernel(x_prompt, x_sample, cache_k, cache_v, state_conv, attn_norm_w, w_in, b_gate, q_norm_w, k_norm_w,
           sinks, conv_dw_w, conv_dw_b, conv_ln_w, conv_ln_b, w_conv_out, w_out, ffn_norm_w, router_group,
           router_expert, w_gate, w_up, w_down):
    x_p, x_s = x_prompt, x_sample
    kp, vp, cp, ks_, vs_, cs_ = [], [], [], [], [], []
    for l in range(cache_k.shape[0]):
        x_p, x_s, new_p, new_s = _layer(
            x_p, x_s, cache_k[l], cache_v[l], state_conv[l], attn_norm_w[l], w_in[l], b_gate[l], q_norm_w[l],
            k_norm_w[l], sinks[l], conv_dw_w[l], conv_dw_b[l], conv_ln_w[l], conv_ln_b[l], w_conv_out[l],
            w_out[l], ffn_norm_w[l], router_group[l], router_expert[l], w_gate[l], w_up[l], w_down[l])
        kp.append(new_p[0]); vp.append(new_p[1]); cp.append(new_p[2])
        ks_.append(new_s[0]); vs_.append(new_s[1]); cs_.append(new_s[2])
    return (x_p, x_s, jnp.stack(kp), jnp.stack(vp), jnp.stack(cp),
            jnp.stack(ks_), jnp.stack(vs_), jnp.stack(cs_))
```

```python
import functools

import jax
import jax.numpy as jnp
from jax import lax
from jax.experimental import pallas as pl
from jax.experimental.pallas import tpu as pltpu

F32 = jnp.float32
BF16 = jnp.bfloat16
EPS = 1e-6
NEG_INF = -1e30

HEAD_DIM = 64
N_KV_HEADS = 4
WINDOW = 128
CONV_WIDTH = 31
N_GROUPS = 4
EXPERTS_PER_GROUP = 8
N_EXPERTS = N_GROUPS * EXPERTS_PER_GROUP
TOP_K = 2

LANES = 128
SUBLANES = 8
MXU_DIM = 256
VMEM_LIMIT = 56 * 1024 * 1024

TOKEN_BLOCK = 512
CHUNK_ROWS = SUBLANES
ROW_PAD = 128
EXPERT_ROWS = 768
D_CHUNK = 512
META_FIELDS = 8


def _cparams(sem):
    return pltpu.CompilerParams(dimension_semantics=sem, vmem_limit_bytes=VMEM_LIMIT)


def _log2(n):
    assert n > 0 and n & (n - 1) == 0, n
    return n.bit_length() - 1


def _div_pow2(x, n):
    return lax.shift_right_logical(x, jnp.int32(_log2(n)))


def _mod_pow2(x, n):
    _log2(n)
    return x & jnp.int32(n - 1)


def _round_up(x, m):
    return (x + m - 1) // m * m


HIGH_HALF = 0xFFFF0000


def _pack_pairs(a, b):
    ua = pltpu.bitcast(a, jnp.uint32) & jnp.uint32(HIGH_HALF)
    return ua | (pltpu.bitcast(b, jnp.uint32) >> 16)


def _unpack_pairs(w):
    return (pltpu.bitcast(w & jnp.uint32(HIGH_HALF), F32), pltpu.bitcast(w << 16, F32))


def _prenorm_kernel(xp_ref, xs_ref, w_ref, o_ref, *, n_p):
    i = pl.program_id(0)

    def body(x_ref):
        x = x_ref[...]
        ms = jnp.mean(x * x, axis=-1, keepdims=True)
        o_ref[...] = (x * lax.rsqrt(ms + EPS) * w_ref[...]).astype(o_ref.dtype)

    @pl.when(i < n_p)
    def _():
        body(xp_ref)

    @pl.when(i >= n_p)
    def _():
        body(xs_ref)


def _prenorm(xp, xs, w, tile=512):
    tp, d = xp.shape
    ts = xs.shape[0]
    n_p, n_s = tp // tile, ts // tile
    return pl.pallas_call(
        functools.partial(_prenorm_kernel, n_p=n_p),
        grid=(n_p + n_s,),
        in_specs=[
            pl.BlockSpec((tile, d), lambda i: (jnp.minimum(i, n_p - 1), 0)),
            pl.BlockSpec((tile, d), lambda i: (jnp.maximum(i - n_p, 0), 0)),
            pl.BlockSpec((1, d), lambda i: (0, 0)),
        ],
        out_specs=pl.BlockSpec((tile, d), lambda i: (i, 0)),
        out_shape=jax.ShapeDtypeStruct((tp + ts, d), BF16),
        compiler_params=_cparams(("arbitrary",)),
        name="prenorm",
    )(xp, xs, w.reshape(1, d))


def _head_rms(z, hw, bd_ref):
    ss = z * z
    hi = ss.astype(BF16)
    lo = (ss - hi.astype(F32)).astype(BF16)
    bd = bd_ref[...]
    tot = jnp.dot(hi, bd, preferred_element_type=F32) + jnp.dot(lo, bd, preferred_element_type=F32)
    return z * lax.rsqrt(tot * (1.0 / HEAD_DIM) + EPS) * hw


def _load_w(i, w_ref, wb_ref):
    @pl.when(i == 0)
    def _():
        wb_ref[...] = w_ref[...].astype(BF16)


PROJ_SUB = 544


def _row_groups(n_rows):
    sub = PROJ_SUB if n_rows % PROJ_SUB == 0 else n_rows
    return [slice(r0, r0 + sub) for r0 in range(0, n_rows, sub)]


def _proj_q_kernel(x_ref, w_ref, hw_ref, bd_ref, o_ref, wb_ref):
    _load_w(pl.program_id(1), w_ref, wb_ref)
    for rs in _row_groups(x_ref.shape[0]):
        z = jnp.dot(x_ref[rs, :], wb_ref[...], preferred_element_type=F32)
        for c in range(z.shape[1] // MXU_DIM):
            sl = slice(c * MXU_DIM, (c + 1) * MXU_DIM)
            o_ref[rs, sl] = _head_rms(z[:, sl], hw_ref[...], bd_ref).astype(o_ref.dtype)


def _proj_kv_kernel(x_ref, w_ref, hw_ref, bd_ref, k_ref, v_ref, wb_ref):
    _load_w(pl.program_id(1), w_ref, wb_ref)
    kw = k_ref.shape[1]
    for rs in _row_groups(x_ref.shape[0]):
        z = jnp.dot(x_ref[rs, :], wb_ref[...], preferred_element_type=F32)
        k_ref[rs, :] = _head_rms(z[:, :kw], hw_ref[...], bd_ref)
        v_ref[rs, :] = z[:, kw:]


def _proj_glu_kernel(x_ref, wa_ref, wg_ref, o_ref, wab_ref, wgb_ref):
    i = pl.program_id(1)
    _load_w(i, wa_ref, wab_ref)
    _load_w(i, wg_ref, wgb_ref)
    for rs in _row_groups(x_ref.shape[0]):
        x = x_ref[rs, :]
        a = jnp.dot(x, wab_ref[...], preferred_element_type=F32)
        g = jnp.dot(x, wgb_ref[...], preferred_element_type=F32)
        o_ref[rs, :] = a * jax.nn.sigmoid(g)


def _proj_gate_kernel(x_ref, w_ref, b_ref, o_ref, wb_ref):
    _load_w(pl.program_id(1), w_ref, wb_ref)
    for rs in _row_groups(x_ref.shape[0]):
        z = jnp.dot(x_ref[rs, :], wb_ref[...], preferred_element_type=F32)
        o_ref[rs, :] = jax.nn.sigmoid(z + b_ref[...]).astype(o_ref.dtype)


def _row_tile(t, cap=1152):
    best = 16
    for m in range(16, cap + 1, 16):
        if t % m == 0:
            best = m
    return best


def _project(xn, w_in, b_gate, q_norm_w, k_norm_w, d_model, conv_ch):
    t, d = xn.shape
    q_w = d_model
    kv_w = N_KV_HEADS * HEAD_DIM
    tn = 2 * kv_w
    tm = _row_tile(t)
    ni = t // tm
    tm_big = _row_tile(t, cap=2304)
    ni_big = t // tm_big
    reps = MXU_DIM // HEAD_DIM
    hq = jnp.tile(q_norm_w.astype(F32) * HEAD_DIM ** -0.5, reps).reshape(1, MXU_DIM)
    hk = jnp.tile(k_norm_w.astype(F32), reps).reshape(1, MXU_DIM)
    gid = jnp.arange(MXU_DIM) // HEAD_DIM
    bd = (gid[:, None] == gid[None, :]).astype(BF16)

    x_spec = pl.BlockSpec((tm, d), lambda s, i: (i, 0))
    x_big = pl.BlockSpec((tm_big, d), lambda s, i: (i, 0))
    small = lambda shape: pl.BlockSpec(shape, lambda s, i: (0, 0))

    def w_spec(c0):
        return pl.BlockSpec((d, tn), lambda s, i: (0, c0 + s))

    wscr = pltpu.VMEM((d, tn), BF16)
    cp = _cparams(("arbitrary", "arbitrary"))

    q = pl.pallas_call(
        _proj_q_kernel, grid=(q_w // tn, ni_big),
        in_specs=[x_big, w_spec(0), small((1, MXU_DIM)), small((MXU_DIM, MXU_DIM))],
        out_specs=pl.BlockSpec((tm_big, tn), lambda s, i: (i, s)),
        out_shape=jax.ShapeDtypeStruct((t, q_w), BF16),
        scratch_shapes=[wscr], compiler_params=cp, name="proj_q",
    )(xn, w_in, hq, bd)

    c_kv = q_w // tn
    k, v = pl.pallas_call(
        _proj_kv_kernel, grid=(1, ni),
        in_specs=[x_spec, w_spec(c_kv), small((1, MXU_DIM)), small((MXU_DIM, MXU_DIM))],
        out_specs=[pl.BlockSpec((tm, kv_w), lambda s, i: (i, 0))] * 2,
        out_shape=[jax.ShapeDtypeStruct((t, kv_w), F32)] * 2,
        scratch_shapes=[wscr], compiler_params=cp, name="proj_kv",
    )(xn, w_in, hk, bd)

    c_a = c_kv + 1
    n_glu = conv_ch // tn
    u = pl.pallas_call(
        _proj_glu_kernel, grid=(n_glu, ni),
        in_specs=[x_spec, w_spec(c_a), w_spec(c_a + n_glu)],
        out_specs=pl.BlockSpec((tm, tn), lambda s, i: (i, s)),
        out_shape=jax.ShapeDtypeStruct((t, conv_ch), F32),
        scratch_shapes=[wscr, wscr], compiler_params=cp, name="proj_glu",
    )(xn, w_in, w_in)

    c_g = c_a + 2 * n_glu
    n_gate = 2 * d_model // tn
    gates = pl.pallas_call(
        _proj_gate_kernel, grid=(n_gate, ni_big),
        in_specs=[x_big, w_spec(c_g), pl.BlockSpec((1, tn), lambda s, i: (0, s))],
        out_specs=pl.BlockSpec((tm_big, tn), lambda s, i: (i, s)),
        out_shape=jax.ShapeDtypeStruct((t, 2 * d_model), BF16),
        scratch_shapes=[wscr], compiler_params=cp, name="proj_gate",
    )(xn, w_in, b_gate.reshape(1, -1))
    return q, k, v, u, gates


def _attn_prompt_kernel(sink_ref, q_ref, kc_ref, kp_ref, vc_ref, vp_ref, o_ref):
    blk = q_ref.shape[0]
    n = pl.program_id(1)
    rows = 2 * blk
    r_iota = lax.broadcasted_iota(jnp.int32, (rows, rows), 0)
    c_iota = lax.broadcasted_iota(jnp.int32, (rows, rows), 1)
    qi = jnp.where(r_iota >= blk, r_iota - blk, r_iota)
    mask = (c_iota >= qi) & (c_iota <= qi + WINDOW) & ((c_iota >= blk) | (n > 0))
    lo_lane = lax.broadcasted_iota(jnp.int32, (blk, LANES), 1) < HEAD_DIM
    top_row = lax.broadcasted_iota(jnp.int32, (rows, 1), 0) < blk
    zero = jnp.zeros((blk, LANES), q_ref.dtype)
    group = q_ref.shape[1] // (N_KV_HEADS * LANES)
    for g in range(N_KV_HEADS):
        hs = slice(g * HEAD_DIM, (g + 1) * HEAD_DIM)
        kg = jnp.concatenate([kp_ref[:, hs], kc_ref[:, hs]], axis=0)
        vg = jnp.concatenate([vp_ref[:, hs], vc_ref[:, hs]], axis=0)
        kdup = jnp.concatenate([kg, kg], axis=1).astype(BF16)
        vdup = jnp.concatenate([vg, vg], axis=1).astype(BF16)
        for p in range(group):
            pair = g * group + p
            ls = slice(pair * LANES, (pair + 1) * LANES)
            qp = q_ref[:, ls]
            lhs = jnp.concatenate([jnp.where(lo_lane, qp, zero), jnp.where(lo_lane, zero, qp)], axis=0)
            s = lax.dot_general(lhs, kdup, (((1,), (1,)), ((), ())), preferred_element_type=F32)
            s = jnp.where(mask, s, NEG_INF)
            sink = jnp.where(top_row, sink_ref[2 * pair], sink_ref[2 * pair + 1])
            m = jnp.maximum(jnp.max(s, axis=-1, keepdims=True), sink)
            e = jnp.exp(s - m)
            denom = jnp.sum(e, axis=-1, keepdims=True) + jnp.exp(sink - m)
            o2 = jnp.dot(e.astype(BF16), vdup, preferred_element_type=F32) / denom
            o_ref[:, ls] = jnp.where(lo_lane, o2[:blk], o2[blk:]).astype(o_ref.dtype)


def _attn_prompt(q, k, v, sinks, bsz, slen):
    blk = WINDOW
    nb = slen // blk
    qw, kw = q.shape[1], k.shape[1]
    cur = lambda b, n: (b * nb + n, 0)
    prev = lambda b, n: (b * nb + jnp.maximum(n - 1, 0), 0)
    return pl.pallas_call(
        _attn_prompt_kernel,
        grid=(bsz, nb),
        in_specs=[
            pl.BlockSpec(memory_space=pltpu.SMEM),
            pl.BlockSpec((blk, qw), cur),
            pl.BlockSpec((blk, kw), cur), pl.BlockSpec((blk, kw), prev),
            pl.BlockSpec((blk, kw), cur), pl.BlockSpec((blk, kw), prev),
        ],
        out_specs=pl.BlockSpec((blk, qw), cur),
        out_shape=jax.ShapeDtypeStruct((bsz * slen, qw), BF16),
        compiler_params=_cparams(("arbitrary", "arbitrary")),
        name="attn_prompt",
    )(sinks.astype(F32), q, k, k, v, v)


def _attn_sample_kernel(sink_ref, q_ref, kt_ref, kn_ref, vt_ref, vn_ref, o_ref, kt_out_ref, vt_out_ref, *, qlen):
    nseq, rows, _ = q_ref.shape
    kvw, ncache = kt_ref.shape[1], kt_ref.shape[2]
    per_kv = rows // N_KV_HEADS
    r_lane = _div_pow2(lax.broadcasted_iota(jnp.int32, (rows, kvw), 0), per_kv)
    c_lane = _div_pow2(lax.broadcasted_iota(jnp.int32, (rows, kvw), 1), HEAD_DIM)
    own = r_lane == c_lane
    mask_c = (lax.broadcasted_iota(jnp.int32, (rows, ncache), 1)
              >= _mod_pow2(lax.broadcasted_iota(jnp.int32, (rows, ncache), 0), qlen))
    mask_n = (lax.broadcasted_iota(jnp.int32, (rows, qlen), 1)
              <= _mod_pow2(lax.broadcasted_iota(jnp.int32, (rows, qlen), 0), qlen))
    keep_old = lax.broadcasted_iota(jnp.int32, (kvw, ncache), 1) < ncache - qlen
    sink = sink_ref[...]
    contract_last = (((1,), (1,)), ((), ()))

    def feature_major(new_ref):
        pad = jnp.zeros((ncache - nseq * qlen, kvw), F32)
        return jnp.concatenate([new_ref[...], pad], axis=0).T

    kn_t = feature_major(kn_ref)
    vn_t = feature_major(vn_ref)

    def shifted(old_t, new_t, b):
        placed = pltpu.roll(new_t, ncache - qlen - b * qlen, axis=1)
        return jnp.where(keep_old, pltpu.roll(old_t, ncache - qlen, axis=1), placed)

    for b in range(nseq):
        q = q_ref[b]
        qe = jnp.where(own, jnp.concatenate([q] * N_KV_HEADS, axis=1), jnp.zeros((), q.dtype))
        kt, vt = kt_ref[b], vt_ref[b]
        kn = kn_ref[b * qlen:(b + 1) * qlen, :]
        vn = vn_ref[b * qlen:(b + 1) * qlen, :]
        sc = jnp.dot(qe, kt.astype(BF16), preferred_element_type=F32)
        sn = lax.dot_general(qe, kn.astype(BF16), contract_last, preferred_element_type=F32)
        sc = jnp.where(mask_c, sc, NEG_INF)
        sn = jnp.where(mask_n, sn, NEG_INF)
        m = jnp.maximum(jnp.maximum(jnp.max(sc, axis=-1, keepdims=True), jnp.max(sn, axis=-1, keepdims=True)), sink)
        ec = jnp.exp(sc - m)
        en = jnp.exp(sn - m)
        denom = jnp.sum(ec, axis=-1, keepdims=True) + jnp.sum(en, axis=-1, keepdims=True) + jnp.exp(sink - m)
        o = (lax.dot_general(ec.astype(BF16), vt.astype(BF16), contract_last, preferred_element_type=F32)
             + jnp.dot(en.astype(BF16), vn.astype(BF16), preferred_element_type=F32))
        o = jnp.where(own, o, 0.0)
        acc = o[:, :HEAD_DIM]
        for g in range(1, N_KV_HEADS):
            acc = acc + o[:, g * HEAD_DIM:(g + 1) * HEAD_DIM]
        o_ref[b] = (acc / denom).astype(o_ref.dtype)
        kt_out_ref[b] = shifted(kt, kn_t, b)
        vt_out_ref[b] = shifted(vt, vn_t, b)


def _attn_sample(q, k_cache, k_new, v_cache, v_new, sinks, seq_per_step=8):
    bsz, ncache = k_cache.shape[:2]
    kvw = k_new.shape[1]
    qlen = k_new.shape[0] // bsz
    assert seq_per_step * qlen <= ncache
    n_heads = q.shape[1] // HEAD_DIM
    group = n_heads // N_KV_HEADS
    rows = n_heads * qlen
    qr = q.reshape(bsz, qlen, N_KV_HEADS, group, HEAD_DIM).transpose(0, 2, 3, 1, 4).reshape(bsz, rows, HEAD_DIM)
    sink_rows = jnp.repeat(sinks.astype(F32), qlen).reshape(rows, 1)
    feature_major = lambda c: c.transpose(0, 2, 3, 1).reshape(bsz, kvw, ncache)
    seq = lambda i: (i, 0, 0)
    cache_spec = pl.BlockSpec((seq_per_step, kvw, ncache), seq)
    new_spec = pl.BlockSpec((seq_per_step * qlen, kvw), lambda i: (i, 0))
    o, kt_next, vt_next = pl.pallas_call(
        functools.partial(_attn_sample_kernel, qlen=qlen),
        grid=(bsz // seq_per_step,),
        in_specs=[
            pl.BlockSpec((rows, 1), lambda i: (0, 0)),
            pl.BlockSpec((seq_per_step, rows, HEAD_DIM), seq),
            cache_spec, new_spec, cache_spec, new_spec,
        ],
        out_specs=[pl.BlockSpec((seq_per_step, rows, HEAD_DIM), seq), cache_spec, cache_spec],
        out_shape=[jax.ShapeDtypeStruct((bsz, rows, HEAD_DIM), BF16),
                   jax.ShapeDtypeStruct((bsz, kvw, ncache), F32), jax.ShapeDtypeStruct((bsz, kvw, ncache), F32)],
        compiler_params=_cparams(("arbitrary",)),
        name="attn_sample",
    )(sink_rows, qr, feature_major(k_cache), k_new, feature_major(v_cache), v_new)
    position_major = lambda t: t.reshape(bsz, N_KV_HEADS, HEAD_DIM, ncache).transpose(0, 3, 1, 2)
    ao = o.reshape(bsz, N_KV_HEADS, group, qlen, HEAD_DIM).transpose(0, 3, 1, 2, 4).reshape(bsz * qlen, -1)
    return ao, position_major(kt_next), position_major(vt_next)


CONV_HALO = 32
CONV_ROWS = 64
CONV_COLS = 256


def _conv_prompt_kernel(uc_ref, up_ref, w_ref, b_ref, y_ref, sh_ref):
    tile = uc_ref.shape[0]
    n = pl.program_id(1)
    halo = jnp.where(n > 0, up_ref[...], 0.0)
    sh_ref[0, :CONV_HALO, :] = halo
    sh_ref[0, CONV_HALO:, :] = uc_ref[...]
    keep = tile + CONV_HALO - SUBLANES
    for r in range(1, SUBLANES):
        sh_ref[r, :keep, :] = sh_ref[0, r:r + keep, :]
    first = CONV_HALO - (CONV_WIDTH - 1)
    ch = uc_ref.shape[1]

    def rows_step(rc, carry):
        r0 = pl.multiple_of(rc * CONV_ROWS, CONV_ROWS)
        for cc in range(ch // CONV_COLS):
            cs = slice(cc * CONV_COLS, (cc + 1) * CONV_COLS)
            acc = jnp.zeros((CONV_ROWS, CONV_COLS), F32)
            for j in range(CONV_WIDTH):
                off = first + j
                a, r = off // SUBLANES, off % SUBLANES
                acc = acc + sh_ref[r, pl.ds(r0 + a * SUBLANES, CONV_ROWS), cs] * w_ref[j:j + 1, cs]
            y_ref[pl.ds(r0, CONV_ROWS), cs] = acc + b_ref[:, cs]
        return carry

    lax.fori_loop(0, tile // CONV_ROWS, rows_step, 0)


def _conv_prompt(u, w, b, bsz, slen, tile=512):
    ch = u.shape[1]
    nt = slen // tile
    per = tile // CONV_HALO
    return pl.pallas_call(
        _conv_prompt_kernel,
        grid=(bsz, nt),
        in_specs=[
            pl.BlockSpec((tile, ch), lambda bb, n: (bb * nt + n, 0)),
            pl.BlockSpec((CONV_HALO, ch), lambda bb, n: (jnp.maximum((bb * nt + n) * per - 1, 0), 0)),
            pl.BlockSpec((CONV_WIDTH, ch), lambda bb, n: (0, 0)),
            pl.BlockSpec((1, ch), lambda bb, n: (0, 0)),
        ],
        out_specs=pl.BlockSpec((tile, ch), lambda bb, n: (bb * nt + n, 0)),
        out_shape=jax.ShapeDtypeStruct((bsz * slen, ch), F32),
        scratch_shapes=[pltpu.VMEM((SUBLANES, tile + CONV_HALO, ch), F32)],
        compiler_params=_cparams(("arbitrary", "arbitrary")),
        name="conv_prompt",
    )(u, u, w, b.reshape(1, ch))


def _conv_sample_kernel(sc_ref, un_ref, w_ref, b_ref, y_ref, next_ref):
    nctx = sc_ref.shape[0]
    qlen = un_ref.shape[0]
    ch = w_ref.shape[1]

    def row(t):
        return sc_ref[t] if t < nctx else un_ref[t - nctx]

    for i in range(qlen):
        acc = jnp.zeros((sc_ref.shape[1], ch), F32)
        for j in range(CONV_WIDTH):
            acc = acc + row(i + j) * w_ref[j:j + 1, :]
        y_ref[i] = acc + b_ref[...]
    for t in range(nctx):
        next_ref[t] = row(t + qlen)


def _conv_sample(state_t, u_new_t, w, b, seq_per_step=16):
    nctx, bsz, ch = state_t.shape
    qlen = u_new_t.shape[0]
    seq = lambda i: (0, i, 0)
    return pl.pallas_call(
        _conv_sample_kernel,
        grid=(bsz // seq_per_step,),
        in_specs=[
            pl.BlockSpec((nctx, seq_per_step, ch), seq),
            pl.BlockSpec((qlen, seq_per_step, ch), seq),
            pl.BlockSpec((CONV_WIDTH, ch), lambda i: (0, 0)),
            pl.BlockSpec((1, ch), lambda i: (0, 0)),
        ],
        out_specs=[pl.BlockSpec((qlen, seq_per_step, ch), seq), pl.BlockSpec((nctx, seq_per_step, ch), seq)],
        out_shape=[jax.ShapeDtypeStruct((qlen, bsz, ch), F32), jax.ShapeDtypeStruct((nctx, bsz, ch), F32)],
        compiler_params=_cparams(("arbitrary",)),
        name="conv_sample",
    )(state_t, u_new_t, w, b.reshape(1, ch))


def _merge_kernel(y_ref, ao_ref, ga_ref, gc_ref, x_ref, lnw_ref, lnb_ref, wco_ref, wo_ref, fw_ref, wr_ref,
                  h_ref, xn_ref, lg_ref):
    y = y_ref[...]
    mu = jnp.mean(y, axis=-1, keepdims=True)
    yc = y - mu
    var = jnp.mean(yc * yc, axis=-1, keepdims=True)
    yl = yc * lax.rsqrt(var + EPS) * lnw_ref[...] + lnb_ref[...]
    act = yl * jax.nn.sigmoid(yl)
    conv_o = jnp.dot(act.astype(BF16), wco_ref[...], preferred_element_type=F32)
    mix = ga_ref[...].astype(F32) * ao_ref[...].astype(F32) + gc_ref[...].astype(F32) * conv_o
    h = x_ref[...] + jnp.dot(mix.astype(BF16), wo_ref[...], preferred_element_type=F32)
    h_ref[...] = h
    ms = jnp.mean(h * h, axis=-1, keepdims=True)
    xn = (h * lax.rsqrt(ms + EPS) * fw_ref[...]).astype(BF16)
    xn_ref[...] = xn
    lg_ref[...] = jnp.dot(xn, wr_ref[...], preferred_element_type=F32)


def _merge(y, ao, gates, row0, x, ln_w, ln_b, wco, wo, fw, wr, tile=256):
    t, d = x.shape
    ch = y.shape[1]
    off = row0 // tile
    row = lambda i: (i, 0)
    const = lambda shape: pl.BlockSpec(shape, lambda i: (0, 0), pipeline_mode=pl.Buffered(1))
    return pl.pallas_call(
        _merge_kernel,
        grid=(t // tile,),
        in_specs=[
            pl.BlockSpec((tile, ch), row),
            pl.BlockSpec((tile, d), row),
            pl.BlockSpec((tile, d), lambda i: (i + off, 0)),
            pl.BlockSpec((tile, d), lambda i: (i + off, 1)),
            pl.BlockSpec((tile, d), row),
            const((1, ch)), const((1, ch)), const((ch, d)), const((d, d)), const((1, d)),
            const((d, LANES)),
        ],
        out_specs=[pl.BlockSpec((tile, d), row), pl.BlockSpec((tile, d), row), pl.BlockSpec((tile, LANES), row)],
        out_shape=[jax.ShapeDtypeStruct((t, d), F32), jax.ShapeDtypeStruct((t, d), BF16),
                   jax.ShapeDtypeStruct((t, LANES), F32)],
        compiler_params=_cparams(("arbitrary",)),
        name="merge",
    )(y, ao, gates, gates, x, ln_w.reshape(1, ch), ln_b.reshape(1, ch), wco, wo, fw.reshape(1, d), wr)


def _route_kernel(lp_ref, ls_ref, tri_ref, upper_ref, meta_ref, metat_ref, cnt_ref, *, n_p):
    i = pl.program_id(0)

    def body(l_ref):
        lg = l_ref[...]
        shape = lg.shape
        lane = lax.broadcasted_iota(jnp.int32, shape, 1)
        big = jnp.int32(LANES)
        is_g = (lane >= N_EXPERTS) & (lane < N_EXPERTS + N_GROUPS)
        gl = jnp.where(is_g, lg, -jnp.inf)
        gmax = jnp.max(gl, axis=-1, keepdims=True)
        g_star = jnp.min(jnp.where(gl == gmax, lane - N_EXPERTS, big), axis=-1, keepdims=True)
        pg_star = 1.0 / jnp.sum(jnp.exp(gl - gmax), axis=-1, keepdims=True)
        in_g = (lane < N_EXPERTS) & (_div_pow2(lane, EXPERTS_PER_GROUP) == g_star)
        el = jnp.where(in_g, lg, -jnp.inf)
        v1 = jnp.max(el, axis=-1, keepdims=True)
        i1 = jnp.min(jnp.where(el == v1, lane, big), axis=-1, keepdims=True)
        el2 = jnp.where(lane == i1, -jnp.inf, el)
        v2 = jnp.max(el2, axis=-1, keepdims=True)
        i2 = jnp.min(jnp.where(el2 == v2, lane, big), axis=-1, keepdims=True)
        e2 = jnp.exp(v2 - v1)
        w1 = pg_star * (1.0 / (1.0 + e2))
        w2 = pg_star * (e2 / (1.0 + e2))
        oh1 = lane == i1
        oh2 = lane == i2
        cnt = jnp.where(oh1 | oh2, 1.0, 0.0)
        before = jnp.dot(tri_ref[...], cnt.astype(BF16), preferred_element_type=F32)
        tot = jnp.sum(cnt, axis=0, keepdims=True)
        runs = jnp.floor((tot + (CHUNK_ROWS - 1)) * (1.0 / CHUNK_ROWS))
        runs8 = jnp.broadcast_to(runs, (SUBLANES, LANES)).astype(BF16)
        start = jnp.dot(runs8, upper_ref[...], preferred_element_type=F32)[0:1] * CHUNK_ROWS
        slot = before + start
        s1 = jnp.sum(jnp.where(oh1, slot, 0.0), axis=-1, keepdims=True)
        s2 = jnp.sum(jnp.where(oh2, slot, 0.0), axis=-1, keepdims=True)
        cols = [i1.astype(F32), i2.astype(F32), w1, w2, s1, s2]
        meta = jnp.zeros(shape, F32)
        for c, val in enumerate(cols):
            meta = jnp.where(lane == c, val, meta)
        meta_ref[...] = meta
        metat_ref[...] = meta.T[:META_FIELDS]
        cnt_ref[...] = tot

    @pl.when(i < n_p)
    def _():
        body(lp_ref)

    @pl.when(i >= n_p)
    def _():
        body(ls_ref)


def _route(lg_p, lg_s):
    tile = TOKEN_BLOCK
    tp, ts = lg_p.shape[0], lg_s.shape[0]
    n_p, n_s = tp // tile, ts // tile
    nb = n_p + n_s
    tri = (jnp.arange(tile)[:, None] > jnp.arange(tile)[None, :]).astype(BF16)
    upper = (jnp.arange(LANES)[:, None] < jnp.arange(LANES)[None, :]).astype(BF16)
    return pl.pallas_call(
        functools.partial(_route_kernel, n_p=n_p),
        grid=(nb,),
        in_specs=[
            pl.BlockSpec((tile, LANES), lambda i: (jnp.minimum(i, n_p - 1), 0)),
            pl.BlockSpec((tile, LANES), lambda i: (jnp.maximum(i - n_p, 0), 0)),
            pl.BlockSpec((tile, tile), lambda i: (0, 0)),
            pl.BlockSpec((LANES, LANES), lambda i: (0, 0)),
        ],
        out_specs=[pl.BlockSpec((tile, LANES), lambda i: (i, 0)),
                   pl.BlockSpec((None, META_FIELDS, tile), lambda i: (i, 0, 0)),
                   pl.BlockSpec((None, 1, LANES), lambda i: (i, 0, 0))],
        out_shape=[jax.ShapeDtypeStruct((tp + ts, LANES), F32),
                   jax.ShapeDtypeStruct((nb, META_FIELDS, tile), F32),
                   jax.ShapeDtypeStruct((nb, 1, LANES), F32)],
        compiler_params=_cparams(("arbitrary",)),
        name="route",
    )(lg_p, lg_s, tri, upper)


def _plan_sizes(n_tok):
    nb = n_tok // TOKEN_BLOCK
    pad_per_block = N_EXPERTS * (CHUNK_ROWS - 1)
    max_chunks = _round_up(-(-(TOKEN_BLOCK * TOP_K + pad_per_block) // CHUNK_ROWS), SUBLANES)
    n_rows = _round_up(n_tok * TOP_K + nb * pad_per_block + N_EXPERTS * (ROW_PAD - 1), ROW_PAD)
    n_items = (n_rows + N_EXPERTS * (EXPERT_ROWS - ROW_PAD)) // EXPERT_ROWS
    max_zero = N_EXPERTS * (ROW_PAD // CHUNK_ROWS)
    return nb, max_chunks, n_rows, n_items, max_zero


def _dispatch_plan(counts, max_chunks, n_items, max_zero):
    i32 = jnp.int32
    c = counts[:, 0, :N_EXPERTS].astype(i32)
    run = (c + CHUNK_ROWS - 1) // CHUNK_ROWS
    run_end = jnp.cumsum(run, axis=1)
    src_start = run_end - run
    nchunk = run_end[:, -1]
    seg = jnp.sum(run, axis=0) * CHUNK_ROWS
    padded = (seg + ROW_PAD - 1) // ROW_PAD * ROW_PAD
    pad_end = jnp.cumsum(padded)
    pad_start = pad_end - padded
    dst_start = pad_start[None, :] // CHUNK_ROWS + (jnp.cumsum(run, axis=0) - run)
    jj = jnp.arange(max_chunks, dtype=i32)[None, :, None]
    in_run = (src_start[:, None, :] <= jj) & (jj < run_end[:, None, :])
    chunk_dst = jnp.sum(jnp.where(in_run, (dst_start - src_start)[:, None, :] + jj, 0), axis=-1) * CHUNK_ROWS
    tail = (padded - seg) // CHUNK_ROWS
    tail_end = jnp.cumsum(tail)
    tail_start = tail_end - tail
    z = jnp.arange(max_zero, dtype=i32)[:, None]
    in_tail = (tail_start[None, :] <= z) & (z < tail_end[None, :])
    zero_base = (pad_start + seg) // CHUNK_ROWS - tail_start
    zero_dst = jnp.sum(jnp.where(in_tail, zero_base[None, :] + z, 0), axis=-1) * CHUNK_ROWS
    nzero = tail_end[-1]
    items_e = (padded + EXPERT_ROWS - 1) // EXPERT_ROWS
    item_end = jnp.cumsum(items_e)
    item_start = item_end - items_e
    w = jnp.arange(n_items, dtype=i32)[:, None]
    in_item = (item_start[None, :] <= w) & (w < item_end[None, :])
    k = w - item_start[None, :]
    e_ids = jnp.arange(N_EXPERTS, dtype=i32)
    last_e = jnp.max(jnp.where(items_e > 0, e_ids, 0))
    item_e = jnp.where(w[:, 0] < item_end[-1], jnp.sum(jnp.where(in_item, e_ids[None, :], 0), axis=-1), last_e)
    item_row0 = jnp.sum(jnp.where(in_item, pad_start[None, :] + k * EXPERT_ROWS, 0), axis=-1)
    item_rows = jnp.sum(jnp.where(in_item, jnp.clip(padded[None, :] - k * EXPERT_ROWS, 0, EXPERT_ROWS), 0), axis=-1)
    item_nsub = item_rows // ROW_PAD
    return (chunk_dst.reshape(-1).astype(i32), nchunk.astype(i32), zero_dst.astype(i32),
            nzero.reshape(1).astype(i32), item_e.astype(i32), item_row0.astype(i32), item_nsub.astype(i32))


def _scatter_kernel(cd_ref, nc_ref, zd_ref, nz_ref, mt_ref, xp_ref, xs_ref, out_hbm, loc_ref, zero_ref, sem,
                    zero_sem, *, n_p, max_chunks):
    i = pl.program_id(0)
    buf = i % 2
    n_slots = loc_ref.shape[1]
    d = xp_ref.shape[1]
    mt = mt_ref[...]
    w1, w2 = mt[2:3, :], mt[3:4, :]
    s1, s2 = mt[4:5, :].astype(jnp.int32), mt[5:6, :].astype(jnp.int32)
    slot = lax.broadcasted_iota(jnp.int32, (n_slots, mt.shape[1]), 0)
    p1 = slot == s1
    p2 = slot == s2
    perm = jnp.where(p1 | p2, 1.0, 0.0).astype(BF16)
    wcol = jnp.sum(jnp.where(p1, w1, 0.0) + jnp.where(p2, w2, 0.0), axis=-1, keepdims=True)
    half = d // 2
    loc_ref[buf, :, half:] = pltpu.bitcast(jnp.broadcast_to(wcol, (n_slots, LANES)), jnp.uint32)

    def sort_rows(x_ref):
        rows = jnp.dot(perm, x_ref[...], preferred_element_type=F32)
        loc_ref[buf, :, :half] = _pack_pairs(rows[:, :half], rows[:, half:])

    @pl.when(i < n_p)
    def _():
        sort_rows(xp_ref)

    @pl.when(i >= n_p)
    def _():
        sort_rows(xs_ref)

    def rows_at(r):
        return pl.ds(pl.multiple_of(r, CHUNK_ROWS), CHUNK_ROWS)

    def chunk_copy(blk, j):
        return pltpu.make_async_copy(loc_ref.at[blk % 2, rows_at(j * CHUNK_ROWS)],
                                     out_hbm.at[rows_at(cd_ref[blk * max_chunks + j])], sem.at[blk % 2])

    def zero_copy(z):
        return pltpu.make_async_copy(zero_ref, out_hbm.at[rows_at(zd_ref[z])], zero_sem)

    def loop(n, fn):
        def body(j, carry):
            fn(j)
            return carry
        lax.fori_loop(0, n, body, 0)

    @pl.when(i > 0)
    def _():
        loop(nc_ref[i - 1], lambda j: chunk_copy(i - 1, j).wait())

    loop(nc_ref[i], lambda j: chunk_copy(i, j).start())

    @pl.when(i == pl.num_programs(0) - 1)
    def _():
        zero_ref[...] = jnp.zeros_like(zero_ref)
        loop(nz_ref[0], lambda z: zero_copy(z).start())
        loop(nc_ref[i], lambda j: chunk_copy(i, j).wait())
        loop(nz_ref[0], lambda z: zero_copy(z).wait())


def _scatter_rows(xn_p, xn_s, metat, chunk_dst, nchunk, zero_dst, nzero, max_chunks, n_rows):
    tile = TOKEN_BLOCK
    tp, d = xn_p.shape
    ts = xn_s.shape[0]
    n_p, n_s = tp // tile, ts // tile
    n_slots = max_chunks * CHUNK_ROWS
    width = d // 2 + LANES
    u32 = jnp.uint32
    return pl.pallas_call(
        functools.partial(_scatter_kernel, n_p=n_p, max_chunks=max_chunks),
        grid_spec=pltpu.PrefetchScalarGridSpec(
            num_scalar_prefetch=4,
            grid=(n_p + n_s,),
            in_specs=[
                pl.BlockSpec((None, META_FIELDS, tile), lambda i, *_: (i, 0, 0)),
                pl.BlockSpec((tile, d), lambda i, *_: (jnp.minimum(i, n_p - 1), 0)),
                pl.BlockSpec((tile, d), lambda i, *_: (jnp.maximum(i - n_p, 0), 0)),
            ],
            out_specs=pl.BlockSpec(memory_space=pl.ANY),
            scratch_shapes=[pltpu.VMEM((2, n_slots, width), u32), pltpu.VMEM((CHUNK_ROWS, width), u32),
                            pltpu.SemaphoreType.DMA((2,)), pltpu.SemaphoreType.DMA(())],
        ),
        out_shape=jax.ShapeDtypeStruct((n_rows, width), u32),
        compiler_params=_cparams(("arbitrary",)),
        name="scatter_rows",
    )(chunk_dst, nchunk, zero_dst, nzero, metat, xn_p, xn_s)


def _expert_kernel(e_ref, row0_ref, nsub_ref, xs_hbm, wg_ref, wu_ref, wd_ref, out_hbm,
                   xf_ref, xb_ref, acc_ref, ob_ref, pend_ref, in_sem, out_sem):
    w = pl.program_id(0)
    c = pl.program_id(1)
    n_items = pl.num_programs(0)
    last_c = pl.num_programs(1) - 1
    nsub = nsub_ref[w]
    d = acc_ref.shape[1]
    half = d // 2
    slot = w % 2

    def sub_rows(s):
        return pl.ds(pl.multiple_of(s * ROW_PAD, ROW_PAD), ROW_PAD)

    def hbm_rows(item, s):
        return pl.ds(pl.multiple_of(row0_ref[item] + s * ROW_PAD, ROW_PAD), ROW_PAD)

    def in_copy(item, s):
        return pltpu.make_async_copy(xs_hbm.at[hbm_rows(item, s)], xf_ref.at[item % 2, sub_rows(s)],
                                     in_sem.at[item % 2])

    def out_copy(item, s):
        return pltpu.make_async_copy(ob_ref.at[sub_rows(s)], out_hbm.at[hbm_rows(item, s)], out_sem)

    def loop(n, fn):
        def body(s, carry):
            fn(s)
            return carry
        lax.fori_loop(0, n, body, 0)

    @pl.when(c == 0)
    def _():
        @pl.when(w == 0)
        def _():
            pend_ref[0] = 0
            pend_ref[1] = 0
            loop(nsub, lambda s: in_copy(w, s).start())

        loop(nsub, lambda s: in_copy(w, s).wait())
        nxt = jnp.minimum(w + 1, n_items - 1)

        @pl.when(w + 1 < n_items)
        def _():
            loop(nsub_ref[nxt], lambda s: in_copy(nxt, s).start())

    def drain_out():
        prev = pend_ref[1]
        loop(pend_ref[0], lambda s: out_copy(prev, s).wait())
        pend_ref[0] = 0

    for n in range(1, EXPERT_ROWS // ROW_PAD + 1):
        m = n * ROW_PAD

        @pl.when(nsub == n)
        def _(m=m):
            @pl.when(c == 0)
            def _():
                acc_ref[:m, :] = jnp.zeros((m, d), F32)
                xa, xb = _unpack_pairs(xf_ref[slot, :m, :half])
                xb_ref[:m, :half] = xa.astype(BF16)
                xb_ref[:m, half:] = xb.astype(BF16)

            x = xb_ref[:m, :]
            g = jnp.dot(x, wg_ref[...].astype(BF16), preferred_element_type=F32)
            u = jnp.dot(x, wu_ref[...].astype(BF16), preferred_element_type=F32)
            hmid = (g * jax.nn.sigmoid(g) * u).astype(BF16)
            acc_ref[:m, :] += jnp.dot(hmid, wd_ref[...].astype(BF16), preferred_element_type=F32)

            @pl.when(c == last_c)
            def _():
                drain_out()
                mix_w = pltpu.bitcast(xf_ref[slot, :m, half:half + 1], F32)
                y = (acc_ref[:m, :] * mix_w).astype(BF16).astype(F32)
                ob_ref[:m, :] = _pack_pairs(y[:, :half], y[:, half:])
                loop(nsub, lambda s: out_copy(w, s).start())
                pend_ref[0] = nsub
                pend_ref[1] = w

    @pl.when((w == n_items - 1) & (c == last_c))
    def _():
        drain_out()


def _experts(xs, item_e, item_row0, item_nsub, w_gate, w_up, w_down):
    n_rows, width = xs.shape
    d = w_gate.shape[1]
    assert width == d // 2 + LANES
    d_exp = w_gate.shape[2]
    nc = d_exp // D_CHUNK
    n_items = item_e.shape[0]

    def chunk_of(w, c, nsub_ref):
        return jnp.where(nsub_ref[w] > 0, c, nc - 1)

    return pl.pallas_call(
        _expert_kernel,
        grid_spec=pltpu.PrefetchScalarGridSpec(
            num_scalar_prefetch=3,
            grid=(n_items, nc),
            in_specs=[
                pl.BlockSpec(memory_space=pl.ANY),
                pl.BlockSpec((None, d, D_CHUNK), lambda w, c, e, r, n: (e[w], 0, chunk_of(w, c, n))),
                pl.BlockSpec((None, d, D_CHUNK), lambda w, c, e, r, n: (e[w], 0, chunk_of(w, c, n))),
                pl.BlockSpec((None, D_CHUNK, d), lambda w, c, e, r, n: (e[w], chunk_of(w, c, n), 0)),
            ],
            out_specs=pl.BlockSpec(memory_space=pl.ANY),
            scratch_shapes=[
                pltpu.VMEM((2, EXPERT_ROWS, width), jnp.uint32),
                pltpu.VMEM((EXPERT_ROWS, d), BF16),
                pltpu.VMEM((EXPERT_ROWS, d), F32),
                pltpu.VMEM((EXPERT_ROWS, d // 2), jnp.uint32),
                pltpu.SMEM((2,), jnp.int32),
                pltpu.SemaphoreType.DMA((2,)),
                pltpu.SemaphoreType.DMA(()),
            ],
        ),
        out_shape=jax.ShapeDtypeStruct((n_rows, d // 2), jnp.uint32),
        compiler_params=_cparams(("arbitrary", "arbitrary")),
        name="experts",
    )(item_e, item_row0, item_nsub, xs, w_gate, w_up, w_down)


COMBINE_SPLIT = 2
SLOT_STEP = 256


def _combine_kernel(cd_ref, nc_ref, meta_ref, hp_ref, hs_ref, yb_hbm, yp_ref, ys_ref, loc_ref, sem,
                    *, n_p, max_chunks):
    i = pl.program_id(0)
    j = pl.program_id(1)
    nb = pl.num_programs(0)
    buf = i % 2
    n_slots, half = loc_ref.shape[1], loc_ref.shape[2]
    d = 2 * half

    def rows_at(r):
        return pl.ds(pl.multiple_of(r, CHUNK_ROWS), CHUNK_ROWS)

    def chunk_copy(blk, c):
        return pltpu.make_async_copy(yb_hbm.at[rows_at(cd_ref[blk * max_chunks + c])],
                                     loc_ref.at[blk % 2, rows_at(c * CHUNK_ROWS)], sem.at[blk % 2])

    def fetch(blk):
        n = nc_ref[blk]

        def start(c, carry):
            chunk_copy(blk, c).start()
            return carry
        lax.fori_loop(0, n, start, 0)

        def clear(c, carry):
            loc_ref[blk % 2, rows_at(c * CHUNK_ROWS), :] = jnp.zeros((CHUNK_ROWS, half), jnp.uint32)
            return carry
        lax.fori_loop(n, max_chunks, clear, 0)

    @pl.when(j == 0)
    def _():
        @pl.when(i == 0)
        def _():
            fetch(i)

        def wait(c, carry):
            chunk_copy(i, c).wait()
            return carry
        lax.fori_loop(0, nc_ref[i], wait, 0)

        @pl.when(i + 1 < nb)
        def _():
            fetch(jnp.minimum(i + 1, nb - 1))

    meta = meta_ref[...]
    s1 = meta[:, 4:5].astype(jnp.int32)
    s2 = meta[:, 5:6].astype(jnp.int32)
    rows = meta.shape[0]
    f = jnp.zeros((rows, d), F32)
    for r0 in range(0, n_slots, SLOT_STEP):
        slot = lax.broadcasted_iota(jnp.int32, (rows, SLOT_STEP), 1) + r0
        pt = jnp.where((slot == s1) | (slot == s2), 1.0, 0.0).astype(BF16)
        ya, yb = _unpack_pairs(loc_ref[buf, r0:r0 + SLOT_STEP, :])
        f = f + jnp.dot(pt, jnp.concatenate([ya, yb], axis=1).astype(BF16), preferred_element_type=F32)
    g = i * pl.num_programs(1) + j

    @pl.when(g < n_p)
    def _():
        yp_ref[...] = hp_ref[...] + f

    @pl.when(g >= n_p)
    def _():
        ys_ref[...] = hs_ref[...] + f


def _combine(h_p, h_s, meta, chunk_dst, nchunk, yb, max_chunks):
    tp, d = h_p.shape
    ts = h_s.shape[0]
    tile = TOKEN_BLOCK // COMBINE_SPLIT
    n_p, n_s = tp // tile, ts // tile
    nb = (tp + ts) // TOKEN_BLOCK
    n_slots = max_chunks * CHUNK_ROWS
    assert n_slots % SLOT_STEP == 0
    blk = lambda i, j: i * COMBINE_SPLIT + j
    p_idx = lambda i, j, *_: (jnp.minimum(blk(i, j), n_p - 1), 0)
    s_idx = lambda i, j, *_: (jnp.maximum(blk(i, j) - n_p, 0), 0)
    return pl.pallas_call(
        functools.partial(_combine_kernel, n_p=n_p, max_chunks=max_chunks),
        grid_spec=pltpu.PrefetchScalarGridSpec(
            num_scalar_prefetch=2,
            grid=(nb, COMBINE_SPLIT),
            in_specs=[
                pl.BlockSpec((tile, LANES), lambda i, j, *_: (blk(i, j), 0)),
                pl.BlockSpec((tile, d), p_idx),
                pl.BlockSpec((tile, d), s_idx),
                pl.BlockSpec(memory_space=pl.ANY),
            ],
            out_specs=[pl.BlockSpec((tile, d), p_idx), pl.BlockSpec((tile, d), s_idx)],
            scratch_shapes=[pltpu.VMEM((2, n_slots, d // 2), jnp.uint32), pltpu.SemaphoreType.DMA((2,))],
        ),
        out_shape=[jax.ShapeDtypeStruct((tp, d), F32), jax.ShapeDtypeStruct((ts, d), F32)],
        compiler_params=_cparams(("arbitrary", "arbitrary")),
        name="combine",
    )(chunk_dst, nchunk, meta, h_p, h_s, yb)


def _layer(x_p, x_s, ck, cv, sc, attn_norm_w, w_in, b_gate, q_norm_w, k_norm_w, sinks, conv_dw_w, conv_dw_b,
           conv_ln_w, conv_ln_b, w_conv_out, w_out, ffn_norm_w, router_group, router_expert, w_gate, w_up,
           w_down):
    bsz, slen, d = x_p.shape
    dbs, dlen, _ = x_s.shape
    tp, ts = bsz * slen, dbs * dlen
    conv_ch = conv_dw_w.shape[1]
    kvw = N_KV_HEADS * HEAD_DIM
    xp2, xs2 = x_p.reshape(tp, d), x_s.reshape(ts, d)

    xn = _prenorm(xp2, xs2, attn_norm_w)
    q, k, v, u, gates = _project(xn, w_in, b_gate, q_norm_w, k_norm_w, d, conv_ch)

    ao_p = _attn_prompt(q, k, v, sinks, bsz, slen)
    ao_s, ck_next, cv_next = _attn_sample(q[tp:], ck, k[tp:], cv, v[tp:], sinks)

    y_p = _conv_prompt(u, conv_dw_w, conv_dw_b, bsz, slen)
    sc_t = sc.transpose(1, 0, 2)
    u_s_t = u[tp:].reshape(dbs, dlen, conv_ch).transpose(1, 0, 2)
    y_s_t, sc_next_t = _conv_sample(sc_t, u_s_t, conv_dw_w, conv_dw_b)
    y_s = y_s_t.transpose(1, 0, 2).reshape(ts, conv_ch)

    wco = w_conv_out.astype(BF16)
    wo = w_out.astype(BF16)
    wr = jnp.concatenate([router_expert, router_group,
                          jnp.zeros((d, LANES - N_EXPERTS - N_GROUPS), router_expert.dtype)], axis=1).astype(BF16)
    margs = (conv_ln_w, conv_ln_b, wco, wo, ffn_norm_w, wr)
    h_p, xn_p, lg_p = _merge(y_p, ao_p, gates, 0, xp2, *margs)
    h_s, xn_s, lg_s = _merge(y_s, ao_s, gates, tp, xs2, *margs)

    meta, metat, counts = _route(lg_p, lg_s)
    _, max_chunks, n_rows, n_items, max_zero = _plan_sizes(tp + ts)
    chunk_dst, nchunk, zero_dst, nzero, item_e, item_row0, item_nsub = _dispatch_plan(
        counts, max_chunks, n_items, max_zero)
    xs_rows = _scatter_rows(xn_p, xn_s, metat, chunk_dst, nchunk, zero_dst, nzero, max_chunks, n_rows)
    yb = _experts(xs_rows, item_e, item_row0, item_nsub, w_gate, w_up, w_down)
    out_p, out_s = _combine(h_p, h_s, meta, chunk_dst, nchunk, yb, max_chunks)

    def seq_tails(a, n):
        return jnp.stack([lax.slice_in_dim(a, (b + 1) * slen - n, (b + 1) * slen, axis=0) for b in range(bsz)])

    new_p = (seq_tails(k, WINDOW).reshape(bsz, WINDOW, N_KV_HEADS, HEAD_DIM),
             seq_tails(v, WINDOW).reshape(bsz, WINDOW, N_KV_HEADS, HEAD_DIM),
             seq_tails(u, CONV_WIDTH - 1))
    new_s = (ck_next, cv_next, sc_next_t.transpose(1, 0, 2))
    return out_p.reshape(bsz, slen, d), out_s.reshape(dbs, dlen, d), new_p, new_s


def kernel(x_prompt, x_sample, cache_k, cache_v, state_conv, attn_norm_w, w_in, b_gate, q_norm_w, k_norm_w,
           sinks, conv_dw_w, conv_dw_b, conv_ln_w, conv_ln_b, w_conv_out, w_out, ffn_norm_w, router_group,
           router_expert, w_gate, w_up, w_down):
    x_p, x_s = x_prompt, x_sample
    kp, vp, cp, ks_, vs_, cs_ = [], [], [], [], [], []
    for l in range(cache_k.shape[0]):
        x_p, x_s, new_p, new_s = _layer(
            x_p, x_s, cache_k[l], cache_v[l], state_conv[l], attn_norm_w[l], w_in[l], b_gate[l], q_norm_w[l],
            k_norm_w[l], sinks[l], conv_dw_w[l], conv_dw_b[l], conv_ln_w[l], conv_ln_b[l], w_conv_out[l],
            w_out[l], ffn_norm_w[l], router_group[l], router_expert[l], w_gate[l], w_up[l], w_down[l])
        kp.append(new_p[0]); vp.append(new_p[1]); cp.append(new_p[2])
        ks_.append(new_s[0]); vs_.append(new_s[1]); cs_.append(new_s[2])
    return (x_p, x_s, jnp.stack(kp), jnp.stack(vp), jnp.stack(cp),
            jnp.stack(ks_), jnp.stack(vs_), jnp.stack(cs_))
```

```python
import functools

import jax
import jax.numpy as jnp
from jax import lax
from jax.experimental import pallas as pl
from jax.experimental.pallas import tpu as pltpu

F32 = jnp.float32
BF16 = jnp.bfloat16
EPS = 1e-6
NEG_INF = -1e30

HEAD_DIM = 64
N_KV_HEADS = 4
WINDOW = 128
CONV_WIDTH = 31
N_GROUPS = 4
EXPERTS_PER_GROUP = 8
N_EXPERTS = N_GROUPS * EXPERTS_PER_GROUP
TOP_K = 2

LANES = 128
SUBLANES = 8
MXU_DIM = 256
VMEM_LIMIT = 56 * 1024 * 1024

TOKEN_BLOCK = 512
CHUNK_ROWS = SUBLANES
ROW_PAD = 128
EXPERT_ROWS = 768
D_CHUNK = 512
META_FIELDS = 8


def _cparams(sem):
    return pltpu.CompilerParams(dimension_semantics=sem, vmem_limit_bytes=VMEM_LIMIT)


def _log2(n):
    assert n > 0 and n & (n - 1) == 0, n
    return n.bit_length() - 1


def _div_pow2(x, n):
    return lax.shift_right_logical(x, jnp.int32(_log2(n)))


def _mod_pow2(x, n):
    _log2(n)
    return x & jnp.int32(n - 1)


def _round_up(x, m):
    return (x + m - 1) // m * m


HIGH_HALF = 0xFFFF0000


def _pack_pairs(a, b):
    ua = pltpu.bitcast(a, jnp.uint32) & jnp.uint32(HIGH_HALF)
    return ua | (pltpu.bitcast(b, jnp.uint32) >> 16)


def _unpack_pairs(w):
    return (pltpu.bitcast(w & jnp.uint32(HIGH_HALF), F32), pltpu.bitcast(w << 16, F32))


def _prenorm_kernel(xp_ref, xs_ref, w_ref, o_ref, *, n_p):
    i = pl.program_id(0)

    def body(x_ref):
        x = x_ref[...]
        ms = jnp.mean(x * x, axis=-1, keepdims=True)
        o_ref[...] = (x * lax.rsqrt(ms + EPS) * w_ref[...]).astype(o_ref.dtype)

    @pl.when(i < n_p)
    def _():
        body(xp_ref)

    @pl.when(i >= n_p)
    def _():
        body(xs_ref)


def _prenorm(xp, xs, w, tile=512):
    tp, d = xp.shape
    ts = xs.shape[0]
    n_p, n_s = tp // tile, ts // tile
    return pl.pallas_call(
        functools.partial(_prenorm_kernel, n_p=n_p),
        grid=(n_p + n_s,),
        in_specs=[
            pl.BlockSpec((tile, d), lambda i: (jnp.minimum(i, n_p - 1), 0)),
            pl.BlockSpec((tile, d), lambda i: (jnp.maximum(i - n_p, 0), 0)),
            pl.BlockSpec((1, d), lambda i: (0, 0)),
        ],
        out_specs=pl.BlockSpec((tile, d), lambda i: (i, 0)),
        out_shape=jax.ShapeDtypeStruct((tp + ts, d), BF16),
        compiler_params=_cparams(("arbitrary",)),
        name="prenorm",
    )(xp, xs, w.reshape(1, d))


def _head_rms(z, hw, bd_ref):
    ss = z * z
    hi = ss.astype(BF16)
    lo = (ss - hi.astype(F32)).astype(BF16)
    bd = bd_ref[...]
    tot = jnp.dot(hi, bd, preferred_element_type=F32) + jnp.dot(lo, bd, preferred_element_type=F32)
    return z * lax.rsqrt(tot * (1.0 / HEAD_DIM) + EPS) * hw


def _load_w(i, w_ref, wb_ref):
    @pl.when(i == 0)
    def _():
        wb_ref[...] = w_ref[...].astype(BF16)


PROJ_SUB = 544


def _row_groups(n_rows):
    sub = PROJ_SUB if n_rows % PROJ_SUB == 0 else n_rows
    return [slice(r0, r0 + sub) for r0 in range(0, n_rows, sub)]


def _proj_q_kernel(x_ref, w_ref, hw_ref, bd_ref, o_ref, wb_ref):
    _load_w(pl.program_id(1), w_ref, wb_ref)
    for rs in _row_groups(x_ref.shape[0]):
        z = jnp.dot(x_ref[rs, :], wb_ref[...], preferred_element_type=F32)
        for c in range(z.shape[1] // MXU_DIM):
            sl = slice(c * MXU_DIM, (c + 1) * MXU_DIM)
            o_ref[rs, sl] = _head_rms(z[:, sl], hw_ref[...], bd_ref).astype(o_ref.dtype)


def _proj_kv_kernel(x_ref, w_ref, hw_ref, bd_ref, k_ref, v_ref, wb_ref):
    _load_w(pl.program_id(1), w_ref, wb_ref)
    kw = k_ref.shape[1]
    for rs in _row_groups(x_ref.shape[0]):
        z = jnp.dot(x_ref[rs, :], wb_ref[...], preferred_element_type=F32)
        k_ref[rs, :] = _head_rms(z[:, :kw], hw_ref[...], bd_ref)
        v_ref[rs, :] = z[:, kw:]


def _proj_glu_kernel(x_ref, wa_ref, wg_ref, o_ref, wab_ref, wgb_ref):
    i = pl.program_id(1)
    _load_w(i, wa_ref, wab_ref)
    _load_w(i, wg_ref, wgb_ref)
    for rs in _row_groups(x_ref.shape[0]):
        x = x_ref[rs, :]
        a = jnp.dot(x, wab_ref[...], preferred_element_type=F32)
        g = jnp.dot(x, wgb_ref[...], preferred_element_type=F32)
        o_ref[rs, :] = a * jax.nn.sigmoid(g)


def _proj_gate_kernel(x_ref, w_ref, b_ref, o_ref, wb_ref):
    _load_w(pl.program_id(1), w_ref, wb_ref)
    for rs in _row_groups(x_ref.shape[0]):
        z = jnp.dot(x_ref[rs, :], wb_ref[...], preferred_element_type=F32)
        o_ref[rs, :] = jax.nn.sigmoid(z + b_ref[...]).astype(o_ref.dtype)


def _row_tile(t, cap=1152):
    best = 16
    for m in range(16, cap + 1, 16):
        if t % m == 0:
            best = m
    return best


def _project(xn, w_in, b_gate, q_norm_w, k_norm_w, d_model, conv_ch):
    t, d = xn.shape
    q_w = d_model
    kv_w = N_KV_HEADS * HEAD_DIM
    tn = 2 * kv_w
    tm = _row_tile(t)
    ni = t // tm
    tm_big = _row_tile(t, cap=2304)
    ni_big = t // tm_big
    reps = MXU_DIM // HEAD_DIM
    hq = jnp.tile(q_norm_w.astype(F32) * HEAD_DIM ** -0.5, reps).reshape(1, MXU_DIM)
    hk = jnp.tile(k_norm_w.astype(F32), reps).reshape(1, MXU_DIM)
    gid = jnp.arange(MXU_DIM) // HEAD_DIM
    bd = (gid[:, None] == gid[None, :]).astype(BF16)

    x_spec = pl.BlockSpec((tm, d), lambda s, i: (i, 0))
    x_big = pl.BlockSpec((tm_big, d), lambda s, i: (i, 0))
    small = lambda shape: pl.BlockSpec(shape, lambda s, i: (0, 0))

    def w_spec(c0):
        return pl.BlockSpec((d, tn), lambda s, i: (0, c0 + s))

    wscr = pltpu.VMEM((d, tn), BF16)
    cp = _cparams(("arbitrary", "arbitrary"))

    q = pl.pallas_call(
        _proj_q_kernel, grid=(q_w // tn, ni_big),
        in_specs=[x_big, w_spec(0), small((1, MXU_DIM)), small((MXU_DIM, MXU_DIM))],
        out_specs=pl.BlockSpec((tm_big, tn), lambda s, i: (i, s)),
        out_shape=jax.ShapeDtypeStruct((t, q_w), BF16),
        scratch_shapes=[wscr], compiler_params=cp, name="proj_q",
    )(xn, w_in, hq, bd)

    c_kv = q_w // tn
    k, v = pl.pallas_call(
        _proj_kv_kernel, grid=(1, ni),
        in_specs=[x_spec, w_spec(c_kv), small((1, MXU_DIM)), small((MXU_DIM, MXU_DIM))],
        out_specs=[pl.BlockSpec((tm, kv_w), lambda s, i: (i, 0))] * 2,
        out_shape=[jax.ShapeDtypeStruct((t, kv_w), F32)] * 2,
        scratch_shapes=[wscr], compiler_params=cp, name="proj_kv",
    )(xn, w_in, hk, bd)

    c_a = c_kv + 1
    n_glu = conv_ch // tn
    u = pl.pallas_call(
        _proj_glu_kernel, grid=(n_glu, ni),
        in_specs=[x_spec, w_spec(c_a), w_spec(c_a + n_glu)],
        out_specs=pl.BlockSpec((tm, tn), lambda s, i: (i, s)),
        out_shape=jax.ShapeDtypeStruct((t, conv_ch), F32),
        scratch_shapes=[wscr, wscr], compiler_params=cp, name="proj_glu",
    )(xn, w_in, w_in)

    c_g = c_a + 2 * n_glu
    n_gate = 2 * d_model // tn
    gates = pl.pallas_call(
        _proj_gate_kernel, grid=(n_gate, ni_big),
        in_specs=[x_big, w_spec(c_g), pl.BlockSpec((1, tn), lambda s, i: (0, s))],
        out_specs=pl.BlockSpec((tm_big, tn), lambda s, i: (i, s)),
        out_shape=jax.ShapeDtypeStruct((t, 2 * d_model), BF16),
        scratch_shapes=[wscr], compiler_params=cp, name="proj_gate",
    )(xn, w_in, b_gate.reshape(1, -1))
    return q, k, v, u, gates


def _attn_prompt_kernel(sink_ref, q_ref, kc_ref, kp_ref, vc_ref, vp_ref, o_ref):
    blk = q_ref.shape[0]
    n = pl.program_id(1)
    rows = 2 * blk
    r_iota = lax.broadcasted_iota(jnp.int32, (rows, rows), 0)
    c_iota = lax.broadcasted_iota(jnp.int32, (rows, rows), 1)
    qi = jnp.where(r_iota >= blk, r_iota - blk, r_iota)
    mask = (c_iota >= qi) & (c_iota <= qi + WINDOW) & ((c_iota >= blk) | (n > 0))
    lo_lane = lax.broadcasted_iota(jnp.int32, (blk, LANES), 1) < HEAD_DIM
    top_row = lax.broadcasted_iota(jnp.int32, (rows, 1), 0) < blk
    zero = jnp.zeros((blk, LANES), q_ref.dtype)
    group = q_ref.shape[1] // (N_KV_HEADS * LANES)
    for g in range(N_KV_HEADS):
        hs = slice(g * HEAD_DIM, (g + 1) * HEAD_DIM)
        kg = jnp.concatenate([kp_ref[:, hs], kc_ref[:, hs]], axis=0)
        vg = jnp.concatenate([vp_ref[:, hs], vc_ref[:, hs]], axis=0)
        kdup = jnp.concatenate([kg, kg], axis=1).astype(BF16)
        vdup = jnp.concatenate([vg, vg], axis=1).astype(BF16)
        for p in range(group):
            pair = g * group + p
            ls = slice(pair * LANES, (pair + 1) * LANES)
            qp = q_ref[:, ls]
            lhs = jnp.concatenate([jnp.where(lo_lane, qp, zero), jnp.where(lo_lane, zero, qp)], axis=0)
            s = lax.dot_general(lhs, kdup, (((1,), (1,)), ((), ())), preferred_element_type=F32)
            s = jnp.where(mask, s, NEG_INF)
            sink = jnp.where(top_row, sink_ref[2 * pair], sink_ref[2 * pair + 1])
            m = jnp.maximum(jnp.max(s, axis=-1, keepdims=True), sink)
            e = jnp.exp(s - m)
            denom = jnp.sum(e, axis=-1, keepdims=True) + jnp.exp(sink - m)
            o2 = jnp.dot(e.astype(BF16), vdup, preferred_element_type=F32) / denom
            o_ref[:, ls] = jnp.where(lo_lane, o2[:blk], o2[blk:]).astype(o_ref.dtype)


def _attn_prompt(q, k, v, sinks, bsz, slen):
    blk = WINDOW
    nb = slen // blk
    qw, kw = q.shape[1], k.shape[1]
    cur = lambda b, n: (b * nb + n, 0)
    prev = lambda b, n: (b * nb + jnp.maximum(n - 1, 0), 0)
    return pl.pallas_call(
        _attn_prompt_kernel,
        grid=(bsz, nb),
        in_specs=[
            pl.BlockSpec(memory_space=pltpu.SMEM),
            pl.BlockSpec((blk, qw), cur),
            pl.BlockSpec((blk, kw), cur), pl.BlockSpec((blk, kw), prev),
            pl.BlockSpec((blk, kw), cur), pl.BlockSpec((blk, kw), prev),
        ],
        out_specs=pl.BlockSpec((blk, qw), cur),
        out_shape=jax.ShapeDtypeStruct((bsz * slen, qw), BF16),
        compiler_params=_cparams(("arbitrary", "arbitrary")),
        name="attn_prompt",
    )(sinks.astype(F32), q, k, k, v, v)


def _attn_sample_kernel(sink_ref, q_ref, kt_ref, kn_ref, vt_ref, vn_ref, o_ref, kt_out_ref, vt_out_ref, *, qlen):
    nseq, rows, _ = q_ref.shape
    kvw, ncache = kt_ref.shape[1], kt_ref.shape[2]
    per_kv = rows // N_KV_HEADS
    all_rows = nseq * rows
    r_lane = _div_pow2(_mod_pow2(lax.broadcasted_iota(jnp.int32, (all_rows, kvw), 0), rows), per_kv)
    c_lane = _div_pow2(lax.broadcasted_iota(jnp.int32, (all_rows, kvw), 1), HEAD_DIM)
    own = r_lane == c_lane
    mask_c = (lax.broadcasted_iota(jnp.int32, (all_rows, ncache), 1)
              >= _mod_pow2(lax.broadcasted_iota(jnp.int32, (all_rows, ncache), 0), qlen))
    mask_n = (lax.broadcasted_iota(jnp.int32, (all_rows, qlen), 1)
              <= _mod_pow2(lax.broadcasted_iota(jnp.int32, (all_rows, qlen), 0), qlen))
    keep_old = lax.broadcasted_iota(jnp.int32, (kvw, ncache), 1) < ncache - qlen
    sink = sink_ref[...]
    contract_last = (((1,), (1,)), ((), ()))
    seq_rows = lambda b: slice(b * rows, (b + 1) * rows)
    new_rows = lambda b: slice(b * qlen, (b + 1) * qlen)

    q_all = q_ref[...].reshape(all_rows, HEAD_DIM)
    qe = jnp.where(own, jnp.concatenate([q_all] * N_KV_HEADS, axis=1), jnp.zeros((), q_all.dtype))
    sc = jnp.concatenate([jnp.dot(qe[seq_rows(b)], kt_ref[b].astype(BF16), preferred_element_type=F32)
                          for b in range(nseq)], axis=0)
    sn = jnp.concatenate([lax.dot_general(qe[seq_rows(b)], kn_ref[new_rows(b), :].astype(BF16), contract_last,
                                          preferred_element_type=F32) for b in range(nseq)], axis=0)
    sc = jnp.where(mask_c, sc, NEG_INF)
    sn = jnp.where(mask_n, sn, NEG_INF)
    m = jnp.maximum(jnp.maximum(jnp.max(sc, axis=-1, keepdims=True), jnp.max(sn, axis=-1, keepdims=True)), sink)
    ec = jnp.exp(sc - m)
    en = jnp.exp(sn - m)
    denom = jnp.sum(ec, axis=-1, keepdims=True) + jnp.sum(en, axis=-1, keepdims=True) + jnp.exp(sink - m)
    ec = ec.astype(BF16)
    en = en.astype(BF16)
    o = jnp.concatenate(
        [lax.dot_general(ec[seq_rows(b)], vt_ref[b].astype(BF16), contract_last, preferred_element_type=F32)
         + jnp.dot(en[seq_rows(b)], vn_ref[new_rows(b), :].astype(BF16), preferred_element_type=F32)
         for b in range(nseq)], axis=0)
    o = jnp.where(own, o, 0.0)
    acc = o[:, :HEAD_DIM]
    for g in range(1, N_KV_HEADS):
        acc = acc + o[:, g * HEAD_DIM:(g + 1) * HEAD_DIM]
    o_ref[...] = (acc / denom).astype(o_ref.dtype).reshape(nseq, rows, HEAD_DIM)

    def feature_major(new_ref):
        pad = jnp.zeros((ncache - nseq * qlen, kvw), F32)
        return jnp.concatenate([new_ref[...], pad], axis=0).T

    def shift_in(old_ref, new_ref, out_ref):
        new_t = feature_major(new_ref)
        for b in range(nseq):
            placed = pltpu.roll(new_t, ncache - qlen - b * qlen, axis=1)
            out_ref[b] = jnp.where(keep_old, pltpu.roll(old_ref[b], ncache - qlen, axis=1), placed)

    shift_in(kt_ref, kn_ref, kt_out_ref)
    shift_in(vt_ref, vn_ref, vt_out_ref)


def _attn_sample(q, k_cache, k_new, v_cache, v_new, sinks, seq_per_step=8):
    bsz, ncache = k_cache.shape[:2]
    kvw = k_new.shape[1]
    qlen = k_new.shape[0] // bsz
    assert seq_per_step * qlen <= ncache
    n_heads = q.shape[1] // HEAD_DIM
    group = n_heads // N_KV_HEADS
    rows = n_heads * qlen
    qr = q.reshape(bsz, qlen, N_KV_HEADS, group, HEAD_DIM).transpose(0, 2, 3, 1, 4).reshape(bsz, rows, HEAD_DIM)
    sink_rows = jnp.tile(jnp.repeat(sinks.astype(F32), qlen), seq_per_step).reshape(seq_per_step * rows, 1)
    feature_major = lambda c: c.transpose(0, 2, 3, 1).reshape(bsz, kvw, ncache)
    seq = lambda i: (i, 0, 0)
    cache_spec = pl.BlockSpec((seq_per_step, kvw, ncache), seq)
    new_spec = pl.BlockSpec((seq_per_step * qlen, kvw), lambda i: (i, 0))
    o, kt_next, vt_next = pl.pallas_call(
        functools.partial(_attn_sample_kernel, qlen=qlen),
        grid=(bsz // seq_per_step,),
        in_specs=[
            pl.BlockSpec((seq_per_step * rows, 1), lambda i: (0, 0)),
            pl.BlockSpec((seq_per_step, rows, HEAD_DIM), seq),
            cache_spec, new_spec, cache_spec, new_spec,
        ],
        out_specs=[pl.BlockSpec((seq_per_step, rows, HEAD_DIM), seq), cache_spec, cache_spec],
        out_shape=[jax.ShapeDtypeStruct((bsz, rows, HEAD_DIM), BF16),
                   jax.ShapeDtypeStruct((bsz, kvw, ncache), F32), jax.ShapeDtypeStruct((bsz, kvw, ncache), F32)],
        compiler_params=_cparams(("arbitrary",)),
        name="attn_sample",
    )(sink_rows, qr, feature_major(k_cache), k_new, feature_major(v_cache), v_new)
    position_major = lambda t: t.reshape(bsz, N_KV_HEADS, HEAD_DIM, ncache).transpose(0, 3, 1, 2)
    ao = o.reshape(bsz, N_KV_HEADS, group, qlen, HEAD_DIM).transpose(0, 3, 1, 2, 4).reshape(bsz * qlen, -1)
    return ao, position_major(kt_next), position_major(vt_next)


CONV_HALO = 32
CONV_ROWS = 64
CONV_COLS = 256


def _conv_prompt_kernel(uc_ref, up_ref, w_ref, b_ref, y_ref, sh_ref):
    tile = uc_ref.shape[0]
    n = pl.program_id(1)
    halo = jnp.where(n > 0, up_ref[...], 0.0)
    sh_ref[0, :CONV_HALO, :] = halo
    sh_ref[0, CONV_HALO:, :] = uc_ref[...]
    keep = tile + CONV_HALO - SUBLANES
    for r in range(1, SUBLANES):
        sh_ref[r, :keep, :] = sh_ref[0, r:r + keep, :]
    first = CONV_HALO - (CONV_WIDTH - 1)
    ch = uc_ref.shape[1]

    def rows_step(rc, carry):
        r0 = pl.multiple_of(rc * CONV_ROWS, CONV_ROWS)
        for cc in range(ch // CONV_COLS):
            cs = slice(cc * CONV_COLS, (cc + 1) * CONV_COLS)
            acc = jnp.zeros((CONV_ROWS, CONV_COLS), F32)
            for j in range(CONV_WIDTH):
                off = first + j
                a, r = off // SUBLANES, off % SUBLANES
                acc = acc + sh_ref[r, pl.ds(r0 + a * SUBLANES, CONV_ROWS), cs] * w_ref[j:j + 1, cs]
            y_ref[pl.ds(r0, CONV_ROWS), cs] = acc + b_ref[:, cs]
        return carry

    lax.fori_loop(0, tile // CONV_ROWS, rows_step, 0)


def _conv_prompt(u, w, b, bsz, slen, tile=512):
    ch = u.shape[1]
    nt = slen // tile
    per = tile // CONV_HALO
    return pl.pallas_call(
        _conv_prompt_kernel,
        grid=(bsz, nt),
        in_specs=[
            pl.BlockSpec((tile, ch), lambda bb, n: (bb * nt + n, 0)),
            pl.BlockSpec((CONV_HALO, ch), lambda bb, n: (jnp.maximum((bb * nt + n) * per - 1, 0), 0)),
            pl.BlockSpec((CONV_WIDTH, ch), lambda bb, n: (0, 0)),
            pl.BlockSpec((1, ch), lambda bb, n: (0, 0)),
        ],
        out_specs=pl.BlockSpec((tile, ch), lambda bb, n: (bb * nt + n, 0)),
        out_shape=jax.ShapeDtypeStruct((bsz * slen, ch), F32),
        scratch_shapes=[pltpu.VMEM((SUBLANES, tile + CONV_HALO, ch), F32)],
        compiler_params=_cparams(("arbitrary", "arbitrary")),
        name="conv_prompt",
    )(u, u, w, b.reshape(1, ch))


def _conv_sample_kernel(sc_ref, un_ref, w_ref, b_ref, y_ref, next_ref):
    nctx = sc_ref.shape[0]
    qlen = un_ref.shape[0]
    ch = w_ref.shape[1]

    def row(t):
        return sc_ref[t] if t < nctx else un_ref[t - nctx]

    for i in range(qlen):
        acc = jnp.zeros((sc_ref.shape[1], ch), F32)
        for j in range(CONV_WIDTH):
            acc = acc + row(i + j) * w_ref[j:j + 1, :]
        y_ref[i] = acc + b_ref[...]
    for t in range(nctx):
        next_ref[t] = row(t + qlen)


def _conv_sample(state_t, u_new_t, w, b, seq_per_step=16):
    nctx, bsz, ch = state_t.shape
    qlen = u_new_t.shape[0]
    seq = lambda i: (0, i, 0)
    return pl.pallas_call(
        _conv_sample_kernel,
        grid=(bsz // seq_per_step,),
        in_specs=[
            pl.BlockSpec((nctx, seq_per_step, ch), seq),
            pl.BlockSpec((qlen, seq_per_step, ch), seq),
            pl.BlockSpec((CONV_WIDTH, ch), lambda i: (0, 0)),
            pl.BlockSpec((1, ch), lambda i: (0, 0)),
        ],
        out_specs=[pl.BlockSpec((qlen, seq_per_step, ch), seq), pl.BlockSpec((nctx, seq_per_step, ch), seq)],
        out_shape=[jax.ShapeDtypeStruct((qlen, bsz, ch), F32), jax.ShapeDtypeStruct((nctx, bsz, ch), F32)],
        compiler_params=_cparams(("arbitrary",)),
        name="conv_sample",
    )(state_t, u_new_t, w, b.reshape(1, ch))


def _merge_kernel(y_ref, ao_ref, ga_ref, gc_ref, x_ref, lnw_ref, lnb_ref, wco_ref, wo_ref, fw_ref, wr_ref,
                  h_ref, xn_ref, lg_ref):
    y = y_ref[...]
    mu = jnp.mean(y, axis=-1, keepdims=True)
    yc = y - mu
    var = jnp.mean(yc * yc, axis=-1, keepdims=True)
    yl = yc * lax.rsqrt(var + EPS) * lnw_ref[...] + lnb_ref[...]
    act = yl * jax.nn.sigmoid(yl)
    conv_o = jnp.dot(act.astype(BF16), wco_ref[...], preferred_element_type=F32)
    mix = ga_ref[...].astype(F32) * ao_ref[...].astype(F32) + gc_ref[...].astype(F32) * conv_o
    h = x_ref[...] + jnp.dot(mix.astype(BF16), wo_ref[...], preferred_element_type=F32)
    h_ref[...] = h
    ms = jnp.mean(h * h, axis=-1, keepdims=True)
    xn = (h * lax.rsqrt(ms + EPS) * fw_ref[...]).astype(BF16)
    xn_ref[...] = xn
    lg_ref[...] = jnp.dot(xn, wr_ref[...], preferred_element_type=F32)


def _merge(y, ao, gates, row0, x, ln_w, ln_b, wco, wo, fw, wr, tile=256):
    t, d = x.shape
    ch = y.shape[1]
    off = row0 // tile
    row = lambda i: (i, 0)
    const = lambda shape: pl.BlockSpec(shape, lambda i: (0, 0), pipeline_mode=pl.Buffered(1))
    return pl.pallas_call(
        _merge_kernel,
        grid=(t // tile,),
        in_specs=[
            pl.BlockSpec((tile, ch), row),
            pl.BlockSpec((tile, d), row),
            pl.BlockSpec((tile, d), lambda i: (i + off, 0)),
            pl.BlockSpec((tile, d), lambda i: (i + off, 1)),
            pl.BlockSpec((tile, d), row),
            const((1, ch)), const((1, ch)), const((ch, d)), const((d, d)), const((1, d)),
            const((d, LANES)),
        ],
        out_specs=[pl.BlockSpec((tile, d), row), pl.BlockSpec((tile, d), row), pl.BlockSpec((tile, LANES), row)],
        out_shape=[jax.ShapeDtypeStruct((t, d), F32), jax.ShapeDtypeStruct((t, d), BF16),
                   jax.ShapeDtypeStruct((t, LANES), F32)],
        compiler_params=_cparams(("arbitrary",)),
        name="merge",
    )(y, ao, gates, gates, x, ln_w.reshape(1, ch), ln_b.reshape(1, ch), wco, wo, fw.reshape(1, d), wr)


def _route_kernel(lp_ref, ls_ref, tri_ref, upper_ref, meta_ref, metat_ref, cnt_ref, *, n_p):
    i = pl.program_id(0)

    def body(l_ref):
        lg = l_ref[...]
        shape = lg.shape
        lane = lax.broadcasted_iota(jnp.int32, shape, 1)
        big = jnp.int32(LANES)
        is_g = (lane >= N_EXPERTS) & (lane < N_EXPERTS + N_GROUPS)
        gl = jnp.where(is_g, lg, -jnp.inf)
        gmax = jnp.max(gl, axis=-1, keepdims=True)
        g_star = jnp.min(jnp.where(gl == gmax, lane - N_EXPERTS, big), axis=-1, keepdims=True)
        pg_star = 1.0 / jnp.sum(jnp.exp(gl - gmax), axis=-1, keepdims=True)
        in_g = (lane < N_EXPERTS) & (_div_pow2(lane, EXPERTS_PER_GROUP) == g_star)
        el = jnp.where(in_g, lg, -jnp.inf)
        v1 = jnp.max(el, axis=-1, keepdims=True)
        i1 = jnp.min(jnp.where(el == v1, lane, big), axis=-1, keepdims=True)
        el2 = jnp.where(lane == i1, -jnp.inf, el)
        v2 = jnp.max(el2, axis=-1, keepdims=True)
        i2 = jnp.min(jnp.where(el2 == v2, lane, big), axis=-1, keepdims=True)
        e2 = jnp.exp(v2 - v1)
        w1 = pg_star * (1.0 / (1.0 + e2))
        w2 = pg_star * (e2 / (1.0 + e2))
        oh1 = lane == i1
        oh2 = lane == i2
        cnt = jnp.where(oh1 | oh2, 1.0, 0.0)
        before = jnp.dot(tri_ref[...], cnt.astype(BF16), preferred_element_type=F32)
        tot = jnp.sum(cnt, axis=0, keepdims=True)
        runs = jnp.floor((tot + (CHUNK_ROWS - 1)) * (1.0 / CHUNK_ROWS))
        runs8 = jnp.broadcast_to(runs, (SUBLANES, LANES)).astype(BF16)
        start = jnp.dot(runs8, upper_ref[...], preferred_element_type=F32)[0:1] * CHUNK_ROWS
        slot = before + start
        s1 = jnp.sum(jnp.where(oh1, slot, 0.0), axis=-1, keepdims=True)
        s2 = jnp.sum(jnp.where(oh2, slot, 0.0), axis=-1, keepdims=True)
        cols = [i1.astype(F32), i2.astype(F32), w1, w2, s1, s2]
        meta = jnp.zeros(shape, F32)
        for c, val in enumerate(cols):
            meta = jnp.where(lane == c, val, meta)
        meta_ref[...] = meta
        metat_ref[...] = meta.T[:META_FIELDS]
        cnt_ref[...] = tot

    @pl.when(i < n_p)
    def _():
        body(lp_ref)

    @pl.when(i >= n_p)
    def _():
        body(ls_ref)


def _route(lg_p, lg_s):
    tile = TOKEN_BLOCK
    tp, ts = lg_p.shape[0], lg_s.shape[0]
    n_p, n_s = tp // tile, ts // tile
    nb = n_p + n_s
    tri = (jnp.arange(tile)[:, None] > jnp.arange(tile)[None, :]).astype(BF16)
    upper = (jnp.arange(LANES)[:, None] < jnp.arange(LANES)[None, :]).astype(BF16)
    return pl.pallas_call(
        functools.partial(_route_kernel, n_p=n_p),
        grid=(nb,),
        in_specs=[
            pl.BlockSpec((tile, LANES), lambda i: (jnp.minimum(i, n_p - 1), 0)),
            pl.BlockSpec((tile, LANES), lambda i: (jnp.maximum(i - n_p, 0), 0)),
            pl.BlockSpec((tile, tile), lambda i: (0, 0)),
            pl.BlockSpec((LANES, LANES), lambda i: (0, 0)),
        ],
        out_specs=[pl.BlockSpec((tile, LANES), lambda i: (i, 0)),
                   pl.BlockSpec((None, META_FIELDS, tile), lambda i: (i, 0, 0)),
                   pl.BlockSpec((None, 1, LANES), lambda i: (i, 0, 0))],
        out_shape=[jax.ShapeDtypeStruct((tp + ts, LANES), F32),
                   jax.ShapeDtypeStruct((nb, META_FIELDS, tile), F32),
                   jax.ShapeDtypeStruct((nb, 1, LANES), F32)],
        compiler_params=_cparams(("arbitrary",)),
        name="route",
    )(lg_p, lg_s, tri, upper)


def _plan_sizes(n_tok):
    nb = n_tok // TOKEN_BLOCK
    pad_per_block = N_EXPERTS * (CHUNK_ROWS - 1)
    max_chunks = _round_up(-(-(TOKEN_BLOCK * TOP_K + pad_per_block) // CHUNK_ROWS), SUBLANES)
    n_rows = _round_up(n_tok * TOP_K + nb * pad_per_block + N_EXPERTS * (ROW_PAD - 1), ROW_PAD)
    n_items = (n_rows + N_EXPERTS * (EXPERT_ROWS - ROW_PAD)) // EXPERT_ROWS
    max_zero = N_EXPERTS * (ROW_PAD // CHUNK_ROWS)
    return nb, max_chunks, n_rows, n_items, max_zero


def _dispatch_plan(counts, max_chunks, n_items, max_zero):
    i32 = jnp.int32
    c = counts[:, 0, :N_EXPERTS].astype(i32)
    run = (c + CHUNK_ROWS - 1) // CHUNK_ROWS
    run_end = jnp.cumsum(run, axis=1)
    src_start = run_end - run
    nchunk = run_end[:, -1]
    seg = jnp.sum(run, axis=0) * CHUNK_ROWS
    padded = (seg + ROW_PAD - 1) // ROW_PAD * ROW_PAD
    pad_end = jnp.cumsum(padded)
    pad_start = pad_end - padded
    dst_start = pad_start[None, :] // CHUNK_ROWS + (jnp.cumsum(run, axis=0) - run)
    jj = jnp.arange(max_chunks, dtype=i32)[None, :, None]
    in_run = (src_start[:, None, :] <= jj) & (jj < run_end[:, None, :])
    chunk_dst = jnp.sum(jnp.where(in_run, (dst_start - src_start)[:, None, :] + jj, 0), axis=-1) * CHUNK_ROWS
    tail = (padded - seg) // CHUNK_ROWS
    tail_end = jnp.cumsum(tail)
    tail_start = tail_end - tail
    z = jnp.arange(max_zero, dtype=i32)[:, None]
    in_tail = (tail_start[None, :] <= z) & (z < tail_end[None, :])
    zero_base = (pad_start + seg) // CHUNK_ROWS - tail_start
    zero_dst = jnp.sum(jnp.where(in_tail, zero_base[None, :] + z, 0), axis=-1) * CHUNK_ROWS
    nzero = tail_end[-1]
    items_e = (padded + EXPERT_ROWS - 1) // EXPERT_ROWS
    item_end = jnp.cumsum(items_e)
    item_start = item_end - items_e
    w = jnp.arange(n_items, dtype=i32)[:, None]
    in_item = (item_start[None, :] <= w) & (w < item_end[None, :])
    k = w - item_start[None, :]
    e_ids = jnp.arange(N_EXPERTS, dtype=i32)
    last_e = jnp.max(jnp.where(items_e > 0, e_ids, 0))
    item_e = jnp.where(w[:, 0] < item_end[-1], jnp.sum(jnp.where(in_item, e_ids[None, :], 0), axis=-1), last_e)
    item_row0 = jnp.sum(jnp.where(in_item, pad_start[None, :] + k * EXPERT_ROWS, 0), axis=-1)
    item_rows = jnp.sum(jnp.where(in_item, jnp.clip(padded[None, :] - k * EXPERT_ROWS, 0, EXPERT_ROWS), 0), axis=-1)
    item_nsub = item_rows // ROW_PAD
    return (chunk_dst.reshape(-1).astype(i32), nchunk.astype(i32), zero_dst.astype(i32),
            nzero.reshape(1).astype(i32), item_e.astype(i32), item_row0.astype(i32), item_nsub.astype(i32))


def _scatter_kernel(cd_ref, nc_ref, zd_ref, nz_ref, mt_ref, xp_ref, xs_ref, out_hbm, loc_ref, zero_ref, sem,
                    zero_sem, *, n_p, max_chunks):
    i = pl.program_id(0)
    buf = i % 2
    n_slots = loc_ref.shape[1]
    d = xp_ref.shape[1]
    mt = mt_ref[...]
    w1, w2 = mt[2:3, :], mt[3:4, :]
    s1, s2 = mt[4:5, :].astype(jnp.int32), mt[5:6, :].astype(jnp.int32)
    slot = lax.broadcasted_iota(jnp.int32, (n_slots, mt.shape[1]), 0)
    p1 = slot == s1
    p2 = slot == s2
    perm = jnp.where(p1 | p2, 1.0, 0.0).astype(BF16)
    wcol = jnp.sum(jnp.where(p1, w1, 0.0) + jnp.where(p2, w2, 0.0), axis=-1, keepdims=True)
    half = d // 2
    loc_ref[buf, :, half:] = pltpu.bitcast(jnp.broadcast_to(wcol, (n_slots, LANES)), jnp.uint32)

    def sort_rows(x_ref):
        rows = jnp.dot(perm, x_ref[...], preferred_element_type=F32)
        loc_ref[buf, :, :half] = _pack_pairs(rows[:, :half], rows[:, half:])

    @pl.when(i < n_p)
    def _():
        sort_rows(xp_ref)

    @pl.when(i >= n_p)
    def _():
        sort_rows(xs_ref)

    def rows_at(r):
        return pl.ds(pl.multiple_of(r, CHUNK_ROWS), CHUNK_ROWS)

    def chunk_copy(blk, j):
        return pltpu.make_async_copy(loc_ref.at[blk % 2, rows_at(j * CHUNK_ROWS)],
                                     out_hbm.at[rows_at(cd_ref[blk * max_chunks + j])], sem.at[blk % 2])

    def zero_copy(z):
        return pltpu.make_async_copy(zero_ref, out_hbm.at[rows_at(zd_ref[z])], zero_sem)

    def loop(n, fn):
        def body(j, carry):
            fn(j)
            return carry
        lax.fori_loop(0, n, body, 0)

    @pl.when(i > 0)
    def _():
        loop(nc_ref[i - 1], lambda j: chunk_copy(i - 1, j).wait())

    loop(nc_ref[i], lambda j: chunk_copy(i, j).start())

    @pl.when(i == pl.num_programs(0) - 1)
    def _():
        zero_ref[...] = jnp.zeros_like(zero_ref)
        loop(nz_ref[0], lambda z: zero_copy(z).start())
        loop(nc_ref[i], lambda j: chunk_copy(i, j).wait())
        loop(nz_ref[0], lambda z: zero_copy(z).wait())


def _scatter_rows(xn_p, xn_s, metat, chunk_dst, nchunk, zero_dst, nzero, max_chunks, n_rows):
    tile = TOKEN_BLOCK
    tp, d = xn_p.shape
    ts = xn_s.shape[0]
    n_p, n_s = tp // tile, ts // tile
    n_slots = max_chunks * CHUNK_ROWS
    width = d // 2 + LANES
    u32 = jnp.uint32
    return pl.pallas_call(
        functools.partial(_scatter_kernel, n_p=n_p, max_chunks=max_chunks),
        grid_spec=pltpu.PrefetchScalarGridSpec(
            num_scalar_prefetch=4,
            grid=(n_p + n_s,),
            in_specs=[
                pl.BlockSpec((None, META_FIELDS, tile), lambda i, *_: (i, 0, 0)),
                pl.BlockSpec((tile, d), lambda i, *_: (jnp.minimum(i, n_p - 1), 0)),
                pl.BlockSpec((tile, d), lambda i, *_: (jnp.maximum(i - n_p, 0), 0)),
            ],
            out_specs=pl.BlockSpec(memory_space=pl.ANY),
            scratch_shapes=[pltpu.VMEM((2, n_slots, width), u32), pltpu.VMEM((CHUNK_ROWS, width), u32),
                            pltpu.SemaphoreType.DMA((2,)), pltpu.SemaphoreType.DMA(())],
        ),
        out_shape=jax.ShapeDtypeStruct((n_rows, width), u32),
        compiler_params=_cparams(("arbitrary",)),
        name="scatter_rows",
    )(chunk_dst, nchunk, zero_dst, nzero, metat, xn_p, xn_s)


def _expert_kernel(e_ref, row0_ref, nsub_ref, xs_hbm, wg_ref, wu_ref, wd_ref, out_hbm,
                   xf_ref, xb_ref, acc_ref, ob_ref, pend_ref, in_sem, out_sem):
    w = pl.program_id(0)
    c = pl.program_id(1)
    n_items = pl.num_programs(0)
    last_c = pl.num_programs(1) - 1
    nsub = nsub_ref[w]
    d = acc_ref.shape[1]
    half = d // 2
    slot = w % 2

    def sub_rows(s):
        return pl.ds(pl.multiple_of(s * ROW_PAD, ROW_PAD), ROW_PAD)

    def hbm_rows(item, s):
        return pl.ds(pl.multiple_of(row0_ref[item] + s * ROW_PAD, ROW_PAD), ROW_PAD)

    def in_copy(item, s):
        return pltpu.make_async_copy(xs_hbm.at[hbm_rows(item, s)], xf_ref.at[item % 2, sub_rows(s)],
                                     in_sem.at[item % 2])

    def out_copy(item, s):
        return pltpu.make_async_copy(ob_ref.at[sub_rows(s)], out_hbm.at[hbm_rows(item, s)], out_sem)

    def loop(n, fn):
        def body(s, carry):
            fn(s)
            return carry
        lax.fori_loop(0, n, body, 0)

    @pl.when(c == 0)
    def _():
        @pl.when(w == 0)
        def _():
            pend_ref[0] = 0
            pend_ref[1] = 0
            loop(nsub, lambda s: in_copy(w, s).start())

        loop(nsub, lambda s: in_copy(w, s).wait())
        nxt = jnp.minimum(w + 1, n_items - 1)

        @pl.when(w + 1 < n_items)
        def _():
            loop(nsub_ref[nxt], lambda s: in_copy(nxt, s).start())

    def drain_out():
        prev = pend_ref[1]
        loop(pend_ref[0], lambda s: out_copy(prev, s).wait())
        pend_ref[0] = 0

    for n in range(1, EXPERT_ROWS // ROW_PAD + 1):
        m = n * ROW_PAD

        @pl.when(nsub == n)
        def _(m=m):
            @pl.when(c == 0)
            def _():
                acc_ref[:m, :] = jnp.zeros((m, d), F32)
                xa, xb = _unpack_pairs(xf_ref[slot, :m, :half])
                xb_ref[:m, :half] = xa.astype(BF16)
                xb_ref[:m, half:] = xb.astype(BF16)

            x = xb_ref[:m, :]
            g = jnp.dot(x, wg_ref[...].astype(BF16), preferred_element_type=F32)
            u = jnp.dot(x, wu_ref[...].astype(BF16), preferred_element_type=F32)
            hmid = (g * jax.nn.sigmoid(g) * u).astype(BF16)
            acc_ref[:m, :] += jnp.dot(hmid, wd_ref[...].astype(BF16), preferred_element_type=F32)

            @pl.when(c == last_c)
            def _():
                drain_out()
                mix_w = pltpu.bitcast(xf_ref[slot, :m, half:half + 1], F32)
                y = (acc_ref[:m, :] * mix_w).astype(BF16).astype(F32)
                ob_ref[:m, :] = _pack_pairs(y[:, :half], y[:, half:])
                loop(nsub, lambda s: out_copy(w, s).start())
                pend_ref[0] = nsub
                pend_ref[1] = w

    @pl.when((w == n_items - 1) & (c == last_c))
    def _():
        drain_out()


def _experts(xs, item_e, item_row0, item_nsub, w_gate, w_up, w_down):
    n_rows, width = xs.shape
    d = w_gate.shape[1]
    assert width == d // 2 + LANES
    d_exp = w_gate.shape[2]
    nc = d_exp // D_CHUNK
    n_items = item_e.shape[0]

    def chunk_of(w, c, nsub_ref):
        return jnp.where(nsub_ref[w] > 0, c, nc - 1)

    return pl.pallas_call(
        _expert_kernel,
        grid_spec=pltpu.PrefetchScalarGridSpec(
            num_scalar_prefetch=3,
            grid=(n_items, nc),
            in_specs=[
                pl.BlockSpec(memory_space=pl.ANY),
                pl.BlockSpec((None, d, D_CHUNK), lambda w, c, e, r, n: (e[w], 0, chunk_of(w, c, n))),
                pl.BlockSpec((None, d, D_CHUNK), lambda w, c, e, r, n: (e[w], 0, chunk_of(w, c, n))),
                pl.BlockSpec((None, D_CHUNK, d), lambda w, c, e, r, n: (e[w], chunk_of(w, c, n), 0)),
            ],
            out_specs=pl.BlockSpec(memory_space=pl.ANY),
            scratch_shapes=[
                pltpu.VMEM((2, EXPERT_ROWS, width), jnp.uint32),
                pltpu.VMEM((EXPERT_ROWS, d), BF16),
                pltpu.VMEM((EXPERT_ROWS, d), F32),
                pltpu.VMEM((EXPERT_ROWS, d // 2), jnp.uint32),
                pltpu.SMEM((2,), jnp.int32),
                pltpu.SemaphoreType.DMA((2,)),
                pltpu.SemaphoreType.DMA(()),
            ],
        ),
        out_shape=jax.ShapeDtypeStruct((n_rows, d // 2), jnp.uint32),
        compiler_params=_cparams(("arbitrary", "arbitrary")),
        name="experts",
    )(item_e, item_row0, item_nsub, xs, w_gate, w_up, w_down)


COMBINE_SPLIT = 2
SLOT_STEP = 256


def _combine_kernel(cd_ref, nc_ref, meta_ref, hp_ref, hs_ref, yb_hbm, yp_ref, ys_ref, loc_ref, sem,
                    *, n_p, max_chunks):
    i = pl.program_id(0)
    j = pl.program_id(1)
    nb = pl.num_programs(0)
    buf = i % 2
    n_slots, half = loc_ref.shape[1], loc_ref.shape[2]
    d = 2 * half

    def rows_at(r):
        return pl.ds(pl.multiple_of(r, CHUNK_ROWS), CHUNK_ROWS)

    def chunk_copy(blk, c):
        return pltpu.make_async_copy(yb_hbm.at[rows_at(cd_ref[blk * max_chunks + c])],
                                     loc_ref.at[blk % 2, rows_at(c * CHUNK_ROWS)], sem.at[blk % 2])

    def fetch(blk):
        n = nc_ref[blk]

        def start(c, carry):
            chunk_copy(blk, c).start()
            return carry
        lax.fori_loop(0, n, start, 0)

        def clear(c, carry):
            loc_ref[blk % 2, rows_at(c * CHUNK_ROWS), :] = jnp.zeros((CHUNK_ROWS, half), jnp.uint32)
            return carry
        lax.fori_loop(n, max_chunks, clear, 0)

    @pl.when(j == 0)
    def _():
        @pl.when(i == 0)
        def _():
            fetch(i)

        def wait(c, carry):
            chunk_copy(i, c).wait()
            return carry
        lax.fori_loop(0, nc_ref[i], wait, 0)

        @pl.when(i + 1 < nb)
        def _():
            fetch(jnp.minimum(i + 1, nb - 1))

    meta = meta_ref[...]
    s1 = meta[:, 4:5].astype(jnp.int32)
    s2 = meta[:, 5:6].astype(jnp.int32)
    rows = meta.shape[0]
    f = jnp.zeros((rows, d), F32)
    for r0 in range(0, n_slots, SLOT_STEP):
        slot = lax.broadcasted_iota(jnp.int32, (rows, SLOT_STEP), 1) + r0
        pt = jnp.where((slot == s1) | (slot == s2), 1.0, 0.0).astype(BF16)
        ya, yb = _unpack_pairs(loc_ref[buf, r0:r0 + SLOT_STEP, :])
        f = f + jnp.dot(pt, jnp.concatenate([ya, yb], axis=1).astype(BF16), preferred_element_type=F32)
    g = i * pl.num_programs(1) + j

    @pl.when(g < n_p)
    def _():
        yp_ref[...] = hp_ref[...] + f

    @pl.when(g >= n_p)
    def _():
        ys_ref[...] = hs_ref[...] + f


def _combine(h_p, h_s, meta, chunk_dst, nchunk, yb, max_chunks):
    tp, d = h_p.shape
    ts = h_s.shape[0]
    tile = TOKEN_BLOCK // COMBINE_SPLIT
    n_p, n_s = tp // tile, ts // tile
    nb = (tp + ts) // TOKEN_BLOCK
    n_slots = max_chunks * CHUNK_ROWS
    assert n_slots % SLOT_STEP == 0
    blk = lambda i, j: i * COMBINE_SPLIT + j
    p_idx = lambda i, j, *_: (jnp.minimum(blk(i, j), n_p - 1), 0)
    s_idx = lambda i, j, *_: (jnp.maximum(blk(i, j) - n_p, 0), 0)
    return pl.pallas_call(
        functools.partial(_combine_kernel, n_p=n_p, max_chunks=max_chunks),
        grid_spec=pltpu.PrefetchScalarGridSpec(
            num_scalar_prefetch=2,
            grid=(nb, COMBINE_SPLIT),
            in_specs=[
                pl.BlockSpec((tile, LANES), lambda i, j, *_: (blk(i, j), 0)),
                pl.BlockSpec((tile, d), p_idx),
                pl.BlockSpec((tile, d), s_idx),
                pl.BlockSpec(memory_space=pl.ANY),
            ],
            out_specs=[pl.BlockSpec((tile, d), p_idx), pl.BlockSpec((tile, d), s_idx)],
            scratch_shapes=[pltpu.VMEM((2, n_slots, d // 2), jnp.uint32), pltpu.SemaphoreType.DMA((2,))],
        ),
        out_shape=[jax.ShapeDtypeStruct((tp, d), F32), jax.ShapeDtypeStruct((ts, d), F32)],
        compiler_params=_cparams(("arbitrary", "arbitrary")),
        name="combine",
    )(chunk_dst, nchunk, meta, h_p, h_s, yb)


def _layer(x_p, x_s, ck, cv, sc, attn_norm_w, w_in, b_gate, q_norm_w, k_norm_w, sinks, conv_dw_w, conv_dw_b,
           conv_ln_w, conv_ln_b, w_conv_out, w_out, ffn_norm_w, router_group, router_expert, w_gate, w_up,
           w_down):
    bsz, slen, d = x_p.shape
    dbs, dlen, _ = x_s.shape
    tp, ts = bsz * slen, dbs * dlen
    conv_ch = conv_dw_w.shape[1]
    kvw = N_KV_HEADS * HEAD_DIM
    xp2, xs2 = x_p.reshape(tp, d), x_s.reshape(ts, d)

    xn = _prenorm(xp2, xs2, attn_norm_w)
    q, k, v, u, gates = _project(xn, w_in, b_gate, q_norm_w, k_norm_w, d, conv_ch)

    ao_p = _attn_prompt(q, k, v, sinks, bsz, slen)
    ao_s, ck_next, cv_next = _attn_sample(q[tp:], ck, k[tp:], cv, v[tp:], sinks)

    y_p = _conv_prompt(u, conv_dw_w, conv_dw_b, bsz, slen)
    sc_t = sc.transpose(1, 0, 2)
    u_s_t = u[tp:].reshape(dbs, dlen, conv_ch).transpose(1, 0, 2)
    y_s_t, sc_next_t = _conv_sample(sc_t, u_s_t, conv_dw_w, conv_dw_b)
    y_s = y_s_t.transpose(1, 0, 2).reshape(ts, conv_ch)

    wco = w_conv_out.astype(BF16)
    wo = w_out.astype(BF16)
    wr = jnp.concatenate([router_expert, router_group,
                          jnp.zeros((d, LANES - N_EXPERTS - N_GROUPS), router_expert.dtype)], axis=1).astype(BF16)
    margs = (conv_ln_w, conv_ln_b, wco, wo, ffn_norm_w, wr)
    h_p, xn_p, lg_p = _merge(y_p, ao_p, gates, 0, xp2, *margs)
    h_s, xn_s, lg_s = _merge(y_s, ao_s, gates, tp, xs2, *margs)

    meta, metat, counts = _route(lg_p, lg_s)
    _, max_chunks, n_rows, n_items, max_zero = _plan_sizes(tp + ts)
    chunk_dst, nchunk, zero_dst, nzero, item_e, item_row0, item_nsub = _dispatch_plan(
        counts, max_chunks, n_items, max_zero)
    xs_rows = _scatter_rows(xn_p, xn_s, metat, chunk_dst, nchunk, zero_dst, nzero, max_chunks, n_rows)
    yb = _experts(xs_rows, item_e, item_row0, item_nsub, w_gate, w_up, w_down)
    out_p, out_s = _combine(h_p, h_s, meta, chunk_dst, nchunk, yb, max_chunks)

    def seq_tails(a, n):
        return jnp.stack([lax.slice_in_dim(a, (b + 1) * slen - n, (b + 1) * slen, axis=0) for b in range(bsz)])

    new_p = (seq_tails(k, WINDOW).reshape(bsz, WINDOW, N_KV_HEADS, HEAD_DIM),
             seq_tails(v, WINDOW).reshape(bsz, WINDOW, N_KV_HEADS, HEAD_DIM),
             seq_tails(u, CONV_WIDTH - 1))
    new_s = (ck_next, cv_next, sc_next_t.transpose(1, 0, 2))
    return out_p.reshape(bsz, slen, d), out_s.reshape(dbs, dlen, d), new_p, new_s


def kernel(x_prompt, x_sample, cache_k, cache_v, state_conv, attn_norm_w, w_in, b_gate, q_norm_w, k_norm_w,
           sinks, conv_dw_w, conv_dw_b, conv_ln_w, conv_ln_b, w_conv_out, w_out, ffn_norm_w, router_group,
           router_expert, w_gate, w_up, w_down):
    x_p, x_s = x_prompt, x_sample
    kp, vp, cp, ks_, vs_, cs_ = [], [], [], [], [], []
    for l in range(cache_k.shape[0]):
        x_p, x_s, new_p, new_s = _layer(
            x_p, x_s, cache_k[l], cache_v[l], state_conv[l], attn_norm_w[l], w_in[l], b_gate[l], q_norm_w[l],
            k_norm_w[l], sinks[l], conv_dw_w[l], conv_dw_b[l], conv_ln_w[l], conv_ln_b[l], w_conv_out[l],
            w_out[l], ffn_norm_w[l], router_group[l], router_expert[l], w_gate[l], w_up[l], w_down[l])
        kp.append(new_p[0]); vp.append(new_p[1]); cp.append(new_p[2])
        ks_.append(new_s[0]); vs_.append(new_s[1]); cs_.append(new_s[2])
    return (x_p, x_s, jnp.stack(kp), jnp.stack(vp), jnp.stack(cp),
            jnp.stack(ks_), jnp.stack(vs_), jnp.stack(cs_))
```

```python
import functools

import jax
import jax.numpy as jnp
from jax import lax
from jax.experimental import pallas as pl
from jax.experimental.pallas import tpu as pltpu

F32 = jnp.float32
BF16 = jnp.bfloat16
EPS = 1e-6
NEG_INF = -1e30

HEAD_DIM = 64
N_KV_HEADS = 4
WINDOW = 128
CONV_WIDTH = 31
N_GROUPS = 4
EXPERTS_PER_GROUP = 8
N_EXPERTS = N_GROUPS * EXPERTS_PER_GROUP
TOP_K = 2

LANES = 128
SUBLANES = 8
MXU_DIM = 256
VMEM_LIMIT = 56 * 1024 * 1024

TOKEN_BLOCK = 512
CHUNK_ROWS = SUBLANES
ROW_PAD = 128
EXPERT_ROWS = 768
D_CHUNK = 512
META_FIELDS = 8


def _cparams(sem):
    return pltpu.CompilerParams(dimension_semantics=sem, vmem_limit_bytes=VMEM_LIMIT)


def _log2(n):
    assert n > 0 and n & (n - 1) == 0, n
    return n.bit_length() - 1


def _div_pow2(x, n):
    return lax.shift_right_logical(x, jnp.int32(_log2(n)))


def _mod_pow2(x, n):
    _log2(n)
    return x & jnp.int32(n - 1)


def _round_up(x, m):
    return (x + m - 1) // m * m


HIGH_HALF = 0xFFFF0000


def _pack_pairs(a, b):
    ua = pltpu.bitcast(a, jnp.uint32) & jnp.uint32(HIGH_HALF)
    return ua | (pltpu.bitcast(b, jnp.uint32) >> 16)


def _unpack_pairs(w):
    return (pltpu.bitcast(w & jnp.uint32(HIGH_HALF), F32), pltpu.bitcast(w << 16, F32))


def _prenorm_kernel(xp_ref, xs_ref, w_ref, o_ref, *, n_p):
    i = pl.program_id(0)

    def body(x_ref):
        x = x_ref[...]
        ms = jnp.mean(x * x, axis=-1, keepdims=True)
        o_ref[...] = (x * lax.rsqrt(ms + EPS) * w_ref[...]).astype(o_ref.dtype)

    @pl.when(i < n_p)
    def _():
        body(xp_ref)

    @pl.when(i >= n_p)
    def _():
        body(xs_ref)


def _prenorm(xp, xs, w, tile=512):
    tp, d = xp.shape
    ts = xs.shape[0]
    n_p, n_s = tp // tile, ts // tile
    return pl.pallas_call(
        functools.partial(_prenorm_kernel, n_p=n_p),
        grid=(n_p + n_s,),
        in_specs=[
            pl.BlockSpec((tile, d), lambda i: (jnp.minimum(i, n_p - 1), 0)),
            pl.BlockSpec((tile, d), lambda i: (jnp.maximum(i - n_p, 0), 0)),
            pl.BlockSpec((1, d), lambda i: (0, 0)),
        ],
        out_specs=pl.BlockSpec((tile, d), lambda i: (i, 0)),
        out_shape=jax.ShapeDtypeStruct((tp + ts, d), BF16),
        compiler_params=_cparams(("arbitrary",)),
        name="prenorm",
    )(xp, xs, w.reshape(1, d))


def _head_rms(z, hw, bd_ref):
    ss = z * z
    hi = ss.astype(BF16)
    lo = (ss - hi.astype(F32)).astype(BF16)
    bd = bd_ref[...]
    tot = jnp.dot(hi, bd, preferred_element_type=F32) + jnp.dot(lo, bd, preferred_element_type=F32)
    return z * lax.rsqrt(tot * (1.0 / HEAD_DIM) + EPS) * hw


def _load_w(i, w_ref, wb_ref):
    @pl.when(i == 0)
    def _():
        wb_ref[...] = w_ref[...].astype(BF16)


PROJ_SUB = 544


def _row_groups(n_rows):
    sub = PROJ_SUB if n_rows % PROJ_SUB == 0 else n_rows
    return [slice(r0, r0 + sub) for r0 in range(0, n_rows, sub)]


def _proj_q_kernel(x_ref, w_ref, hw_ref, bd_ref, o_ref, wb_ref):
    _load_w(pl.program_id(1), w_ref, wb_ref)
    for rs in _row_groups(x_ref.shape[0]):
        z = jnp.dot(x_ref[rs, :], wb_ref[...], preferred_element_type=F32)
        for c in range(z.shape[1] // MXU_DIM):
            sl = slice(c * MXU_DIM, (c + 1) * MXU_DIM)
            o_ref[rs, sl] = _head_rms(z[:, sl], hw_ref[...], bd_ref).astype(o_ref.dtype)


def _proj_kv_kernel(x_ref, w_ref, hw_ref, bd_ref, k_ref, v_ref, wb_ref):
    _load_w(pl.program_id(1), w_ref, wb_ref)
    kw = k_ref.shape[1]
    for rs in _row_groups(x_ref.shape[0]):
        z = jnp.dot(x_ref[rs, :], wb_ref[...], preferred_element_type=F32)
        k_ref[rs, :] = _head_rms(z[:, :kw], hw_ref[...], bd_ref)
        v_ref[rs, :] = z[:, kw:]


def _proj_glu_kernel(x_ref, wa_ref, wg_ref, o_ref, wab_ref, wgb_ref):
    i = pl.program_id(1)
    _load_w(i, wa_ref, wab_ref)
    _load_w(i, wg_ref, wgb_ref)
    for rs in _row_groups(x_ref.shape[0]):
        x = x_ref[rs, :]
        a = jnp.dot(x, wab_ref[...], preferred_element_type=F32)
        g = jnp.dot(x, wgb_ref[...], preferred_element_type=F32)
        o_ref[rs, :] = a * jax.nn.sigmoid(g)


def _proj_gate_kernel(x_ref, w_ref, b_ref, o_ref, wb_ref):
    _load_w(pl.program_id(1), w_ref, wb_ref)
    for rs in _row_groups(x_ref.shape[0]):
        z = jnp.dot(x_ref[rs, :], wb_ref[...], preferred_element_type=F32)
        o_ref[rs, :] = jax.nn.sigmoid(z + b_ref[...]).astype(o_ref.dtype)


def _row_tile(t, cap=1152):
    best = 16
    for m in range(16, cap + 1, 16):
        if t % m == 0:
            best = m
    return best


def _project(xn, w_in, b_gate, q_norm_w, k_norm_w, d_model, conv_ch):
    t, d = xn.shape
    q_w = d_model
    kv_w = N_KV_HEADS * HEAD_DIM
    tn = 2 * kv_w
    tm = _row_tile(t)
    ni = t // tm
    tm_big = _row_tile(t, cap=2304)
    ni_big = t // tm_big
    reps = MXU_DIM // HEAD_DIM
    hq = jnp.tile(q_norm_w.astype(F32) * HEAD_DIM ** -0.5, reps).reshape(1, MXU_DIM)
    hk = jnp.tile(k_norm_w.astype(F32), reps).reshape(1, MXU_DIM)
    gid = jnp.arange(MXU_DIM) // HEAD_DIM
    bd = (gid[:, None] == gid[None, :]).astype(BF16)

    x_spec = pl.BlockSpec((tm, d), lambda s, i: (i, 0))
    x_big = pl.BlockSpec((tm_big, d), lambda s, i: (i, 0))
    small = lambda shape: pl.BlockSpec(shape, lambda s, i: (0, 0))

    def w_spec(c0):
        return pl.BlockSpec((d, tn), lambda s, i: (0, c0 + s))

    wscr = pltpu.VMEM((d, tn), BF16)
    cp = _cparams(("arbitrary", "arbitrary"))

    q = pl.pallas_call(
        _proj_q_kernel, grid=(q_w // tn, ni_big),
        in_specs=[x_big, w_spec(0), small((1, MXU_DIM)), small((MXU_DIM, MXU_DIM))],
        out_specs=pl.BlockSpec((tm_big, tn), lambda s, i: (i, s)),
        out_shape=jax.ShapeDtypeStruct((t, q_w), BF16),
        scratch_shapes=[wscr], compiler_params=cp, name="proj_q",
    )(xn, w_in, hq, bd)

    c_kv = q_w // tn
    k, v = pl.pallas_call(
        _proj_kv_kernel, grid=(1, ni),
        in_specs=[x_spec, w_spec(c_kv), small((1, MXU_DIM)), small((MXU_DIM, MXU_DIM))],
        out_specs=[pl.BlockSpec((tm, kv_w), lambda s, i: (i, 0))] * 2,
        out_shape=[jax.ShapeDtypeStruct((t, kv_w), F32)] * 2,
        scratch_shapes=[wscr], compiler_params=cp, name="proj_kv",
    )(xn, w_in, hk, bd)

    c_a = c_kv + 1
    n_glu = conv_ch // tn
    u = pl.pallas_call(
        _proj_glu_kernel, grid=(n_glu, ni),
        in_specs=[x_spec, w_spec(c_a), w_spec(c_a + n_glu)],
        out_specs=pl.BlockSpec((tm, tn), lambda s, i: (i, s)),
        out_shape=jax.ShapeDtypeStruct((t, conv_ch), F32),
        scratch_shapes=[wscr, wscr], compiler_params=cp, name="proj_glu",
    )(xn, w_in, w_in)

    c_g = c_a + 2 * n_glu
    n_gate = 2 * d_model // tn
    gates = pl.pallas_call(
        _proj_gate_kernel, grid=(n_gate, ni_big),
        in_specs=[x_big, w_spec(c_g), pl.BlockSpec((1, tn), lambda s, i: (0, s))],
        out_specs=pl.BlockSpec((tm_big, tn), lambda s, i: (i, s)),
        out_shape=jax.ShapeDtypeStruct((t, 2 * d_model), BF16),
        scratch_shapes=[wscr], compiler_params=cp, name="proj_gate",
    )(xn, w_in, b_gate.reshape(1, -1))
    return q, k, v, u, gates


ATTN_BLOCKS = 4


def _attn_prompt_kernel(sink_ref, q_ref, kc_ref, kp_ref, vc_ref, vp_ref, o_ref, kbuf_ref, vbuf_ref):
    blk = WINDOW
    n = pl.program_id(1)
    rows = 2 * blk
    kbuf_ref[:blk, :] = kp_ref[...]
    kbuf_ref[blk:, :] = kc_ref[...]
    vbuf_ref[:blk, :] = vp_ref[...]
    vbuf_ref[blk:, :] = vc_ref[...]
    r_iota = lax.broadcasted_iota(jnp.int32, (rows, rows), 0)
    c_iota = lax.broadcasted_iota(jnp.int32, (rows, rows), 1)
    qi = jnp.where(r_iota >= blk, r_iota - blk, r_iota)
    band = (c_iota >= qi) & (c_iota <= qi + WINDOW)
    lo_lane = lax.broadcasted_iota(jnp.int32, (blk, LANES), 1) < HEAD_DIM
    top_row = lax.broadcasted_iota(jnp.int32, (rows, 1), 0) < blk
    zero = jnp.zeros((blk, LANES), q_ref.dtype)
    group = q_ref.shape[1] // (N_KV_HEADS * LANES)

    def query_block(sb, carry):
        r0 = pl.multiple_of(sb * blk, blk)
        q_rows = pl.ds(r0, blk)
        kv_rows = pl.ds(r0, rows)
        mask = band & ((c_iota >= blk) | (n > 0) | (sb > 0))
        for g in range(N_KV_HEADS):
            hs = slice(g * HEAD_DIM, (g + 1) * HEAD_DIM)
            kg = kbuf_ref[kv_rows, hs]
            vg = vbuf_ref[kv_rows, hs]
            kdup = jnp.concatenate([kg, kg], axis=1).astype(BF16)
            vdup = jnp.concatenate([vg, vg], axis=1).astype(BF16)
            for p in range(group):
                pair = g * group + p
                ls = slice(pair * LANES, (pair + 1) * LANES)
                qp = q_ref[q_rows, ls]
                lhs = jnp.concatenate([jnp.where(lo_lane, qp, zero), jnp.where(lo_lane, zero, qp)], axis=0)
                s = lax.dot_general(lhs, kdup, (((1,), (1,)), ((), ())), preferred_element_type=F32)
                s = jnp.where(mask, s, NEG_INF)
                sink = jnp.where(top_row, sink_ref[2 * pair], sink_ref[2 * pair + 1])
                m = jnp.maximum(jnp.max(s, axis=-1, keepdims=True), sink)
                e = jnp.exp(s - m)
                denom = jnp.sum(e, axis=-1, keepdims=True) + jnp.exp(sink - m)
                o2 = jnp.dot(e.astype(BF16), vdup, preferred_element_type=F32) / denom
                o_ref[q_rows, ls] = jnp.where(lo_lane, o2[:blk], o2[blk:]).astype(o_ref.dtype)
        return carry

    lax.fori_loop(0, q_ref.shape[0] // blk, query_block, 0)


def _attn_prompt(q, k, v, sinks, bsz, slen):
    blk = WINDOW
    step = ATTN_BLOCKS * blk
    ns = slen // step
    qw, kw = q.shape[1], k.shape[1]
    cur = lambda b, n: (b * ns + n, 0)
    prev = lambda b, n: (jnp.maximum((b * ns + n) * ATTN_BLOCKS - 1, 0), 0)
    return pl.pallas_call(
        _attn_prompt_kernel,
        grid=(bsz, ns),
        in_specs=[
            pl.BlockSpec(memory_space=pltpu.SMEM),
            pl.BlockSpec((step, qw), cur),
            pl.BlockSpec((step, kw), cur), pl.BlockSpec((blk, kw), prev),
            pl.BlockSpec((step, kw), cur), pl.BlockSpec((blk, kw), prev),
        ],
        out_specs=pl.BlockSpec((step, qw), cur),
        out_shape=jax.ShapeDtypeStruct((bsz * slen, qw), BF16),
        scratch_shapes=[pltpu.VMEM((blk + step, kw), F32), pltpu.VMEM((blk + step, kw), F32)],
        compiler_params=_cparams(("arbitrary", "arbitrary")),
        name="attn_prompt",
    )(sinks.astype(F32), q, k, k, v, v)


def _attn_sample_kernel(sink_ref, q_ref, kt_ref, kn_ref, vt_ref, vn_ref, o_ref, kt_out_ref, vt_out_ref, *, qlen):
    nseq, rows, _ = q_ref.shape
    kvw, ncache = kt_ref.shape[1], kt_ref.shape[2]
    per_kv = rows // N_KV_HEADS
    all_rows = nseq * rows
    r_lane = _div_pow2(_mod_pow2(lax.broadcasted_iota(jnp.int32, (all_rows, kvw), 0), rows), per_kv)
    c_lane = _div_pow2(lax.broadcasted_iota(jnp.int32, (all_rows, kvw), 1), HEAD_DIM)
    own = r_lane == c_lane
    mask_c = (lax.broadcasted_iota(jnp.int32, (all_rows, ncache), 1)
              >= _mod_pow2(lax.broadcasted_iota(jnp.int32, (all_rows, ncache), 0), qlen))
    mask_n = (lax.broadcasted_iota(jnp.int32, (all_rows, qlen), 1)
              <= _mod_pow2(lax.broadcasted_iota(jnp.int32, (all_rows, qlen), 0), qlen))
    keep_old = lax.broadcasted_iota(jnp.int32, (kvw, ncache), 1) < ncache - qlen
    sink = sink_ref[...]
    contract_last = (((1,), (1,)), ((), ()))
    seq_rows = lambda b: slice(b * rows, (b + 1) * rows)
    new_rows = lambda b: slice(b * qlen, (b + 1) * qlen)

    q_all = q_ref[...].reshape(all_rows, HEAD_DIM)
    qe = jnp.where(own, jnp.concatenate([q_all] * N_KV_HEADS, axis=1), jnp.zeros((), q_all.dtype))
    sc = jnp.concatenate([jnp.dot(qe[seq_rows(b)], kt_ref[b].astype(BF16), preferred_element_type=F32)
                          for b in range(nseq)], axis=0)
    sn = jnp.concatenate([lax.dot_general(qe[seq_rows(b)], kn_ref[new_rows(b), :].astype(BF16), contract_last,
                                          preferred_element_type=F32) for b in range(nseq)], axis=0)
    sc = jnp.where(mask_c, sc, NEG_INF)
    sn = jnp.where(mask_n, sn, NEG_INF)
    m = jnp.maximum(jnp.maximum(jnp.max(sc, axis=-1, keepdims=True), jnp.max(sn, axis=-1, keepdims=True)), sink)
    ec = jnp.exp(sc - m)
    en = jnp.exp(sn - m)
    denom = jnp.sum(ec, axis=-1, keepdims=True) + jnp.sum(en, axis=-1, keepdims=True) + jnp.exp(sink - m)
    ec = ec.astype(BF16)
    en = en.astype(BF16)
    o = jnp.concatenate(
        [lax.dot_general(ec[seq_rows(b)], vt_ref[b].astype(BF16), contract_last, preferred_element_type=F32)
         + jnp.dot(en[seq_rows(b)], vn_ref[new_rows(b), :].astype(BF16), preferred_element_type=F32)
         for b in range(nseq)], axis=0)
    o = jnp.where(own, o, 0.0)
    acc = o[:, :HEAD_DIM]
    for g in range(1, N_KV_HEADS):
        acc = acc + o[:, g * HEAD_DIM:(g + 1) * HEAD_DIM]
    o_ref[...] = (acc / denom).astype(o_ref.dtype).reshape(nseq, rows, HEAD_DIM)

    def feature_major(new_ref):
        pad = jnp.zeros((ncache - nseq * qlen, kvw), F32)
        return jnp.concatenate([new_ref[...], pad], axis=0).T

    def shift_in(old_ref, new_ref, out_ref):
        new_t = feature_major(new_ref)
        for b in range(nseq):
            placed = pltpu.roll(new_t, ncache - qlen - b * qlen, axis=1)
            out_ref[b] = jnp.where(keep_old, pltpu.roll(old_ref[b], ncache - qlen, axis=1), placed)

    shift_in(kt_ref, kn_ref, kt_out_ref)
    shift_in(vt_ref, vn_ref, vt_out_ref)


def _attn_sample(q, k_cache, k_new, v_cache, v_new, sinks, seq_per_step=8):
    bsz, ncache = k_cache.shape[:2]
    kvw = k_new.shape[1]
    qlen = k_new.shape[0] // bsz
    assert seq_per_step * qlen <= ncache
    n_heads = q.shape[1] // HEAD_DIM
    group = n_heads // N_KV_HEADS
    rows = n_heads * qlen
    qr = q.reshape(bsz, qlen, N_KV_HEADS, group, HEAD_DIM).transpose(0, 2, 3, 1, 4).reshape(bsz, rows, HEAD_DIM)
    sink_rows = jnp.tile(jnp.repeat(sinks.astype(F32), qlen), seq_per_step).reshape(seq_per_step * rows, 1)
    feature_major = lambda c: c.transpose(0, 2, 3, 1).reshape(bsz, kvw, ncache)
    seq = lambda i: (i, 0, 0)
    cache_spec = pl.BlockSpec((seq_per_step, kvw, ncache), seq)
    new_spec = pl.BlockSpec((seq_per_step * qlen, kvw), lambda i: (i, 0))
    o, kt_next, vt_next = pl.pallas_call(
        functools.partial(_attn_sample_kernel, qlen=qlen),
        grid=(bsz // seq_per_step,),
        in_specs=[
            pl.BlockSpec((seq_per_step * rows, 1), lambda i: (0, 0)),
            pl.BlockSpec((seq_per_step, rows, HEAD_DIM), seq),
            cache_spec, new_spec, cache_spec, new_spec,
        ],
        out_specs=[pl.BlockSpec((seq_per_step, rows, HEAD_DIM), seq), cache_spec, cache_spec],
        out_shape=[jax.ShapeDtypeStruct((bsz, rows, HEAD_DIM), BF16),
                   jax.ShapeDtypeStruct((bsz, kvw, ncache), F32), jax.ShapeDtypeStruct((bsz, kvw, ncache), F32)],
        compiler_params=_cparams(("arbitrary",)),
        name="attn_sample",
    )(sink_rows, qr, feature_major(k_cache), k_new, feature_major(v_cache), v_new)
    position_major = lambda t: t.reshape(bsz, N_KV_HEADS, HEAD_DIM, ncache).transpose(0, 3, 1, 2)
    ao = o.reshape(bsz, N_KV_HEADS, group, qlen, HEAD_DIM).transpose(0, 3, 1, 2, 4).reshape(bsz * qlen, -1)
    return ao, position_major(kt_next), position_major(vt_next)


CONV_HALO = 32
CONV_ROWS = 64
CONV_COLS = 256


def _conv_prompt_kernel(uc_ref, up_ref, w_ref, b_ref, y_ref, sh_ref):
    tile = uc_ref.shape[0]
    n = pl.program_id(1)
    halo = jnp.where(n > 0, up_ref[...], 0.0)
    sh_ref[0, :CONV_HALO, :] = halo
    sh_ref[0, CONV_HALO:, :] = uc_ref[...]
    keep = tile + CONV_HALO - SUBLANES
    for r in range(1, SUBLANES):
        sh_ref[r, :keep, :] = sh_ref[0, r:r + keep, :]
    first = CONV_HALO - (CONV_WIDTH - 1)
    ch = uc_ref.shape[1]

    def rows_step(rc, carry):
        r0 = pl.multiple_of(rc * CONV_ROWS, CONV_ROWS)
        for cc in range(ch // CONV_COLS):
            cs = slice(cc * CONV_COLS, (cc + 1) * CONV_COLS)
            acc = jnp.zeros((CONV_ROWS, CONV_COLS), F32)
            for j in range(CONV_WIDTH):
                off = first + j
                a, r = off // SUBLANES, off % SUBLANES
                acc = acc + sh_ref[r, pl.ds(r0 + a * SUBLANES, CONV_ROWS), cs] * w_ref[j:j + 1, cs]
            y_ref[pl.ds(r0, CONV_ROWS), cs] = acc + b_ref[:, cs]
        return carry

    lax.fori_loop(0, tile // CONV_ROWS, rows_step, 0)


def _conv_prompt(u, w, b, bsz, slen, tile=512):
    ch = u.shape[1]
    nt = slen // tile
    per = tile // CONV_HALO
    return pl.pallas_call(
        _conv_prompt_kernel,
        grid=(bsz, nt),
        in_specs=[
            pl.BlockSpec((tile, ch), lambda bb, n: (bb * nt + n, 0)),
            pl.BlockSpec((CONV_HALO, ch), lambda bb, n: (jnp.maximum((bb * nt + n) * per - 1, 0), 0)),
            pl.BlockSpec((CONV_WIDTH, ch), lambda bb, n: (0, 0)),
            pl.BlockSpec((1, ch), lambda bb, n: (0, 0)),
        ],
        out_specs=pl.BlockSpec((tile, ch), lambda bb, n: (bb * nt + n, 0)),
        out_shape=jax.ShapeDtypeStruct((bsz * slen, ch), F32),
        scratch_shapes=[pltpu.VMEM((SUBLANES, tile + CONV_HALO, ch), F32)],
        compiler_params=_cparams(("arbitrary", "arbitrary")),
        name="conv_prompt",
    )(u, u, w, b.reshape(1, ch))


def _conv_sample_kernel(sc_ref, un_ref, w_ref, b_ref, y_ref, next_ref):
    nctx = sc_ref.shape[0]
    qlen = un_ref.shape[0]
    ch = w_ref.shape[1]

    def row(t):
        return sc_ref[t] if t < nctx else un_ref[t - nctx]

    for i in range(qlen):
        acc = jnp.zeros((sc_ref.shape[1], ch), F32)
        for j in range(CONV_WIDTH):
            acc = acc + row(i + j) * w_ref[j:j + 1, :]
        y_ref[i] = acc + b_ref[...]
    for t in range(nctx):
        next_ref[t] = row(t + qlen)


def _conv_sample(state_t, u_new_t, w, b, seq_per_step=16):
    nctx, bsz, ch = state_t.shape
    qlen = u_new_t.shape[0]
    seq = lambda i: (0, i, 0)
    return pl.pallas_call(
        _conv_sample_kernel,
        grid=(bsz // seq_per_step,),
        in_specs=[
            pl.BlockSpec((nctx, seq_per_step, ch), seq),
            pl.BlockSpec((qlen, seq_per_step, ch), seq),
            pl.BlockSpec((CONV_WIDTH, ch), lambda i: (0, 0)),
            pl.BlockSpec((1, ch), lambda i: (0, 0)),
        ],
        out_specs=[pl.BlockSpec((qlen, seq_per_step, ch), seq), pl.BlockSpec((nctx, seq_per_step, ch), seq)],
        out_shape=[jax.ShapeDtypeStruct((qlen, bsz, ch), F32), jax.ShapeDtypeStruct((nctx, bsz, ch), F32)],
        compiler_params=_cparams(("arbitrary",)),
        name="conv_sample",
    )(state_t, u_new_t, w, b.reshape(1, ch))


def _merge_kernel(y_ref, ao_ref, ga_ref, gc_ref, x_ref, lnw_ref, lnb_ref, wco_ref, wo_ref, fw_ref, wr_ref,
                  h_ref, xn_ref, lg_ref):
    y = y_ref[...]
    mu = jnp.mean(y, axis=-1, keepdims=True)
    yc = y - mu
    var = jnp.mean(yc * yc, axis=-1, keepdims=True)
    yl = yc * lax.rsqrt(var + EPS) * lnw_ref[...] + lnb_ref[...]
    act = yl * jax.nn.sigmoid(yl)
    conv_o = jnp.dot(act.astype(BF16), wco_ref[...], preferred_element_type=F32)
    mix = ga_ref[...].astype(F32) * ao_ref[...].astype(F32) + gc_ref[...].astype(F32) * conv_o
    h = x_ref[...] + jnp.dot(mix.astype(BF16), wo_ref[...], preferred_element_type=F32)
    h_ref[...] = h
    ms = jnp.mean(h * h, axis=-1, keepdims=True)
    xn = (h * lax.rsqrt(ms + EPS) * fw_ref[...]).astype(BF16)
    xn_ref[...] = xn
    lg_ref[...] = jnp.dot(xn, wr_ref[...], preferred_element_type=F32)


def _merge(y, ao, gates, row0, x, ln_w, ln_b, wco, wo, fw, wr, tile=256):
    t, d = x.shape
    ch = y.shape[1]
    off = row0 // tile
    row = lambda i: (i, 0)
    const = lambda shape: pl.BlockSpec(shape, lambda i: (0, 0), pipeline_mode=pl.Buffered(1))
    return pl.pallas_call(
        _merge_kernel,
        grid=(t // tile,),
        in_specs=[
            pl.BlockSpec((tile, ch), row),
            pl.BlockSpec((tile, d), row),
            pl.BlockSpec((tile, d), lambda i: (i + off, 0)),
            pl.BlockSpec((tile, d), lambda i: (i + off, 1)),
            pl.BlockSpec((tile, d), row),
            const((1, ch)), const((1, ch)), const((ch, d)), const((d, d)), const((1, d)),
            const((d, LANES)),
        ],
        out_specs=[pl.BlockSpec((tile, d), row), pl.BlockSpec((tile, d), row), pl.BlockSpec((tile, LANES), row)],
        out_shape=[jax.ShapeDtypeStruct((t, d), F32), jax.ShapeDtypeStruct((t, d), BF16),
                   jax.ShapeDtypeStruct((t, LANES), F32)],
        compiler_params=_cparams(("arbitrary",)),
        name="merge",
    )(y, ao, gates, gates, x, ln_w.reshape(1, ch), ln_b.reshape(1, ch), wco, wo, fw.reshape(1, d), wr)


def _route_kernel(lp_ref, ls_ref, tri_ref, lower_ref, meta_ref, metat_ref, cnt_ref, *, n_p):
    i = pl.program_id(0)
    used = _round_up(N_EXPERTS + N_GROUPS, SUBLANES)

    def body(l_ref):
        lg = l_ref[...].T[:used]
        shape = lg.shape
        row = lax.broadcasted_iota(jnp.int32, shape, 0)
        big = jnp.int32(LANES)
        is_g = (row >= N_EXPERTS) & (row < N_EXPERTS + N_GROUPS)
        gl = jnp.where(is_g, lg, -jnp.inf)
        gmax = jnp.max(gl, axis=0, keepdims=True)
        g_star = jnp.min(jnp.where(gl == gmax, row - N_EXPERTS, big), axis=0, keepdims=True)
        pg_star = 1.0 / jnp.sum(jnp.exp(gl - gmax), axis=0, keepdims=True)
        in_g = (row < N_EXPERTS) & (_div_pow2(row, EXPERTS_PER_GROUP) == g_star)
        el = jnp.where(in_g, lg, -jnp.inf)
        v1 = jnp.max(el, axis=0, keepdims=True)
        i1 = jnp.min(jnp.where(el == v1, row, big), axis=0, keepdims=True)
        el2 = jnp.where(row == i1, -jnp.inf, el)
        v2 = jnp.max(el2, axis=0, keepdims=True)
        i2 = jnp.min(jnp.where(el2 == v2, row, big), axis=0, keepdims=True)
        e2 = jnp.exp(v2 - v1)
        w1 = pg_star * (1.0 / (1.0 + e2))
        w2 = pg_star * (e2 / (1.0 + e2))
        oh1 = row == i1
        oh2 = row == i2
        cnt = jnp.where(oh1 | oh2, 1.0, 0.0)
        before = jnp.dot(cnt.astype(BF16), tri_ref[...], preferred_element_type=F32)
        tot = jnp.sum(cnt, axis=-1, keepdims=True)
        runs = jnp.floor((tot + (CHUNK_ROWS - 1)) * (1.0 / CHUNK_ROWS))
        runs_b = jnp.broadcast_to(runs, (used, LANES)).astype(BF16)
        start = jnp.dot(lower_ref[...], runs_b, preferred_element_type=F32)[:, 0:1] * CHUNK_ROWS
        slot = before + start
        s1 = jnp.sum(jnp.where(oh1, slot, 0.0), axis=0, keepdims=True)
        s2 = jnp.sum(jnp.where(oh2, slot, 0.0), axis=0, keepdims=True)
        fields = [i1.astype(F32), i2.astype(F32), w1, w2, s1, s2]
        fields += [jnp.zeros_like(w1)] * (META_FIELDS - len(fields))
        metat = jnp.concatenate(fields, axis=0)
        metat_ref[...] = metat
        pad = jnp.zeros((LANES - META_FIELDS, shape[1]), F32)
        meta_ref[...] = jnp.concatenate([metat, pad], axis=0).T
        tot_pad = jnp.concatenate([tot, jnp.zeros((LANES - used, 1), F32)], axis=0)
        cnt_ref[...] = jnp.broadcast_to(tot_pad, (LANES, LANES)).T[0:1]

    @pl.when(i < n_p)
    def _():
        body(lp_ref)

    @pl.when(i >= n_p)
    def _():
        body(ls_ref)


def _route(lg_p, lg_s):
    tile = TOKEN_BLOCK
    tp, ts = lg_p.shape[0], lg_s.shape[0]
    n_p, n_s = tp // tile, ts // tile
    nb = n_p + n_s
    used = _round_up(N_EXPERTS + N_GROUPS, SUBLANES)
    tri = (jnp.arange(tile)[:, None] < jnp.arange(tile)[None, :]).astype(BF16)
    lower = (jnp.arange(used)[:, None] > jnp.arange(used)[None, :]).astype(BF16)
    return pl.pallas_call(
        functools.partial(_route_kernel, n_p=n_p),
        grid=(nb,),
        in_specs=[
            pl.BlockSpec((tile, LANES), lambda i: (jnp.minimum(i, n_p - 1), 0)),
            pl.BlockSpec((tile, LANES), lambda i: (jnp.maximum(i - n_p, 0), 0)),
            pl.BlockSpec((tile, tile), lambda i: (0, 0)),
            pl.BlockSpec((used, used), lambda i: (0, 0)),
        ],
        out_specs=[pl.BlockSpec((tile, LANES), lambda i: (i, 0)),
                   pl.BlockSpec((None, META_FIELDS, tile), lambda i: (i, 0, 0)),
                   pl.BlockSpec((None, 1, LANES), lambda i: (i, 0, 0))],
        out_shape=[jax.ShapeDtypeStruct((tp + ts, LANES), F32),
                   jax.ShapeDtypeStruct((nb, META_FIELDS, tile), F32),
                   jax.ShapeDtypeStruct((nb, 1, LANES), F32)],
        compiler_params=_cparams(("arbitrary",)),
        name="route",
    )(lg_p, lg_s, tri, lower)


def _plan_sizes(n_tok):
    nb = n_tok // TOKEN_BLOCK
    pad_per_block = N_EXPERTS * (CHUNK_ROWS - 1)
    max_chunks = _round_up(-(-(TOKEN_BLOCK * TOP_K + pad_per_block) // CHUNK_ROWS), SUBLANES)
    n_rows = _round_up(n_tok * TOP_K + nb * pad_per_block + N_EXPERTS * (ROW_PAD - 1), ROW_PAD)
    n_items = (n_rows + N_EXPERTS * (EXPERT_ROWS - ROW_PAD)) // EXPERT_ROWS
    max_zero = N_EXPERTS * (ROW_PAD // CHUNK_ROWS)
    return nb, max_chunks, n_rows, n_items, max_zero


def _dispatch_plan(counts, max_chunks, n_items, max_zero):
    i32 = jnp.int32
    c = counts[:, 0, :N_EXPERTS].astype(i32)
    run = (c + CHUNK_ROWS - 1) // CHUNK_ROWS
    run_end = jnp.cumsum(run, axis=1)
    src_start = run_end - run
    nchunk = run_end[:, -1]
    seg = jnp.sum(run, axis=0) * CHUNK_ROWS
    padded = (seg + ROW_PAD - 1) // ROW_PAD * ROW_PAD
    pad_end = jnp.cumsum(padded)
    pad_start = pad_end - padded
    dst_start = pad_start[None, :] // CHUNK_ROWS + (jnp.cumsum(run, axis=0) - run)
    jj = jnp.arange(max_chunks, dtype=i32)[None, :, None]
    in_run = (src_start[:, None, :] <= jj) & (jj < run_end[:, None, :])
    chunk_dst = jnp.sum(jnp.where(in_run, (dst_start - src_start)[:, None, :] + jj, 0), axis=-1) * CHUNK_ROWS
    tail = (padded - seg) // CHUNK_ROWS
    tail_end = jnp.cumsum(tail)
    tail_start = tail_end - tail
    z = jnp.arange(max_zero, dtype=i32)[:, None]
    in_tail = (tail_start[None, :] <= z) & (z < tail_end[None, :])
    zero_base = (pad_start + seg) // CHUNK_ROWS - tail_start
    zero_dst = jnp.sum(jnp.where(in_tail, zero_base[None, :] + z, 0), axis=-1) * CHUNK_ROWS
    nzero = tail_end[-1]
    items_e = (padded + EXPERT_ROWS - 1) // EXPERT_ROWS
    item_end = jnp.cumsum(items_e)
    item_start = item_end - items_e
    w = jnp.arange(n_items, dtype=i32)[:, None]
    in_item = (item_start[None, :] <= w) & (w < item_end[None, :])
    k = w - item_start[None, :]
    e_ids = jnp.arange(N_EXPERTS, dtype=i32)
    last_e = jnp.max(jnp.where(items_e > 0, e_ids, 0))
    item_e = jnp.where(w[:, 0] < item_end[-1], jnp.sum(jnp.where(in_item, e_ids[None, :], 0), axis=-1), last_e)
    item_row0 = jnp.sum(jnp.where(in_item, pad_start[None, :] + k * EXPERT_ROWS, 0), axis=-1)
    item_rows = jnp.sum(jnp.where(in_item, jnp.clip(padded[None, :] - k * EXPERT_ROWS, 0, EXPERT_ROWS), 0), axis=-1)
    item_nsub = item_rows // ROW_PAD
    return (chunk_dst.reshape(-1).astype(i32), nchunk.astype(i32), zero_dst.astype(i32),
            nzero.reshape(1).astype(i32), item_e.astype(i32), item_row0.astype(i32), item_nsub.astype(i32))


def _scatter_kernel(cd_ref, nc_ref, zd_ref, nz_ref, mt_ref, xp_ref, xs_ref, out_hbm, loc_ref, zero_ref, sem,
                    zero_sem, *, n_p, max_chunks):
    i = pl.program_id(0)
    buf = i % 2
    n_slots = loc_ref.shape[1]
    d = xp_ref.shape[1]
    mt = mt_ref[...]
    w1, w2 = mt[2:3, :], mt[3:4, :]
    s1, s2 = mt[4:5, :].astype(jnp.int32), mt[5:6, :].astype(jnp.int32)
    slot = lax.broadcasted_iota(jnp.int32, (n_slots, mt.shape[1]), 0)
    p1 = slot == s1
    p2 = slot == s2
    perm = jnp.where(p1 | p2, 1.0, 0.0).astype(BF16)
    wcol = jnp.sum(jnp.where(p1, w1, 0.0) + jnp.where(p2, w2, 0.0), axis=-1, keepdims=True)
    half = d // 2
    loc_ref[buf, :, half:] = pltpu.bitcast(jnp.broadcast_to(wcol, (n_slots, LANES)), jnp.uint32)

    def sort_rows(x_ref):
        rows = jnp.dot(perm, x_ref[...], preferred_element_type=F32)
        loc_ref[buf, :, :half] = _pack_pairs(rows[:, :half], rows[:, half:])

    @pl.when(i < n_p)
    def _():
        sort_rows(xp_ref)

    @pl.when(i >= n_p)
    def _():
        sort_rows(xs_ref)

    def rows_at(r):
        return pl.ds(pl.multiple_of(r, CHUNK_ROWS), CHUNK_ROWS)

    def chunk_copy(blk, j):
        return pltpu.make_async_copy(loc_ref.at[blk % 2, rows_at(j * CHUNK_ROWS)],
                                     out_hbm.at[rows_at(cd_ref[blk * max_chunks + j])], sem.at[blk % 2])

    def zero_copy(z):
        return pltpu.make_async_copy(zero_ref, out_hbm.at[rows_at(zd_ref[z])], zero_sem)

    def loop(n, fn):
        def body(j, carry):
            fn(j)
            return carry
        lax.fori_loop(0, n, body, 0)

    @pl.when(i > 0)
    def _():
        loop(nc_ref[i - 1], lambda j: chunk_copy(i - 1, j).wait())

    loop(nc_ref[i], lambda j: chunk_copy(i, j).start())

    @pl.when(i == pl.num_programs(0) - 1)
    def _():
        zero_ref[...] = jnp.zeros_like(zero_ref)
        loop(nz_ref[0], lambda z: zero_copy(z).start())
        loop(nc_ref[i], lambda j: chunk_copy(i, j).wait())
        loop(nz_ref[0], lambda z: zero_copy(z).wait())


def _scatter_rows(xn_p, xn_s, metat, chunk_dst, nchunk, zero_dst, nzero, max_chunks, n_rows):
    tile = TOKEN_BLOCK
    tp, d = xn_p.shape
    ts = xn_s.shape[0]
    n_p, n_s = tp // tile, ts // tile
    n_slots = max_chunks * CHUNK_ROWS
    width = d // 2 + LANES
    u32 = jnp.uint32
    return pl.pallas_call(
        functools.partial(_scatter_kernel, n_p=n_p, max_chunks=max_chunks),
        grid_spec=pltpu.PrefetchScalarGridSpec(
            num_scalar_prefetch=4,
            grid=(n_p + n_s,),
            in_specs=[
                pl.BlockSpec((None, META_FIELDS, tile), lambda i, *_: (i, 0, 0)),
                pl.BlockSpec((tile, d), lambda i, *_: (jnp.minimum(i, n_p - 1), 0)),
                pl.BlockSpec((tile, d), lambda i, *_: (jnp.maximum(i - n_p, 0), 0)),
            ],
            out_specs=pl.BlockSpec(memory_space=pl.ANY),
            scratch_shapes=[pltpu.VMEM((2, n_slots, width), u32), pltpu.VMEM((CHUNK_ROWS, width), u32),
                            pltpu.SemaphoreType.DMA((2,)), pltpu.SemaphoreType.DMA(())],
        ),
        out_shape=jax.ShapeDtypeStruct((n_rows, width), u32),
        compiler_params=_cparams(("arbitrary",)),
        name="scatter_rows",
    )(chunk_dst, nchunk, zero_dst, nzero, metat, xn_p, xn_s)


def _expert_kernel(e_ref, row0_ref, nsub_ref, xs_hbm, wg_ref, wu_ref, wd_ref, out_hbm,
                   xf_ref, xb_ref, acc_ref, ob_ref, pend_ref, in_sem, out_sem):
    w = pl.program_id(0)
    c = pl.program_id(1)
    n_items = pl.num_programs(0)
    last_c = pl.num_programs(1) - 1
    nsub = nsub_ref[w]
    d = acc_ref.shape[1]
    half = d // 2
    slot = w % 2

    def sub_rows(s):
        return pl.ds(pl.multiple_of(s * ROW_PAD, ROW_PAD), ROW_PAD)

    def hbm_rows(item, s):
        return pl.ds(pl.multiple_of(row0_ref[item] + s * ROW_PAD, ROW_PAD), ROW_PAD)

    def in_copy(item, s):
        return pltpu.make_async_copy(xs_hbm.at[hbm_rows(item, s)], xf_ref.at[item % 2, sub_rows(s)],
                                     in_sem.at[item % 2])

    def out_copy(item, s):
        return pltpu.make_async_copy(ob_ref.at[sub_rows(s)], out_hbm.at[hbm_rows(item, s)], out_sem)

    def loop(n, fn):
        def body(s, carry):
            fn(s)
            return carry
        lax.fori_loop(0, n, body, 0)

    @pl.when(c == 0)
    def _():
        @pl.when(w == 0)
        def _():
            pend_ref[0] = 0
            pend_ref[1] = 0
            loop(nsub, lambda s: in_copy(w, s).start())

        loop(nsub, lambda s: in_copy(w, s).wait())
        nxt = jnp.minimum(w + 1, n_items - 1)

        @pl.when(w + 1 < n_items)
        def _():
            loop(nsub_ref[nxt], lambda s: in_copy(nxt, s).start())

    def drain_out():
        prev = pend_ref[1]
        loop(pend_ref[0], lambda s: out_copy(prev, s).wait())
        pend_ref[0] = 0

    for n in range(1, EXPERT_ROWS // ROW_PAD + 1):
        m = n * ROW_PAD

        @pl.when(nsub == n)
        def _(m=m):
            @pl.when(c == 0)
            def _():
                acc_ref[:m, :] = jnp.zeros((m, d), F32)
                xa, xb = _unpack_pairs(xf_ref[slot, :m, :half])
                xb_ref[:m, :half] = xa.astype(BF16)
                xb_ref[:m, half:] = xb.astype(BF16)

            x = xb_ref[:m, :]
            g = jnp.dot(x, wg_ref[...].astype(BF16), preferred_element_type=F32)
            u = jnp.dot(x, wu_ref[...].astype(BF16), preferred_element_type=F32)
            hmid = (g * jax.nn.sigmoid(g) * u).astype(BF16)
            acc_ref[:m, :] += jnp.dot(hmid, wd_ref[...].astype(BF16), preferred_element_type=F32)

            @pl.when(c == last_c)
            def _():
                drain_out()
                mix_w = pltpu.bitcast(xf_ref[slot, :m, half:half + 1], F32)
                y = (acc_ref[:m, :] * mix_w).astype(BF16).astype(F32)
                ob_ref[:m, :] = _pack_pairs(y[:, :half], y[:, half:])
                loop(nsub, lambda s: out_copy(w, s).start())
                pend_ref[0] = nsub
                pend_ref[1] = w

    @pl.when((w == n_items - 1) & (c == last_c))
    def _():
        drain_out()


def _experts(xs, item_e, item_row0, item_nsub, w_gate, w_up, w_down):
    n_rows, width = xs.shape
    d = w_gate.shape[1]
    assert width == d // 2 + LANES
    d_exp = w_gate.shape[2]
    nc = d_exp // D_CHUNK
    n_items = item_e.shape[0]

    def chunk_of(w, c, nsub_ref):
        return jnp.where(nsub_ref[w] > 0, c, nc - 1)

    return pl.pallas_call(
        _expert_kernel,
        grid_spec=pltpu.PrefetchScalarGridSpec(
            num_scalar_prefetch=3,
            grid=(n_items, nc),
            in_specs=[
                pl.BlockSpec(memory_space=pl.ANY),
                pl.BlockSpec((None, d, D_CHUNK), lambda w, c, e, r, n: (e[w], 0, chunk_of(w, c, n))),
                pl.BlockSpec((None, d, D_CHUNK), lambda w, c, e, r, n: (e[w], 0, chunk_of(w, c, n))),
                pl.BlockSpec((None, D_CHUNK, d), lambda w, c, e, r, n: (e[w], chunk_of(w, c, n), 0)),
            ],
            out_specs=pl.BlockSpec(memory_space=pl.ANY),
            scratch_shapes=[
                pltpu.VMEM((2, EXPERT_ROWS, width), jnp.uint32),
                pltpu.VMEM((EXPERT_ROWS, d), BF16),
                pltpu.VMEM((EXPERT_ROWS, d), F32),
                pltpu.VMEM((EXPERT_ROWS, d // 2), jnp.uint32),
                pltpu.SMEM((2,), jnp.int32),
                pltpu.SemaphoreType.DMA((2,)),
                pltpu.SemaphoreType.DMA(()),
            ],
        ),
        out_shape=jax.ShapeDtypeStruct((n_rows, d // 2), jnp.uint32),
        compiler_params=_cparams(("arbitrary", "arbitrary")),
        name="experts",
    )(item_e, item_row0, item_nsub, xs, w_gate, w_up, w_down)


COMBINE_SPLIT = 2
SLOT_STEP = 256


def _combine_kernel(cd_ref, nc_ref, meta_ref, hp_ref, hs_ref, yb_hbm, yp_ref, ys_ref, loc_ref, sem,
                    *, n_p, max_chunks):
    i = pl.program_id(0)
    j = pl.program_id(1)
    nb = pl.num_programs(0)
    buf = i % 2
    n_slots, half = loc_ref.shape[1], loc_ref.shape[2]
    d = 2 * half

    def rows_at(r):
        return pl.ds(pl.multiple_of(r, CHUNK_ROWS), CHUNK_ROWS)

    def chunk_copy(blk, c):
        return pltpu.make_async_copy(yb_hbm.at[rows_at(cd_ref[blk * max_chunks + c])],
                                     loc_ref.at[blk % 2, rows_at(c * CHUNK_ROWS)], sem.at[blk % 2])

    def fetch(blk):
        n = nc_ref[blk]

        def start(c, carry):
            chunk_copy(blk, c).start()
            return carry
        lax.fori_loop(0, n, start, 0)

        def clear(c, carry):
            loc_ref[blk % 2, rows_at(c * CHUNK_ROWS), :] = jnp.zeros((CHUNK_ROWS, half), jnp.uint32)
            return carry
        lax.fori_loop(n, max_chunks, clear, 0)

    @pl.when(j == 0)
    def _():
        @pl.when(i == 0)
        def _():
            fetch(i)

        def wait(c, carry):
            chunk_copy(i, c).wait()
            return carry
        lax.fori_loop(0, nc_ref[i], wait, 0)

        @pl.when(i + 1 < nb)
        def _():
            fetch(jnp.minimum(i + 1, nb - 1))

    meta = meta_ref[...]
    s1 = meta[:, 4:5].astype(jnp.int32)
    s2 = meta[:, 5:6].astype(jnp.int32)
    rows = meta.shape[0]
    f = jnp.zeros((rows, d), F32)
    for r0 in range(0, n_slots, SLOT_STEP):
        slot = lax.broadcasted_iota(jnp.int32, (rows, SLOT_STEP), 1) + r0
        pt = jnp.where((slot == s1) | (slot == s2), 1.0, 0.0).astype(BF16)
        ya, yb = _unpack_pairs(loc_ref[buf, r0:r0 + SLOT_STEP, :])
        f = f + jnp.dot(pt, jnp.concatenate([ya, yb], axis=1).astype(BF16), preferred_element_type=F32)
    g = i * pl.num_programs(1) + j

    @pl.when(g < n_p)
    def _():
        yp_ref[...] = hp_ref[...] + f

    @pl.when(g >= n_p)
    def _():
        ys_ref[...] = hs_ref[...] + f


def _combine(h_p, h_s, meta, chunk_dst, nchunk, yb, max_chunks):
    tp, d = h_p.shape
    ts = h_s.shape[0]
    tile = TOKEN_BLOCK // COMBINE_SPLIT
    n_p, n_s = tp // tile, ts // tile
    nb = (tp + ts) // TOKEN_BLOCK
    n_slots = max_chunks * CHUNK_ROWS
    assert n_slots % SLOT_STEP == 0
    blk = lambda i, j: i * COMBINE_SPLIT + j
    p_idx = lambda i, j, *_: (jnp.minimum(blk(i, j), n_p - 1), 0)
    s_idx = lambda i, j, *_: (jnp.maximum(blk(i, j) - n_p, 0), 0)
    return pl.pallas_call(
        functools.partial(_combine_kernel, n_p=n_p, max_chunks=max_chunks),
        grid_spec=pltpu.PrefetchScalarGridSpec(
            num_scalar_prefetch=2,
            grid=(nb, COMBINE_SPLIT),
            in_specs=[
                pl.BlockSpec((tile, LANES), lambda i, j, *_: (blk(i, j), 0)),
                pl.BlockSpec((tile, d), p_idx),
                pl.BlockSpec((tile, d), s_idx),
                pl.BlockSpec(memory_space=pl.ANY),
            ],
            out_specs=[pl.BlockSpec((tile, d), p_idx), pl.BlockSpec((tile, d), s_idx)],
            scratch_shapes=[pltpu.VMEM((2, n_slots, d // 2), jnp.uint32), pltpu.SemaphoreType.DMA((2,))],
        ),
        out_shape=[jax.ShapeDtypeStruct((tp, d), F32), jax.ShapeDtypeStruct((ts, d), F32)],
        compiler_params=_cparams(("arbitrary", "arbitrary")),
        name="combine",
    )(chunk_dst, nchunk, meta, h_p, h_s, yb)


def _layer(x_p, x_s, ck, cv, sc, attn_norm_w, w_in, b_gate, q_norm_w, k_norm_w, sinks, conv_dw_w, conv_dw_b,
           conv_ln_w, conv_ln_b, w_conv_out, w_out, ffn_norm_w, router_group, router_expert, w_gate, w_up,
           w_down):
    bsz, slen, d = x_p.shape
    dbs, dlen, _ = x_s.shape
    tp, ts = bsz * slen, dbs * dlen
    conv_ch = conv_dw_w.shape[1]
    kvw = N_KV_HEADS * HEAD_DIM
    xp2, xs2 = x_p.reshape(tp, d), x_s.reshape(ts, d)

    xn = _prenorm(xp2, xs2, attn_norm_w)
    q, k, v, u, gates = _project(xn, w_in, b_gate, q_norm_w, k_norm_w, d, conv_ch)

    ao_p = _attn_prompt(q, k, v, sinks, bsz, slen)
    ao_s, ck_next, cv_next = _attn_sample(q[tp:], ck, k[tp:], cv, v[tp:], sinks)

    y_p = _conv_prompt(u, conv_dw_w, conv_dw_b, bsz, slen)
    sc_t = sc.transpose(1, 0, 2)
    u_s_t = u[tp:].reshape(dbs, dlen, conv_ch).transpose(1, 0, 2)
    y_s_t, sc_next_t = _conv_sample(sc_t, u_s_t, conv_dw_w, conv_dw_b)
    y_s = y_s_t.transpose(1, 0, 2).reshape(ts, conv_ch)

    wco = w_conv_out.astype(BF16)
    wo = w_out.astype(BF16)
    wr = jnp.concatenate([router_expert, router_group,
                          jnp.zeros((d, LANES - N_EXPERTS - N_GROUPS), router_expert.dtype)], axis=1).astype(BF16)
    margs = (conv_ln_w, conv_ln_b, wco, wo, ffn_norm_w, wr)
    h_p, xn_p, lg_p = _merge(y_p, ao_p, gates, 0, xp2, *margs)
    h_s, xn_s, lg_s = _merge(y_s, ao_s, gates, tp, xs2, *margs)

    meta, metat, counts = _route(lg_p, lg_s)
    _, max_chunks, n_rows, n_items, max_zero = _plan_sizes(tp + ts)
    chunk_dst, nchunk, zero_dst, nzero, item_e, item_row0, item_nsub = _dispatch_plan(
        counts, max_chunks, n_items, max_zero)
    xs_rows = _scatter_rows(xn_p, xn_s, metat, chunk_dst, nchunk, zero_dst, nzero, max_chunks, n_rows)
    yb = _experts(xs_rows, item_e, item_row0, item_nsub, w_gate, w_up, w_down)
    out_p, out_s = _combine(h_p, h_s, meta, chunk_dst, nchunk, yb, max_chunks)

    def seq_tails(a, n):
        return jnp.stack([lax.slice_in_dim(a, (b + 1) * slen - n, (b + 1) * slen, axis=0) for b in range(bsz)])

    new_p = (seq_tails(k, WINDOW).reshape(bsz, WINDOW, N_KV_HEADS, HEAD_DIM),
             seq_tails(v, WINDOW).reshape(bsz, WINDOW, N_KV_HEADS, HEAD_DIM),
             seq_tails(u, CONV_WIDTH - 1))
    new_s = (ck_next, cv_next, sc_next_t.transpose(1, 0, 2))
    return out_p.reshape(bsz, slen, d), out_s.reshape(dbs, dlen, d), new_p, new_s


def kernel(x_prompt, x_sample, cache_k, cache_v, state_conv, attn_norm_w, w_in, b_gate, q_norm_w, k_norm_w,
           sinks, conv_dw_w, conv_dw_b, conv_ln_w, conv_ln_b, w_conv_out, w_out, ffn_norm_w, router_group,
           router_expert, w_gate, w_up, w_down):
    x_p, x_s = x_prompt, x_sample
    kp, vp, cp, ks_, vs_, cs_ = [], [], [], [], [], []
    for l in range(cache_k.shape[0]):
        x_p, x_s, new_p, new_s = _layer(
            x_p, x_s, cache_k[l], cache_v[l], state_conv[l], attn_norm_w[l], w_in[l], b_gate[l], q_norm_w[l],
            k_norm_w[l], sinks[l], conv_dw_w[l], conv_dw_b[l], conv_ln_w[l], conv_ln_b[l], w_conv_out[l],
            w_out[l], ffn_norm_w[l], router_group[l], router_expert[l], w_gate[l], w_up[l], w_down[l])
        kp.append(new_p[0]); vp.append(new_p[1]); cp.append(new_p[2])
        ks_.append(new_s[0]); vs_.append(new_s[1]); cs_.append(new_s[2])
    return (x_p, x_s, jnp.stack(kp), jnp.stack(vp), jnp.stack(cp),
            jnp.stack(ks_), jnp.stack(vs_), jnp.stack(cs_))
```

```python
import functools

import jax
import jax.numpy as jnp
from jax import lax
from jax.experimental import pallas as pl
from jax.experimental.pallas import tpu as pltpu

F32 = jnp.float32
BF16 = jnp.bfloat16
EPS = 1e-6
NEG_INF = -1e30

HEAD_DIM = 64
N_KV_HEADS = 4
WINDOW = 128
CONV_WIDTH = 31
N_GROUPS = 4
EXPERTS_PER_GROUP = 8
N_EXPERTS = N_GROUPS * EXPERTS_PER_GROUP
TOP_K = 2

LANES = 128
SUBLANES = 8
MXU_DIM = 256
VMEM_LIMIT = 56 * 1024 * 1024

TOKEN_BLOCK = 512
CHUNK_ROWS = SUBLANES
ROW_PAD = 128
EXPERT_ROWS = 768
D_CHUNK = 512
META_FIELDS = 8


def _cparams(sem):
    return pltpu.CompilerParams(dimension_semantics=sem, vmem_limit_bytes=VMEM_LIMIT)


def _log2(n):
    assert n > 0 and n & (n - 1) == 0, n
    return n.bit_length() - 1


def _div_pow2(x, n):
    return lax.shift_right_logical(x, jnp.int32(_log2(n)))


def _mod_pow2(x, n):
    _log2(n)
    return x & jnp.int32(n - 1)


def _round_up(x, m):
    return (x + m - 1) // m * m


HIGH_HALF = 0xFFFF0000


def _pack_pairs(a, b):
    ua = pltpu.bitcast(a, jnp.uint32) & jnp.uint32(HIGH_HALF)
    return ua | (pltpu.bitcast(b, jnp.uint32) >> 16)


def _unpack_pairs(w):
    return (pltpu.bitcast(w & jnp.uint32(HIGH_HALF), F32), pltpu.bitcast(w << 16, F32))


def _prenorm_kernel(xp_ref, xs_ref, w_ref, o_ref, *, n_p):
    i = pl.program_id(0)

    def body(x_ref):
        x = x_ref[...]
        ms = jnp.mean(x * x, axis=-1, keepdims=True)
        o_ref[...] = (x * lax.rsqrt(ms + EPS) * w_ref[...]).astype(o_ref.dtype)

    @pl.when(i < n_p)
    def _():
        body(xp_ref)

    @pl.when(i >= n_p)
    def _():
        body(xs_ref)


def _prenorm(xp, xs, w, tile=512):
    tp, d = xp.shape
    ts = xs.shape[0]
    n_p, n_s = tp // tile, ts // tile
    return pl.pallas_call(
        functools.partial(_prenorm_kernel, n_p=n_p),
        grid=(n_p + n_s,),
        in_specs=[
            pl.BlockSpec((tile, d), lambda i: (jnp.minimum(i, n_p - 1), 0)),
            pl.BlockSpec((tile, d), lambda i: (jnp.maximum(i - n_p, 0), 0)),
            pl.BlockSpec((1, d), lambda i: (0, 0)),
        ],
        out_specs=pl.BlockSpec((tile, d), lambda i: (i, 0)),
        out_shape=jax.ShapeDtypeStruct((tp + ts, d), BF16),
        compiler_params=_cparams(("arbitrary",)),
        name="prenorm",
    )(xp, xs, w.reshape(1, d))


def _head_rms(z, hw, bd_ref):
    ss = z * z
    hi = ss.astype(BF16)
    lo = (ss - hi.astype(F32)).astype(BF16)
    bd = bd_ref[...]
    tot = jnp.dot(hi, bd, preferred_element_type=F32) + jnp.dot(lo, bd, preferred_element_type=F32)
    return z * lax.rsqrt(tot * (1.0 / HEAD_DIM) + EPS) * hw


def _load_w(i, w_ref, wb_ref):
    @pl.when(i == 0)
    def _():
        wb_ref[...] = w_ref[...].astype(BF16)


PROJ_SUB_NORM = 544
PROJ_SUB_ACT = 272


def _row_groups(n_rows, sub):
    sub = sub if n_rows % sub == 0 else n_rows
    return [slice(r0, r0 + sub) for r0 in range(0, n_rows, sub)]


def _proj_q_kernel(x_ref, w_ref, hw_ref, bd_ref, o_ref, wb_ref):
    _load_w(pl.program_id(1), w_ref, wb_ref)
    for rs in _row_groups(x_ref.shape[0], PROJ_SUB_NORM):
        z = jnp.dot(x_ref[rs, :], wb_ref[...], preferred_element_type=F32)
        for c in range(z.shape[1] // MXU_DIM):
            sl = slice(c * MXU_DIM, (c + 1) * MXU_DIM)
            o_ref[rs, sl] = _head_rms(z[:, sl], hw_ref[...], bd_ref).astype(o_ref.dtype)


def _proj_kv_kernel(x_ref, w_ref, hw_ref, bd_ref, k_ref, v_ref, wb_ref):
    _load_w(pl.program_id(1), w_ref, wb_ref)
    kw = k_ref.shape[1]
    for rs in _row_groups(x_ref.shape[0], PROJ_SUB_NORM):
        z = jnp.dot(x_ref[rs, :], wb_ref[...], preferred_element_type=F32)
        k_ref[rs, :] = _head_rms(z[:, :kw], hw_ref[...], bd_ref)
        v_ref[rs, :] = z[:, kw:]


def _proj_glu_kernel(x_ref, wa_ref, wg_ref, o_ref, wab_ref, wgb_ref):
    i = pl.program_id(1)
    _load_w(i, wa_ref, wab_ref)
    _load_w(i, wg_ref, wgb_ref)
    for rs in _row_groups(x_ref.shape[0], PROJ_SUB_ACT):
        x = x_ref[rs, :]
        a = jnp.dot(x, wab_ref[...], preferred_element_type=F32)
        g = jnp.dot(x, wgb_ref[...], preferred_element_type=F32)
        o_ref[rs, :] = a * jax.nn.sigmoid(g)


def _proj_gate_kernel(x_ref, w_ref, b_ref, o_ref, wb_ref):
    _load_w(pl.program_id(1), w_ref, wb_ref)
    for rs in _row_groups(x_ref.shape[0], PROJ_SUB_ACT):
        z = jnp.dot(x_ref[rs, :], wb_ref[...], preferred_element_type=F32)
        o_ref[rs, :] = jax.nn.sigmoid(z + b_ref[...]).astype(o_ref.dtype)


def _row_tile(t, cap=1152):
    best = 16
    for m in range(16, cap + 1, 16):
        if t % m == 0:
            best = m
    return best


def _project(xn, w_in, b_gate, q_norm_w, k_norm_w, d_model, conv_ch):
    t, d = xn.shape
    q_w = d_model
    kv_w = N_KV_HEADS * HEAD_DIM
    tn = 2 * kv_w
    tm = _row_tile(t)
    ni = t // tm
    tm_big = _row_tile(t, cap=2304)
    ni_big = t // tm_big
    reps = MXU_DIM // HEAD_DIM
    hq = jnp.tile(q_norm_w.astype(F32) * HEAD_DIM ** -0.5, reps).reshape(1, MXU_DIM)
    hk = jnp.tile(k_norm_w.astype(F32), reps).reshape(1, MXU_DIM)
    gid = jnp.arange(MXU_DIM) // HEAD_DIM
    bd = (gid[:, None] == gid[None, :]).astype(BF16)

    x_spec = pl.BlockSpec((tm, d), lambda s, i: (i, 0))
    x_big = pl.BlockSpec((tm_big, d), lambda s, i: (i, 0))
    small = lambda shape: pl.BlockSpec(shape, lambda s, i: (0, 0))

    def w_spec(c0):
        return pl.BlockSpec((d, tn), lambda s, i: (0, c0 + s))

    wscr = pltpu.VMEM((d, tn), BF16)
    cp = _cparams(("arbitrary", "arbitrary"))

    q = pl.pallas_call(
        _proj_q_kernel, grid=(q_w // tn, ni_big),
        in_specs=[x_big, w_spec(0), small((1, MXU_DIM)), small((MXU_DIM, MXU_DIM))],
        out_specs=pl.BlockSpec((tm_big, tn), lambda s, i: (i, s)),
        out_shape=jax.ShapeDtypeStruct((t, q_w), BF16),
        scratch_shapes=[wscr], compiler_params=cp, name="proj_q",
    )(xn, w_in, hq, bd)

    c_kv = q_w // tn
    k, v = pl.pallas_call(
        _proj_kv_kernel, grid=(1, ni),
        in_specs=[x_spec, w_spec(c_kv), small((1, MXU_DIM)), small((MXU_DIM, MXU_DIM))],
        out_specs=[pl.BlockSpec((tm, kv_w), lambda s, i: (i, 0))] * 2,
        out_shape=[jax.ShapeDtypeStruct((t, kv_w), F32)] * 2,
        scratch_shapes=[wscr], compiler_params=cp, name="proj_kv",
    )(xn, w_in, hk, bd)

    c_a = c_kv + 1
    n_glu = conv_ch // tn
    u = pl.pallas_call(
        _proj_glu_kernel, grid=(n_glu, ni),
        in_specs=[x_spec, w_spec(c_a), w_spec(c_a + n_glu)],
        out_specs=pl.BlockSpec((tm, tn), lambda s, i: (i, s)),
        out_shape=jax.ShapeDtypeStruct((t, conv_ch), F32),
        scratch_shapes=[wscr, wscr], compiler_params=cp, name="proj_glu",
    )(xn, w_in, w_in)

    c_g = c_a + 2 * n_glu
    n_gate = 2 * d_model // tn
    gates = pl.pallas_call(
        _proj_gate_kernel, grid=(n_gate, ni_big),
        in_specs=[x_big, w_spec(c_g), pl.BlockSpec((1, tn), lambda s, i: (0, s))],
        out_specs=pl.BlockSpec((tm_big, tn), lambda s, i: (i, s)),
        out_shape=jax.ShapeDtypeStruct((t, 2 * d_model), BF16),
        scratch_shapes=[wscr], compiler_params=cp, name="proj_gate",
    )(xn, w_in, b_gate.reshape(1, -1))
    return q, k, v, u, gates


ATTN_BLOCKS = 4


def _attn_prompt_kernel(sink_ref, q_ref, kc_ref, kp_ref, vc_ref, vp_ref, o_ref, kbuf_ref, vbuf_ref):
    blk = WINDOW
    n = pl.program_id(1)
    rows = 2 * blk
    kbuf_ref[:blk, :] = kp_ref[...]
    kbuf_ref[blk:, :] = kc_ref[...]
    vbuf_ref[:blk, :] = vp_ref[...]
    vbuf_ref[blk:, :] = vc_ref[...]
    r_iota = lax.broadcasted_iota(jnp.int32, (rows, rows), 0)
    c_iota = lax.broadcasted_iota(jnp.int32, (rows, rows), 1)
    qi = jnp.where(r_iota >= blk, r_iota - blk, r_iota)
    band = (c_iota >= qi) & (c_iota <= qi + WINDOW)
    lo_lane = lax.broadcasted_iota(jnp.int32, (blk, LANES), 1) < HEAD_DIM
    top_row = lax.broadcasted_iota(jnp.int32, (rows, 1), 0) < blk
    zero = jnp.zeros((blk, LANES), q_ref.dtype)
    group = q_ref.shape[1] // (N_KV_HEADS * LANES)

    def query_block(sb, carry):
        r0 = pl.multiple_of(sb * blk, blk)
        q_rows = pl.ds(r0, blk)
        kv_rows = pl.ds(r0, rows)
        mask = band & ((c_iota >= blk) | (n > 0) | (sb > 0))
        for g in range(N_KV_HEADS):
            hs = slice(g * HEAD_DIM, (g + 1) * HEAD_DIM)
            kg = kbuf_ref[kv_rows, hs]
            vg = vbuf_ref[kv_rows, hs]
            kdup = jnp.concatenate([kg, kg], axis=1).astype(BF16)
            vdup = jnp.concatenate([vg, vg], axis=1).astype(BF16)
            for p in range(group):
                pair = g * group + p
                ls = slice(pair * LANES, (pair + 1) * LANES)
                qp = q_ref[q_rows, ls]
                lhs = jnp.concatenate([jnp.where(lo_lane, qp, zero), jnp.where(lo_lane, zero, qp)], axis=0)
                s = lax.dot_general(lhs, kdup, (((1,), (1,)), ((), ())), preferred_element_type=F32)
                s = jnp.where(mask, s, NEG_INF)
                sink = jnp.where(top_row, sink_ref[2 * pair], sink_ref[2 * pair + 1])
                m = jnp.maximum(jnp.max(s, axis=-1, keepdims=True), sink)
                e = jnp.exp(s - m)
                denom = jnp.sum(e, axis=-1, keepdims=True) + jnp.exp(sink - m)
                o2 = jnp.dot(e.astype(BF16), vdup, preferred_element_type=F32) / denom
                o_ref[q_rows, ls] = jnp.where(lo_lane, o2[:blk], o2[blk:]).astype(o_ref.dtype)
        return carry

    lax.fori_loop(0, q_ref.shape[0] // blk, query_block, 0)


def _attn_prompt(q, k, v, sinks, bsz, slen):
    blk = WINDOW
    step = ATTN_BLOCKS * blk
    ns = slen // step
    qw, kw = q.shape[1], k.shape[1]
    cur = lambda b, n: (b * ns + n, 0)
    prev = lambda b, n: (jnp.maximum((b * ns + n) * ATTN_BLOCKS - 1, 0), 0)
    return pl.pallas_call(
        _attn_prompt_kernel,
        grid=(bsz, ns),
        in_specs=[
            pl.BlockSpec(memory_space=pltpu.SMEM),
            pl.BlockSpec((step, qw), cur),
            pl.BlockSpec((step, kw), cur), pl.BlockSpec((blk, kw), prev),
            pl.BlockSpec((step, kw), cur), pl.BlockSpec((blk, kw), prev),
        ],
        out_specs=pl.BlockSpec((step, qw), cur),
        out_shape=jax.ShapeDtypeStruct((bsz * slen, qw), BF16),
        scratch_shapes=[pltpu.VMEM((blk + step, kw), F32), pltpu.VMEM((blk + step, kw), F32)],
        compiler_params=_cparams(("arbitrary", "arbitrary")),
        name="attn_prompt",
    )(sinks.astype(F32), q, k, k, v, v)


def _attn_sample_kernel(sink_ref, q_ref, kt_ref, kn_ref, vt_ref, vn_ref, o_ref, kt_out_ref, vt_out_ref, *, qlen):
    nseq, rows, _ = q_ref.shape
    kvw, ncache = kt_ref.shape[1], kt_ref.shape[2]
    per_kv = rows // N_KV_HEADS
    all_rows = nseq * rows
    r_lane = _div_pow2(_mod_pow2(lax.broadcasted_iota(jnp.int32, (all_rows, kvw), 0), rows), per_kv)
    c_lane = _div_pow2(lax.broadcasted_iota(jnp.int32, (all_rows, kvw), 1), HEAD_DIM)
    own = r_lane == c_lane
    mask_c = (lax.broadcasted_iota(jnp.int32, (all_rows, ncache), 1)
              >= _mod_pow2(lax.broadcasted_iota(jnp.int32, (all_rows, ncache), 0), qlen))
    mask_n = (lax.broadcasted_iota(jnp.int32, (all_rows, qlen), 1)
              <= _mod_pow2(lax.broadcasted_iota(jnp.int32, (all_rows, qlen), 0), qlen))
    keep_old = lax.broadcasted_iota(jnp.int32, (kvw, ncache), 1) < ncache - qlen
    sink = sink_ref[...]
    contract_last = (((1,), (1,)), ((), ()))
    seq_rows = lambda b: slice(b * rows, (b + 1) * rows)
    new_rows = lambda b: slice(b * qlen, (b + 1) * qlen)

    q_all = q_ref[...].reshape(all_rows, HEAD_DIM)
    qe = jnp.where(own, jnp.concatenate([q_all] * N_KV_HEADS, axis=1), jnp.zeros((), q_all.dtype))
    sc = jnp.concatenate([jnp.dot(qe[seq_rows(b)], kt_ref[b].astype(BF16), preferred_element_type=F32)
                          for b in range(nseq)], axis=0)
    sn = jnp.concatenate([lax.dot_general(qe[seq_rows(b)], kn_ref[new_rows(b), :].astype(BF16), contract_last,
                                          preferred_element_type=F32) for b in range(nseq)], axis=0)
    sc = jnp.where(mask_c, sc, NEG_INF)
    sn = jnp.where(mask_n, sn, NEG_INF)
    m = jnp.maximum(jnp.maximum(jnp.max(sc, axis=-1, keepdims=True), jnp.max(sn, axis=-1, keepdims=True)), sink)
    ec = jnp.exp(sc - m)
    en = jnp.exp(sn - m)
    denom = jnp.sum(ec, axis=-1, keepdims=True) + jnp.sum(en, axis=-1, keepdims=True) + jnp.exp(sink - m)
    ec = ec.astype(BF16)
    en = en.astype(BF16)
    o = jnp.concatenate(
        [lax.dot_general(ec[seq_rows(b)], vt_ref[b].astype(BF16), contract_last, preferred_element_type=F32)
         + jnp.dot(en[seq_rows(b)], vn_ref[new_rows(b), :].astype(BF16), preferred_element_type=F32)
         for b in range(nseq)], axis=0)
    o = jnp.where(own, o, 0.0)
    acc = o[:, :HEAD_DIM]
    for g in range(1, N_KV_HEADS):
        acc = acc + o[:, g * HEAD_DIM:(g + 1) * HEAD_DIM]
    o_ref[...] = (acc / denom).astype(o_ref.dtype).reshape(nseq, rows, HEAD_DIM)

    def feature_major(new_ref):
        pad = jnp.zeros((ncache - nseq * qlen, kvw), F32)
        return jnp.concatenate([new_ref[...], pad], axis=0).T

    def shift_in(old_ref, new_ref, out_ref):
        new_t = feature_major(new_ref)
        for b in range(nseq):
            placed = pltpu.roll(new_t, ncache - qlen - b * qlen, axis=1)
            out_ref[b] = jnp.where(keep_old, pltpu.roll(old_ref[b], ncache - qlen, axis=1), placed)

    shift_in(kt_ref, kn_ref, kt_out_ref)
    shift_in(vt_ref, vn_ref, vt_out_ref)


def _attn_sample(q, k_cache, k_new, v_cache, v_new, sinks, seq_per_step=8):
    bsz, ncache = k_cache.shape[:2]
    kvw = k_new.shape[1]
    qlen = k_new.shape[0] // bsz
    assert seq_per_step * qlen <= ncache
    n_heads = q.shape[1] // HEAD_DIM
    group = n_heads // N_KV_HEADS
    rows = n_heads * qlen
    qr = q.reshape(bsz, qlen, N_KV_HEADS, group, HEAD_DIM).transpose(0, 2, 3, 1, 4).reshape(bsz, rows, HEAD_DIM)
    sink_rows = jnp.tile(jnp.repeat(sinks.astype(F32), qlen), seq_per_step).reshape(seq_per_step * rows, 1)
    feature_major = lambda c: c.transpose(0, 2, 3, 1).reshape(bsz, kvw, ncache)
    seq = lambda i: (i, 0, 0)
    cache_spec = pl.BlockSpec((seq_per_step, kvw, ncache), seq)
    new_spec = pl.BlockSpec((seq_per_step * qlen, kvw), lambda i: (i, 0))
    o, kt_next, vt_next = pl.pallas_call(
        functools.partial(_attn_sample_kernel, qlen=qlen),
        grid=(bsz // seq_per_step,),
        in_specs=[
            pl.BlockSpec((seq_per_step * rows, 1), lambda i: (0, 0)),
            pl.BlockSpec((seq_per_step, rows, HEAD_DIM), seq),
            cache_spec, new_spec, cache_spec, new_spec,
        ],
        out_specs=[pl.BlockSpec((seq_per_step, rows, HEAD_DIM), seq), cache_spec, cache_spec],
        out_shape=[jax.ShapeDtypeStruct((bsz, rows, HEAD_DIM), BF16),
                   jax.ShapeDtypeStruct((bsz, kvw, ncache), F32), jax.ShapeDtypeStruct((bsz, kvw, ncache), F32)],
        compiler_params=_cparams(("arbitrary",)),
        name="attn_sample",
    )(sink_rows, qr, feature_major(k_cache), k_new, feature_major(v_cache), v_new)
    position_major = lambda t: t.reshape(bsz, N_KV_HEADS, HEAD_DIM, ncache).transpose(0, 3, 1, 2)
    ao = o.reshape(bsz, N_KV_HEADS, group, qlen, HEAD_DIM).transpose(0, 3, 1, 2, 4).reshape(bsz * qlen, -1)
    return ao, position_major(kt_next), position_major(vt_next)


CONV_HALO = 32
CONV_ROWS = 64
CONV_COLS = 256


def _conv_prompt_kernel(uc_ref, up_ref, w_ref, b_ref, y_ref, sh_ref):
    tile = uc_ref.shape[0]
    n = pl.program_id(1)
    halo = jnp.where(n > 0, up_ref[...], 0.0)
    sh_ref[0, :CONV_HALO, :] = halo
    sh_ref[0, CONV_HALO:, :] = uc_ref[...]
    keep = tile + CONV_HALO - SUBLANES
    for r in range(1, SUBLANES):
        sh_ref[r, :keep, :] = sh_ref[0, r:r + keep, :]
    first = CONV_HALO - (CONV_WIDTH - 1)
    ch = uc_ref.shape[1]

    def rows_step(rc, carry):
        r0 = pl.multiple_of(rc * CONV_ROWS, CONV_ROWS)
        for cc in range(ch // CONV_COLS):
            cs = slice(cc * CONV_COLS, (cc + 1) * CONV_COLS)
            acc = jnp.zeros((CONV_ROWS, CONV_COLS), F32)
            for j in range(CONV_WIDTH):
                off = first + j
                a, r = off // SUBLANES, off % SUBLANES
                acc = acc + sh_ref[r, pl.ds(r0 + a * SUBLANES, CONV_ROWS), cs] * w_ref[j:j + 1, cs]
            y_ref[pl.ds(r0, CONV_ROWS), cs] = acc + b_ref[:, cs]
        return carry

    lax.fori_loop(0, tile // CONV_ROWS, rows_step, 0)


def _conv_prompt(u, w, b, bsz, slen, tile=512):
    ch = u.shape[1]
    nt = slen // tile
    per = tile // CONV_HALO
    return pl.pallas_call(
        _conv_prompt_kernel,
        grid=(bsz, nt),
        in_specs=[
            pl.BlockSpec((tile, ch), lambda bb, n: (bb * nt + n, 0)),
            pl.BlockSpec((CONV_HALO, ch), lambda bb, n: (jnp.maximum((bb * nt + n) * per - 1, 0), 0)),
            pl.BlockSpec((CONV_WIDTH, ch), lambda bb, n: (0, 0)),
            pl.BlockSpec((1, ch), lambda bb, n: (0, 0)),
        ],
        out_specs=pl.BlockSpec((tile, ch), lambda bb, n: (bb * nt + n, 0)),
        out_shape=jax.ShapeDtypeStruct((bsz * slen, ch), F32),
        scratch_shapes=[pltpu.VMEM((SUBLANES, tile + CONV_HALO, ch), F32)],
        compiler_params=_cparams(("arbitrary", "arbitrary")),
        name="conv_prompt",
    )(u, u, w, b.reshape(1, ch))


def _conv_sample_kernel(sc_ref, un_ref, w_ref, b_ref, y_ref, next_ref):
    nctx = sc_ref.shape[0]
    qlen = un_ref.shape[0]
    ch = w_ref.shape[1]

    def row(t):
        return sc_ref[t] if t < nctx else un_ref[t - nctx]

    for i in range(qlen):
        acc = jnp.zeros((sc_ref.shape[1], ch), F32)
        for j in range(CONV_WIDTH):
            acc = acc + row(i + j) * w_ref[j:j + 1, :]
        y_ref[i] = acc + b_ref[...]
    for t in range(nctx):
        next_ref[t] = row(t + qlen)


def _conv_sample(state_t, u_new_t, w, b, seq_per_step=16):
    nctx, bsz, ch = state_t.shape
    qlen = u_new_t.shape[0]
    seq = lambda i: (0, i, 0)
    return pl.pallas_call(
        _conv_sample_kernel,
        grid=(bsz // seq_per_step,),
        in_specs=[
            pl.BlockSpec((nctx, seq_per_step, ch), seq),
            pl.BlockSpec((qlen, seq_per_step, ch), seq),
            pl.BlockSpec((CONV_WIDTH, ch), lambda i: (0, 0)),
            pl.BlockSpec((1, ch), lambda i: (0, 0)),
        ],
        out_specs=[pl.BlockSpec((qlen, seq_per_step, ch), seq), pl.BlockSpec((nctx, seq_per_step, ch), seq)],
        out_shape=[jax.ShapeDtypeStruct((qlen, bsz, ch), F32), jax.ShapeDtypeStruct((nctx, bsz, ch), F32)],
        compiler_params=_cparams(("arbitrary",)),
        name="conv_sample",
    )(state_t, u_new_t, w, b.reshape(1, ch))


def _merge_kernel(y_ref, ao_ref, ga_ref, gc_ref, x_ref, lnw_ref, lnb_ref, wco_ref, wo_ref, fw_ref, wr_ref,
                  h_ref, xn_ref, lg_ref):
    y = y_ref[...]
    mu = jnp.mean(y, axis=-1, keepdims=True)
    yc = y - mu
    var = jnp.mean(yc * yc, axis=-1, keepdims=True)
    yl = yc * lax.rsqrt(var + EPS) * lnw_ref[...] + lnb_ref[...]
    act = yl * jax.nn.sigmoid(yl)
    conv_o = jnp.dot(act.astype(BF16), wco_ref[...], preferred_element_type=F32)
    mix = ga_ref[...].astype(F32) * ao_ref[...].astype(F32) + gc_ref[...].astype(F32) * conv_o
    h = x_ref[...] + jnp.dot(mix.astype(BF16), wo_ref[...], preferred_element_type=F32)
    h_ref[...] = h
    ms = jnp.mean(h * h, axis=-1, keepdims=True)
    xn = (h * lax.rsqrt(ms + EPS) * fw_ref[...]).astype(BF16)
    xn_ref[...] = xn
    lg_ref[...] = jnp.dot(xn, wr_ref[...], preferred_element_type=F32)


def _merge(y, ao, gates, row0, x, ln_w, ln_b, wco, wo, fw, wr, tile=256):
    t, d = x.shape
    ch = y.shape[1]
    off = row0 // tile
    row = lambda i: (i, 0)
    const = lambda shape: pl.BlockSpec(shape, lambda i: (0, 0), pipeline_mode=pl.Buffered(1))
    return pl.pallas_call(
        _merge_kernel,
        grid=(t // tile,),
        in_specs=[
            pl.BlockSpec((tile, ch), row),
            pl.BlockSpec((tile, d), row),
            pl.BlockSpec((tile, d), lambda i: (i + off, 0)),
            pl.BlockSpec((tile, d), lambda i: (i + off, 1)),
            pl.BlockSpec((tile, d), row),
            const((1, ch)), const((1, ch)), const((ch, d)), const((d, d)), const((1, d)),
            const((d, LANES)),
        ],
        out_specs=[pl.BlockSpec((tile, d), row), pl.BlockSpec((tile, d), row), pl.BlockSpec((tile, LANES), row)],
        out_shape=[jax.ShapeDtypeStruct((t, d), F32), jax.ShapeDtypeStruct((t, d), BF16),
                   jax.ShapeDtypeStruct((t, LANES), F32)],
        compiler_params=_cparams(("arbitrary",)),
        name="merge",
    )(y, ao, gates, gates, x, ln_w.reshape(1, ch), ln_b.reshape(1, ch), wco, wo, fw.reshape(1, d), wr)


def _route_kernel(lp_ref, ls_ref, tri_ref, lower_ref, meta_ref, metat_ref, cnt_ref, *, n_p):
    i = pl.program_id(0)
    used = _round_up(N_EXPERTS + N_GROUPS, SUBLANES)

    def body(l_ref):
        lg = l_ref[...].T[:used]
        shape = lg.shape
        row = lax.broadcasted_iota(jnp.int32, shape, 0)
        big = jnp.int32(LANES)
        is_g = (row >= N_EXPERTS) & (row < N_EXPERTS + N_GROUPS)
        gl = jnp.where(is_g, lg, -jnp.inf)
        gmax = jnp.max(gl, axis=0, keepdims=True)
        g_star = jnp.min(jnp.where(gl == gmax, row - N_EXPERTS, big), axis=0, keepdims=True)
        pg_star = 1.0 / jnp.sum(jnp.exp(gl - gmax), axis=0, keepdims=True)
        in_g = (row < N_EXPERTS) & (_div_pow2(row, EXPERTS_PER_GROUP) == g_star)
        el = jnp.where(in_g, lg, -jnp.inf)
        v1 = jnp.max(el, axis=0, keepdims=True)
        i1 = jnp.min(jnp.where(el == v1, row, big), axis=0, keepdims=True)
        el2 = jnp.where(row == i1, -jnp.inf, el)
        v2 = jnp.max(el2, axis=0, keepdims=True)
        i2 = jnp.min(jnp.where(el2 == v2, row, big), axis=0, keepdims=True)
        e2 = jnp.exp(v2 - v1)
        w1 = pg_star * (1.0 / (1.0 + e2))
        w2 = pg_star * (e2 / (1.0 + e2))
        oh1 = row == i1
        oh2 = row == i2
        cnt = jnp.where(oh1 | oh2, 1.0, 0.0)
        before = jnp.dot(cnt.astype(BF16), tri_ref[...], preferred_element_type=F32)
        tot = jnp.sum(cnt, axis=-1, keepdims=True)
        runs = jnp.floor((tot + (CHUNK_ROWS - 1)) * (1.0 / CHUNK_ROWS))
        runs_b = jnp.broadcast_to(runs, (used, LANES)).astype(BF16)
        start = jnp.dot(lower_ref[...], runs_b, preferred_element_type=F32)[:, 0:1] * CHUNK_ROWS
        slot = before + start
        s1 = jnp.sum(jnp.where(oh1, slot, 0.0), axis=0, keepdims=True)
        s2 = jnp.sum(jnp.where(oh2, slot, 0.0), axis=0, keepdims=True)
        fields = [i1.astype(F32), i2.astype(F32), w1, w2, s1, s2]
        fields += [jnp.zeros_like(w1)] * (META_FIELDS - len(fields))
        metat = jnp.concatenate(fields, axis=0)
        metat_ref[...] = metat
        pad = jnp.zeros((LANES - META_FIELDS, shape[1]), F32)
        meta_ref[...] = jnp.concatenate([metat, pad], axis=0).T
        tot_pad = jnp.concatenate([tot, jnp.zeros((LANES - used, 1), F32)], axis=0)
        cnt_ref[...] = jnp.broadcast_to(tot_pad, (LANES, LANES)).T[0:1]

    @pl.when(i < n_p)
    def _():
        body(lp_ref)

    @pl.when(i >= n_p)
    def _():
        body(ls_ref)


def _route(lg_p, lg_s):
    tile = TOKEN_BLOCK
    tp, ts = lg_p.shape[0], lg_s.shape[0]
    n_p, n_s = tp // tile, ts // tile
    nb = n_p + n_s
    used = _round_up(N_EXPERTS + N_GROUPS, SUBLANES)
    tri = (jnp.arange(tile)[:, None] < jnp.arange(tile)[None, :]).astype(BF16)
    lower = (jnp.arange(used)[:, None] > jnp.arange(used)[None, :]).astype(BF16)
    return pl.pallas_call(
        functools.partial(_route_kernel, n_p=n_p),
        grid=(nb,),
        in_specs=[
            pl.BlockSpec((tile, LANES), lambda i: (jnp.minimum(i, n_p - 1), 0)),
            pl.BlockSpec((tile, LANES), lambda i: (jnp.maximum(i - n_p, 0), 0)),
            pl.BlockSpec((tile, tile), lambda i: (0, 0)),
            pl.BlockSpec((used, used), lambda i: (0, 0)),
        ],
        out_specs=[pl.BlockSpec((tile, LANES), lambda i: (i, 0)),
                   pl.BlockSpec((None, META_FIELDS, tile), lambda i: (i, 0, 0)),
                   pl.BlockSpec((None, 1, LANES), lambda i: (i, 0, 0))],
        out_shape=[jax.ShapeDtypeStruct((tp + ts, LANES), F32),
                   jax.ShapeDtypeStruct((nb, META_FIELDS, tile), F32),
                   jax.ShapeDtypeStruct((nb, 1, LANES), F32)],
        compiler_params=_cparams(("arbitrary",)),
        name="route",
    )(lg_p, lg_s, tri, lower)


def _plan_sizes(n_tok):
    nb = n_tok // TOKEN_BLOCK
    pad_per_block = N_EXPERTS * (CHUNK_ROWS - 1)
    max_chunks = _round_up(-(-(TOKEN_BLOCK * TOP_K + pad_per_block) // CHUNK_ROWS), SUBLANES)
    n_rows = _round_up(n_tok * TOP_K + nb * pad_per_block + N_EXPERTS * (ROW_PAD - 1), ROW_PAD)
    n_items = (n_rows + N_EXPERTS * (EXPERT_ROWS - ROW_PAD)) // EXPERT_ROWS
    max_zero = N_EXPERTS * (ROW_PAD // CHUNK_ROWS)
    return nb, max_chunks, n_rows, n_items, max_zero


def _dispatch_plan(counts, max_chunks, n_items, max_zero):
    i32 = jnp.int32
    c = counts[:, 0, :N_EXPERTS].astype(i32)
    run = (c + CHUNK_ROWS - 1) // CHUNK_ROWS
    run_end = jnp.cumsum(run, axis=1)
    src_start = run_end - run
    nchunk = run_end[:, -1]
    seg = jnp.sum(run, axis=0) * CHUNK_ROWS
    padded = (seg + ROW_PAD - 1) // ROW_PAD * ROW_PAD
    pad_end = jnp.cumsum(padded)
    pad_start = pad_end - padded
    dst_start = pad_start[None, :] // CHUNK_ROWS + (jnp.cumsum(run, axis=0) - run)
    jj = jnp.arange(max_chunks, dtype=i32)[None, :, None]
    in_run = (src_start[:, None, :] <= jj) & (jj < run_end[:, None, :])
    chunk_dst = jnp.sum(jnp.where(in_run, (dst_start - src_start)[:, None, :] + jj, 0), axis=-1) * CHUNK_ROWS
    tail = (padded - seg) // CHUNK_ROWS
    tail_end = jnp.cumsum(tail)
    tail_start = tail_end - tail
    z = jnp.arange(max_zero, dtype=i32)[:, None]
    in_tail = (tail_start[None, :] <= z) & (z < tail_end[None, :])
    zero_base = (pad_start + seg) // CHUNK_ROWS - tail_start
    zero_dst = jnp.sum(jnp.where(in_tail, zero_base[None, :] + z, 0), axis=-1) * CHUNK_ROWS
    nzero = tail_end[-1]
    items_e = (padded + EXPERT_ROWS - 1) // EXPERT_ROWS
    item_end = jnp.cumsum(items_e)
    item_start = item_end - items_e
    w = jnp.arange(n_items, dtype=i32)[:, None]
    in_item = (item_start[None, :] <= w) & (w < item_end[None, :])
    k = w - item_start[None, :]
    e_ids = jnp.arange(N_EXPERTS, dtype=i32)
    last_e = jnp.max(jnp.where(items_e > 0, e_ids, 0))
    item_e = jnp.where(w[:, 0] < item_end[-1], jnp.sum(jnp.where(in_item, e_ids[None, :], 0), axis=-1), last_e)
    item_row0 = jnp.sum(jnp.where(in_item, pad_start[None, :] + k * EXPERT_ROWS, 0), axis=-1)
    item_rows = jnp.sum(jnp.where(in_item, jnp.clip(padded[None, :] - k * EXPERT_ROWS, 0, EXPERT_ROWS), 0), axis=-1)
    item_nsub = item_rows // ROW_PAD
    return (chunk_dst.reshape(-1).astype(i32), nchunk.astype(i32), zero_dst.astype(i32),
            nzero.reshape(1).astype(i32), item_e.astype(i32), item_row0.astype(i32), item_nsub.astype(i32))


SCATTER_GROUPS = 2


def _scatter_kernel(cd_ref, nc_ref, zd_ref, nz_ref, mt_ref, xp_ref, xs_ref, out_hbm, loc_ref, zero_ref, sem,
                    zero_sem, *, n_p, max_chunks):
    i = pl.program_id(0)
    buf = i % 2
    n_slots = loc_ref.shape[1]
    d = xp_ref.shape[1]
    mt = mt_ref[...]
    w1, w2 = mt[2:3, :], mt[3:4, :]
    s1, s2 = mt[4:5, :].astype(jnp.int32), mt[5:6, :].astype(jnp.int32)
    half = d // 2
    group = n_slots // SCATTER_GROUPS

    def sort_rows(x_ref):
        for s0 in range(0, n_slots, group):
            slot = lax.broadcasted_iota(jnp.int32, (group, mt.shape[1]), 0) + s0
            p1 = slot == s1
            p2 = slot == s2
            perm = jnp.where(p1 | p2, 1.0, 0.0).astype(BF16)
            wcol = jnp.sum(jnp.where(p1, w1, 0.0) + jnp.where(p2, w2, 0.0), axis=-1, keepdims=True)
            loc_ref[buf, s0:s0 + group, half:] = pltpu.bitcast(jnp.broadcast_to(wcol, (group, LANES)), jnp.uint32)
            rows = jnp.dot(perm, x_ref[...], preferred_element_type=F32)
            loc_ref[buf, s0:s0 + group, :half] = _pack_pairs(rows[:, :half], rows[:, half:])

    @pl.when(i < n_p)
    def _():
        sort_rows(xp_ref)

    @pl.when(i >= n_p)
    def _():
        sort_rows(xs_ref)

    def rows_at(r):
        return pl.ds(pl.multiple_of(r, CHUNK_ROWS), CHUNK_ROWS)

    def chunk_copy(blk, j):
        return pltpu.make_async_copy(loc_ref.at[blk % 2, rows_at(j * CHUNK_ROWS)],
                                     out_hbm.at[rows_at(cd_ref[blk * max_chunks + j])], sem.at[blk % 2])

    def zero_copy(z):
        return pltpu.make_async_copy(zero_ref, out_hbm.at[rows_at(zd_ref[z])], zero_sem)

    def loop(n, fn):
        def body(j, carry):
            fn(j)
            return carry
        lax.fori_loop(0, n, body, 0)

    @pl.when(i > 0)
    def _():
        loop(nc_ref[i - 1], lambda j: chunk_copy(i - 1, j).wait())

    loop(nc_ref[i], lambda j: chunk_copy(i, j).start())

    @pl.when(i == pl.num_programs(0) - 1)
    def _():
        zero_ref[...] = jnp.zeros_like(zero_ref)
        loop(nz_ref[0], lambda z: zero_copy(z).start())
        loop(nc_ref[i], lambda j: chunk_copy(i, j).wait())
        loop(nz_ref[0], lambda z: zero_copy(z).wait())


def _scatter_rows(xn_p, xn_s, metat, chunk_dst, nchunk, zero_dst, nzero, max_chunks, n_rows):
    tile = TOKEN_BLOCK
    tp, d = xn_p.shape
    ts = xn_s.shape[0]
    n_p, n_s = tp // tile, ts // tile
    n_slots = max_chunks * CHUNK_ROWS
    width = d // 2 + LANES
    u32 = jnp.uint32
    return pl.pallas_call(
        functools.partial(_scatter_kernel, n_p=n_p, max_chunks=max_chunks),
        grid_spec=pltpu.PrefetchScalarGridSpec(
            num_scalar_prefetch=4,
            grid=(n_p + n_s,),
            in_specs=[
                pl.BlockSpec((None, META_FIELDS, tile), lambda i, *_: (i, 0, 0)),
                pl.BlockSpec((tile, d), lambda i, *_: (jnp.minimum(i, n_p - 1), 0)),
                pl.BlockSpec((tile, d), lambda i, *_: (jnp.maximum(i - n_p, 0), 0)),
            ],
            out_specs=pl.BlockSpec(memory_space=pl.ANY),
            scratch_shapes=[pltpu.VMEM((2, n_slots, width), u32), pltpu.VMEM((CHUNK_ROWS, width), u32),
                            pltpu.SemaphoreType.DMA((2,)), pltpu.SemaphoreType.DMA(())],
        ),
        out_shape=jax.ShapeDtypeStruct((n_rows, width), u32),
        compiler_params=_cparams(("arbitrary",)),
        name="scatter_rows",
    )(chunk_dst, nchunk, zero_dst, nzero, metat, xn_p, xn_s)


def _expert_kernel(e_ref, row0_ref, nsub_ref, xs_hbm, wg_ref, wu_ref, wd_ref, out_hbm,
                   xf_ref, xb_ref, acc_ref, ob_ref, pend_ref, in_sem, out_sem):
    w = pl.program_id(0)
    c = pl.program_id(1)
    n_items = pl.num_programs(0)
    last_c = pl.num_programs(1) - 1
    nsub = nsub_ref[w]
    d = acc_ref.shape[1]
    half = d // 2
    slot = w % 2

    def sub_rows(s):
        return pl.ds(pl.multiple_of(s * ROW_PAD, ROW_PAD), ROW_PAD)

    def hbm_rows(item, s):
        return pl.ds(pl.multiple_of(row0_ref[item] + s * ROW_PAD, ROW_PAD), ROW_PAD)

    def in_copy(item, s):
        return pltpu.make_async_copy(xs_hbm.at[hbm_rows(item, s)], xf_ref.at[item % 2, sub_rows(s)],
                                     in_sem.at[item % 2])

    def out_copy(item, s):
        return pltpu.make_async_copy(ob_ref.at[sub_rows(s)], out_hbm.at[hbm_rows(item, s)], out_sem)

    def loop(n, fn):
        def body(s, carry):
            fn(s)
            return carry
        lax.fori_loop(0, n, body, 0)

    @pl.when(c == 0)
    def _():
        @pl.when(w == 0)
        def _():
            pend_ref[0] = 0
            pend_ref[1] = 0
            loop(nsub, lambda s: in_copy(w, s).start())

        loop(nsub, lambda s: in_copy(w, s).wait())
        nxt = jnp.minimum(w + 1, n_items - 1)

        @pl.when(w + 1 < n_items)
        def _():
            loop(nsub_ref[nxt], lambda s: in_copy(nxt, s).start())

    def drain_out():
        prev = pend_ref[1]
        loop(pend_ref[0], lambda s: out_copy(prev, s).wait())
        pend_ref[0] = 0

    for n in range(1, EXPERT_ROWS // ROW_PAD + 1):
        m = n * ROW_PAD

        @pl.when(nsub == n)
        def _(m=m):
            @pl.when(c == 0)
            def _():
                acc_ref[:m, :] = jnp.zeros((m, d), F32)
                xa, xb = _unpack_pairs(xf_ref[slot, :m, :half])
                xb_ref[:m, :half] = xa.astype(BF16)
                xb_ref[:m, half:] = xb.astype(BF16)

            x = xb_ref[:m, :]
            g = jnp.dot(x, wg_ref[...].astype(BF16), preferred_element_type=F32)
            u = jnp.dot(x, wu_ref[...].astype(BF16), preferred_element_type=F32)
            hmid = (g * jax.nn.sigmoid(g) * u).astype(BF16)
            acc_ref[:m, :] += jnp.dot(hmid, wd_ref[...].astype(BF16), preferred_element_type=F32)

            @pl.when(c == last_c)
            def _():
                drain_out()
                mix_w = pltpu.bitcast(xf_ref[slot, :m, half:half + 1], F32)
                y = (acc_ref[:m, :] * mix_w).astype(BF16).astype(F32)
                ob_ref[:m, :] = _pack_pairs(y[:, :half], y[:, half:])
                loop(nsub, lambda s: out_copy(w, s).start())
                pend_ref[0] = nsub
                pend_ref[1] = w

    @pl.when((w == n_items - 1) & (c == last_c))
    def _():
        drain_out()


def _experts(xs, item_e, item_row0, item_nsub, w_gate, w_up, w_down):
    n_rows, width = xs.shape
    d = w_gate.shape[1]
    assert width == d // 2 + LANES
    d_exp = w_gate.shape[2]
    nc = d_exp // D_CHUNK
    n_items = item_e.shape[0]

    def chunk_of(w, c, nsub_ref):
        return jnp.where(nsub_ref[w] > 0, c, nc - 1)

    return pl.pallas_call(
        _expert_kernel,
        grid_spec=pltpu.PrefetchScalarGridSpec(
            num_scalar_prefetch=3,
            grid=(n_items, nc),
            in_specs=[
                pl.BlockSpec(memory_space=pl.ANY),
                pl.BlockSpec((None, d, D_CHUNK), lambda w, c, e, r, n: (e[w], 0, chunk_of(w, c, n))),
                pl.BlockSpec((None, d, D_CHUNK), lambda w, c, e, r, n: (e[w], 0, chunk_of(w, c, n))),
                pl.BlockSpec((None, D_CHUNK, d), lambda w, c, e, r, n: (e[w], chunk_of(w, c, n), 0)),
            ],
            out_specs=pl.BlockSpec(memory_space=pl.ANY),
            scratch_shapes=[
                pltpu.VMEM((2, EXPERT_ROWS, width), jnp.uint32),
                pltpu.VMEM((EXPERT_ROWS, d), BF16),
                pltpu.VMEM((EXPERT_ROWS, d), F32),
                pltpu.VMEM((EXPERT_ROWS, d // 2), jnp.uint32),
                pltpu.SMEM((2,), jnp.int32),
                pltpu.SemaphoreType.DMA((2,)),
                pltpu.SemaphoreType.DMA(()),
            ],
        ),
        out_shape=jax.ShapeDtypeStruct((n_rows, d // 2), jnp.uint32),
        compiler_params=_cparams(("arbitrary", "arbitrary")),
        name="experts",
    )(item_e, item_row0, item_nsub, xs, w_gate, w_up, w_down)


COMBINE_SPLIT = 2
SLOT_STEP = 256


def _combine_kernel(cd_ref, nc_ref, meta_ref, hp_ref, hs_ref, yb_hbm, yp_ref, ys_ref, loc_ref, sem,
                    *, n_p, max_chunks):
    i = pl.program_id(0)
    j = pl.program_id(1)
    nb = pl.num_programs(0)
    buf = i % 2
    n_slots, half = loc_ref.shape[1], loc_ref.shape[2]
    d = 2 * half

    def rows_at(r):
        return pl.ds(pl.multiple_of(r, CHUNK_ROWS), CHUNK_ROWS)

    def chunk_copy(blk, c):
        return pltpu.make_async_copy(yb_hbm.at[rows_at(cd_ref[blk * max_chunks + c])],
                                     loc_ref.at[blk % 2, rows_at(c * CHUNK_ROWS)], sem.at[blk % 2])

    def fetch(blk):
        n = nc_ref[blk]

        def start(c, carry):
            chunk_copy(blk, c).start()
            return carry
        lax.fori_loop(0, n, start, 0)

        def clear(c, carry):
            loc_ref[blk % 2, rows_at(c * CHUNK_ROWS), :] = jnp.zeros((CHUNK_ROWS, half), jnp.uint32)
            return carry
        lax.fori_loop(n, max_chunks, clear, 0)

    @pl.when(j == 0)
    def _():
        @pl.when(i == 0)
        def _():
            fetch(i)

        def wait(c, carry):
            chunk_copy(i, c).wait()
            return carry
        lax.fori_loop(0, nc_ref[i], wait, 0)

        @pl.when(i + 1 < nb)
        def _():
            fetch(jnp.minimum(i + 1, nb - 1))

    meta = meta_ref[...]
    s1 = meta[:, 4:5].astype(jnp.int32)
    s2 = meta[:, 5:6].astype(jnp.int32)
    rows = meta.shape[0]
    f = jnp.zeros((rows, d), F32)
    for r0 in range(0, n_slots, SLOT_STEP):
        slot = lax.broadcasted_iota(jnp.int32, (rows, SLOT_STEP), 1) + r0
        pt = jnp.where((slot == s1) | (slot == s2), 1.0, 0.0).astype(BF16)
        ya, yb = _unpack_pairs(loc_ref[buf, r0:r0 + SLOT_STEP, :])
        f = f + jnp.dot(pt, jnp.concatenate([ya, yb], axis=1).astype(BF16), preferred_element_type=F32)
    g = i * pl.num_programs(1) + j

    @pl.when(g < n_p)
    def _():
        yp_ref[...] = hp_ref[...] + f

    @pl.when(g >= n_p)
    def _():
        ys_ref[...] = hs_ref[...] + f


def _combine(h_p, h_s, meta, chunk_dst, nchunk, yb, max_chunks):
    tp, d = h_p.shape
    ts = h_s.shape[0]
    tile = TOKEN_BLOCK // COMBINE_SPLIT
    n_p, n_s = tp // tile, ts // tile
    nb = (tp + ts) // TOKEN_BLOCK
    n_slots = max_chunks * CHUNK_ROWS
    assert n_slots % SLOT_STEP == 0
    blk = lambda i, j: i * COMBINE_SPLIT + j
    p_idx = lambda i, j, *_: (jnp.minimum(blk(i, j), n_p - 1), 0)
    s_idx = lambda i, j, *_: (jnp.maximum(blk(i, j) - n_p, 0), 0)
    return pl.pallas_call(
        functools.partial(_combine_kernel, n_p=n_p, max_chunks=max_chunks),
        grid_spec=pltpu.PrefetchScalarGridSpec(
            num_scalar_prefetch=2,
            grid=(nb, COMBINE_SPLIT),
            in_specs=[
                pl.BlockSpec((tile, LANES), lambda i, j, *_: (blk(i, j), 0)),
                pl.BlockSpec((tile, d), p_idx),
                pl.BlockSpec((tile, d), s_idx),
                pl.BlockSpec(memory_space=pl.ANY),
            ],
            out_specs=[pl.BlockSpec((tile, d), p_idx), pl.BlockSpec((tile, d), s_idx)],
            scratch_shapes=[pltpu.VMEM((2, n_slots, d // 2), jnp.uint32), pltpu.SemaphoreType.DMA((2,))],
        ),
        out_shape=[jax.ShapeDtypeStruct((tp, d), F32), jax.ShapeDtypeStruct((ts, d), F32)],
        compiler_params=_cparams(("arbitrary", "arbitrary")),
        name="combine",
    )(chunk_dst, nchunk, meta, h_p, h_s, yb)


def _layer(x_p, x_s, ck, cv, sc, attn_norm_w, w_in, b_gate, q_norm_w, k_norm_w, sinks, conv_dw_w, conv_dw_b,
           conv_ln_w, conv_ln_b, w_conv_out, w_out, ffn_norm_w, router_group, router_expert, w_gate, w_up,
           w_down):
    bsz, slen, d = x_p.shape
    dbs, dlen, _ = x_s.shape
    tp, ts = bsz * slen, dbs * dlen
    conv_ch = conv_dw_w.shape[1]
    kvw = N_KV_HEADS * HEAD_DIM
    xp2, xs2 = x_p.reshape(tp, d), x_s.reshape(ts, d)

    xn = _prenorm(xp2, xs2, attn_norm_w)
    q, k, v, u, gates = _project(xn, w_in, b_gate, q_norm_w, k_norm_w, d, conv_ch)

    ao_p = _attn_prompt(q, k, v, sinks, bsz, slen)
    ao_s, ck_next, cv_next = _attn_sample(q[tp:], ck, k[tp:], cv, v[tp:], sinks)

    y_p = _conv_prompt(u, conv_dw_w, conv_dw_b, bsz, slen)
    sc_t = sc.transpose(1, 0, 2)
    u_s_t = u[tp:].reshape(dbs, dlen, conv_ch).transpose(1, 0, 2)
    y_s_t, sc_next_t = _conv_sample(sc_t, u_s_t, conv_dw_w, conv_dw_b)
    y_s = y_s_t.transpose(1, 0, 2).reshape(ts, conv_ch)

    wco = w_conv_out.astype(BF16)
    wo = w_out.astype(BF16)
    wr = jnp.concatenate([router_expert, router_group,
                          jnp.zeros((d, LANES - N_EXPERTS - N_GROUPS), router_expert.dtype)], axis=1).astype(BF16)
    margs = (conv_ln_w, conv_ln_b, wco, wo, ffn_norm_w, wr)
    h_p, xn_p, lg_p = _merge(y_p, ao_p, gates, 0, xp2, *margs)
    h_s, xn_s, lg_s = _merge(y_s, ao_s, gates, tp, xs2, *margs)

    meta, metat, counts = _route(lg_p, lg_s)
    _, max_chunks, n_rows, n_items, max_zero = _plan_sizes(tp + ts)
    chunk_dst, nchunk, zero_dst, nzero, item_e, item_row0, item_nsub = _dispatch_plan(
        counts, max_chunks, n_items, max_zero)
    xs_rows = _scatter_rows(xn_p, xn_s, metat, chunk_dst, nchunk, zero_dst, nzero, max_chunks, n_rows)
    yb = _experts(xs_rows, item_e, item_row0, item_nsub, w_gate, w_up, w_down)
    out_p, out_s = _combine(h_p, h_s, meta, chunk_dst, nchunk, yb, max_chunks)

    def seq_tails(a, n):
        return jnp.stack([lax.slice_in_dim(a, (b + 1) * slen - n, (b + 1) * slen, axis=0) for b in range(bsz)])

    new_p = (seq_tails(k, WINDOW).reshape(bsz, WINDOW, N_KV_HEADS, HEAD_DIM),
             seq_tails(v, WINDOW).reshape(bsz, WINDOW, N_KV_HEADS, HEAD_DIM),
             seq_tails(u, CONV_WIDTH - 1))
    new_s = (ck_next, cv_next, sc_next_t.transpose(1, 0, 2))
    return out_p.reshape(bsz, slen, d), out_s.reshape(dbs, dlen, d), new_p, new_s


def kernel(x_prompt, x_sample, cache_k, cache_v, state_conv, attn_norm_w, w_in, b_gate, q_norm_w, k_norm_w,
           sinks, conv_dw_w, conv_dw_b, conv_ln_w, conv_ln_b, w_conv_out, w_out, ffn_norm_w, router_group,
           router_expert, w_gate, w_up, w_down):
    x_p, x_s = x_prompt, x_sample
    kp, vp, cp, ks_, vs_, cs_ = [], [], [], [], [], []
    for l in range(cache_k.shape[0]):
        x_p, x_s, new_p, new_s = _layer(
            x_p, x_s, cache_k[l], cache_v[l], state_conv[l], attn_norm_w[l], w_in[l], b_gate[l], q_norm_w[l],
            k_norm_w[l], sinks[l], conv_dw_w[l], conv_dw_b[l], conv_ln_w[l], conv_ln_b[l], w_conv_out[l],
            w_out[l], ffn_norm_w[l], router_group[l], router_expert[l], w_gate[l], w_up[l], w_down[l])
        kp.append(new_p[0]); vp.append(new_p[1]); cp.append(new_p[2])
        ks_.append(new_s[0]); vs_.append(new_s[1]); cs_.append(new_s[2])
    return (x_p, x_s, jnp.stack(kp), jnp.stack(vp), jnp.stack(cp),
            jnp.stack(ks_), jnp.stack(vs_), jnp.stack(cs_))
```

```python
import functools

import jax
import jax.numpy as jnp
from jax import lax
from jax.experimental import pallas as pl
from jax.experimental.pallas import tpu as pltpu

F32 = jnp.float32
BF16 = jnp.bfloat16
EPS = 1e-6
NEG_INF = -1e30

HEAD_DIM = 64
N_KV_HEADS = 4
WINDOW = 128
CONV_WIDTH = 31
N_GROUPS = 4
EXPERTS_PER_GROUP = 8
N_EXPERTS = N_GROUPS * EXPERTS_PER_GROUP
TOP_K = 2

LANES = 128
SUBLANES = 8
MXU_DIM = 256
VMEM_LIMIT = 56 * 1024 * 1024

TOKEN_BLOCK = 512
CHUNK_ROWS = SUBLANES
ROW_PAD = 128
EXPERT_ROWS = 768
D_CHUNK = 512
META_FIELDS = 8


def _cparams(sem):
    return pltpu.CompilerParams(dimension_semantics=sem, vmem_limit_bytes=VMEM_LIMIT)


def _log2(n):
    assert n > 0 and n & (n - 1) == 0, n
    return n.bit_length() - 1


def _div_pow2(x, n):
    return lax.shift_right_logical(x, jnp.int32(_log2(n)))


def _mod_pow2(x, n):
    _log2(n)
    return x & jnp.int32(n - 1)


def _round_up(x, m):
    return (x + m - 1) // m * m


HIGH_HALF = 0xFFFF0000


def _pack_pairs(a, b):
    ua = pltpu.bitcast(a, jnp.uint32) & jnp.uint32(HIGH_HALF)
    return ua | (pltpu.bitcast(b, jnp.uint32) >> 16)


def _unpack_pairs(w):
    return (pltpu.bitcast(w & jnp.uint32(HIGH_HALF), F32), pltpu.bitcast(w << 16, F32))


def _prenorm_kernel(xp_ref, xs_ref, w_ref, o_ref, *, n_p):
    i = pl.program_id(0)

    def body(x_ref):
        x = x_ref[...]
        ms = jnp.mean(x * x, axis=-1, keepdims=True)
        o_ref[...] = (x * lax.rsqrt(ms + EPS) * w_ref[...]).astype(o_ref.dtype)

    @pl.when(i < n_p)
    def _():
        body(xp_ref)

    @pl.when(i >= n_p)
    def _():
        body(xs_ref)


def _prenorm(xp, xs, w, tile=512):
    tp, d = xp.shape
    ts = xs.shape[0]
    n_p, n_s = tp // tile, ts // tile
    return pl.pallas_call(
        functools.partial(_prenorm_kernel, n_p=n_p),
        grid=(n_p + n_s,),
        in_specs=[
            pl.BlockSpec((tile, d), lambda i: (jnp.minimum(i, n_p - 1), 0)),
            pl.BlockSpec((tile, d), lambda i: (jnp.maximum(i - n_p, 0), 0)),
            pl.BlockSpec((1, d), lambda i: (0, 0)),
        ],
        out_specs=pl.BlockSpec((tile, d), lambda i: (i, 0)),
        out_shape=jax.ShapeDtypeStruct((tp + ts, d), BF16),
        compiler_params=_cparams(("arbitrary",)),
        name="prenorm",
    )(xp, xs, w.reshape(1, d))


def _head_rms(z, hw, bd_ref):
    ss = z * z
    hi = ss.astype(BF16)
    lo = (ss - hi.astype(F32)).astype(BF16)
    bd = bd_ref[...]
    tot = jnp.dot(hi, bd, preferred_element_type=F32) + jnp.dot(lo, bd, preferred_element_type=F32)
    return z * lax.rsqrt(tot * (1.0 / HEAD_DIM) + EPS) * hw


def _load_w(i, w_ref, wb_ref):
    @pl.when(i == 0)
    def _():
        wb_ref[...] = w_ref[...].astype(BF16)


PROJ_SUB_NORM = 544
PROJ_SUB_ACT = 272


def _row_groups(n_rows, sub):
    sub = sub if n_rows % sub == 0 else n_rows
    return [slice(r0, r0 + sub) for r0 in range(0, n_rows, sub)]


def _proj_q_kernel(x_ref, w_ref, hw_ref, bd_ref, o_ref, wb_ref):
    _load_w(pl.program_id(1), w_ref, wb_ref)
    for rs in _row_groups(x_ref.shape[0], PROJ_SUB_NORM):
        z = jnp.dot(x_ref[rs, :], wb_ref[...], preferred_element_type=F32)
        for c in range(z.shape[1] // MXU_DIM):
            sl = slice(c * MXU_DIM, (c + 1) * MXU_DIM)
            o_ref[rs, sl] = _head_rms(z[:, sl], hw_ref[...], bd_ref).astype(o_ref.dtype)


def _proj_kv_kernel(x_ref, w_ref, hw_ref, bd_ref, k_ref, v_ref, wb_ref):
    _load_w(pl.program_id(1), w_ref, wb_ref)
    kw = k_ref.shape[1]
    for rs in _row_groups(x_ref.shape[0], PROJ_SUB_NORM):
        z = jnp.dot(x_ref[rs, :], wb_ref[...], preferred_element_type=F32)
        k_ref[rs, :] = _head_rms(z[:, :kw], hw_ref[...], bd_ref)
        v_ref[rs, :] = z[:, kw:]


def _proj_glu_kernel(x_ref, wa_ref, wg_ref, o_ref, wab_ref, wgb_ref):
    i = pl.program_id(1)
    _load_w(i, wa_ref, wab_ref)
    _load_w(i, wg_ref, wgb_ref)
    for rs in _row_groups(x_ref.shape[0], PROJ_SUB_ACT):
        x = x_ref[rs, :]
        a = jnp.dot(x, wab_ref[...], preferred_element_type=F32)
        g = jnp.dot(x, wgb_ref[...], preferred_element_type=F32)
        o_ref[rs, :] = a * jax.nn.sigmoid(g)


def _proj_gate_kernel(x_ref, w_ref, b_ref, o_ref, wb_ref):
    _load_w(pl.program_id(1), w_ref, wb_ref)
    for rs in _row_groups(x_ref.shape[0], PROJ_SUB_ACT):
        z = jnp.dot(x_ref[rs, :], wb_ref[...], preferred_element_type=F32)
        o_ref[rs, :] = jax.nn.sigmoid(z + b_ref[...]).astype(o_ref.dtype)


def _row_tile(t, cap=1152):
    best = 16
    for m in range(16, cap + 1, 16):
        if t % m == 0:
            best = m
    return best


def _project(xn, w_in, b_gate, q_norm_w, k_norm_w, d_model, conv_ch):
    t, d = xn.shape
    q_w = d_model
    kv_w = N_KV_HEADS * HEAD_DIM
    tn = 2 * kv_w
    tm = _row_tile(t)
    ni = t // tm
    tm_big = _row_tile(t, cap=2304)
    ni_big = t // tm_big
    reps = MXU_DIM // HEAD_DIM
    hq = jnp.tile(q_norm_w.astype(F32) * HEAD_DIM ** -0.5, reps).reshape(1, MXU_DIM)
    hk = jnp.tile(k_norm_w.astype(F32), reps).reshape(1, MXU_DIM)
    gid = jnp.arange(MXU_DIM) // HEAD_DIM
    bd = (gid[:, None] == gid[None, :]).astype(BF16)

    x_spec = pl.BlockSpec((tm, d), lambda s, i: (i, 0))
    x_big = pl.BlockSpec((tm_big, d), lambda s, i: (i, 0))
    small = lambda shape: pl.BlockSpec(shape, lambda s, i: (0, 0))

    def w_spec(c0):
        return pl.BlockSpec((d, tn), lambda s, i: (0, c0 + s))

    wscr = pltpu.VMEM((d, tn), BF16)
    cp = _cparams(("arbitrary", "arbitrary"))

    q = pl.pallas_call(
        _proj_q_kernel, grid=(q_w // tn, ni_big),
        in_specs=[x_big, w_spec(0), small((1, MXU_DIM)), small((MXU_DIM, MXU_DIM))],
        out_specs=pl.BlockSpec((tm_big, tn), lambda s, i: (i, s)),
        out_shape=jax.ShapeDtypeStruct((t, q_w), BF16),
        scratch_shapes=[wscr], compiler_params=cp, name="proj_q",
    )(xn, w_in, hq, bd)

    c_kv = q_w // tn
    k, v = pl.pallas_call(
        _proj_kv_kernel, grid=(1, ni),
        in_specs=[x_spec, w_spec(c_kv), small((1, MXU_DIM)), small((MXU_DIM, MXU_DIM))],
        out_specs=[pl.BlockSpec((tm, kv_w), lambda s, i: (i, 0))] * 2,
        out_shape=[jax.ShapeDtypeStruct((t, kv_w), F32)] * 2,
        scratch_shapes=[wscr], compiler_params=cp, name="proj_kv",
    )(xn, w_in, hk, bd)

    c_a = c_kv + 1
    n_glu = conv_ch // tn
    u = pl.pallas_call(
        _proj_glu_kernel, grid=(n_glu, ni),
        in_specs=[x_spec, w_spec(c_a), w_spec(c_a + n_glu)],
        out_specs=pl.BlockSpec((tm, tn), lambda s, i: (i, s)),
        out_shape=jax.ShapeDtypeStruct((t, conv_ch), F32),
        scratch_shapes=[wscr, wscr], compiler_params=cp, name="proj_glu",
    )(xn, w_in, w_in)

    c_g = c_a + 2 * n_glu
    n_gate = 2 * d_model // tn
    gates = pl.pallas_call(
        _proj_gate_kernel, grid=(n_gate, ni_big),
        in_specs=[x_big, w_spec(c_g), pl.BlockSpec((1, tn), lambda s, i: (0, s))],
        out_specs=pl.BlockSpec((tm_big, tn), lambda s, i: (i, s)),
        out_shape=jax.ShapeDtypeStruct((t, 2 * d_model), BF16),
        scratch_shapes=[wscr], compiler_params=cp, name="proj_gate",
    )(xn, w_in, b_gate.reshape(1, -1))
    return q, k, v, u, gates


ATTN_BLOCKS = 4


def _attn_prompt_kernel(sink_ref, q_ref, kc_ref, kp_ref, vc_ref, vp_ref, o_ref, kbuf_ref, vbuf_ref):
    blk = WINDOW
    n = pl.program_id(1)
    rows = 2 * blk
    kbuf_ref[:blk, :] = kp_ref[...]
    kbuf_ref[blk:, :] = kc_ref[...]
    vbuf_ref[:blk, :] = vp_ref[...]
    vbuf_ref[blk:, :] = vc_ref[...]
    r_iota = lax.broadcasted_iota(jnp.int32, (rows, rows), 0)
    c_iota = lax.broadcasted_iota(jnp.int32, (rows, rows), 1)
    qi = jnp.where(r_iota >= blk, r_iota - blk, r_iota)
    band = (c_iota >= qi) & (c_iota <= qi + WINDOW)
    lo_lane = lax.broadcasted_iota(jnp.int32, (blk, LANES), 1) < HEAD_DIM
    top_row = lax.broadcasted_iota(jnp.int32, (rows, 1), 0) < blk
    zero = jnp.zeros((blk, LANES), q_ref.dtype)
    group = q_ref.shape[1] // (N_KV_HEADS * LANES)

    def query_block(sb, carry):
        r0 = pl.multiple_of(sb * blk, blk)
        q_rows = pl.ds(r0, blk)
        kv_rows = pl.ds(r0, rows)
        mask = band & ((c_iota >= blk) | (n > 0) | (sb > 0))
        for g in range(N_KV_HEADS):
            hs = slice(g * HEAD_DIM, (g + 1) * HEAD_DIM)
            kg = kbuf_ref[kv_rows, hs]
            vg = vbuf_ref[kv_rows, hs]
            kdup = jnp.concatenate([kg, kg], axis=1).astype(BF16)
            vdup = jnp.concatenate([vg, vg], axis=1).astype(BF16)
            for p in range(group):
                pair = g * group + p
                ls = slice(pair * LANES, (pair + 1) * LANES)
                qp = q_ref[q_rows, ls]
                lhs = jnp.concatenate([jnp.where(lo_lane, qp, zero), jnp.where(lo_lane, zero, qp)], axis=0)
                s = lax.dot_general(lhs, kdup, (((1,), (1,)), ((), ())), preferred_element_type=F32)
                s = jnp.where(mask, s, NEG_INF)
                sink = jnp.where(top_row, sink_ref[2 * pair], sink_ref[2 * pair + 1])
                m = jnp.maximum(jnp.max(s, axis=-1, keepdims=True), sink)
                e = jnp.exp(s - m)
                denom = jnp.sum(e, axis=-1, keepdims=True) + jnp.exp(sink - m)
                o2 = jnp.dot(e.astype(BF16), vdup, preferred_element_type=F32) / denom
                o_ref[q_rows, ls] = jnp.where(lo_lane, o2[:blk], o2[blk:]).astype(o_ref.dtype)
        return carry

    lax.fori_loop(0, q_ref.shape[0] // blk, query_block, 0)


def _attn_prompt(q, k, v, sinks, bsz, slen):
    blk = WINDOW
    step = ATTN_BLOCKS * blk
    ns = slen // step
    qw, kw = q.shape[1], k.shape[1]
    cur = lambda b, n: (b * ns + n, 0)
    prev = lambda b, n: (jnp.maximum((b * ns + n) * ATTN_BLOCKS - 1, 0), 0)
    return pl.pallas_call(
        _attn_prompt_kernel,
        grid=(bsz, ns),
        in_specs=[
            pl.BlockSpec(memory_space=pltpu.SMEM),
            pl.BlockSpec((step, qw), cur),
            pl.BlockSpec((step, kw), cur), pl.BlockSpec((blk, kw), prev),
            pl.BlockSpec((step, kw), cur), pl.BlockSpec((blk, kw), prev),
        ],
        out_specs=pl.BlockSpec((step, qw), cur),
        out_shape=jax.ShapeDtypeStruct((bsz * slen, qw), BF16),
        scratch_shapes=[pltpu.VMEM((blk + step, kw), F32), pltpu.VMEM((blk + step, kw), F32)],
        compiler_params=_cparams(("arbitrary", "arbitrary")),
        name="attn_prompt",
    )(sinks.astype(F32), q, k, k, v, v)


def _attn_sample_kernel(sink_ref, q_ref, kt_ref, kn_ref, vt_ref, vn_ref, o_ref, kt_out_ref, vt_out_ref, *, qlen):
    nseq, rows, _ = q_ref.shape
    kvw, ncache = kt_ref.shape[1], kt_ref.shape[2]
    per_kv = rows // N_KV_HEADS
    all_rows = nseq * rows
    r_lane = _div_pow2(_mod_pow2(lax.broadcasted_iota(jnp.int32, (all_rows, kvw), 0), rows), per_kv)
    c_lane = _div_pow2(lax.broadcasted_iota(jnp.int32, (all_rows, kvw), 1), HEAD_DIM)
    own = r_lane == c_lane
    mask_c = (lax.broadcasted_iota(jnp.int32, (all_rows, ncache), 1)
              >= _mod_pow2(lax.broadcasted_iota(jnp.int32, (all_rows, ncache), 0), qlen))
    mask_n = (lax.broadcasted_iota(jnp.int32, (all_rows, qlen), 1)
              <= _mod_pow2(lax.broadcasted_iota(jnp.int32, (all_rows, qlen), 0), qlen))
    keep_old = lax.broadcasted_iota(jnp.int32, (kvw, ncache), 1) < ncache - qlen
    sink = sink_ref[...]
    contract_last = (((1,), (1,)), ((), ()))
    seq_rows = lambda b: slice(b * rows, (b + 1) * rows)
    new_rows = lambda b: slice(b * qlen, (b + 1) * qlen)

    q_all = q_ref[...].reshape(all_rows, HEAD_DIM)
    qe = jnp.where(own, jnp.concatenate([q_all] * N_KV_HEADS, axis=1), jnp.zeros((), q_all.dtype))
    sc = jnp.concatenate([jnp.dot(qe[seq_rows(b)], kt_ref[b].astype(BF16), preferred_element_type=F32)
                          for b in range(nseq)], axis=0)
    sn = jnp.concatenate([lax.dot_general(qe[seq_rows(b)], kn_ref[new_rows(b), :].astype(BF16), contract_last,
                                          preferred_element_type=F32) for b in range(nseq)], axis=0)
    sc = jnp.where(mask_c, sc, NEG_INF)
    sn = jnp.where(mask_n, sn, NEG_INF)
    m = jnp.maximum(jnp.maximum(jnp.max(sc, axis=-1, keepdims=True), jnp.max(sn, axis=-1, keepdims=True)), sink)
    ec = jnp.exp(sc - m)
    en = jnp.exp(sn - m)
    denom = jnp.sum(ec, axis=-1, keepdims=True) + jnp.sum(en, axis=-1, keepdims=True) + jnp.exp(sink - m)
    ec = ec.astype(BF16)
    en = en.astype(BF16)
    o = jnp.concatenate(
        [lax.dot_general(ec[seq_rows(b)], vt_ref[b].astype(BF16), contract_last, preferred_element_type=F32)
         + jnp.dot(en[seq_rows(b)], vn_ref[new_rows(b), :].astype(BF16), preferred_element_type=F32)
         for b in range(nseq)], axis=0)
    o = jnp.where(own, o, 0.0)
    acc = o[:, :HEAD_DIM]
    for g in range(1, N_KV_HEADS):
        acc = acc + o[:, g * HEAD_DIM:(g + 1) * HEAD_DIM]
    o_ref[...] = (acc / denom).astype(o_ref.dtype).reshape(nseq, rows, HEAD_DIM)

    def feature_major(new_ref):
        pad = jnp.zeros((ncache - nseq * qlen, kvw), F32)
        return jnp.concatenate([new_ref[...], pad], axis=0).T

    def shift_in(old_ref, new_ref, out_ref):
        new_t = feature_major(new_ref)
        for b in range(nseq):
            placed = pltpu.roll(new_t, ncache - qlen - b * qlen, axis=1)
            out_ref[b] = jnp.where(keep_old, pltpu.roll(old_ref[b], ncache - qlen, axis=1), placed)

    shift_in(kt_ref, kn_ref, kt_out_ref)
    shift_in(vt_ref, vn_ref, vt_out_ref)


def _attn_sample(q, k_cache, k_new, v_cache, v_new, sinks, seq_per_step=8):
    bsz, ncache = k_cache.shape[:2]
    kvw = k_new.shape[1]
    qlen = k_new.shape[0] // bsz
    assert seq_per_step * qlen <= ncache
    n_heads = q.shape[1] // HEAD_DIM
    group = n_heads // N_KV_HEADS
    rows = n_heads * qlen
    qr = q.reshape(bsz, qlen, N_KV_HEADS, group, HEAD_DIM).transpose(0, 2, 3, 1, 4).reshape(bsz, rows, HEAD_DIM)
    sink_rows = jnp.tile(jnp.repeat(sinks.astype(F32), qlen), seq_per_step).reshape(seq_per_step * rows, 1)
    feature_major = lambda c: c.transpose(0, 2, 3, 1).reshape(bsz, kvw, ncache)
    seq = lambda i: (i, 0, 0)
    cache_spec = pl.BlockSpec((seq_per_step, kvw, ncache), seq)
    new_spec = pl.BlockSpec((seq_per_step * qlen, kvw), lambda i: (i, 0))
    o, kt_next, vt_next = pl.pallas_call(
        functools.partial(_attn_sample_kernel, qlen=qlen),
        grid=(bsz // seq_per_step,),
        in_specs=[
            pl.BlockSpec((seq_per_step * rows, 1), lambda i: (0, 0)),
            pl.BlockSpec((seq_per_step, rows, HEAD_DIM), seq),
            cache_spec, new_spec, cache_spec, new_spec,
        ],
        out_specs=[pl.BlockSpec((seq_per_step, rows, HEAD_DIM), seq), cache_spec, cache_spec],
        out_shape=[jax.ShapeDtypeStruct((bsz, rows, HEAD_DIM), BF16),
                   jax.ShapeDtypeStruct((bsz, kvw, ncache), F32), jax.ShapeDtypeStruct((bsz, kvw, ncache), F32)],
        compiler_params=_cparams(("arbitrary",)),
        name="attn_sample",
    )(sink_rows, qr, feature_major(k_cache), k_new, feature_major(v_cache), v_new)
    position_major = lambda t: t.reshape(bsz, N_KV_HEADS, HEAD_DIM, ncache).transpose(0, 3, 1, 2)
    ao = o.reshape(bsz, N_KV_HEADS, group, qlen, HEAD_DIM).transpose(0, 3, 1, 2, 4).reshape(bsz * qlen, -1)
    return ao, position_major(kt_next), position_major(vt_next)


CONV_HALO = 32
CONV_ROWS = 64
CONV_COLS = 256


def _conv_prompt_kernel(uc_ref, up_ref, w_ref, b_ref, y_ref, sh_ref):
    tile = uc_ref.shape[0]
    n = pl.program_id(1)
    halo = jnp.where(n > 0, up_ref[...], 0.0)
    sh_ref[0, :CONV_HALO, :] = halo
    sh_ref[0, CONV_HALO:, :] = uc_ref[...]
    keep = tile + CONV_HALO - SUBLANES
    for r in range(1, SUBLANES):
        sh_ref[r, :keep, :] = sh_ref[0, r:r + keep, :]
    first = CONV_HALO - (CONV_WIDTH - 1)
    ch = uc_ref.shape[1]

    def rows_step(rc, carry):
        r0 = pl.multiple_of(rc * CONV_ROWS, CONV_ROWS)
        for cc in range(ch // CONV_COLS):
            cs = slice(cc * CONV_COLS, (cc + 1) * CONV_COLS)
            acc = jnp.zeros((CONV_ROWS, CONV_COLS), F32)
            for j in range(CONV_WIDTH):
                off = first + j
                a, r = off // SUBLANES, off % SUBLANES
                acc = acc + sh_ref[r, pl.ds(r0 + a * SUBLANES, CONV_ROWS), cs] * w_ref[j:j + 1, cs]
            y_ref[pl.ds(r0, CONV_ROWS), cs] = acc + b_ref[:, cs]
        return carry

    lax.fori_loop(0, tile // CONV_ROWS, rows_step, 0)


def _conv_prompt(u, w, b, bsz, slen, tile=512):
    ch = u.shape[1]
    nt = slen // tile
    per = tile // CONV_HALO
    return pl.pallas_call(
        _conv_prompt_kernel,
        grid=(bsz, nt),
        in_specs=[
            pl.BlockSpec((tile, ch), lambda bb, n: (bb * nt + n, 0)),
            pl.BlockSpec((CONV_HALO, ch), lambda bb, n: (jnp.maximum((bb * nt + n) * per - 1, 0), 0)),
            pl.BlockSpec((CONV_WIDTH, ch), lambda bb, n: (0, 0)),
            pl.BlockSpec((1, ch), lambda bb, n: (0, 0)),
        ],
        out_specs=pl.BlockSpec((tile, ch), lambda bb, n: (bb * nt + n, 0)),
        out_shape=jax.ShapeDtypeStruct((bsz * slen, ch), F32),
        scratch_shapes=[pltpu.VMEM((SUBLANES, tile + CONV_HALO, ch), F32)],
        compiler_params=_cparams(("arbitrary", "arbitrary")),
        name="conv_prompt",
    )(u, u, w, b.reshape(1, ch))


def _conv_sample_kernel(sc_ref, un_ref, w_ref, b_ref, y_ref, next_ref):
    nctx = sc_ref.shape[0]
    qlen = un_ref.shape[0]
    ch = w_ref.shape[1]

    def row(t):
        return sc_ref[t] if t < nctx else un_ref[t - nctx]

    for i in range(qlen):
        acc = jnp.zeros((sc_ref.shape[1], ch), F32)
        for j in range(CONV_WIDTH):
            acc = acc + row(i + j) * w_ref[j:j + 1, :]
        y_ref[i] = acc + b_ref[...]
    for t in range(nctx):
        next_ref[t] = row(t + qlen)


def _conv_sample(state_t, u_new_t, w, b, seq_per_step=16):
    nctx, bsz, ch = state_t.shape
    qlen = u_new_t.shape[0]
    seq = lambda i: (0, i, 0)
    return pl.pallas_call(
        _conv_sample_kernel,
        grid=(bsz // seq_per_step,),
        in_specs=[
            pl.BlockSpec((nctx, seq_per_step, ch), seq),
            pl.BlockSpec((qlen, seq_per_step, ch), seq),
            pl.BlockSpec((CONV_WIDTH, ch), lambda i: (0, 0)),
            pl.BlockSpec((1, ch), lambda i: (0, 0)),
        ],
        out_specs=[pl.BlockSpec((qlen, seq_per_step, ch), seq), pl.BlockSpec((nctx, seq_per_step, ch), seq)],
        out_shape=[jax.ShapeDtypeStruct((qlen, bsz, ch), F32), jax.ShapeDtypeStruct((nctx, bsz, ch), F32)],
        compiler_params=_cparams(("arbitrary",)),
        name="conv_sample",
    )(state_t, u_new_t, w, b.reshape(1, ch))


def _merge_kernel(y_ref, ao_ref, ga_ref, gc_ref, x_ref, lnw_ref, lnb_ref, wco_ref, wo_ref, fw_ref, wr_ref,
                  h_ref, xn_ref, lg_ref):
    y = y_ref[...]
    mu = jnp.mean(y, axis=-1, keepdims=True)
    yc = y - mu
    var = jnp.mean(yc * yc, axis=-1, keepdims=True)
    yl = yc * lax.rsqrt(var + EPS) * lnw_ref[...] + lnb_ref[...]
    act = yl * jax.nn.sigmoid(yl)
    conv_o = jnp.dot(act.astype(BF16), wco_ref[...], preferred_element_type=F32)
    mix = ga_ref[...].astype(F32) * ao_ref[...].astype(F32) + gc_ref[...].astype(F32) * conv_o
    h = x_ref[...] + jnp.dot(mix.astype(BF16), wo_ref[...], preferred_element_type=F32)
    h_ref[...] = h
    ms = jnp.mean(h * h, axis=-1, keepdims=True)
    xn = (h * lax.rsqrt(ms + EPS) * fw_ref[...]).astype(BF16)
    xn_ref[...] = xn
    lg_ref[...] = jnp.dot(xn, wr_ref[...], preferred_element_type=F32)


def _merge(y, ao, gates, row0, x, ln_w, ln_b, wco, wo, fw, wr, tile=256):
    t, d = x.shape
    ch = y.shape[1]
    off = row0 // tile
    row = lambda i: (i, 0)
    const = lambda shape: pl.BlockSpec(shape, lambda i: (0, 0), pipeline_mode=pl.Buffered(1))
    return pl.pallas_call(
        _merge_kernel,
        grid=(t // tile,),
        in_specs=[
            pl.BlockSpec((tile, ch), row),
            pl.BlockSpec((tile, d), row),
            pl.BlockSpec((tile, d), lambda i: (i + off, 0)),
            pl.BlockSpec((tile, d), lambda i: (i + off, 1)),
            pl.BlockSpec((tile, d), row),
            const((1, ch)), const((1, ch)), const((ch, d)), const((d, d)), const((1, d)),
            const((d, LANES)),
        ],
        out_specs=[pl.BlockSpec((tile, d), row), pl.BlockSpec((tile, d), row), pl.BlockSpec((tile, LANES), row)],
        out_shape=[jax.ShapeDtypeStruct((t, d), F32), jax.ShapeDtypeStruct((t, d), BF16),
                   jax.ShapeDtypeStruct((t, LANES), F32)],
        compiler_params=_cparams(("arbitrary",)),
        name="merge",
    )(y, ao, gates, gates, x, ln_w.reshape(1, ch), ln_b.reshape(1, ch), wco, wo, fw.reshape(1, d), wr)


def _route_kernel(lp_ref, ls_ref, tri_ref, lower_ref, meta_ref, metat_ref, cnt_ref, *, n_p):
    i = pl.program_id(0)
    used = _round_up(N_EXPERTS + N_GROUPS, SUBLANES)

    def body(l_ref):
        lg = l_ref[...].T[:used]
        shape = lg.shape
        row = lax.broadcasted_iota(jnp.int32, shape, 0)
        big = jnp.int32(LANES)
        is_g = (row >= N_EXPERTS) & (row < N_EXPERTS + N_GROUPS)
        gl = jnp.where(is_g, lg, -jnp.inf)
        gmax = jnp.max(gl, axis=0, keepdims=True)
        g_star = jnp.min(jnp.where(gl == gmax, row - N_EXPERTS, big), axis=0, keepdims=True)
        pg_star = 1.0 / jnp.sum(jnp.exp(gl - gmax), axis=0, keepdims=True)
        in_g = (row < N_EXPERTS) & (_div_pow2(row, EXPERTS_PER_GROUP) == g_star)
        el = jnp.where(in_g, lg, -jnp.inf)
        v1 = jnp.max(el, axis=0, keepdims=True)
        i1 = jnp.min(jnp.where(el == v1, row, big), axis=0, keepdims=True)
        el2 = jnp.where(row == i1, -jnp.inf, el)
        v2 = jnp.max(el2, axis=0, keepdims=True)
        i2 = jnp.min(jnp.where(el2 == v2, row, big), axis=0, keepdims=True)
        e2 = jnp.exp(v2 - v1)
        w1 = pg_star * (1.0 / (1.0 + e2))
        w2 = pg_star * (e2 / (1.0 + e2))
        oh1 = row == i1
        oh2 = row == i2
        cnt = jnp.where(oh1 | oh2, 1.0, 0.0)
        before = jnp.dot(cnt.astype(BF16), tri_ref[...], preferred_element_type=F32)
        tot = jnp.sum(cnt, axis=-1, keepdims=True)
        runs = jnp.floor((tot + (CHUNK_ROWS - 1)) * (1.0 / CHUNK_ROWS))
        runs_b = jnp.broadcast_to(runs, (used, LANES)).astype(BF16)
        start = jnp.dot(lower_ref[...], runs_b, preferred_element_type=F32)[:, 0:1] * CHUNK_ROWS
        slot = before + start
        s1 = jnp.sum(jnp.where(oh1, slot, 0.0), axis=0, keepdims=True)
        s2 = jnp.sum(jnp.where(oh2, slot, 0.0), axis=0, keepdims=True)
        fields = [i1.astype(F32), i2.astype(F32), w1, w2, s1, s2]
        fields += [jnp.zeros_like(w1)] * (META_FIELDS - len(fields))
        metat = jnp.concatenate(fields, axis=0)
        metat_ref[...] = metat
        pad = jnp.zeros((LANES - META_FIELDS, shape[1]), F32)
        meta_ref[...] = jnp.concatenate([metat, pad], axis=0).T
        tot_pad = jnp.concatenate([tot, jnp.zeros((LANES - used, 1), F32)], axis=0)
        cnt_ref[...] = jnp.broadcast_to(tot_pad, (LANES, LANES)).T[0:1]

    @pl.when(i < n_p)
    def _():
        body(lp_ref)

    @pl.when(i >= n_p)
    def _():
        body(ls_ref)


def _route(lg_p, lg_s):
    tile = TOKEN_BLOCK
    tp, ts = lg_p.shape[0], lg_s.shape[0]
    n_p, n_s = tp // tile, ts // tile
    nb = n_p + n_s
    used = _round_up(N_EXPERTS + N_GROUPS, SUBLANES)
    tri = (jnp.arange(tile)[:, None] < jnp.arange(tile)[None, :]).astype(BF16)
    lower = (jnp.arange(used)[:, None] > jnp.arange(used)[None, :]).astype(BF16)
    return pl.pallas_call(
        functools.partial(_route_kernel, n_p=n_p),
        grid=(nb,),
        in_specs=[
            pl.BlockSpec((tile, LANES), lambda i: (jnp.minimum(i, n_p - 1), 0)),
            pl.BlockSpec((tile, LANES), lambda i: (jnp.maximum(i - n_p, 0), 0)),
            pl.BlockSpec((tile, tile), lambda i: (0, 0)),
            pl.BlockSpec((used, used), lambda i: (0, 0)),
        ],
        out_specs=[pl.BlockSpec((tile, LANES), lambda i: (i, 0)),
                   pl.BlockSpec((None, META_FIELDS, tile), lambda i: (i, 0, 0)),
                   pl.BlockSpec((None, 1, LANES), lambda i: (i, 0, 0))],
        out_shape=[jax.ShapeDtypeStruct((tp + ts, LANES), F32),
                   jax.ShapeDtypeStruct((nb, META_FIELDS, tile), F32),
                   jax.ShapeDtypeStruct((nb, 1, LANES), F32)],
        compiler_params=_cparams(("arbitrary",)),
        name="route",
    )(lg_p, lg_s, tri, lower)


def _plan_sizes(n_tok):
    nb = n_tok // TOKEN_BLOCK
    pad_per_block = N_EXPERTS * (CHUNK_ROWS - 1)
    max_chunks = _round_up(-(-(TOKEN_BLOCK * TOP_K + pad_per_block) // CHUNK_ROWS), SUBLANES)
    n_rows = _round_up(n_tok * TOP_K + nb * pad_per_block + N_EXPERTS * (ROW_PAD - 1), ROW_PAD)
    n_items = (n_rows + N_EXPERTS * (EXPERT_ROWS - ROW_PAD)) // EXPERT_ROWS
    max_zero = N_EXPERTS * (ROW_PAD // CHUNK_ROWS)
    return nb, max_chunks, n_rows, n_items, max_zero


def _dispatch_plan(counts, max_chunks, n_items, max_zero):
    i32 = jnp.int32
    c = counts[:, 0, :N_EXPERTS].astype(i32)
    run = (c + CHUNK_ROWS - 1) // CHUNK_ROWS
    run_end = jnp.cumsum(run, axis=1)
    src_start = run_end - run
    nchunk = run_end[:, -1]
    seg = jnp.sum(run, axis=0) * CHUNK_ROWS
    padded = (seg + ROW_PAD - 1) // ROW_PAD * ROW_PAD
    pad_end = jnp.cumsum(padded)
    pad_start = pad_end - padded
    dst_start = pad_start[None, :] // CHUNK_ROWS + (jnp.cumsum(run, axis=0) - run)
    jj = jnp.arange(max_chunks, dtype=i32)[None, :, None]
    in_run = (src_start[:, None, :] <= jj) & (jj < run_end[:, None, :])
    chunk_dst = jnp.sum(jnp.where(in_run, (dst_start - src_start)[:, None, :] + jj, 0), axis=-1) * CHUNK_ROWS
    tail = (padded - seg) // CHUNK_ROWS
    tail_end = jnp.cumsum(tail)
    tail_start = tail_end - tail
    z = jnp.arange(max_zero, dtype=i32)[:, None]
    in_tail = (tail_start[None, :] <= z) & (z < tail_end[None, :])
    zero_base = (pad_start + seg) // CHUNK_ROWS - tail_start
    zero_dst = jnp.sum(jnp.where(in_tail, zero_base[None, :] + z, 0), axis=-1) * CHUNK_ROWS
    nzero = tail_end[-1]
    items_e = (padded + EXPERT_ROWS - 1) // EXPERT_ROWS
    item_end = jnp.cumsum(items_e)
    item_start = item_end - items_e
    w = jnp.arange(n_items, dtype=i32)[:, None]
    in_item = (item_start[None, :] <= w) & (w < item_end[None, :])
    k = w - item_start[None, :]
    e_ids = jnp.arange(N_EXPERTS, dtype=i32)
    last_e = jnp.max(jnp.where(items_e > 0, e_ids, 0))
    item_e = jnp.where(w[:, 0] < item_end[-1], jnp.sum(jnp.where(in_item, e_ids[None, :], 0), axis=-1), last_e)
    item_row0 = jnp.sum(jnp.where(in_item, pad_start[None, :] + k * EXPERT_ROWS, 0), axis=-1)
    item_rows = jnp.sum(jnp.where(in_item, jnp.clip(padded[None, :] - k * EXPERT_ROWS, 0, EXPERT_ROWS), 0), axis=-1)
    item_nsub = item_rows // ROW_PAD
    return (chunk_dst.reshape(-1).astype(i32), nchunk.astype(i32), zero_dst.astype(i32),
            nzero.reshape(1).astype(i32), item_e.astype(i32), item_row0.astype(i32), item_nsub.astype(i32))


SCATTER_GROUPS = 2


def _scatter_kernel(cd_ref, nc_ref, zd_ref, nz_ref, mt_ref, xp_ref, xs_ref, out_hbm, loc_ref, zero_ref, sem,
                    zero_sem, *, n_p, max_chunks):
    i = pl.program_id(0)
    buf = i % 2
    n_slots = loc_ref.shape[1]
    d = xp_ref.shape[1]
    mt = mt_ref[...]
    w1, w2 = mt[2:3, :], mt[3:4, :]
    s1, s2 = mt[4:5, :].astype(jnp.int32), mt[5:6, :].astype(jnp.int32)
    half = d // 2
    group = n_slots // SCATTER_GROUPS

    def sort_rows(x_ref):
        for s0 in range(0, n_slots, group):
            slot = lax.broadcasted_iota(jnp.int32, (group, mt.shape[1]), 0) + s0
            p1 = slot == s1
            p2 = slot == s2
            perm = jnp.where(p1 | p2, 1.0, 0.0).astype(BF16)
            wcol = jnp.sum(jnp.where(p1, w1, 0.0) + jnp.where(p2, w2, 0.0), axis=-1, keepdims=True)
            loc_ref[buf, s0:s0 + group, half:] = pltpu.bitcast(jnp.broadcast_to(wcol, (group, LANES)), jnp.uint32)
            rows = jnp.dot(perm, x_ref[...], preferred_element_type=F32)
            loc_ref[buf, s0:s0 + group, :half] = _pack_pairs(rows[:, :half], rows[:, half:])

    @pl.when(i < n_p)
    def _():
        sort_rows(xp_ref)

    @pl.when(i >= n_p)
    def _():
        sort_rows(xs_ref)

    def rows_at(r):
        return pl.ds(pl.multiple_of(r, CHUNK_ROWS), CHUNK_ROWS)

    def chunk_copy(blk, j):
        return pltpu.make_async_copy(loc_ref.at[blk % 2, rows_at(j * CHUNK_ROWS)],
                                     out_hbm.at[rows_at(cd_ref[blk * max_chunks + j])], sem.at[blk % 2])

    def zero_copy(z):
        return pltpu.make_async_copy(zero_ref, out_hbm.at[rows_at(zd_ref[z])], zero_sem)

    def loop(n, fn):
        def body(j, carry):
            fn(j)
            return carry
        lax.fori_loop(0, n, body, 0)

    @pl.when(i > 0)
    def _():
        loop(nc_ref[i - 1], lambda j: chunk_copy(i - 1, j).wait())

    loop(nc_ref[i], lambda j: chunk_copy(i, j).start())

    @pl.when(i == pl.num_programs(0) - 1)
    def _():
        zero_ref[...] = jnp.zeros_like(zero_ref)
        loop(nz_ref[0], lambda z: zero_copy(z).start())
        loop(nc_ref[i], lambda j: chunk_copy(i, j).wait())
        loop(nz_ref[0], lambda z: zero_copy(z).wait())


def _scatter_rows(xn_p, xn_s, metat, chunk_dst, nchunk, zero_dst, nzero, max_chunks, n_rows):
    tile = TOKEN_BLOCK
    tp, d = xn_p.shape
    ts = xn_s.shape[0]
    n_p, n_s = tp // tile, ts // tile
    n_slots = max_chunks * CHUNK_ROWS
    width = d // 2 + LANES
    u32 = jnp.uint32
    return pl.pallas_call(
        functools.partial(_scatter_kernel, n_p=n_p, max_chunks=max_chunks),
        grid_spec=pltpu.PrefetchScalarGridSpec(
            num_scalar_prefetch=4,
            grid=(n_p + n_s,),
            in_specs=[
                pl.BlockSpec((None, META_FIELDS, tile), lambda i, *_: (i, 0, 0)),
                pl.BlockSpec((tile, d), lambda i, *_: (jnp.minimum(i, n_p - 1), 0)),
                pl.BlockSpec((tile, d), lambda i, *_: (jnp.maximum(i - n_p, 0), 0)),
            ],
            out_specs=pl.BlockSpec(memory_space=pl.ANY),
            scratch_shapes=[pltpu.VMEM((2, n_slots, width), u32), pltpu.VMEM((CHUNK_ROWS, width), u32),
                            pltpu.SemaphoreType.DMA((2,)), pltpu.SemaphoreType.DMA(())],
        ),
        out_shape=jax.ShapeDtypeStruct((n_rows, width), u32),
        compiler_params=_cparams(("arbitrary",)),
        name="scatter_rows",
    )(chunk_dst, nchunk, zero_dst, nzero, metat, xn_p, xn_s)


def _expert_kernel(e_ref, row0_ref, nsub_ref, xs_hbm, wg_ref, wu_ref, wd_ref, out_hbm,
                   xf_ref, xb_ref, acc_ref, ob_ref, pend_ref, in_sem, out_sem, *, n_chunks):
    w = pl.program_id(0)
    c = pl.program_id(1)
    n_items = pl.num_programs(0)
    last_c = pl.num_programs(1) - 1
    nsub = nsub_ref[w]
    d = acc_ref.shape[1]
    half = d // 2
    slot = w % 2

    def sub_rows(s):
        return pl.ds(pl.multiple_of(s * ROW_PAD, ROW_PAD), ROW_PAD)

    def hbm_rows(item, s):
        return pl.ds(pl.multiple_of(row0_ref[item] + s * ROW_PAD, ROW_PAD), ROW_PAD)

    def in_copy(item, s):
        return pltpu.make_async_copy(xs_hbm.at[hbm_rows(item, s)], xf_ref.at[item % 2, sub_rows(s)],
                                     in_sem.at[item % 2])

    def out_copy(item, s):
        return pltpu.make_async_copy(ob_ref.at[sub_rows(s)], out_hbm.at[hbm_rows(item, s)], out_sem)

    def loop(n, fn):
        def body(s, carry):
            fn(s)
            return carry
        lax.fori_loop(0, n, body, 0)

    @pl.when(c == 0)
    def _():
        @pl.when(w == 0)
        def _():
            pend_ref[0] = 0
            pend_ref[1] = 0
            loop(nsub, lambda s: in_copy(w, s).start())

        loop(nsub, lambda s: in_copy(w, s).wait())
        nxt = jnp.minimum(w + 1, n_items - 1)

        @pl.when(w + 1 < n_items)
        def _():
            loop(nsub_ref[nxt], lambda s: in_copy(nxt, s).start())

    def drain_out():
        prev = pend_ref[1]
        loop(pend_ref[0], lambda s: out_copy(prev, s).wait())
        pend_ref[0] = 0

    def ffn_chunk(x):
        g = jnp.dot(x, wg_ref[...].astype(BF16), preferred_element_type=F32)
        u = jnp.dot(x, wu_ref[...].astype(BF16), preferred_element_type=F32)
        hmid = (g * jax.nn.sigmoid(g) * u).astype(BF16)
        return jnp.dot(hmid, wd_ref[...].astype(BF16), preferred_element_type=F32)

    for n in range(1, EXPERT_ROWS // ROW_PAD + 1):
        m = n * ROW_PAD

        @pl.when((nsub == n) & (c == 0))
        def _(m=m):
            xa, xb = _unpack_pairs(xf_ref[slot, :m, :half])
            x = jnp.concatenate([xa, xb], axis=1).astype(BF16)
            xb_ref[:m, :] = x
            acc_ref[:m, :] = ffn_chunk(x)

        if n_chunks > 2:
            @pl.when((nsub == n) & (c > 0) & (c < last_c))
            def _(m=m):
                acc_ref[:m, :] += ffn_chunk(xb_ref[:m, :])

        @pl.when((nsub == n) & (c == last_c))
        def _(m=m):
            drain_out()
            mix_w = pltpu.bitcast(xf_ref[slot, :m, half:half + 1], F32)
            y = ((acc_ref[:m, :] + ffn_chunk(xb_ref[:m, :])) * mix_w).astype(BF16).astype(F32)
            ob_ref[:m, :] = _pack_pairs(y[:, :half], y[:, half:])
            loop(nsub, lambda s: out_copy(w, s).start())
            pend_ref[0] = nsub
            pend_ref[1] = w

    @pl.when((w == n_items - 1) & (c == last_c))
    def _():
        drain_out()


def _experts(xs, item_e, item_row0, item_nsub, w_gate, w_up, w_down):
    n_rows, width = xs.shape
    d = w_gate.shape[1]
    assert width == d // 2 + LANES
    d_exp = w_gate.shape[2]
    nc = d_exp // D_CHUNK
    assert nc >= 2
    n_items = item_e.shape[0]

    def chunk_of(w, c, nsub_ref):
        return jnp.where(nsub_ref[w] > 0, c, nc - 1)

    return pl.pallas_call(
        functools.partial(_expert_kernel, n_chunks=nc),
        grid_spec=pltpu.PrefetchScalarGridSpec(
            num_scalar_prefetch=3,
            grid=(n_items, nc),
            in_specs=[
                pl.BlockSpec(memory_space=pl.ANY),
                pl.BlockSpec((None, d, D_CHUNK), lambda w, c, e, r, n: (e[w], 0, chunk_of(w, c, n))),
                pl.BlockSpec((None, d, D_CHUNK), lambda w, c, e, r, n: (e[w], 0, chunk_of(w, c, n))),
                pl.BlockSpec((None, D_CHUNK, d), lambda w, c, e, r, n: (e[w], chunk_of(w, c, n), 0)),
            ],
            out_specs=pl.BlockSpec(memory_space=pl.ANY),
            scratch_shapes=[
                pltpu.VMEM((2, EXPERT_ROWS, width), jnp.uint32),
                pltpu.VMEM((EXPERT_ROWS, d), BF16),
                pltpu.VMEM((EXPERT_ROWS, d), F32),
                pltpu.VMEM((EXPERT_ROWS, d // 2), jnp.uint32),
                pltpu.SMEM((2,), jnp.int32),
                pltpu.SemaphoreType.DMA((2,)),
                pltpu.SemaphoreType.DMA(()),
            ],
        ),
        out_shape=jax.ShapeDtypeStruct((n_rows, d // 2), jnp.uint32),
        compiler_params=_cparams(("arbitrary", "arbitrary")),
        name="experts",
    )(item_e, item_row0, item_nsub, xs, w_gate, w_up, w_down)


COMBINE_SPLIT = 2
SLOT_STEP = 256


def _combine_kernel(cd_ref, nc_ref, meta_ref, hp_ref, hs_ref, yb_hbm, yp_ref, ys_ref, loc_ref, sem,
                    *, n_p, max_chunks):
    i = pl.program_id(0)
    j = pl.program_id(1)
    nb = pl.num_programs(0)
    buf = i % 2
    n_slots, half = loc_ref.shape[1], loc_ref.shape[2]
    d = 2 * half

    def rows_at(r):
        return pl.ds(pl.multiple_of(r, CHUNK_ROWS), CHUNK_ROWS)

    def chunk_copy(blk, c):
        return pltpu.make_async_copy(yb_hbm.at[rows_at(cd_ref[blk * max_chunks + c])],
                                     loc_ref.at[blk % 2, rows_at(c * CHUNK_ROWS)], sem.at[blk % 2])

    def fetch(blk):
        n = nc_ref[blk]

        def start(c, carry):
            chunk_copy(blk, c).start()
            return carry
        lax.fori_loop(0, n, start, 0)

        def clear(c, carry):
            loc_ref[blk % 2, rows_at(c * CHUNK_ROWS), :] = jnp.zeros((CHUNK_ROWS, half), jnp.uint32)
            return carry
        lax.fori_loop(n, max_chunks, clear, 0)

    @pl.when(j == 0)
    def _():
        @pl.when(i == 0)
        def _():
            fetch(i)

        def wait(c, carry):
            chunk_copy(i, c).wait()
            return carry
        lax.fori_loop(0, nc_ref[i], wait, 0)

        @pl.when(i + 1 < nb)
        def _():
            fetch(jnp.minimum(i + 1, nb - 1))

    meta = meta_ref[...]
    s1 = meta[:, 4:5].astype(jnp.int32)
    s2 = meta[:, 5:6].astype(jnp.int32)
    rows = meta.shape[0]
    f = jnp.zeros((rows, d), F32)
    for r0 in range(0, n_slots, SLOT_STEP):
        slot = lax.broadcasted_iota(jnp.int32, (rows, SLOT_STEP), 1) + r0
        pt = jnp.where((slot == s1) | (slot == s2), 1.0, 0.0).astype(BF16)
        ya, yb = _unpack_pairs(loc_ref[buf, r0:r0 + SLOT_STEP, :])
        f = f + jnp.dot(pt, jnp.concatenate([ya, yb], axis=1).astype(BF16), preferred_element_type=F32)
    g = i * pl.num_programs(1) + j

    @pl.when(g < n_p)
    def _():
        yp_ref[...] = hp_ref[...] + f

    @pl.when(g >= n_p)
    def _():
        ys_ref[...] = hs_ref[...] + f


def _combine(h_p, h_s, meta, chunk_dst, nchunk, yb, max_chunks):
    tp, d = h_p.shape
    ts = h_s.shape[0]
    tile = TOKEN_BLOCK // COMBINE_SPLIT
    n_p, n_s = tp // tile, ts // tile
    nb = (tp + ts) // TOKEN_BLOCK
    n_slots = max_chunks * CHUNK_ROWS
    assert n_slots % SLOT_STEP == 0
    blk = lambda i, j: i * COMBINE_SPLIT + j
    p_idx = lambda i, j, *_: (jnp.minimum(blk(i, j), n_p - 1), 0)
    s_idx = lambda i, j, *_: (jnp.maximum(blk(i, j) - n_p, 0), 0)
    return pl.pallas_call(
        functools.partial(_combine_kernel, n_p=n_p, max_chunks=max_chunks),
        grid_spec=pltpu.PrefetchScalarGridSpec(
            num_scalar_prefetch=2,
            grid=(nb, COMBINE_SPLIT),
            in_specs=[
                pl.BlockSpec((tile, LANES), lambda i, j, *_: (blk(i, j), 0)),
                pl.BlockSpec((tile, d), p_idx),
                pl.BlockSpec((tile, d), s_idx),
                pl.BlockSpec(memory_space=pl.ANY),
            ],
            out_specs=[pl.BlockSpec((tile, d), p_idx), pl.BlockSpec((tile, d), s_idx)],
            scratch_shapes=[pltpu.VMEM((2, n_slots, d // 2), jnp.uint32), pltpu.SemaphoreType.DMA((2,))],
        ),
        out_shape=[jax.ShapeDtypeStruct((tp, d), F32), jax.ShapeDtypeStruct((ts, d), F32)],
        compiler_params=_cparams(("arbitrary", "arbitrary")),
        name="combine",
    )(chunk_dst, nchunk, meta, h_p, h_s, yb)


def _layer(x_p, x_s, ck, cv, sc, attn_norm_w, w_in, b_gate, q_norm_w, k_norm_w, sinks, conv_dw_w, conv_dw_b,
           conv_ln_w, conv_ln_b, w_conv_out, w_out, ffn_norm_w, router_group, router_expert, w_gate, w_up,
           w_down):
    bsz, slen, d = x_p.shape
    dbs, dlen, _ = x_s.shape
    tp, ts = bsz * slen, dbs * dlen
    conv_ch = conv_dw_w.shape[1]
    kvw = N_KV_HEADS * HEAD_DIM
    xp2, xs2 = x_p.reshape(tp, d), x_s.reshape(ts, d)

    xn = _prenorm(xp2, xs2, attn_norm_w)
    q, k, v, u, gates = _project(xn, w_in, b_gate, q_norm_w, k_norm_w, d, conv_ch)

    ao_p = _attn_prompt(q, k, v, sinks, bsz, slen)
    ao_s, ck_next, cv_next = _attn_sample(q[tp:], ck, k[tp:], cv, v[tp:], sinks)

    y_p = _conv_prompt(u, conv_dw_w, conv_dw_b, bsz, slen)
    sc_t = sc.transpose(1, 0, 2)
    u_s_t = u[tp:].reshape(dbs, dlen, conv_ch).transpose(1, 0, 2)
    y_s_t, sc_next_t = _conv_sample(sc_t, u_s_t, conv_dw_w, conv_dw_b)
    y_s = y_s_t.transpose(1, 0, 2).reshape(ts, conv_ch)

    wco = w_conv_out.astype(BF16)
    wo = w_out.astype(BF16)
    wr = jnp.concatenate([router_expert, router_group,
                          jnp.zeros((d, LANES - N_EXPERTS - N_GROUPS), router_expert.dtype)], axis=1).astype(BF16)
    margs = (conv_ln_w, conv_ln_b, wco, wo, ffn_norm_w, wr)
    h_p, xn_p, lg_p = _merge(y_p, ao_p, gates, 0, xp2, *margs)
    h_s, xn_s, lg_s = _merge(y_s, ao_s, gates, tp, xs2, *margs)

    meta, metat, counts = _route(lg_p, lg_s)
    _, max_chunks, n_rows, n_items, max_zero = _plan_sizes(tp + ts)
    chunk_dst, nchunk, zero_dst, nzero, item_e, item_row0, item_nsub = _dispatch_plan(
        counts, max_chunks, n_items, max_zero)
    xs_rows = _scatter_rows(xn_p, xn_s, metat, chunk_dst, nchunk, zero_dst, nzero, max_chunks, n_rows)
    yb = _experts(xs_rows, item_e, item_row0, item_nsub, w_gate, w_up, w_down)
    out_p, out_s = _combine(h_p, h_s, meta, chunk_dst, nchunk, yb, max_chunks)

    def seq_tails(a, n):
        return jnp.stack([lax.slice_in_dim(a, (b + 1) * slen - n, (b + 1) * slen, axis=0) for b in range(bsz)])

    new_p = (seq_tails(k, WINDOW).reshape(bsz, WINDOW, N_KV_HEADS, HEAD_DIM),
             seq_tails(v, WINDOW).reshape(bsz, WINDOW, N_KV_HEADS, HEAD_DIM),
             seq_tails(u, CONV_WIDTH - 1))
    new_s = (ck_next, cv_next, sc_next_t.transpose(1, 0, 2))
    return out_p.reshape(bsz, slen, d), out_s.reshape(dbs, dlen, d), new_p, new_s


def kernel(x_prompt, x_sample, cache_k, cache_v, state_conv, attn_norm_w, w_in, b_gate, q_norm_w, k_norm_w,
           sinks, conv_dw_w, conv_dw_b, conv_ln_w, conv_ln_b, w_conv_out, w_out, ffn_norm_w, router_group,
           router_expert, w_gate, w_up, w_down):
    x_p, x_s = x_prompt, x_sample
    kp, vp, cp, ks_, vs_, cs_ = [], [], [], [], [], []
    for l in range(cache_k.shape[0]):
        x_p, x_s, new_p, new_s = _layer(
            x_p, x_s, cache_k[l], cache_v[l], state_conv[l], attn_norm_w[l], w_in[l], b_gate[l], q_norm_w[l],
            k_norm_w[l], sinks[l], conv_dw_w[l], conv_dw_b[l], conv_ln_w[l], conv_ln_b[l], w_conv_out[l],
            w_out[l], ffn_norm_w[l], router_group[l], router_expert[l], w_gate[l], w_up[l], w_down[l])
        kp.append(new_p[0]); vp.append(new_p[1]); cp.append(new_p[2])
        ks_.append(new_s[0]); vs_.append(new_s[1]); cs_.append(new_s[2])
    return (x_p, x_s, jnp.stack(kp), jnp.stack(vp), jnp.stack(cp),
            jnp.stack(ks_), jnp.stack(vs_), jnp.stack(cs_))
```

```python
import functools

import jax
import jax.numpy as jnp
from jax import lax
from jax.experimental import pallas as pl
from jax.experimental.pallas import tpu as pltpu

F32 = jnp.float32
BF16 = jnp.bfloat16
EPS = 1e-6
NEG_INF = -1e30

HEAD_DIM = 64
N_KV_HEADS = 4
WINDOW = 128
CONV_WIDTH = 31
N_GROUPS = 4
EXPERTS_PER_GROUP = 8
N_EXPERTS = N_GROUPS * EXPERTS_PER_GROUP
TOP_K = 2

LANES = 128
SUBLANES = 8
MXU_DIM = 256
VMEM_LIMIT = 56 * 1024 * 1024

TOKEN_BLOCK = 512
CHUNK_ROWS = SUBLANES
ROW_PAD = 128
EXPERT_ROWS = 768
D_CHUNK = 512
META_FIELDS = 8


def _cparams(sem):
    return pltpu.CompilerParams(dimension_semantics=sem, vmem_limit_bytes=VMEM_LIMIT)


def _log2(n):
    assert n > 0 and n & (n - 1) == 0, n
    return n.bit_length() - 1


def _div_pow2(x, n):
    return lax.shift_right_logical(x, jnp.int32(_log2(n)))


def _mod_pow2(x, n):
    _log2(n)
    return x & jnp.int32(n - 1)


def _round_up(x, m):
    return (x + m - 1) // m * m


HIGH_HALF = 0xFFFF0000


def _pack_pairs(a, b):
    ua = pltpu.bitcast(a, jnp.uint32) & jnp.uint32(HIGH_HALF)
    return ua | (pltpu.bitcast(b, jnp.uint32) >> 16)


def _unpack_pairs(w):
    return (pltpu.bitcast(w & jnp.uint32(HIGH_HALF), F32), pltpu.bitcast(w << 16, F32))


def _cast_stream_step(k, n_slabs, src_hbm, dst_hbm, in_ref, out_ref, in_sem, out_sem):
    rows = in_ref.shape[1]

    def in_copy(j):
        return pltpu.make_async_copy(src_hbm.at[pl.ds(pl.multiple_of(j * rows, rows), rows)], in_ref.at[j % 2],
                                     in_sem.at[j % 2])

    def out_copy(j):
        return pltpu.make_async_copy(out_ref.at[j % 2], dst_hbm.at[pl.ds(pl.multiple_of(j * rows, rows), rows)],
                                     out_sem.at[j % 2])

    @pl.when(k == 0)
    def _():
        in_copy(k).start()

    @pl.when(k + 1 < n_slabs)
    def _():
        in_copy(k + 1).start()

    in_copy(k).wait()

    @pl.when(k >= 2)
    def _():
        out_copy(k - 2).wait()

    out_ref[k % 2] = in_ref[k % 2].astype(out_ref.dtype)
    out_copy(k).start()

    @pl.when(k == n_slabs - 1)
    def _():
        @pl.when(n_slabs >= 2)
        def _():
            out_copy(k - 1).wait()
        out_copy(k).wait()


def _cast_stream_scratch(rows, cols, n_slabs, src_dtype, dst_dtype):
    assert rows % n_slabs == 0 and (rows // n_slabs) % (2 * SUBLANES) == 0, (rows, n_slabs)
    slab = rows // n_slabs
    return [pltpu.VMEM((2, slab, cols), src_dtype), pltpu.VMEM((2, slab, cols), dst_dtype),
            pltpu.SemaphoreType.DMA((2,)), pltpu.SemaphoreType.DMA((2,))]


def _prenorm_kernel(xp_ref, xs_ref, w_ref, o_ref, *, n_p):
    i = pl.program_id(0)

    def body(x_ref):
        x = x_ref[...]
        ms = jnp.mean(x * x, axis=-1, keepdims=True)
        o_ref[...] = (x * lax.rsqrt(ms + EPS) * w_ref[...]).astype(o_ref.dtype)

    @pl.when(i < n_p)
    def _():
        body(xp_ref)

    @pl.when(i >= n_p)
    def _():
        body(xs_ref)


def _prenorm(xp, xs, w, tile=512):
    tp, d = xp.shape
    ts = xs.shape[0]
    n_p, n_s = tp // tile, ts // tile
    return pl.pallas_call(
        functools.partial(_prenorm_kernel, n_p=n_p),
        grid=(n_p + n_s,),
        in_specs=[
            pl.BlockSpec((tile, d), lambda i: (jnp.minimum(i, n_p - 1), 0)),
            pl.BlockSpec((tile, d), lambda i: (jnp.maximum(i - n_p, 0), 0)),
            pl.BlockSpec((1, d), lambda i: (0, 0)),
        ],
        out_specs=pl.BlockSpec((tile, d), lambda i: (i, 0)),
        out_shape=jax.ShapeDtypeStruct((tp + ts, d), BF16),
        compiler_params=_cparams(("arbitrary",)),
        name="prenorm",
    )(xp, xs, w.reshape(1, d))


def _head_rms(z, hw, bd_ref):
    ss = z * z
    hi = ss.astype(BF16)
    lo = (ss - hi.astype(F32)).astype(BF16)
    bd = bd_ref[...]
    tot = jnp.dot(hi, bd, preferred_element_type=F32) + jnp.dot(lo, bd, preferred_element_type=F32)
    return z * lax.rsqrt(tot * (1.0 / HEAD_DIM) + EPS) * hw


def _load_w(i, w_ref, wb_ref):
    @pl.when(i == 0)
    def _():
        wb_ref[...] = w_ref[...].astype(BF16)


PROJ_SUB_NORM = 544
PROJ_SUB_ACT = 272


def _row_groups(n_rows, sub):
    sub = sub if n_rows % sub == 0 else n_rows
    return [slice(r0, r0 + sub) for r0 in range(0, n_rows, sub)]


def _proj_q_kernel(x_ref, w_ref, hw_ref, bd_ref, o_ref, wb_ref):
    _load_w(pl.program_id(1), w_ref, wb_ref)
    for rs in _row_groups(x_ref.shape[0], PROJ_SUB_NORM):
        z = jnp.dot(x_ref[rs, :], wb_ref[...], preferred_element_type=F32)
        for c in range(z.shape[1] // MXU_DIM):
            sl = slice(c * MXU_DIM, (c + 1) * MXU_DIM)
            o_ref[rs, sl] = _head_rms(z[:, sl], hw_ref[...], bd_ref).astype(o_ref.dtype)


def _proj_kv_kernel(x_ref, w_ref, hw_ref, bd_ref, k_ref, v_ref, wb_ref):
    _load_w(pl.program_id(1), w_ref, wb_ref)
    kw = k_ref.shape[1]
    for rs in _row_groups(x_ref.shape[0], PROJ_SUB_NORM):
        z = jnp.dot(x_ref[rs, :], wb_ref[...], preferred_element_type=F32)
        k_ref[rs, :] = _head_rms(z[:, :kw], hw_ref[...], bd_ref)
        v_ref[rs, :] = z[:, kw:]


def _proj_glu_kernel(x_ref, wa_ref, wg_ref, o_ref, wab_ref, wgb_ref):
    i = pl.program_id(1)
    _load_w(i, wa_ref, wab_ref)
    _load_w(i, wg_ref, wgb_ref)
    for rs in _row_groups(x_ref.shape[0], PROJ_SUB_ACT):
        x = x_ref[rs, :]
        a = jnp.dot(x, wab_ref[...], preferred_element_type=F32)
        g = jnp.dot(x, wgb_ref[...], preferred_element_type=F32)
        o_ref[rs, :] = a * jax.nn.sigmoid(g)


def _proj_gate_kernel(x_ref, w_ref, b_ref, o_ref, wb_ref):
    _load_w(pl.program_id(1), w_ref, wb_ref)
    for rs in _row_groups(x_ref.shape[0], PROJ_SUB_ACT):
        z = jnp.dot(x_ref[rs, :], wb_ref[...], preferred_element_type=F32)
        o_ref[rs, :] = jax.nn.sigmoid(z + b_ref[...]).astype(o_ref.dtype)


def _row_tile(t, cap=1152):
    best = 16
    for m in range(16, cap + 1, 16):
        if t % m == 0:
            best = m
    return best


def _project(xn, w_in, b_gate, q_norm_w, k_norm_w, d_model, conv_ch):
    t, d = xn.shape
    q_w = d_model
    kv_w = N_KV_HEADS * HEAD_DIM
    tn = 2 * kv_w
    tm = _row_tile(t)
    ni = t // tm
    tm_big = _row_tile(t, cap=2304)
    ni_big = t // tm_big
    reps = MXU_DIM // HEAD_DIM
    hq = jnp.tile(q_norm_w.astype(F32) * HEAD_DIM ** -0.5, reps).reshape(1, MXU_DIM)
    hk = jnp.tile(k_norm_w.astype(F32), reps).reshape(1, MXU_DIM)
    gid = jnp.arange(MXU_DIM) // HEAD_DIM
    bd = (gid[:, None] == gid[None, :]).astype(BF16)

    x_spec = pl.BlockSpec((tm, d), lambda s, i: (i, 0))
    x_big = pl.BlockSpec((tm_big, d), lambda s, i: (i, 0))
    small = lambda shape: pl.BlockSpec(shape, lambda s, i: (0, 0))

    def w_spec(c0):
        return pl.BlockSpec((d, tn), lambda s, i: (0, c0 + s))

    wscr = pltpu.VMEM((d, tn), BF16)
    cp = _cparams(("arbitrary", "arbitrary"))

    q = pl.pallas_call(
        _proj_q_kernel, grid=(q_w // tn, ni_big),
        in_specs=[x_big, w_spec(0), small((1, MXU_DIM)), small((MXU_DIM, MXU_DIM))],
        out_specs=pl.BlockSpec((tm_big, tn), lambda s, i: (i, s)),
        out_shape=jax.ShapeDtypeStruct((t, q_w), BF16),
        scratch_shapes=[wscr], compiler_params=cp, name="proj_q",
    )(xn, w_in, hq, bd)

    c_kv = q_w // tn
    k, v = pl.pallas_call(
        _proj_kv_kernel, grid=(1, ni),
        in_specs=[x_spec, w_spec(c_kv), small((1, MXU_DIM)), small((MXU_DIM, MXU_DIM))],
        out_specs=[pl.BlockSpec((tm, kv_w), lambda s, i: (i, 0))] * 2,
        out_shape=[jax.ShapeDtypeStruct((t, kv_w), F32)] * 2,
        scratch_shapes=[wscr], compiler_params=cp, name="proj_kv",
    )(xn, w_in, hk, bd)

    c_a = c_kv + 1
    n_glu = conv_ch // tn
    u = pl.pallas_call(
        _proj_glu_kernel, grid=(n_glu, ni),
        in_specs=[x_spec, w_spec(c_a), w_spec(c_a + n_glu)],
        out_specs=pl.BlockSpec((tm, tn), lambda s, i: (i, s)),
        out_shape=jax.ShapeDtypeStruct((t, conv_ch), F32),
        scratch_shapes=[wscr, wscr], compiler_params=cp, name="proj_glu",
    )(xn, w_in, w_in)

    c_g = c_a + 2 * n_glu
    n_gate = 2 * d_model // tn
    gates = pl.pallas_call(
        _proj_gate_kernel, grid=(n_gate, ni_big),
        in_specs=[x_big, w_spec(c_g), pl.BlockSpec((1, tn), lambda s, i: (0, s))],
        out_specs=pl.BlockSpec((tm_big, tn), lambda s, i: (i, s)),
        out_shape=jax.ShapeDtypeStruct((t, 2 * d_model), BF16),
        scratch_shapes=[wscr], compiler_params=cp, name="proj_gate",
    )(xn, w_in, b_gate.reshape(1, -1))
    return q, k, v, u, gates


ATTN_BLOCKS = 4


def _attn_prompt_kernel(sink_ref, q_ref, kc_ref, kp_ref, vc_ref, vp_ref, wsrc_hbm, o_ref, wdst_hbm,
                        kbuf_ref, vbuf_ref, *cast_scratch):
    blk = WINDOW
    n = pl.program_id(1)
    rows = 2 * blk
    kbuf_ref[:blk, :] = kp_ref[...]
    kbuf_ref[blk:, :] = kc_ref[...]
    vbuf_ref[:blk, :] = vp_ref[...]
    vbuf_ref[blk:, :] = vc_ref[...]
    r_iota = lax.broadcasted_iota(jnp.int32, (rows, rows), 0)
    c_iota = lax.broadcasted_iota(jnp.int32, (rows, rows), 1)
    qi = jnp.where(r_iota >= blk, r_iota - blk, r_iota)
    band = (c_iota >= qi) & (c_iota <= qi + WINDOW)
    lo_lane = lax.broadcasted_iota(jnp.int32, (blk, LANES), 1) < HEAD_DIM
    top_row = lax.broadcasted_iota(jnp.int32, (rows, 1), 0) < blk
    zero = jnp.zeros((blk, LANES), q_ref.dtype)
    group = q_ref.shape[1] // (N_KV_HEADS * LANES)

    n_blocks = q_ref.shape[0] // blk
    first_slab = (pl.program_id(0) * pl.num_programs(1) + n) * n_blocks
    n_slabs = pl.num_programs(0) * pl.num_programs(1) * n_blocks

    def query_block(sb, carry):
        _cast_stream_step(first_slab + sb, n_slabs, wsrc_hbm, wdst_hbm, *cast_scratch)
        r0 = pl.multiple_of(sb * blk, blk)
        q_rows = pl.ds(r0, blk)
        kv_rows = pl.ds(r0, rows)
        mask = band & ((c_iota >= blk) | (n > 0) | (sb > 0))
        for g in range(N_KV_HEADS):
            hs = slice(g * HEAD_DIM, (g + 1) * HEAD_DIM)
            kg = kbuf_ref[kv_rows, hs]
            vg = vbuf_ref[kv_rows, hs]
            kdup = jnp.concatenate([kg, kg], axis=1).astype(BF16)
            vdup = jnp.concatenate([vg, vg], axis=1).astype(BF16)
            for p in range(group):
                pair = g * group + p
                ls = slice(pair * LANES, (pair + 1) * LANES)
                qp = q_ref[q_rows, ls]
                lhs = jnp.concatenate([jnp.where(lo_lane, qp, zero), jnp.where(lo_lane, zero, qp)], axis=0)
                s = lax.dot_general(lhs, kdup, (((1,), (1,)), ((), ())), preferred_element_type=F32)
                s = jnp.where(mask, s, NEG_INF)
                sink = jnp.where(top_row, sink_ref[2 * pair], sink_ref[2 * pair + 1])
                m = jnp.maximum(jnp.max(s, axis=-1, keepdims=True), sink)
                e = jnp.exp(s - m)
                denom = jnp.sum(e, axis=-1, keepdims=True) + jnp.exp(sink - m)
                o2 = jnp.dot(e.astype(BF16), vdup, preferred_element_type=F32) / denom
                o_ref[q_rows, ls] = jnp.where(lo_lane, o2[:blk], o2[blk:]).astype(o_ref.dtype)
        return carry

    lax.fori_loop(0, q_ref.shape[0] // blk, query_block, 0)


def _attn_prompt(q, k, v, sinks, bsz, slen, w_side):
    blk = WINDOW
    step = ATTN_BLOCKS * blk
    ns = slen // step
    qw, kw = q.shape[1], k.shape[1]
    cur = lambda b, n: (b * ns + n, 0)
    prev = lambda b, n: (jnp.maximum((b * ns + n) * ATTN_BLOCKS - 1, 0), 0)
    return pl.pallas_call(
        _attn_prompt_kernel,
        grid=(bsz, ns),
        in_specs=[
            pl.BlockSpec(memory_space=pltpu.SMEM),
            pl.BlockSpec((step, qw), cur),
            pl.BlockSpec((step, kw), cur), pl.BlockSpec((blk, kw), prev),
            pl.BlockSpec((step, kw), cur), pl.BlockSpec((blk, kw), prev),
            pl.BlockSpec(memory_space=pl.ANY),
        ],
        out_specs=[pl.BlockSpec((step, qw), cur), pl.BlockSpec(memory_space=pl.ANY)],
        out_shape=[jax.ShapeDtypeStruct((bsz * slen, qw), BF16), jax.ShapeDtypeStruct(w_side.shape, BF16)],
        scratch_shapes=[pltpu.VMEM((blk + step, kw), F32), pltpu.VMEM((blk + step, kw), F32)]
        + _cast_stream_scratch(w_side.shape[0], w_side.shape[1], bsz * slen // blk, w_side.dtype, BF16),
        compiler_params=_cparams(("arbitrary", "arbitrary")),
        name="attn_prompt",
    )(sinks.astype(F32), q, k, k, v, v, w_side)


def _attn_sample_kernel(sink_ref, q_ref, kt_ref, kn_ref, vt_ref, vn_ref, o_ref, kt_out_ref, vt_out_ref, *, qlen):
    nseq, rows, _ = q_ref.shape
    kvw, ncache = kt_ref.shape[1], kt_ref.shape[2]
    per_kv = rows // N_KV_HEADS
    all_rows = nseq * rows
    r_lane = _div_pow2(_mod_pow2(lax.broadcasted_iota(jnp.int32, (all_rows, kvw), 0), rows), per_kv)
    c_lane = _div_pow2(lax.broadcasted_iota(jnp.int32, (all_rows, kvw), 1), HEAD_DIM)
    own = r_lane == c_lane
    mask_c = (lax.broadcasted_iota(jnp.int32, (all_rows, ncache), 1)
              >= _mod_pow2(lax.broadcasted_iota(jnp.int32, (all_rows, ncache), 0), qlen))
    mask_n = (lax.broadcasted_iota(jnp.int32, (all_rows, qlen), 1)
              <= _mod_pow2(lax.broadcasted_iota(jnp.int32, (all_rows, qlen), 0), qlen))
    keep_old = lax.broadcasted_iota(jnp.int32, (kvw, ncache), 1) < ncache - qlen
    sink = sink_ref[...]
    contract_last = (((1,), (1,)), ((), ()))
    seq_rows = lambda b: slice(b * rows, (b + 1) * rows)
    new_rows = lambda b: slice(b * qlen, (b + 1) * qlen)

    q_all = q_ref[...].reshape(all_rows, HEAD_DIM)
    qe = jnp.where(own, jnp.concatenate([q_all] * N_KV_HEADS, axis=1), jnp.zeros((), q_all.dtype))
    sc = jnp.concatenate([jnp.dot(qe[seq_rows(b)], kt_ref[b].astype(BF16), preferred_element_type=F32)
                          for b in range(nseq)], axis=0)
    sn = jnp.concatenate([lax.dot_general(qe[seq_rows(b)], kn_ref[new_rows(b), :].astype(BF16), contract_last,
                                          preferred_element_type=F32) for b in range(nseq)], axis=0)
    sc = jnp.where(mask_c, sc, NEG_INF)
    sn = jnp.where(mask_n, sn, NEG_INF)
    m = jnp.maximum(jnp.maximum(jnp.max(sc, axis=-1, keepdims=True), jnp.max(sn, axis=-1, keepdims=True)), sink)
    ec = jnp.exp(sc - m)
    en = jnp.exp(sn - m)
    denom = jnp.sum(ec, axis=-1, keepdims=True) + jnp.sum(en, axis=-1, keepdims=True) + jnp.exp(sink - m)
    ec = ec.astype(BF16)
    en = en.astype(BF16)
    o = jnp.concatenate(
        [lax.dot_general(ec[seq_rows(b)], vt_ref[b].astype(BF16), contract_last, preferred_element_type=F32)
         + jnp.dot(en[seq_rows(b)], vn_ref[new_rows(b), :].astype(BF16), preferred_element_type=F32)
         for b in range(nseq)], axis=0)
    o = jnp.where(own, o, 0.0)
    acc = o[:, :HEAD_DIM]
    for g in range(1, N_KV_HEADS):
        acc = acc + o[:, g * HEAD_DIM:(g + 1) * HEAD_DIM]
    o_ref[...] = (acc / denom).astype(o_ref.dtype).reshape(nseq, rows, HEAD_DIM)

    def feature_major(new_ref):
        pad = jnp.zeros((ncache - nseq * qlen, kvw), F32)
        return jnp.concatenate([new_ref[...], pad], axis=0).T

    def shift_in(old_ref, new_ref, out_ref):
        new_t = feature_major(new_ref)
        for b in range(nseq):
            placed = pltpu.roll(new_t, ncache - qlen - b * qlen, axis=1)
            out_ref[b] = jnp.where(keep_old, pltpu.roll(old_ref[b], ncache - qlen, axis=1), placed)

    shift_in(kt_ref, kn_ref, kt_out_ref)
    shift_in(vt_ref, vn_ref, vt_out_ref)


def _attn_sample(q, k_cache, k_new, v_cache, v_new, sinks, seq_per_step=8):
    bsz, ncache = k_cache.shape[:2]
    kvw = k_new.shape[1]
    qlen = k_new.shape[0] // bsz
    assert seq_per_step * qlen <= ncache
    n_heads = q.shape[1] // HEAD_DIM
    group = n_heads // N_KV_HEADS
    rows = n_heads * qlen
    qr = q.reshape(bsz, qlen, N_KV_HEADS, group, HEAD_DIM).transpose(0, 2, 3, 1, 4).reshape(bsz, rows, HEAD_DIM)
    sink_rows = jnp.tile(jnp.repeat(sinks.astype(F32), qlen), seq_per_step).reshape(seq_per_step * rows, 1)
    feature_major = lambda c: c.transpose(0, 2, 3, 1).reshape(bsz, kvw, ncache)
    seq = lambda i: (i, 0, 0)
    cache_spec = pl.BlockSpec((seq_per_step, kvw, ncache), seq)
    new_spec = pl.BlockSpec((seq_per_step * qlen, kvw), lambda i: (i, 0))
    o, kt_next, vt_next = pl.pallas_call(
        functools.partial(_attn_sample_kernel, qlen=qlen),
        grid=(bsz // seq_per_step,),
        in_specs=[
            pl.BlockSpec((seq_per_step * rows, 1), lambda i: (0, 0)),
            pl.BlockSpec((seq_per_step, rows, HEAD_DIM), seq),
            cache_spec, new_spec, cache_spec, new_spec,
        ],
        out_specs=[pl.BlockSpec((seq_per_step, rows, HEAD_DIM), seq), cache_spec, cache_spec],
        out_shape=[jax.ShapeDtypeStruct((bsz, rows, HEAD_DIM), BF16),
                   jax.ShapeDtypeStruct((bsz, kvw, ncache), F32), jax.ShapeDtypeStruct((bsz, kvw, ncache), F32)],
        compiler_params=_cparams(("arbitrary",)),
        name="attn_sample",
    )(sink_rows, qr, feature_major(k_cache), k_new, feature_major(v_cache), v_new)
    position_major = lambda t: t.reshape(bsz, N_KV_HEADS, HEAD_DIM, ncache).transpose(0, 3, 1, 2)
    ao = o.reshape(bsz, N_KV_HEADS, group, qlen, HEAD_DIM).transpose(0, 3, 1, 2, 4).reshape(bsz * qlen, -1)
    return ao, position_major(kt_next), position_major(vt_next)


CONV_HALO = 32
CONV_ROWS = 64
CONV_COLS = 256


def _conv_prompt_kernel(uc_ref, up_ref, w_ref, b_ref, wsrc_hbm, y_ref, wdst_hbm, sh_ref, *cast_scratch):
    tile = uc_ref.shape[0]
    n = pl.program_id(1)
    n_inner = tile // CONV_ROWS
    first_slab = (pl.program_id(0) * pl.num_programs(1) + n) * n_inner
    n_slabs = pl.num_programs(0) * pl.num_programs(1) * n_inner
    halo = jnp.where(n > 0, up_ref[...], 0.0)
    sh_ref[0, :CONV_HALO, :] = halo
    sh_ref[0, CONV_HALO:, :] = uc_ref[...]
    keep = tile + CONV_HALO - SUBLANES
    for r in range(1, SUBLANES):
        sh_ref[r, :keep, :] = sh_ref[0, r:r + keep, :]
    first = CONV_HALO - (CONV_WIDTH - 1)
    ch = uc_ref.shape[1]

    def rows_step(rc, carry):
        _cast_stream_step(first_slab + rc, n_slabs, wsrc_hbm, wdst_hbm, *cast_scratch)
        r0 = pl.multiple_of(rc * CONV_ROWS, CONV_ROWS)
        for cc in range(ch // CONV_COLS):
            cs = slice(cc * CONV_COLS, (cc + 1) * CONV_COLS)
            acc = jnp.zeros((CONV_ROWS, CONV_COLS), F32)
            for j in range(CONV_WIDTH):
                off = first + j
                a, r = off // SUBLANES, off % SUBLANES
                acc = acc + sh_ref[r, pl.ds(r0 + a * SUBLANES, CONV_ROWS), cs] * w_ref[j:j + 1, cs]
            y_ref[pl.ds(r0, CONV_ROWS), cs] = acc + b_ref[:, cs]
        return carry

    lax.fori_loop(0, n_inner, rows_step, 0)


def _conv_prompt(u, w, b, bsz, slen, w_side, tile=512):
    ch = u.shape[1]
    nt = slen // tile
    per = tile // CONV_HALO
    return pl.pallas_call(
        _conv_prompt_kernel,
        grid=(bsz, nt),
        in_specs=[
            pl.BlockSpec((tile, ch), lambda bb, n: (bb * nt + n, 0)),
            pl.BlockSpec((CONV_HALO, ch), lambda bb, n: (jnp.maximum((bb * nt + n) * per - 1, 0), 0)),
            pl.BlockSpec((CONV_WIDTH, ch), lambda bb, n: (0, 0)),
            pl.BlockSpec((1, ch), lambda bb, n: (0, 0)),
            pl.BlockSpec(memory_space=pl.ANY),
        ],
        out_specs=[pl.BlockSpec((tile, ch), lambda bb, n: (bb * nt + n, 0)), pl.BlockSpec(memory_space=pl.ANY)],
        out_shape=[jax.ShapeDtypeStruct((bsz * slen, ch), F32), jax.ShapeDtypeStruct(w_side.shape, BF16)],
        scratch_shapes=[pltpu.VMEM((SUBLANES, tile + CONV_HALO, ch), F32)]
        + _cast_stream_scratch(w_side.shape[0], w_side.shape[1], bsz * slen // CONV_ROWS, w_side.dtype, BF16),
        compiler_params=_cparams(("arbitrary", "arbitrary")),
        name="conv_prompt",
    )(u, u, w, b.reshape(1, ch), w_side)


def _conv_sample_kernel(sc_ref, un_ref, w_ref, b_ref, y_ref, next_ref):
    nctx = sc_ref.shape[0]
    qlen = un_ref.shape[0]
    ch = w_ref.shape[1]

    def row(t):
        return sc_ref[t] if t < nctx else un_ref[t - nctx]

    for i in range(qlen):
        acc = jnp.zeros((sc_ref.shape[1], ch), F32)
        for j in range(CONV_WIDTH):
            acc = acc + row(i + j) * w_ref[j:j + 1, :]
        y_ref[i] = acc + b_ref[...]
    for t in range(nctx):
        next_ref[t] = row(t + qlen)


def _conv_sample(state_t, u_new_t, w, b, seq_per_step=16):
    nctx, bsz, ch = state_t.shape
    qlen = u_new_t.shape[0]
    seq = lambda i: (0, i, 0)
    return pl.pallas_call(
        _conv_sample_kernel,
        grid=(bsz // seq_per_step,),
        in_specs=[
            pl.BlockSpec((nctx, seq_per_step, ch), seq),
            pl.BlockSpec((qlen, seq_per_step, ch), seq),
            pl.BlockSpec((CONV_WIDTH, ch), lambda i: (0, 0)),
            pl.BlockSpec((1, ch), lambda i: (0, 0)),
        ],
        out_specs=[pl.BlockSpec((qlen, seq_per_step, ch), seq), pl.BlockSpec((nctx, seq_per_step, ch), seq)],
        out_shape=[jax.ShapeDtypeStruct((qlen, bsz, ch), F32), jax.ShapeDtypeStruct((nctx, bsz, ch), F32)],
        compiler_params=_cparams(("arbitrary",)),
        name="conv_sample",
    )(state_t, u_new_t, w, b.reshape(1, ch))


def _merge_kernel(y_ref, ao_ref, ga_ref, gc_ref, x_ref, lnw_ref, lnb_ref, wco_ref, wo_ref, fw_ref, wr_ref,
                  h_ref, xn_ref, lg_ref):
    y = y_ref[...]
    mu = jnp.mean(y, axis=-1, keepdims=True)
    yc = y - mu
    var = jnp.mean(yc * yc, axis=-1, keepdims=True)
    yl = yc * lax.rsqrt(var + EPS) * lnw_ref[...] + lnb_ref[...]
    act = yl * jax.nn.sigmoid(yl)
    conv_o = jnp.dot(act.astype(BF16), wco_ref[...], preferred_element_type=F32)
    mix = ga_ref[...].astype(F32) * ao_ref[...].astype(F32) + gc_ref[...].astype(F32) * conv_o
    h = x_ref[...] + jnp.dot(mix.astype(BF16), wo_ref[...], preferred_element_type=F32)
    h_ref[...] = h
    ms = jnp.mean(h * h, axis=-1, keepdims=True)
    xn = (h * lax.rsqrt(ms + EPS) * fw_ref[...]).astype(BF16)
    xn_ref[...] = xn
    lg_ref[...] = jnp.dot(xn, wr_ref[...], preferred_element_type=F32)


def _merge(y, ao, gates, row0, x, ln_w, ln_b, wco, wo, fw, wr, tile=256):
    t, d = x.shape
    ch = y.shape[1]
    off = row0 // tile
    row = lambda i: (i, 0)
    const = lambda shape: pl.BlockSpec(shape, lambda i: (0, 0), pipeline_mode=pl.Buffered(1))
    return pl.pallas_call(
        _merge_kernel,
        grid=(t // tile,),
        in_specs=[
            pl.BlockSpec((tile, ch), row),
            pl.BlockSpec((tile, d), row),
            pl.BlockSpec((tile, d), lambda i: (i + off, 0)),
            pl.BlockSpec((tile, d), lambda i: (i + off, 1)),
            pl.BlockSpec((tile, d), row),
            const((1, ch)), const((1, ch)), const((ch, d)), const((d, d)), const((1, d)),
            const((d, LANES)),
        ],
        out_specs=[pl.BlockSpec((tile, d), row), pl.BlockSpec((tile, d), row), pl.BlockSpec((tile, LANES), row)],
        out_shape=[jax.ShapeDtypeStruct((t, d), F32), jax.ShapeDtypeStruct((t, d), BF16),
                   jax.ShapeDtypeStruct((t, LANES), F32)],
        compiler_params=_cparams(("arbitrary",)),
        name="merge",
    )(y, ao, gates, gates, x, ln_w.reshape(1, ch), ln_b.reshape(1, ch), wco, wo, fw.reshape(1, d), wr)


def _route_kernel(lp_ref, ls_ref, tri_ref, lower_ref, meta_ref, metat_ref, cnt_ref, *, n_p):
    i = pl.program_id(0)
    used = _round_up(N_EXPERTS + N_GROUPS, SUBLANES)

    def body(l_ref):
        lg = l_ref[...].T[:used]
        shape = lg.shape
        row = lax.broadcasted_iota(jnp.int32, shape, 0)
        big = jnp.int32(LANES)
        is_g = (row >= N_EXPERTS) & (row < N_EXPERTS + N_GROUPS)
        gl = jnp.where(is_g, lg, -jnp.inf)
        gmax = jnp.max(gl, axis=0, keepdims=True)
        g_star = jnp.min(jnp.where(gl == gmax, row - N_EXPERTS, big), axis=0, keepdims=True)
        pg_star = 1.0 / jnp.sum(jnp.exp(gl - gmax), axis=0, keepdims=True)
        in_g = (row < N_EXPERTS) & (_div_pow2(row, EXPERTS_PER_GROUP) == g_star)
        el = jnp.where(in_g, lg, -jnp.inf)
        v1 = jnp.max(el, axis=0, keepdims=True)
        i1 = jnp.min(jnp.where(el == v1, row, big), axis=0, keepdims=True)
        el2 = jnp.where(row == i1, -jnp.inf, el)
        v2 = jnp.max(el2, axis=0, keepdims=True)
        i2 = jnp.min(jnp.where(el2 == v2, row, big), axis=0, keepdims=True)
        e2 = jnp.exp(v2 - v1)
        w1 = pg_star * (1.0 / (1.0 + e2))
        w2 = pg_star * (e2 / (1.0 + e2))
        oh1 = row == i1
        oh2 = row == i2
        cnt = jnp.where(oh1 | oh2, 1.0, 0.0)
        before = jnp.dot(cnt.astype(BF16), tri_ref[...], preferred_element_type=F32)
        tot = jnp.sum(cnt, axis=-1, keepdims=True)
        runs = jnp.floor((tot + (CHUNK_ROWS - 1)) * (1.0 / CHUNK_ROWS))
        runs_b = jnp.broadcast_to(runs, (used, LANES)).astype(BF16)
        start = jnp.dot(lower_ref[...], runs_b, preferred_element_type=F32)[:, 0:1] * CHUNK_ROWS
        slot = before + start
        s1 = jnp.sum(jnp.where(oh1, slot, 0.0), axis=0, keepdims=True)
        s2 = jnp.sum(jnp.where(oh2, slot, 0.0), axis=0, keepdims=True)
        fields = [i1.astype(F32), i2.astype(F32), w1, w2, s1, s2]
        fields += [jnp.zeros_like(w1)] * (META_FIELDS - len(fields))
        metat = jnp.concatenate(fields, axis=0)
        metat_ref[...] = metat
        pad = jnp.zeros((LANES - META_FIELDS, shape[1]), F32)
        meta_ref[...] = jnp.concatenate([metat, pad], axis=0).T
        tot_pad = jnp.concatenate([tot, jnp.zeros((LANES - used, 1), F32)], axis=0)
        cnt_ref[...] = jnp.broadcast_to(tot_pad, (LANES, LANES)).T[0:1]

    @pl.when(i < n_p)
    def _():
        body(lp_ref)

    @pl.when(i >= n_p)
    def _():
        body(ls_ref)


def _route(lg_p, lg_s):
    tile = TOKEN_BLOCK
    tp, ts = lg_p.shape[0], lg_s.shape[0]
    n_p, n_s = tp // tile, ts // tile
    nb = n_p + n_s
    used = _round_up(N_EXPERTS + N_GROUPS, SUBLANES)
    tri = (jnp.arange(tile)[:, None] < jnp.arange(tile)[None, :]).astype(BF16)
    lower = (jnp.arange(used)[:, None] > jnp.arange(used)[None, :]).astype(BF16)
    return pl.pallas_call(
        functools.partial(_route_kernel, n_p=n_p),
        grid=(nb,),
        in_specs=[
            pl.BlockSpec((tile, LANES), lambda i: (jnp.minimum(i, n_p - 1), 0)),
            pl.BlockSpec((tile, LANES), lambda i: (jnp.maximum(i - n_p, 0), 0)),
            pl.BlockSpec((tile, tile), lambda i: (0, 0)),
            pl.BlockSpec((used, used), lambda i: (0, 0)),
        ],
        out_specs=[pl.BlockSpec((tile, LANES), lambda i: (i, 0)),
                   pl.BlockSpec((None, META_FIELDS, tile), lambda i: (i, 0, 0)),
                   pl.BlockSpec((None, 1, LANES), lambda i: (i, 0, 0))],
        out_shape=[jax.ShapeDtypeStruct((tp + ts, LANES), F32),
                   jax.ShapeDtypeStruct((nb, META_FIELDS, tile), F32),
                   jax.ShapeDtypeStruct((nb, 1, LANES), F32)],
        compiler_params=_cparams(("arbitrary",)),
        name="route",
    )(lg_p, lg_s, tri, lower)


def _plan_sizes(n_tok):
    nb = n_tok // TOKEN_BLOCK
    pad_per_block = N_EXPERTS * (CHUNK_ROWS - 1)
    max_chunks = _round_up(-(-(TOKEN_BLOCK * TOP_K + pad_per_block) // CHUNK_ROWS), SUBLANES)
    n_rows = _round_up(n_tok * TOP_K + nb * pad_per_block + N_EXPERTS * (ROW_PAD - 1), ROW_PAD)
    n_items = (n_rows + N_EXPERTS * (EXPERT_ROWS - ROW_PAD)) // EXPERT_ROWS
    max_zero = N_EXPERTS * (ROW_PAD // CHUNK_ROWS)
    return nb, max_chunks, n_rows, n_items, max_zero


def _dispatch_plan(counts, max_chunks, n_items, max_zero):
    i32 = jnp.int32
    c = counts[:, 0, :N_EXPERTS].astype(i32)
    run = (c + CHUNK_ROWS - 1) // CHUNK_ROWS
    run_end = jnp.cumsum(run, axis=1)
    src_start = run_end - run
    nchunk = run_end[:, -1]
    seg = jnp.sum(run, axis=0) * CHUNK_ROWS
    padded = (seg + ROW_PAD - 1) // ROW_PAD * ROW_PAD
    pad_end = jnp.cumsum(padded)
    pad_start = pad_end - padded
    dst_start = pad_start[None, :] // CHUNK_ROWS + (jnp.cumsum(run, axis=0) - run)
    jj = jnp.arange(max_chunks, dtype=i32)[None, :, None]
    in_run = (src_start[:, None, :] <= jj) & (jj < run_end[:, None, :])
    chunk_dst = jnp.sum(jnp.where(in_run, (dst_start - src_start)[:, None, :] + jj, 0), axis=-1) * CHUNK_ROWS
    tail = (padded - seg) // CHUNK_ROWS
    tail_end = jnp.cumsum(tail)
    tail_start = tail_end - tail
    z = jnp.arange(max_zero, dtype=i32)[:, None]
    in_tail = (tail_start[None, :] <= z) & (z < tail_end[None, :])
    zero_base = (pad_start + seg) // CHUNK_ROWS - tail_start
    zero_dst = jnp.sum(jnp.where(in_tail, zero_base[None, :] + z, 0), axis=-1) * CHUNK_ROWS
    nzero = tail_end[-1]
    items_e = (padded + EXPERT_ROWS - 1) // EXPERT_ROWS
    item_end = jnp.cumsum(items_e)
    item_start = item_end - items_e
    w = jnp.arange(n_items, dtype=i32)[:, None]
    in_item = (item_start[None, :] <= w) & (w < item_end[None, :])
    k = w - item_start[None, :]
    e_ids = jnp.arange(N_EXPERTS, dtype=i32)
    last_e = jnp.max(jnp.where(items_e > 0, e_ids, 0))
    item_e = jnp.where(w[:, 0] < item_end[-1], jnp.sum(jnp.where(in_item, e_ids[None, :], 0), axis=-1), last_e)
    item_row0 = jnp.sum(jnp.where(in_item, pad_start[None, :] + k * EXPERT_ROWS, 0), axis=-1)
    item_rows = jnp.sum(jnp.where(in_item, jnp.clip(padded[None, :] - k * EXPERT_ROWS, 0, EXPERT_ROWS), 0), axis=-1)
    item_nsub = item_rows // ROW_PAD
    return (chunk_dst.reshape(-1).astype(i32), nchunk.astype(i32), zero_dst.astype(i32),
            nzero.reshape(1).astype(i32), item_e.astype(i32), item_row0.astype(i32), item_nsub.astype(i32))


SCATTER_GROUPS = 2


def _scatter_kernel(cd_ref, nc_ref, zd_ref, nz_ref, mt_ref, xp_ref, xs_ref, out_hbm, loc_ref, zero_ref, sem,
                    zero_sem, *, n_p, max_chunks):
    i = pl.program_id(0)
    buf = i % 2
    n_slots = loc_ref.shape[1]
    d = xp_ref.shape[1]
    mt = mt_ref[...]
    w1, w2 = mt[2:3, :], mt[3:4, :]
    s1, s2 = mt[4:5, :].astype(jnp.int32), mt[5:6, :].astype(jnp.int32)
    half = d // 2
    group = n_slots // SCATTER_GROUPS

    def sort_rows(x_ref):
        for s0 in range(0, n_slots, group):
            slot = lax.broadcasted_iota(jnp.int32, (group, mt.shape[1]), 0) + s0
            p1 = slot == s1
            p2 = slot == s2
            perm = jnp.where(p1 | p2, 1.0, 0.0).astype(BF16)
            wcol = jnp.sum(jnp.where(p1, w1, 0.0) + jnp.where(p2, w2, 0.0), axis=-1, keepdims=True)
            loc_ref[buf, s0:s0 + group, half:] = pltpu.bitcast(jnp.broadcast_to(wcol, (group, LANES)), jnp.uint32)
            rows = jnp.dot(perm, x_ref[...], preferred_element_type=F32)
            loc_ref[buf, s0:s0 + group, :half] = _pack_pairs(rows[:, :half], rows[:, half:])

    @pl.when(i < n_p)
    def _():
        sort_rows(xp_ref)

    @pl.when(i >= n_p)
    def _():
        sort_rows(xs_ref)

    def rows_at(r):
        return pl.ds(pl.multiple_of(r, CHUNK_ROWS), CHUNK_ROWS)

    def chunk_copy(blk, j):
        return pltpu.make_async_copy(loc_ref.at[blk % 2, rows_at(j * CHUNK_ROWS)],
                                     out_hbm.at[rows_at(cd_ref[blk * max_chunks + j])], sem.at[blk % 2])

    def zero_copy(z):
        return pltpu.make_async_copy(zero_ref, out_hbm.at[rows_at(zd_ref[z])], zero_sem)

    def loop(n, fn):
        def body(j, carry):
            fn(j)
            return carry
        lax.fori_loop(0, n, body, 0)

    @pl.when(i > 0)
    def _():
        loop(nc_ref[i - 1], lambda j: chunk_copy(i - 1, j).wait())

    loop(nc_ref[i], lambda j: chunk_copy(i, j).start())

    @pl.when(i == pl.num_programs(0) - 1)
    def _():
        zero_ref[...] = jnp.zeros_like(zero_ref)
        loop(nz_ref[0], lambda z: zero_copy(z).start())
        loop(nc_ref[i], lambda j: chunk_copy(i, j).wait())
        loop(nz_ref[0], lambda z: zero_copy(z).wait())


def _scatter_rows(xn_p, xn_s, metat, chunk_dst, nchunk, zero_dst, nzero, max_chunks, n_rows):
    tile = TOKEN_BLOCK
    tp, d = xn_p.shape
    ts = xn_s.shape[0]
    n_p, n_s = tp // tile, ts // tile
    n_slots = max_chunks * CHUNK_ROWS
    width = d // 2 + LANES
    u32 = jnp.uint32
    return pl.pallas_call(
        functools.partial(_scatter_kernel, n_p=n_p, max_chunks=max_chunks),
        grid_spec=pltpu.PrefetchScalarGridSpec(
            num_scalar_prefetch=4,
            grid=(n_p + n_s,),
            in_specs=[
                pl.BlockSpec((None, META_FIELDS, tile), lambda i, *_: (i, 0, 0)),
                pl.BlockSpec((tile, d), lambda i, *_: (jnp.minimum(i, n_p - 1), 0)),
                pl.BlockSpec((tile, d), lambda i, *_: (jnp.maximum(i - n_p, 0), 0)),
            ],
            out_specs=pl.BlockSpec(memory_space=pl.ANY),
            scratch_shapes=[pltpu.VMEM((2, n_slots, width), u32), pltpu.VMEM((CHUNK_ROWS, width), u32),
                            pltpu.SemaphoreType.DMA((2,)), pltpu.SemaphoreType.DMA(())],
        ),
        out_shape=jax.ShapeDtypeStruct((n_rows, width), u32),
        compiler_params=_cparams(("arbitrary",)),
        name="scatter_rows",
    )(chunk_dst, nchunk, zero_dst, nzero, metat, xn_p, xn_s)


def _expert_kernel(e_ref, row0_ref, nsub_ref, xs_hbm, wg_ref, wu_ref, wd_ref, out_hbm,
                   xf_ref, xb_ref, acc_ref, ob_ref, pend_ref, in_sem, out_sem, *, n_chunks):
    w = pl.program_id(0)
    c = pl.program_id(1)
    n_items = pl.num_programs(0)
    last_c = pl.num_programs(1) - 1
    nsub = nsub_ref[w]
    d = acc_ref.shape[1]
    half = d // 2
    slot = w % 2

    def sub_rows(s):
        return pl.ds(pl.multiple_of(s * ROW_PAD, ROW_PAD), ROW_PAD)

    def hbm_rows(item, s):
        return pl.ds(pl.multiple_of(row0_ref[item] + s * ROW_PAD, ROW_PAD), ROW_PAD)

    def in_copy(item, s):
        return pltpu.make_async_copy(xs_hbm.at[hbm_rows(item, s)], xf_ref.at[item % 2, sub_rows(s)],
                                     in_sem.at[item % 2])

    def out_copy(item, s):
        return pltpu.make_async_copy(ob_ref.at[sub_rows(s)], out_hbm.at[hbm_rows(item, s)], out_sem)

    def loop(n, fn):
        def body(s, carry):
            fn(s)
            return carry
        lax.fori_loop(0, n, body, 0)

    @pl.when(c == 0)
    def _():
        @pl.when(w == 0)
        def _():
            pend_ref[0] = 0
            pend_ref[1] = 0
            loop(nsub, lambda s: in_copy(w, s).start())

        loop(nsub, lambda s: in_copy(w, s).wait())
        nxt = jnp.minimum(w + 1, n_items - 1)

        @pl.when(w + 1 < n_items)
        def _():
            loop(nsub_ref[nxt], lambda s: in_copy(nxt, s).start())

    def drain_out():
        prev = pend_ref[1]
        loop(pend_ref[0], lambda s: out_copy(prev, s).wait())
        pend_ref[0] = 0

    def ffn_chunk(x):
        g = jnp.dot(x, wg_ref[...].astype(BF16), preferred_element_type=F32)
        u = jnp.dot(x, wu_ref[...].astype(BF16), preferred_element_type=F32)
        hmid = (g * jax.nn.sigmoid(g) * u).astype(BF16)
        return jnp.dot(hmid, wd_ref[...].astype(BF16), preferred_element_type=F32)

    for n in range(1, EXPERT_ROWS // ROW_PAD + 1):
        m = n * ROW_PAD

        @pl.when((nsub == n) & (c == 0))
        def _(m=m):
            xa, xb = _unpack_pairs(xf_ref[slot, :m, :half])
            x = jnp.concatenate([xa, xb], axis=1).astype(BF16)
            xb_ref[:m, :] = x
            acc_ref[:m, :] = ffn_chunk(x)

        if n_chunks > 2:
            @pl.when((nsub == n) & (c > 0) & (c < last_c))
            def _(m=m):
                acc_ref[:m, :] += ffn_chunk(xb_ref[:m, :])

        @pl.when((nsub == n) & (c == last_c))
        def _(m=m):
            drain_out()
            mix_w = pltpu.bitcast(xf_ref[slot, :m, half:half + 1], F32)
            y = ((acc_ref[:m, :] + ffn_chunk(xb_ref[:m, :])) * mix_w).astype(BF16).astype(F32)
            ob_ref[:m, :] = _pack_pairs(y[:, :half], y[:, half:])
            loop(nsub, lambda s: out_copy(w, s).start())
            pend_ref[0] = nsub
            pend_ref[1] = w

    @pl.when((w == n_items - 1) & (c == last_c))
    def _():
        drain_out()


def _experts(xs, item_e, item_row0, item_nsub, w_gate, w_up, w_down):
    n_rows, width = xs.shape
    d = w_gate.shape[1]
    assert width == d // 2 + LANES
    d_exp = w_gate.shape[2]
    nc = d_exp // D_CHUNK
    assert nc >= 2
    n_items = item_e.shape[0]

    def chunk_of(w, c, nsub_ref):
        return jnp.where(nsub_ref[w] > 0, c, nc - 1)

    return pl.pallas_call(
        functools.partial(_expert_kernel, n_chunks=nc),
        grid_spec=pltpu.PrefetchScalarGridSpec(
            num_scalar_prefetch=3,
            grid=(n_items, nc),
            in_specs=[
                pl.BlockSpec(memory_space=pl.ANY),
                pl.BlockSpec((None, d, D_CHUNK), lambda w, c, e, r, n: (e[w], 0, chunk_of(w, c, n))),
                pl.BlockSpec((None, d, D_CHUNK), lambda w, c, e, r, n: (e[w], 0, chunk_of(w, c, n))),
                pl.BlockSpec((None, D_CHUNK, d), lambda w, c, e, r, n: (e[w], chunk_of(w, c, n), 0)),
            ],
            out_specs=pl.BlockSpec(memory_space=pl.ANY),
            scratch_shapes=[
                pltpu.VMEM((2, EXPERT_ROWS, width), jnp.uint32),
                pltpu.VMEM((EXPERT_ROWS, d), BF16),
                pltpu.VMEM((EXPERT_ROWS, d), F32),
                pltpu.VMEM((EXPERT_ROWS, d // 2), jnp.uint32),
                pltpu.SMEM((2,), jnp.int32),
                pltpu.SemaphoreType.DMA((2,)),
                pltpu.SemaphoreType.DMA(()),
            ],
        ),
        out_shape=jax.ShapeDtypeStruct((n_rows, d // 2), jnp.uint32),
        compiler_params=_cparams(("arbitrary", "arbitrary")),
        name="experts",
    )(item_e, item_row0, item_nsub, xs, w_gate, w_up, w_down)


COMBINE_SPLIT = 2
SLOT_STEP = 256


def _combine_kernel(cd_ref, nc_ref, meta_ref, hp_ref, hs_ref, yb_hbm, yp_ref, ys_ref, loc_ref, sem,
                    *, n_p, max_chunks):
    i = pl.program_id(0)
    j = pl.program_id(1)
    nb = pl.num_programs(0)
    buf = i % 2
    n_slots, half = loc_ref.shape[1], loc_ref.shape[2]
    d = 2 * half

    def rows_at(r):
        return pl.ds(pl.multiple_of(r, CHUNK_ROWS), CHUNK_ROWS)

    def chunk_copy(blk, c):
        return pltpu.make_async_copy(yb_hbm.at[rows_at(cd_ref[blk * max_chunks + c])],
                                     loc_ref.at[blk % 2, rows_at(c * CHUNK_ROWS)], sem.at[blk % 2])

    def fetch(blk):
        n = nc_ref[blk]

        def start(c, carry):
            chunk_copy(blk, c).start()
            return carry
        lax.fori_loop(0, n, start, 0)

        def clear(c, carry):
            loc_ref[blk % 2, rows_at(c * CHUNK_ROWS), :] = jnp.zeros((CHUNK_ROWS, half), jnp.uint32)
            return carry
        lax.fori_loop(n, max_chunks, clear, 0)

    @pl.when(j == 0)
    def _():
        @pl.when(i == 0)
        def _():
            fetch(i)

        def wait(c, carry):
            chunk_copy(i, c).wait()
            return carry
        lax.fori_loop(0, nc_ref[i], wait, 0)

        @pl.when(i + 1 < nb)
        def _():
            fetch(jnp.minimum(i + 1, nb - 1))

    meta = meta_ref[...]
    s1 = meta[:, 4:5].astype(jnp.int32)
    s2 = meta[:, 5:6].astype(jnp.int32)
    rows = meta.shape[0]
    f = jnp.zeros((rows, d), F32)
    for r0 in range(0, n_slots, SLOT_STEP):
        slot = lax.broadcasted_iota(jnp.int32, (rows, SLOT_STEP), 1) + r0
        pt = jnp.where((slot == s1) | (slot == s2), 1.0, 0.0).astype(BF16)
        ya, yb = _unpack_pairs(loc_ref[buf, r0:r0 + SLOT_STEP, :])
        f = f + jnp.dot(pt, jnp.concatenate([ya, yb], axis=1).astype(BF16), preferred_element_type=F32)
    g = i * pl.num_programs(1) + j

    @pl.when(g < n_p)
    def _():
        yp_ref[...] = hp_ref[...] + f

    @pl.when(g >= n_p)
    def _():
        ys_ref[...] = hs_ref[...] + f


def _combine(h_p, h_s, meta, chunk_dst, nchunk, yb, max_chunks):
    tp, d = h_p.shape
    ts = h_s.shape[0]
    tile = TOKEN_BLOCK // COMBINE_SPLIT
    n_p, n_s = tp // tile, ts // tile
    nb = (tp + ts) // TOKEN_BLOCK
    n_slots = max_chunks * CHUNK_ROWS
    assert n_slots % SLOT_STEP == 0
    blk = lambda i, j: i * COMBINE_SPLIT + j
    p_idx = lambda i, j, *_: (jnp.minimum(blk(i, j), n_p - 1), 0)
    s_idx = lambda i, j, *_: (jnp.maximum(blk(i, j) - n_p, 0), 0)
    return pl.pallas_call(
        functools.partial(_combine_kernel, n_p=n_p, max_chunks=max_chunks),
        grid_spec=pltpu.PrefetchScalarGridSpec(
            num_scalar_prefetch=2,
            grid=(nb, COMBINE_SPLIT),
            in_specs=[
                pl.BlockSpec((tile, LANES), lambda i, j, *_: (blk(i, j), 0)),
                pl.BlockSpec((tile, d), p_idx),
                pl.BlockSpec((tile, d), s_idx),
                pl.BlockSpec(memory_space=pl.ANY),
            ],
            out_specs=[pl.BlockSpec((tile, d), p_idx), pl.BlockSpec((tile, d), s_idx)],
            scratch_shapes=[pltpu.VMEM((2, n_slots, d // 2), jnp.uint32), pltpu.SemaphoreType.DMA((2,))],
        ),
        out_shape=[jax.ShapeDtypeStruct((tp, d), F32), jax.ShapeDtypeStruct((ts, d), F32)],
        compiler_params=_cparams(("arbitrary", "arbitrary")),
        name="combine",
    )(chunk_dst, nchunk, meta, h_p, h_s, yb)


def _layer(x_p, x_s, ck, cv, sc, attn_norm_w, w_in, b_gate, q_norm_w, k_norm_w, sinks, conv_dw_w, conv_dw_b,
           conv_ln_w, conv_ln_b, w_conv_out, w_out, ffn_norm_w, router_group, router_expert, w_gate, w_up,
           w_down):
    bsz, slen, d = x_p.shape
    dbs, dlen, _ = x_s.shape
    tp, ts = bsz * slen, dbs * dlen
    conv_ch = conv_dw_w.shape[1]
    kvw = N_KV_HEADS * HEAD_DIM
    xp2, xs2 = x_p.reshape(tp, d), x_s.reshape(ts, d)

    xn = _prenorm(xp2, xs2, attn_norm_w)
    q, k, v, u, gates = _project(xn, w_in, b_gate, q_norm_w, k_norm_w, d, conv_ch)

    n_exp, _, d_exp = w_gate.shape
    ao_p, wg_bf = _attn_prompt(q, k, v, sinks, bsz, slen, w_gate.reshape(n_exp * d, d_exp))
    wg_bf = wg_bf.reshape(w_gate.shape)
    ao_s, ck_next, cv_next = _attn_sample(q[tp:], ck, k[tp:], cv, v[tp:], sinks)

    y_p, wu_bf = _conv_prompt(u, conv_dw_w, conv_dw_b, bsz, slen, w_up.reshape(n_exp * d, d_exp))
    wu_bf = wu_bf.reshape(w_up.shape)
    sc_t = sc.transpose(1, 0, 2)
    u_s_t = u[tp:].reshape(dbs, dlen, conv_ch).transpose(1, 0, 2)
    y_s_t, sc_next_t = _conv_sample(sc_t, u_s_t, conv_dw_w, conv_dw_b)
    y_s = y_s_t.transpose(1, 0, 2).reshape(ts, conv_ch)

    wco = w_conv_out.astype(BF16)
    wo = w_out.astype(BF16)
    wr = jnp.concatenate([router_expert, router_group,
                          jnp.zeros((d, LANES - N_EXPERTS - N_GROUPS), router_expert.dtype)], axis=1).astype(BF16)
    margs = (conv_ln_w, conv_ln_b, wco, wo, ffn_norm_w, wr)
    h_p, xn_p, lg_p = _merge(y_p, ao_p, gates, 0, xp2, *margs)
    h_s, xn_s, lg_s = _merge(y_s, ao_s, gates, tp, xs2, *margs)

    meta, metat, counts = _route(lg_p, lg_s)
    _, max_chunks, n_rows, n_items, max_zero = _plan_sizes(tp + ts)
    chunk_dst, nchunk, zero_dst, nzero, item_e, item_row0, item_nsub = _dispatch_plan(
        counts, max_chunks, n_items, max_zero)
    xs_rows = _scatter_rows(xn_p, xn_s, metat, chunk_dst, nchunk, zero_dst, nzero, max_chunks, n_rows)
    yb = _experts(xs_rows, item_e, item_row0, item_nsub, wg_bf, wu_bf, w_down)
    out_p, out_s = _combine(h_p, h_s, meta, chunk_dst, nchunk, yb, max_chunks)

    def seq_tails(a, n):
        return jnp.stack([lax.slice_in_dim(a, (b + 1) * slen - n, (b + 1) * slen, axis=0) for b in range(bsz)])

    new_p = (seq_tails(k, WINDOW).reshape(bsz, WINDOW, N_KV_HEADS, HEAD_DIM),
             seq_tails(v, WINDOW).reshape(bsz, WINDOW, N_KV_HEADS, HEAD_DIM),
             seq_tails(u, CONV_WIDTH - 1))
    new_s = (ck_next, cv_next, sc_next_t.transpose(1, 0, 2))
    return out_p.reshape(bsz, slen, d), out_s.reshape(dbs, dlen, d), new_p, new_s


def kernel(x_prompt, x_sample, cache_k, cache_v, state_conv, attn_norm_w, w_in, b_gate, q_norm_w, k_norm_w,
           sinks, conv_dw_w, conv_dw_b, conv_ln_w, conv_ln_b, w_conv_out, w_out, ffn_norm_w, router_group,
           router_expert, w_gate, w_up, w_down):
    x_p, x_s = x_prompt, x_sample
    kp, vp, cp, ks_, vs_, cs_ = [], [], [], [], [], []
    for l in range(cache_k.shape[0]):
        x_p, x_s, new_p, new_s = _layer(
            x_p, x_s, cache_k[l], cache_v[l], state_conv[l], attn_norm_w[l], w_in[l], b_gate[l], q_norm_w[l],
            k_norm_w[l], sinks[l], conv_dw_w[l], conv_dw_b[l], conv_ln_w[l], conv_ln_b[l], w_conv_out[l],
            w_out[l], ffn_norm_w[l], router_group[l], router_expert[l], w_gate[l], w_up[l], w_down[l])
        kp.append(new_p[0]); vp.append(new_p[1]); cp.append(new_p[2])
        ks_.append(new_s[0]); vs_.append(new_s[1]); cs_.append(new_s[2])
    return (x_p, x_s, jnp.stack(kp), jnp.stack(vp), jnp.stack(cp),
            jnp.stack(ks_), jnp.stack(vs_), jnp.stack(cs_))
```

```python
import functools

import jax
import jax.numpy as jnp
from jax import lax
from jax.experimental import pallas as pl
from jax.experimental.pallas import tpu as pltpu

F32 = jnp.float32
BF16 = jnp.bfloat16
EPS = 1e-6
NEG_INF = -1e30

HEAD_DIM = 64
N_KV_HEADS = 4
WINDOW = 128
CONV_WIDTH = 31
N_GROUPS = 4
EXPERTS_PER_GROUP = 8
N_EXPERTS = N_GROUPS * EXPERTS_PER_GROUP
TOP_K = 2

LANES = 128
SUBLANES = 8
MXU_DIM = 256
VMEM_LIMIT = 56 * 1024 * 1024

TOKEN_BLOCK = 512
CHUNK_ROWS = SUBLANES
ROW_PAD = 128
EXPERT_ROWS = 768
D_CHUNK = 512
META_FIELDS = 8


def _cparams(sem):
    return pltpu.CompilerParams(dimension_semantics=sem, vmem_limit_bytes=VMEM_LIMIT)


def _log2(n):
    assert n > 0 and n & (n - 1) == 0, n
    return n.bit_length() - 1


def _div_pow2(x, n):
    return lax.shift_right_logical(x, jnp.int32(_log2(n)))


def _mod_pow2(x, n):
    _log2(n)
    return x & jnp.int32(n - 1)


def _round_up(x, m):
    return (x + m - 1) // m * m


HIGH_HALF = 0xFFFF0000


def _pack_pairs(a, b):
    ua = pltpu.bitcast(a, jnp.uint32) & jnp.uint32(HIGH_HALF)
    return ua | (pltpu.bitcast(b, jnp.uint32) >> 16)


def _unpack_pairs(w):
    return (pltpu.bitcast(w & jnp.uint32(HIGH_HALF), F32), pltpu.bitcast(w << 16, F32))


def _prenorm_kernel(xp_ref, xs_ref, w_ref, o_ref, *, n_p):
    i = pl.program_id(0)

    def body(x_ref):
        x = x_ref[...]
        ms = jnp.mean(x * x, axis=-1, keepdims=True)
        o_ref[...] = (x * lax.rsqrt(ms + EPS) * w_ref[...]).astype(o_ref.dtype)

    @pl.when(i < n_p)
    def _():
        body(xp_ref)

    @pl.when(i >= n_p)
    def _():
        body(xs_ref)


def _prenorm(xp, xs, w, tile=512):
    tp, d = xp.shape
    ts = xs.shape[0]
    n_p, n_s = tp // tile, ts // tile
    return pl.pallas_call(
        functools.partial(_prenorm_kernel, n_p=n_p),
        grid=(n_p + n_s,),
        in_specs=[
            pl.BlockSpec((tile, d), lambda i: (jnp.minimum(i, n_p - 1), 0)),
            pl.BlockSpec((tile, d), lambda i: (jnp.maximum(i - n_p, 0), 0)),
            pl.BlockSpec((1, d), lambda i: (0, 0)),
        ],
        out_specs=pl.BlockSpec((tile, d), lambda i: (i, 0)),
        out_shape=jax.ShapeDtypeStruct((tp + ts, d), BF16),
        compiler_params=_cparams(("arbitrary",)),
        name="prenorm",
    )(xp, xs, w.reshape(1, d))


def _head_rms(z, hw, bd_ref):
    ss = z * z
    hi = ss.astype(BF16)
    lo = (ss - hi.astype(F32)).astype(BF16)
    bd = bd_ref[...]
    tot = jnp.dot(hi, bd, preferred_element_type=F32) + jnp.dot(lo, bd, preferred_element_type=F32)
    return z * lax.rsqrt(tot * (1.0 / HEAD_DIM) + EPS) * hw


def _load_w(i, w_ref, wb_ref):
    @pl.when(i == 0)
    def _():
        wb_ref[...] = w_ref[...].astype(BF16)


PROJ_SUB_NORM = 544
PROJ_SUB_ACT = 272


def _row_groups(n_rows, sub):
    sub = sub if n_rows % sub == 0 else n_rows
    return [slice(r0, r0 + sub) for r0 in range(0, n_rows, sub)]


def _proj_q_kernel(x_ref, w_ref, hw_ref, bd_ref, o_ref, wb_ref):
    _load_w(pl.program_id(1), w_ref, wb_ref)
    for rs in _row_groups(x_ref.shape[0], PROJ_SUB_NORM):
        z = jnp.dot(x_ref[rs, :], wb_ref[...], preferred_element_type=F32)
        for c in range(z.shape[1] // MXU_DIM):
            sl = slice(c * MXU_DIM, (c + 1) * MXU_DIM)
            o_ref[rs, sl] = _head_rms(z[:, sl], hw_ref[...], bd_ref).astype(o_ref.dtype)


def _proj_kv_kernel(x_ref, w_ref, hw_ref, bd_ref, k_ref, v_ref, wb_ref):
    _load_w(pl.program_id(1), w_ref, wb_ref)
    kw = k_ref.shape[1]
    for rs in _row_groups(x_ref.shape[0], PROJ_SUB_NORM):
        z = jnp.dot(x_ref[rs, :], wb_ref[...], preferred_element_type=F32)
        k_ref[rs, :] = _head_rms(z[:, :kw], hw_ref[...], bd_ref)
        v_ref[rs, :] = z[:, kw:]


def _proj_glu_kernel(x_ref, wa_ref, wg_ref, o_ref, wab_ref, wgb_ref):
    i = pl.program_id(1)
    _load_w(i, wa_ref, wab_ref)
    _load_w(i, wg_ref, wgb_ref)
    for rs in _row_groups(x_ref.shape[0], PROJ_SUB_ACT):
        x = x_ref[rs, :]
        a = jnp.dot(x, wab_ref[...], preferred_element_type=F32)
        g = jnp.dot(x, wgb_ref[...], preferred_element_type=F32)
        o_ref[rs, :] = a * jax.nn.sigmoid(g)


def _proj_gate_kernel(x_ref, w_ref, b_ref, o_ref, wb_ref):
    _load_w(pl.program_id(1), w_ref, wb_ref)
    for rs in _row_groups(x_ref.shape[0], PROJ_SUB_ACT):
        z = jnp.dot(x_ref[rs, :], wb_ref[...], preferred_element_type=F32)
        o_ref[rs, :] = jax.nn.sigmoid(z + b_ref[...]).astype(o_ref.dtype)


def _row_tile(t, cap=2304):
    best = 16
    for m in range(16, cap + 1, 16):
        if t % m == 0:
            best = m
    return best


def _project(xn, w_in, b_gate, q_norm_w, k_norm_w, d_model, conv_ch):
    t, d = xn.shape
    q_w = d_model
    kv_w = N_KV_HEADS * HEAD_DIM
    tn = 2 * kv_w
    tm = _row_tile(t)
    ni = t // tm
    reps = MXU_DIM // HEAD_DIM
    hq = jnp.tile(q_norm_w.astype(F32) * HEAD_DIM ** -0.5, reps).reshape(1, MXU_DIM)
    hk = jnp.tile(k_norm_w.astype(F32), reps).reshape(1, MXU_DIM)
    gid = jnp.arange(MXU_DIM) // HEAD_DIM
    bd = (gid[:, None] == gid[None, :]).astype(BF16)

    x_spec = pl.BlockSpec((tm, d), lambda s, i: (i, 0))
    small = lambda shape: pl.BlockSpec(shape, lambda s, i: (0, 0))

    def w_spec(c0):
        return pl.BlockSpec((d, tn), lambda s, i: (0, c0 + s))

    wscr = pltpu.VMEM((d, tn), BF16)
    cp = _cparams(("arbitrary", "arbitrary"))

    q = pl.pallas_call(
        _proj_q_kernel, grid=(q_w // tn, ni),
        in_specs=[x_spec, w_spec(0), small((1, MXU_DIM)), small((MXU_DIM, MXU_DIM))],
        out_specs=pl.BlockSpec((tm, tn), lambda s, i: (i, s)),
        out_shape=jax.ShapeDtypeStruct((t, q_w), BF16),
        scratch_shapes=[wscr], compiler_params=cp, name="proj_q",
    )(xn, w_in, hq, bd)

    c_kv = q_w // tn
    k, v = pl.pallas_call(
        _proj_kv_kernel, grid=(1, ni),
        in_specs=[x_spec, w_spec(c_kv), small((1, MXU_DIM)), small((MXU_DIM, MXU_DIM))],
        out_specs=[pl.BlockSpec((tm, kv_w), lambda s, i: (i, 0))] * 2,
        out_shape=[jax.ShapeDtypeStruct((t, kv_w), F32)] * 2,
        scratch_shapes=[wscr], compiler_params=cp, name="proj_kv",
    )(xn, w_in, hk, bd)

    c_a = c_kv + 1
    n_glu = conv_ch // tn
    u = pl.pallas_call(
        _proj_glu_kernel, grid=(n_glu, ni),
        in_specs=[x_spec, w_spec(c_a), w_spec(c_a + n_glu)],
        out_specs=pl.BlockSpec((tm, tn), lambda s, i: (i, s)),
        out_shape=jax.ShapeDtypeStruct((t, conv_ch), F32),
        scratch_shapes=[wscr, wscr], compiler_params=cp, name="proj_glu",
    )(xn, w_in, w_in)

    c_g = c_a + 2 * n_glu
    n_gate = 2 * d_model // tn
    gates = pl.pallas_call(
        _proj_gate_kernel, grid=(n_gate, ni),
        in_specs=[x_spec, w_spec(c_g), pl.BlockSpec((1, tn), lambda s, i: (0, s))],
        out_specs=pl.BlockSpec((tm, tn), lambda s, i: (i, s)),
        out_shape=jax.ShapeDtypeStruct((t, 2 * d_model), BF16),
        scratch_shapes=[wscr], compiler_params=cp, name="proj_gate",
    )(xn, w_in, b_gate.reshape(1, -1))
    return q, k, v, u, gates


ATTN_BLOCKS = 4


def _attn_prompt_kernel(sink_ref, q_ref, kc_ref, kp_ref, vc_ref, vp_ref, o_ref, kbuf_ref, vbuf_ref):
    blk = WINDOW
    n = pl.program_id(1)
    rows = 2 * blk
    kbuf_ref[:blk, :] = kp_ref[...]
    kbuf_ref[blk:, :] = kc_ref[...]
    vbuf_ref[:blk, :] = vp_ref[...]
    vbuf_ref[blk:, :] = vc_ref[...]
    r_iota = lax.broadcasted_iota(jnp.int32, (rows, rows), 0)
    c_iota = lax.broadcasted_iota(jnp.int32, (rows, rows), 1)
    qi = jnp.where(r_iota >= blk, r_iota - blk, r_iota)
    band = (c_iota >= qi) & (c_iota <= qi + WINDOW)
    lo_lane = lax.broadcasted_iota(jnp.int32, (blk, LANES), 1) < HEAD_DIM
    top_row = lax.broadcasted_iota(jnp.int32, (rows, 1), 0) < blk
    zero = jnp.zeros((blk, LANES), q_ref.dtype)
    group = q_ref.shape[1] // (N_KV_HEADS * LANES)

    def query_block(sb, carry):
        r0 = pl.multiple_of(sb * blk, blk)
        q_rows = pl.ds(r0, blk)
        kv_rows = pl.ds(r0, rows)
        mask = band & ((c_iota >= blk) | (n > 0) | (sb > 0))
        for g in range(N_KV_HEADS):
            hs = slice(g * HEAD_DIM, (g + 1) * HEAD_DIM)
            kg = kbuf_ref[kv_rows, hs]
            vg = vbuf_ref[kv_rows, hs]
            kdup = jnp.concatenate([kg, kg], axis=1).astype(BF16)
            vdup = jnp.concatenate([vg, vg], axis=1).astype(BF16)
            for p in range(group):
                pair = g * group + p
                ls = slice(pair * LANES, (pair + 1) * LANES)
                qp = q_ref[q_rows, ls]
                lhs = jnp.concatenate([jnp.where(lo_lane, qp, zero), jnp.where(lo_lane, zero, qp)], axis=0)
                s = lax.dot_general(lhs, kdup, (((1,), (1,)), ((), ())), preferred_element_type=F32)
                s = jnp.where(mask, s, NEG_INF)
                sink = jnp.where(top_row, sink_ref[2 * pair], sink_ref[2 * pair + 1])
                m = jnp.maximum(jnp.max(s, axis=-1, keepdims=True), sink)
                e = jnp.exp(s - m)
                denom = jnp.sum(e, axis=-1, keepdims=True) + jnp.exp(sink - m)
                o2 = jnp.dot(e.astype(BF16), vdup, preferred_element_type=F32) / denom
                o_ref[q_rows, ls] = jnp.where(lo_lane, o2[:blk], o2[blk:]).astype(o_ref.dtype)
        return carry

    lax.fori_loop(0, q_ref.shape[0] // blk, query_block, 0)


def _attn_prompt(q, k, v, sinks, bsz, slen):
    blk = WINDOW
    step = ATTN_BLOCKS * blk
    ns = slen // step
    qw, kw = q.shape[1], k.shape[1]
    cur = lambda b, n: (b * ns + n, 0)
    prev = lambda b, n: (jnp.maximum((b * ns + n) * ATTN_BLOCKS - 1, 0), 0)
    return pl.pallas_call(
        _attn_prompt_kernel,
        grid=(bsz, ns),
        in_specs=[
            pl.BlockSpec(memory_space=pltpu.SMEM),
            pl.BlockSpec((step, qw), cur),
            pl.BlockSpec((step, kw), cur), pl.BlockSpec((blk, kw), prev),
            pl.BlockSpec((step, kw), cur), pl.BlockSpec((blk, kw), prev),
        ],
        out_specs=pl.BlockSpec((step, qw), cur),
        out_shape=jax.ShapeDtypeStruct((bsz * slen, qw), BF16),
        scratch_shapes=[pltpu.VMEM((blk + step, kw), F32), pltpu.VMEM((blk + step, kw), F32)],
        compiler_params=_cparams(("arbitrary", "arbitrary")),
        name="attn_prompt",
    )(sinks.astype(F32), q, k, k, v, v)


def _attn_sample_kernel(sink_ref, q_ref, kt_ref, kn_ref, vt_ref, vn_ref, o_ref, kt_out_ref, vt_out_ref, *, qlen):
    nseq, rows, _ = q_ref.shape
    kvw, ncache = kt_ref.shape[1], kt_ref.shape[2]
    per_kv = rows // N_KV_HEADS
    all_rows = nseq * rows
    r_lane = _div_pow2(_mod_pow2(lax.broadcasted_iota(jnp.int32, (all_rows, kvw), 0), rows), per_kv)
    c_lane = _div_pow2(lax.broadcasted_iota(jnp.int32, (all_rows, kvw), 1), HEAD_DIM)
    own = r_lane == c_lane
    mask_c = (lax.broadcasted_iota(jnp.int32, (all_rows, ncache), 1)
              >= _mod_pow2(lax.broadcasted_iota(jnp.int32, (all_rows, ncache), 0), qlen))
    mask_n = (lax.broadcasted_iota(jnp.int32, (all_rows, qlen), 1)
              <= _mod_pow2(lax.broadcasted_iota(jnp.int32, (all_rows, qlen), 0), qlen))
    keep_old = lax.broadcasted_iota(jnp.int32, (kvw, ncache), 1) < ncache - qlen
    sink = sink_ref[...]
    contract_last = (((1,), (1,)), ((), ()))
    seq_rows = lambda b: slice(b * rows, (b + 1) * rows)
    new_rows = lambda b: slice(b * qlen, (b + 1) * qlen)

    q_all = q_ref[...].reshape(all_rows, HEAD_DIM)
    qe = jnp.where(own, jnp.concatenate([q_all] * N_KV_HEADS, axis=1), jnp.zeros((), q_all.dtype))
    sc = jnp.concatenate([jnp.dot(qe[seq_rows(b)], kt_ref[b].astype(BF16), preferred_element_type=F32)
                          for b in range(nseq)], axis=0)
    sn = jnp.concatenate([lax.dot_general(qe[seq_rows(b)], kn_ref[new_rows(b), :].astype(BF16), contract_last,
                                          preferred_element_type=F32) for b in range(nseq)], axis=0)
    sc = jnp.where(mask_c, sc, NEG_INF)
    sn = jnp.where(mask_n, sn, NEG_INF)
    m = jnp.maximum(jnp.maximum(jnp.max(sc, axis=-1, keepdims=True), jnp.max(sn, axis=-1, keepdims=True)), sink)
    ec = jnp.exp(sc - m)
    en = jnp.exp(sn - m)
    denom = jnp.sum(ec, axis=-1, keepdims=True) + jnp.sum(en, axis=-1, keepdims=True) + jnp.exp(sink - m)
    ec = ec.astype(BF16)
    en = en.astype(BF16)
    o = jnp.concatenate(
        [lax.dot_general(ec[seq_rows(b)], vt_ref[b].astype(BF16), contract_last, preferred_element_type=F32)
         + jnp.dot(en[seq_rows(b)], vn_ref[new_rows(b), :].astype(BF16), preferred_element_type=F32)
         for b in range(nseq)], axis=0)
    o = jnp.where(own, o, 0.0)
    acc = o[:, :HEAD_DIM]
    for g in range(1, N_KV_HEADS):
        acc = acc + o[:, g * HEAD_DIM:(g + 1) * HEAD_DIM]
    o_ref[...] = (acc / denom).astype(o_ref.dtype).reshape(nseq, rows, HEAD_DIM)

    def feature_major(new_ref):
        pad = jnp.zeros((ncache - nseq * qlen, kvw), F32)
        return jnp.concatenate([new_ref[...], pad], axis=0).T

    def shift_in(old_ref, new_ref, out_ref):
        new_t = feature_major(new_ref)
        for b in range(nseq):
            placed = pltpu.roll(new_t, ncache - qlen - b * qlen, axis=1)
            out_ref[b] = jnp.where(keep_old, pltpu.roll(old_ref[b], ncache - qlen, axis=1), placed)

    shift_in(kt_ref, kn_ref, kt_out_ref)
    shift_in(vt_ref, vn_ref, vt_out_ref)


def _attn_sample(q, k_cache, k_new, v_cache, v_new, sinks, seq_per_step=8):
    bsz, ncache = k_cache.shape[:2]
    kvw = k_new.shape[1]
    qlen = k_new.shape[0] // bsz
    assert seq_per_step * qlen <= ncache
    n_heads = q.shape[1] // HEAD_DIM
    group = n_heads // N_KV_HEADS
    rows = n_heads * qlen
    qr = q.reshape(bsz, qlen, N_KV_HEADS, group, HEAD_DIM).transpose(0, 2, 3, 1, 4).reshape(bsz, rows, HEAD_DIM)
    sink_rows = jnp.tile(jnp.repeat(sinks.astype(F32), qlen), seq_per_step).reshape(seq_per_step * rows, 1)
    feature_major = lambda c: c.transpose(0, 2, 3, 1).reshape(bsz, kvw, ncache)
    seq = lambda i: (i, 0, 0)
    cache_spec = pl.BlockSpec((seq_per_step, kvw, ncache), seq)
    new_spec = pl.BlockSpec((seq_per_step * qlen, kvw), lambda i: (i, 0))
    o, kt_next, vt_next = pl.pallas_call(
        functools.partial(_attn_sample_kernel, qlen=qlen),
        grid=(bsz // seq_per_step,),
        in_specs=[
            pl.BlockSpec((seq_per_step * rows, 1), lambda i: (0, 0)),
            pl.BlockSpec((seq_per_step, rows, HEAD_DIM), seq),
            cache_spec, new_spec, cache_spec, new_spec,
        ],
        out_specs=[pl.BlockSpec((seq_per_step, rows, HEAD_DIM), seq), cache_spec, cache_spec],
        out_shape=[jax.ShapeDtypeStruct((bsz, rows, HEAD_DIM), BF16),
                   jax.ShapeDtypeStruct((bsz, kvw, ncache), F32), jax.ShapeDtypeStruct((bsz, kvw, ncache), F32)],
        compiler_params=_cparams(("arbitrary",)),
        name="attn_sample",
    )(sink_rows, qr, feature_major(k_cache), k_new, feature_major(v_cache), v_new)
    position_major = lambda t: t.reshape(bsz, N_KV_HEADS, HEAD_DIM, ncache).transpose(0, 3, 1, 2)
    ao = o.reshape(bsz, N_KV_HEADS, group, qlen, HEAD_DIM).transpose(0, 3, 1, 2, 4).reshape(bsz * qlen, -1)
    return ao, position_major(kt_next), position_major(vt_next)


CONV_HALO = 32
CONV_ROWS = 64
CONV_COLS = 256


def _conv_prompt_kernel(uc_ref, up_ref, w_ref, b_ref, y_ref, sh_ref):
    tile = uc_ref.shape[0]
    n = pl.program_id(1)
    halo = jnp.where(n > 0, up_ref[...], 0.0)
    sh_ref[0, :CONV_HALO, :] = halo
    sh_ref[0, CONV_HALO:, :] = uc_ref[...]
    keep = tile + CONV_HALO - SUBLANES
    for r in range(1, SUBLANES):
        sh_ref[r, :keep, :] = sh_ref[0, r:r + keep, :]
    first = CONV_HALO - (CONV_WIDTH - 1)
    ch = uc_ref.shape[1]

    def rows_step(rc, carry):
        r0 = pl.multiple_of(rc * CONV_ROWS, CONV_ROWS)
        for cc in range(ch // CONV_COLS):
            cs = slice(cc * CONV_COLS, (cc + 1) * CONV_COLS)
            acc = jnp.zeros((CONV_ROWS, CONV_COLS), F32)
            for j in range(CONV_WIDTH):
                off = first + j
                a, r = off // SUBLANES, off % SUBLANES
                acc = acc + sh_ref[r, pl.ds(r0 + a * SUBLANES, CONV_ROWS), cs] * w_ref[j:j + 1, cs]
            y_ref[pl.ds(r0, CONV_ROWS), cs] = acc + b_ref[:, cs]
        return carry

    lax.fori_loop(0, tile // CONV_ROWS, rows_step, 0)


def _conv_prompt(u, w, b, bsz, slen, tile=512):
    ch = u.shape[1]
    nt = slen // tile
    per = tile // CONV_HALO
    return pl.pallas_call(
        _conv_prompt_kernel,
        grid=(bsz, nt),
        in_specs=[
            pl.BlockSpec((tile, ch), lambda bb, n: (bb * nt + n, 0)),
            pl.BlockSpec((CONV_HALO, ch), lambda bb, n: (jnp.maximum((bb * nt + n) * per - 1, 0), 0)),
            pl.BlockSpec((CONV_WIDTH, ch), lambda bb, n: (0, 0)),
            pl.BlockSpec((1, ch), lambda bb, n: (0, 0)),
        ],
        out_specs=pl.BlockSpec((tile, ch), lambda bb, n: (bb * nt + n, 0)),
        out_shape=jax.ShapeDtypeStruct((bsz * slen, ch), F32),
        scratch_shapes=[pltpu.VMEM((SUBLANES, tile + CONV_HALO, ch), F32)],
        compiler_params=_cparams(("arbitrary", "arbitrary")),
        name="conv_prompt",
    )(u, u, w, b.reshape(1, ch))


def _conv_sample_kernel(sc_ref, un_ref, w_ref, b_ref, y_ref, next_ref):
    nctx = sc_ref.shape[0]
    qlen = un_ref.shape[0]
    ch = w_ref.shape[1]

    def row(t):
        return sc_ref[t] if t < nctx else un_ref[t - nctx]

    for i in range(qlen):
        acc = jnp.zeros((sc_ref.shape[1], ch), F32)
        for j in range(CONV_WIDTH):
            acc = acc + row(i + j) * w_ref[j:j + 1, :]
        y_ref[i] = acc + b_ref[...]
    for t in range(nctx):
        next_ref[t] = row(t + qlen)


def _conv_sample(state_t, u_new_t, w, b, seq_per_step=16):
    nctx, bsz, ch = state_t.shape
    qlen = u_new_t.shape[0]
    seq = lambda i: (0, i, 0)
    return pl.pallas_call(
        _conv_sample_kernel,
        grid=(bsz // seq_per_step,),
        in_specs=[
            pl.BlockSpec((nctx, seq_per_step, ch), seq),
            pl.BlockSpec((qlen, seq_per_step, ch), seq),
            pl.BlockSpec((CONV_WIDTH, ch), lambda i: (0, 0)),
            pl.BlockSpec((1, ch), lambda i: (0, 0)),
        ],
        out_specs=[pl.BlockSpec((qlen, seq_per_step, ch), seq), pl.BlockSpec((nctx, seq_per_step, ch), seq)],
        out_shape=[jax.ShapeDtypeStruct((qlen, bsz, ch), F32), jax.ShapeDtypeStruct((nctx, bsz, ch), F32)],
        compiler_params=_cparams(("arbitrary",)),
        name="conv_sample",
    )(state_t, u_new_t, w, b.reshape(1, ch))


def _merge_kernel(y_ref, ao_ref, ga_ref, gc_ref, x_ref, lnw_ref, lnb_ref, wco_ref, wo_ref, fw_ref, wr_ref,
                  h_ref, xn_ref, lg_ref):
    y = y_ref[...]
    mu = jnp.mean(y, axis=-1, keepdims=True)
    yc = y - mu
    var = jnp.mean(yc * yc, axis=-1, keepdims=True)
    yl = yc * lax.rsqrt(var + EPS) * lnw_ref[...] + lnb_ref[...]
    act = yl * jax.nn.sigmoid(yl)
    conv_o = jnp.dot(act.astype(BF16), wco_ref[...], preferred_element_type=F32)
    mix = ga_ref[...].astype(F32) * ao_ref[...].astype(F32) + gc_ref[...].astype(F32) * conv_o
    h = x_ref[...] + jnp.dot(mix.astype(BF16), wo_ref[...], preferred_element_type=F32)
    h_ref[...] = h
    ms = jnp.mean(h * h, axis=-1, keepdims=True)
    xn = (h * lax.rsqrt(ms + EPS) * fw_ref[...]).astype(BF16)
    xn_ref[...] = xn
    lg_ref[...] = jnp.dot(xn, wr_ref[...], preferred_element_type=F32)


def _merge(y, ao, gates, row0, x, ln_w, ln_b, wco, wo, fw, wr, tile=256):
    t, d = x.shape
    ch = y.shape[1]
    off = row0 // tile
    row = lambda i: (i, 0)
    const = lambda shape: pl.BlockSpec(shape, lambda i: (0, 0), pipeline_mode=pl.Buffered(1))
    return pl.pallas_call(
        _merge_kernel,
        grid=(t // tile,),
        in_specs=[
            pl.BlockSpec((tile, ch), row),
            pl.BlockSpec((tile, d), row),
            pl.BlockSpec((tile, d), lambda i: (i + off, 0)),
            pl.BlockSpec((tile, d), lambda i: (i + off, 1)),
            pl.BlockSpec((tile, d), row),
            const((1, ch)), const((1, ch)), const((ch, d)), const((d, d)), const((1, d)),
            const((d, LANES)),
        ],
        out_specs=[pl.BlockSpec((tile, d), row), pl.BlockSpec((tile, d), row), pl.BlockSpec((tile, LANES), row)],
        out_shape=[jax.ShapeDtypeStruct((t, d), F32), jax.ShapeDtypeStruct((t, d), BF16),
                   jax.ShapeDtypeStruct((t, LANES), F32)],
        compiler_params=_cparams(("arbitrary",)),
        name="merge",
    )(y, ao, gates, gates, x, ln_w.reshape(1, ch), ln_b.reshape(1, ch), wco, wo, fw.reshape(1, d), wr)


def _route_kernel(lp_ref, ls_ref, tri_ref, lower_ref, meta_ref, metat_ref, cnt_ref, *, n_p):
    i = pl.program_id(0)
    used = _round_up(N_EXPERTS + N_GROUPS, SUBLANES)

    def body(l_ref):
        lg = l_ref[...].T[:used]
        shape = lg.shape
        row = lax.broadcasted_iota(jnp.int32, shape, 0)
        big = jnp.int32(LANES)
        is_g = (row >= N_EXPERTS) & (row < N_EXPERTS + N_GROUPS)
        gl = jnp.where(is_g, lg, -jnp.inf)
        gmax = jnp.max(gl, axis=0, keepdims=True)
        g_star = jnp.min(jnp.where(gl == gmax, row - N_EXPERTS, big), axis=0, keepdims=True)
        pg_star = 1.0 / jnp.sum(jnp.exp(gl - gmax), axis=0, keepdims=True)
        in_g = (row < N_EXPERTS) & (_div_pow2(row, EXPERTS_PER_GROUP) == g_star)
        el = jnp.where(in_g, lg, -jnp.inf)
        v1 = jnp.max(el, axis=0, keepdims=True)
        i1 = jnp.min(jnp.where(el == v1, row, big), axis=0, keepdims=True)
        el2 = jnp.where(row == i1, -jnp.inf, el)
        v2 = jnp.max(el2, axis=0, keepdims=True)
        i2 = jnp.min(jnp.where(el2 == v2, row, big), axis=0, keepdims=True)
        e2 = jnp.exp(v2 - v1)
        w1 = pg_star * (1.0 / (1.0 + e2))
        w2 = pg_star * (e2 / (1.0 + e2))
        oh1 = row == i1
        oh2 = row == i2
        cnt = jnp.where(oh1 | oh2, 1.0, 0.0)
        before = jnp.dot(cnt.astype(BF16), tri_ref[...], preferred_element_type=F32)
        tot = jnp.sum(cnt, axis=-1, keepdims=True)
        runs = jnp.floor((tot + (CHUNK_ROWS - 1)) * (1.0 / CHUNK_ROWS))
        runs_b = jnp.broadcast_to(runs, (used, LANES)).astype(BF16)
        start = jnp.dot(lower_ref[...], runs_b, preferred_element_type=F32)[:, 0:1] * CHUNK_ROWS
        slot = before + start
        s1 = jnp.sum(jnp.where(oh1, slot, 0.0), axis=0, keepdims=True)
        s2 = jnp.sum(jnp.where(oh2, slot, 0.0), axis=0, keepdims=True)
        fields = [i1.astype(F32), i2.astype(F32), w1, w2, s1, s2]
        fields += [jnp.zeros_like(w1)] * (META_FIELDS - len(fields))
        metat = jnp.concatenate(fields, axis=0)
        metat_ref[...] = metat
        pad = jnp.zeros((LANES - META_FIELDS, shape[1]), F32)
        meta_ref[...] = jnp.concatenate([metat, pad], axis=0).T
        tot_pad = jnp.concatenate([tot, jnp.zeros((LANES - used, 1), F32)], axis=0)
        cnt_ref[...] = jnp.broadcast_to(tot_pad, (LANES, LANES)).T[0:1]

    @pl.when(i < n_p)
    def _():
        body(lp_ref)

    @pl.when(i >= n_p)
    def _():
        body(ls_ref)


def _route(lg_p, lg_s):
    tile = TOKEN_BLOCK
    tp, ts = lg_p.shape[0], lg_s.shape[0]
    n_p, n_s = tp // tile, ts // tile
    nb = n_p + n_s
    used = _round_up(N_EXPERTS + N_GROUPS, SUBLANES)
    tri = (jnp.arange(tile)[:, None] < jnp.arange(tile)[None, :]).astype(BF16)
    lower = (jnp.arange(used)[:, None] > jnp.arange(used)[None, :]).astype(BF16)
    return pl.pallas_call(
        functools.partial(_route_kernel, n_p=n_p),
        grid=(nb,),
        in_specs=[
            pl.BlockSpec((tile, LANES), lambda i: (jnp.minimum(i, n_p - 1), 0)),
            pl.BlockSpec((tile, LANES), lambda i: (jnp.maximum(i - n_p, 0), 0)),
            pl.BlockSpec((tile, tile), lambda i: (0, 0)),
            pl.BlockSpec((used, used), lambda i: (0, 0)),
        ],
        out_specs=[pl.BlockSpec((tile, LANES), lambda i: (i, 0)),
                   pl.BlockSpec((None, META_FIELDS, tile), lambda i: (i, 0, 0)),
                   pl.BlockSpec((None, 1, LANES), lambda i: (i, 0, 0))],
        out_shape=[jax.ShapeDtypeStruct((tp + ts, LANES), F32),
                   jax.ShapeDtypeStruct((nb, META_FIELDS, tile), F32),
                   jax.ShapeDtypeStruct((nb, 1, LANES), F32)],
        compiler_params=_cparams(("arbitrary",)),
        name="route",
    )(lg_p, lg_s, tri, lower)


def _plan_sizes(n_tok):
    nb = n_tok // TOKEN_BLOCK
    pad_per_block = N_EXPERTS * (CHUNK_ROWS - 1)
    max_chunks = _round_up(-(-(TOKEN_BLOCK * TOP_K + pad_per_block) // CHUNK_ROWS), SUBLANES)
    n_rows = _round_up(n_tok * TOP_K + nb * pad_per_block + N_EXPERTS * (ROW_PAD - 1), ROW_PAD)
    n_items = (n_rows + N_EXPERTS * (EXPERT_ROWS - ROW_PAD)) // EXPERT_ROWS
    max_zero = N_EXPERTS * (ROW_PAD // CHUNK_ROWS)
    return nb, max_chunks, n_rows, n_items, max_zero


def _dispatch_plan(counts, max_chunks, n_items, max_zero):
    i32 = jnp.int32
    c = counts[:, 0, :N_EXPERTS].astype(i32)
    run = (c + CHUNK_ROWS - 1) // CHUNK_ROWS
    run_end = jnp.cumsum(run, axis=1)
    src_start = run_end - run
    nchunk = run_end[:, -1]
    seg = jnp.sum(run, axis=0) * CHUNK_ROWS
    padded = (seg + ROW_PAD - 1) // ROW_PAD * ROW_PAD
    pad_end = jnp.cumsum(padded)
    pad_start = pad_end - padded
    dst_start = pad_start[None, :] // CHUNK_ROWS + (jnp.cumsum(run, axis=0) - run)
    jj = jnp.arange(max_chunks, dtype=i32)[None, :, None]
    in_run = (src_start[:, None, :] <= jj) & (jj < run_end[:, None, :])
    chunk_dst = jnp.sum(jnp.where(in_run, (dst_start - src_start)[:, None, :] + jj, 0), axis=-1) * CHUNK_ROWS
    tail = (padded - seg) // CHUNK_ROWS
    tail_end = jnp.cumsum(tail)
    tail_start = tail_end - tail
    z = jnp.arange(max_zero, dtype=i32)[:, None]
    in_tail = (tail_start[None, :] <= z) & (z < tail_end[None, :])
    zero_base = (pad_start + seg) // CHUNK_ROWS - tail_start
    zero_dst = jnp.sum(jnp.where(in_tail, zero_base[None, :] + z, 0), axis=-1) * CHUNK_ROWS
    nzero = tail_end[-1]
    items_e = (padded + EXPERT_ROWS - 1) // EXPERT_ROWS
    item_end = jnp.cumsum(items_e)
    item_start = item_end - items_e
    w = jnp.arange(n_items, dtype=i32)[:, None]
    in_item = (item_start[None, :] <= w) & (w < item_end[None, :])
    k = w - item_start[None, :]
    e_ids = jnp.arange(N_EXPERTS, dtype=i32)
    last_e = jnp.max(jnp.where(items_e > 0, e_ids, 0))
    item_e = jnp.where(w[:, 0] < item_end[-1], jnp.sum(jnp.where(in_item, e_ids[None, :], 0), axis=-1), last_e)
    item_row0 = jnp.sum(jnp.where(in_item, pad_start[None, :] + k * EXPERT_ROWS, 0), axis=-1)
    item_rows = jnp.sum(jnp.where(in_item, jnp.clip(padded[None, :] - k * EXPERT_ROWS, 0, EXPERT_ROWS), 0), axis=-1)
    item_nsub = item_rows // ROW_PAD
    return (chunk_dst.reshape(-1).astype(i32), nchunk.astype(i32), zero_dst.astype(i32),
            nzero.reshape(1).astype(i32), item_e.astype(i32), item_row0.astype(i32), item_nsub.astype(i32))


SCATTER_GROUPS = 2


def _scatter_kernel(cd_ref, nc_ref, zd_ref, nz_ref, mt_ref, xp_ref, xs_ref, out_hbm, loc_ref, zero_ref, sem,
                    zero_sem, *, n_p, max_chunks):
    i = pl.program_id(0)
    buf = i % 2
    n_slots = loc_ref.shape[1]
    d = xp_ref.shape[1]
    mt = mt_ref[...]
    w1, w2 = mt[2:3, :], mt[3:4, :]
    s1, s2 = mt[4:5, :].astype(jnp.int32), mt[5:6, :].astype(jnp.int32)
    half = d // 2
    group = n_slots // SCATTER_GROUPS

    def sort_rows(x_ref):
        for s0 in range(0, n_slots, group):
            slot = lax.broadcasted_iota(jnp.int32, (group, mt.shape[1]), 0) + s0
            p1 = slot == s1
            p2 = slot == s2
            perm = jnp.where(p1 | p2, 1.0, 0.0).astype(BF16)
            wcol = jnp.sum(jnp.where(p1, w1, 0.0) + jnp.where(p2, w2, 0.0), axis=-1, keepdims=True)
            loc_ref[buf, s0:s0 + group, half:] = pltpu.bitcast(jnp.broadcast_to(wcol, (group, LANES)), jnp.uint32)
            rows = jnp.dot(perm, x_ref[...], preferred_element_type=F32)
            loc_ref[buf, s0:s0 + group, :half] = _pack_pairs(rows[:, :half], rows[:, half:])

    @pl.when(i < n_p)
    def _():
        sort_rows(xp_ref)

    @pl.when(i >= n_p)
    def _():
        sort_rows(xs_ref)

    def rows_at(r):
        return pl.ds(pl.multiple_of(r, CHUNK_ROWS), CHUNK_ROWS)

    def chunk_copy(blk, j):
        return pltpu.make_async_copy(loc_ref.at[blk % 2, rows_at(j * CHUNK_ROWS)],
                                     out_hbm.at[rows_at(cd_ref[blk * max_chunks + j])], sem.at[blk % 2])

    def zero_copy(z):
        return pltpu.make_async_copy(zero_ref, out_hbm.at[rows_at(zd_ref[z])], zero_sem)

    def loop(n, fn):
        def body(j, carry):
            fn(j)
            return carry
        lax.fori_loop(0, n, body, 0)

    @pl.when(i > 0)
    def _():
        loop(nc_ref[i - 1], lambda j: chunk_copy(i - 1, j).wait())

    loop(nc_ref[i], lambda j: chunk_copy(i, j).start())

    @pl.when(i == pl.num_programs(0) - 1)
    def _():
        zero_ref[...] = jnp.zeros_like(zero_ref)
        loop(nz_ref[0], lambda z: zero_copy(z).start())
        loop(nc_ref[i], lambda j: chunk_copy(i, j).wait())
        loop(nz_ref[0], lambda z: zero_copy(z).wait())


def _scatter_rows(xn_p, xn_s, metat, chunk_dst, nchunk, zero_dst, nzero, max_chunks, n_rows):
    tile = TOKEN_BLOCK
    tp, d = xn_p.shape
    ts = xn_s.shape[0]
    n_p, n_s = tp // tile, ts // tile
    n_slots = max_chunks * CHUNK_ROWS
    width = d // 2 + LANES
    u32 = jnp.uint32
    return pl.pallas_call(
        functools.partial(_scatter_kernel, n_p=n_p, max_chunks=max_chunks),
        grid_spec=pltpu.PrefetchScalarGridSpec(
            num_scalar_prefetch=4,
            grid=(n_p + n_s,),
            in_specs=[
                pl.BlockSpec((None, META_FIELDS, tile), lambda i, *_: (i, 0, 0)),
                pl.BlockSpec((tile, d), lambda i, *_: (jnp.minimum(i, n_p - 1), 0)),
                pl.BlockSpec((tile, d), lambda i, *_: (jnp.maximum(i - n_p, 0), 0)),
            ],
            out_specs=pl.BlockSpec(memory_space=pl.ANY),
            scratch_shapes=[pltpu.VMEM((2, n_slots, width), u32), pltpu.VMEM((CHUNK_ROWS, width), u32),
                            pltpu.SemaphoreType.DMA((2,)), pltpu.SemaphoreType.DMA(())],
        ),
        out_shape=jax.ShapeDtypeStruct((n_rows, width), u32),
        compiler_params=_cparams(("arbitrary",)),
        name="scatter_rows",
    )(chunk_dst, nchunk, zero_dst, nzero, metat, xn_p, xn_s)


def _expert_kernel(e_ref, row0_ref, nsub_ref, xs_hbm, wg_ref, wu_ref, wd_ref, out_hbm,
                   xf_ref, xb_ref, acc_ref, ob_ref, pend_ref, in_sem, out_sem, *, n_chunks):
    w = pl.program_id(0)
    c = pl.program_id(1)
    n_items = pl.num_programs(0)
    last_c = pl.num_programs(1) - 1
    nsub = nsub_ref[w]
    d = acc_ref.shape[1]
    half = d // 2
    slot = w % 2

    def sub_rows(s):
        return pl.ds(pl.multiple_of(s * ROW_PAD, ROW_PAD), ROW_PAD)

    def hbm_rows(item, s):
        return pl.ds(pl.multiple_of(row0_ref[item] + s * ROW_PAD, ROW_PAD), ROW_PAD)

    def in_copy(item, s):
        return pltpu.make_async_copy(xs_hbm.at[hbm_rows(item, s)], xf_ref.at[item % 2, sub_rows(s)],
                                     in_sem.at[item % 2])

    def out_copy(item, s):
        return pltpu.make_async_copy(ob_ref.at[sub_rows(s)], out_hbm.at[hbm_rows(item, s)], out_sem)

    def loop(n, fn):
        def body(s, carry):
            fn(s)
            return carry
        lax.fori_loop(0, n, body, 0)

    @pl.when(c == 0)
    def _():
        @pl.when(w == 0)
        def _():
            pend_ref[0] = 0
            pend_ref[1] = 0
            loop(nsub, lambda s: in_copy(w, s).start())

        loop(nsub, lambda s: in_copy(w, s).wait())
        nxt = jnp.minimum(w + 1, n_items - 1)

        @pl.when(w + 1 < n_items)
        def _():
            loop(nsub_ref[nxt], lambda s: in_copy(nxt, s).start())

    def drain_out():
        prev = pend_ref[1]
        loop(pend_ref[0], lambda s: out_copy(prev, s).wait())
        pend_ref[0] = 0

    def ffn_chunk(x):
        g = jnp.dot(x, wg_ref[...].astype(BF16), preferred_element_type=F32)
        u = jnp.dot(x, wu_ref[...].astype(BF16), preferred_element_type=F32)
        hmid = (g * jax.nn.sigmoid(g) * u).astype(BF16)
        return jnp.dot(hmid, wd_ref[...].astype(BF16), preferred_element_type=F32)

    for n in range(1, EXPERT_ROWS // ROW_PAD + 1):
        m = n * ROW_PAD

        @pl.when((nsub == n) & (c == 0))
        def _(m=m):
            xa, xb = _unpack_pairs(xf_ref[slot, :m, :half])
            x = jnp.concatenate([xa, xb], axis=1).astype(BF16)
            xb_ref[:m, :] = x
            acc_ref[:m, :] = ffn_chunk(x)

        if n_chunks > 2:
            @pl.when((nsub == n) & (c > 0) & (c < last_c))
            def _(m=m):
                acc_ref[:m, :] += ffn_chunk(xb_ref[:m, :])

        @pl.when((nsub == n) & (c == last_c))
        def _(m=m):
            drain_out()
            mix_w = pltpu.bitcast(xf_ref[slot, :m, half:half + 1], F32)
            y = ((acc_ref[:m, :] + ffn_chunk(xb_ref[:m, :])) * mix_w).astype(BF16).astype(F32)
            ob_ref[:m, :] = _pack_pairs(y[:, :half], y[:, half:])
            loop(nsub, lambda s: out_copy(w, s).start())
            pend_ref[0] = nsub
            pend_ref[1] = w

    @pl.when((w == n_items - 1) & (c == last_c))
    def _():
        drain_out()


def _experts(xs, item_e, item_row0, item_nsub, w_gate, w_up, w_down):
    n_rows, width = xs.shape
    d = w_gate.shape[1]
    assert width == d // 2 + LANES
    d_exp = w_gate.shape[2]
    nc = d_exp // D_CHUNK
    assert nc >= 2
    n_items = item_e.shape[0]

    def chunk_of(w, c, nsub_ref):
        return jnp.where(nsub_ref[w] > 0, c, nc - 1)

    return pl.pallas_call(
        functools.partial(_expert_kernel, n_chunks=nc),
        grid_spec=pltpu.PrefetchScalarGridSpec(
            num_scalar_prefetch=3,
            grid=(n_items, nc),
            in_specs=[
                pl.BlockSpec(memory_space=pl.ANY),
                pl.BlockSpec((None, d, D_CHUNK), lambda w, c, e, r, n: (e[w], 0, chunk_of(w, c, n))),
                pl.BlockSpec((None, d, D_CHUNK), lambda w, c, e, r, n: (e[w], 0, chunk_of(w, c, n))),
                pl.BlockSpec((None, D_CHUNK, d), lambda w, c, e, r, n: (e[w], chunk_of(w, c, n), 0)),
            ],
            out_specs=pl.BlockSpec(memory_space=pl.ANY),
            scratch_shapes=[
                pltpu.VMEM((2, EXPERT_ROWS, width), jnp.uint32),
                pltpu.VMEM((EXPERT_ROWS, d), BF16),
                pltpu.VMEM((EXPERT_ROWS, d), F32),
                pltpu.VMEM((EXPERT_ROWS, d // 2), jnp.uint32),
                pltpu.SMEM((2,), jnp.int32),
                pltpu.SemaphoreType.DMA((2,)),
                pltpu.SemaphoreType.DMA(()),
            ],
        ),
        out_shape=jax.ShapeDtypeStruct((n_rows, d // 2), jnp.uint32),
        compiler_params=_cparams(("arbitrary", "arbitrary")),
        name="experts",
    )(item_e, item_row0, item_nsub, xs, w_gate, w_up, w_down)


COMBINE_SPLIT = 2
SLOT_STEP = 256


def _combine_kernel(cd_ref, nc_ref, meta_ref, hp_ref, hs_ref, yb_hbm, yp_ref, ys_ref, loc_ref, sem,
                    *, n_p, max_chunks):
    i = pl.program_id(0)
    j = pl.program_id(1)
    nb = pl.num_programs(0)
    buf = i % 2
    n_slots, half = loc_ref.shape[1], loc_ref.shape[2]
    d = 2 * half

    def rows_at(r):
        return pl.ds(pl.multiple_of(r, CHUNK_ROWS), CHUNK_ROWS)

    def chunk_copy(blk, c):
        return pltpu.make_async_copy(yb_hbm.at[rows_at(cd_ref[blk * max_chunks + c])],
                                     loc_ref.at[blk % 2, rows_at(c * CHUNK_ROWS)], sem.at[blk % 2])

    def fetch(blk):
        n = nc_ref[blk]

        def start(c, carry):
            chunk_copy(blk, c).start()
            return carry
        lax.fori_loop(0, n, start, 0)

        def clear(c, carry):
            loc_ref[blk % 2, rows_at(c * CHUNK_ROWS), :] = jnp.zeros((CHUNK_ROWS, half), jnp.uint32)
            return carry
        lax.fori_loop(n, max_chunks, clear, 0)

    @pl.when(j == 0)
    def _():
        @pl.when(i == 0)
        def _():
            fetch(i)

        def wait(c, carry):
            chunk_copy(i, c).wait()
            return carry
        lax.fori_loop(0, nc_ref[i], wait, 0)

        @pl.when(i + 1 < nb)
        def _():
            fetch(jnp.minimum(i + 1, nb - 1))

    meta = meta_ref[...]
    s1 = meta[:, 4:5].astype(jnp.int32)
    s2 = meta[:, 5:6].astype(jnp.int32)
    rows = meta.shape[0]
    f = jnp.zeros((rows, d), F32)
    for r0 in range(0, n_slots, SLOT_STEP):
        slot = lax.broadcasted_iota(jnp.int32, (rows, SLOT_STEP), 1) + r0
        pt = jnp.where((slot == s1) | (slot == s2), 1.0, 0.0).astype(BF16)
        ya, yb = _unpack_pairs(loc_ref[buf, r0:r0 + SLOT_STEP, :])
        f = f + jnp.dot(pt, jnp.concatenate([ya, yb], axis=1).astype(BF16), preferred_element_type=F32)
    g = i * pl.num_programs(1) + j

    @pl.when(g < n_p)
    def _():
        yp_ref[...] = hp_ref[...] + f

    @pl.when(g >= n_p)
    def _():
        ys_ref[...] = hs_ref[...] + f


def _combine(h_p, h_s, meta, chunk_dst, nchunk, yb, max_chunks):
    tp, d = h_p.shape
    ts = h_s.shape[0]
    tile = TOKEN_BLOCK // COMBINE_SPLIT
    n_p, n_s = tp // tile, ts // tile
    nb = (tp + ts) // TOKEN_BLOCK
    n_slots = max_chunks * CHUNK_ROWS
    assert n_slots % SLOT_STEP == 0
    blk = lambda i, j: i * COMBINE_SPLIT + j
    p_idx = lambda i, j, *_: (jnp.minimum(blk(i, j), n_p - 1), 0)
    s_idx = lambda i, j, *_: (jnp.maximum(blk(i, j) - n_p, 0), 0)
    return pl.pallas_call(
        functools.partial(_combine_kernel, n_p=n_p, max_chunks=max_chunks),
        grid_spec=pltpu.PrefetchScalarGridSpec(
            num_scalar_prefetch=2,
            grid=(nb, COMBINE_SPLIT),
            in_specs=[
                pl.BlockSpec((tile, LANES), lambda i, j, *_: (blk(i, j), 0)),
                pl.BlockSpec((tile, d), p_idx),
                pl.BlockSpec((tile, d), s_idx),
                pl.BlockSpec(memory_space=pl.ANY),
            ],
            out_specs=[pl.BlockSpec((tile, d), p_idx), pl.BlockSpec((tile, d), s_idx)],
            scratch_shapes=[pltpu.VMEM((2, n_slots, d // 2), jnp.uint32), pltpu.SemaphoreType.DMA((2,))],
        ),
        out_shape=[jax.ShapeDtypeStruct((tp, d), F32), jax.ShapeDtypeStruct((ts, d), F32)],
        compiler_params=_cparams(("arbitrary", "arbitrary")),
        name="combine",
    )(chunk_dst, nchunk, meta, h_p, h_s, yb)


def _layer(x_p, x_s, ck, cv, sc, attn_norm_w, w_in, b_gate, q_norm_w, k_norm_w, sinks, conv_dw_w, conv_dw_b,
           conv_ln_w, conv_ln_b, w_conv_out, w_out, ffn_norm_w, router_group, router_expert, w_gate, w_up,
           w_down):
    bsz, slen, d = x_p.shape
    dbs, dlen, _ = x_s.shape
    tp, ts = bsz * slen, dbs * dlen
    conv_ch = conv_dw_w.shape[1]
    kvw = N_KV_HEADS * HEAD_DIM
    xp2, xs2 = x_p.reshape(tp, d), x_s.reshape(ts, d)

    xn = _prenorm(xp2, xs2, attn_norm_w)
    q, k, v, u, gates = _project(xn, w_in, b_gate, q_norm_w, k_norm_w, d, conv_ch)

    ao_p = _attn_prompt(q, k, v, sinks, bsz, slen)
    ao_s, ck_next, cv_next = _attn_sample(q[tp:], ck, k[tp:], cv, v[tp:], sinks)

    y_p = _conv_prompt(u, conv_dw_w, conv_dw_b, bsz, slen)
    sc_t = sc.transpose(1, 0, 2)
    u_s_t = u[tp:].reshape(dbs, dlen, conv_ch).transpose(1, 0, 2)
    y_s_t, sc_next_t = _conv_sample(sc_t, u_s_t, conv_dw_w, conv_dw_b)
    y_s = y_s_t.transpose(1, 0, 2).reshape(ts, conv_ch)

    wco = w_conv_out.astype(BF16)
    wo = w_out.astype(BF16)
    wr = jnp.concatenate([router_expert, router_group,
                          jnp.zeros((d, LANES - N_EXPERTS - N_GROUPS), router_expert.dtype)], axis=1).astype(BF16)
    margs = (conv_ln_w, conv_ln_b, wco, wo, ffn_norm_w, wr)
    h_p, xn_p, lg_p = _merge(y_p, ao_p, gates, 0, xp2, *margs)
    h_s, xn_s, lg_s = _merge(y_s, ao_s, gates, tp, xs2, *margs)

    meta, metat, counts = _route(lg_p, lg_s)
    _, max_chunks, n_rows, n_items, max_zero = _plan_sizes(tp + ts)
    chunk_dst, nchunk, zero_dst, nzero, item_e, item_row0, item_nsub = _dispatch_plan(
        counts, max_chunks, n_items, max_zero)
    xs_rows = _scatter_rows(xn_p, xn_s, metat, chunk_dst, nchunk, zero_dst, nzero, max_chunks, n_rows)
    yb = _experts(xs_rows, item_e, item_row0, item_nsub, w_gate, w_up, w_down)
    out_p, out_s = _combine(h_p, h_s, meta, chunk_dst, nchunk, yb, max_chunks)

    def seq_tails(a, n):
        return jnp.stack([lax.slice_in_dim(a, (b + 1) * slen - n, (b + 1) * slen, axis=0) for b in range(bsz)])

    new_p = (seq_tails(k, WINDOW).reshape(bsz, WINDOW, N_KV_HEADS, HEAD_DIM),
             seq_tails(v, WINDOW).reshape(bsz, WINDOW, N_KV_HEADS, HEAD_DIM),
             seq_tails(u, CONV_WIDTH - 1))
    new_s = (ck_next, cv_next, sc_next_t.transpose(1, 0, 2))
    return out_p.reshape(bsz, slen, d), out_s.reshape(dbs, dlen, d), new_p, new_s


def kernel(x_prompt, x_sample, cache_k, cache_v, state_conv, attn_norm_w, w_in, b_gate, q_norm_w, k_norm_w,
           sinks, conv_dw_w, conv_dw_b, conv_ln_w, conv_ln_b, w_conv_out, w_out, ffn_norm_w, router_group,
           router_expert, w_gate, w_up, w_down):
    x_p, x_s = x_prompt, x_sample
    kp, vp, cp, ks_, vs_, cs_ = [], [], [], [], [], []
    for l in range(cache_k.shape[0]):
        x_p, x_s, new_p, new_s = _layer(
            x_p, x_s, cache_k[l], cache_v[l], state_conv[l], attn_norm_w[l], w_in[l], b_gate[l], q_norm_w[l],
            k_norm_w[l], sinks[l], conv_dw_w[l], conv_dw_b[l], conv_ln_w[l], conv_ln_b[l], w_conv_out[l],
            w_out[l], ffn_norm_w[l], router_group[l], router_expert[l], w_gate[l], w_up[l], w_down[l])
        kp.append(new_p[0]); vp.append(new_p[1]); cp.append(new_p[2])
        ks_.append(new_s[0]); vs_.append(new_s[1]); cs_.append(new_s[2])
    return (x_p, x_s, jnp.stack(kp), jnp.stack(vp), jnp.stack(cp),
            jnp.stack(ks_), jnp.stack(vs_), jnp.stack(cs_))
```

```python
import functools

import jax
import jax.numpy as jnp
from jax import lax
from jax.experimental import pallas as pl
from jax.experimental.pallas import tpu as pltpu

F32 = jnp.float32
BF16 = jnp.bfloat16
EPS = 1e-6
NEG_INF = -1e30

HEAD_DIM = 64
N_KV_HEADS = 4
WINDOW = 128
CONV_WIDTH = 31
N_GROUPS = 4
EXPERTS_PER_GROUP = 8
N_EXPERTS = N_GROUPS * EXPERTS_PER_GROUP
TOP_K = 2

LANES = 128
SUBLANES = 8
MXU_DIM = 256
VMEM_LIMIT = 56 * 1024 * 1024

TOKEN_BLOCK = 512
CHUNK_ROWS = SUBLANES
ROW_PAD = 128
EXPERT_ROWS = 768
D_CHUNK = 512
META_FIELDS = 8


def _cparams(sem):
    return pltpu.CompilerParams(dimension_semantics=sem, vmem_limit_bytes=VMEM_LIMIT)


def _log2(n):
    assert n > 0 and n & (n - 1) == 0, n
    return n.bit_length() - 1


def _div_pow2(x, n):
    return lax.shift_right_logical(x, jnp.int32(_log2(n)))


def _mod_pow2(x, n):
    _log2(n)
    return x & jnp.int32(n - 1)


def _round_up(x, m):
    return (x + m - 1) // m * m


DMA_UNROLL = 8


def _for_each(n, fn):
    groups = lax.shift_right_logical(n, jnp.int32(_log2(DMA_UNROLL)))

    def group(g, carry):
        for u in range(DMA_UNROLL):
            fn(g * DMA_UNROLL + u)
        return carry
    lax.fori_loop(0, groups, group, 0)

    def single(j, carry):
        fn(j)
        return carry
    lax.fori_loop(groups * DMA_UNROLL, n, single, 0)


HIGH_HALF = 0xFFFF0000


def _pack_pairs(a, b):
    ua = pltpu.bitcast(a, jnp.uint32) & jnp.uint32(HIGH_HALF)
    return ua | (pltpu.bitcast(b, jnp.uint32) >> 16)


def _unpack_pairs(w):
    return (pltpu.bitcast(w & jnp.uint32(HIGH_HALF), F32), pltpu.bitcast(w << 16, F32))


def _prenorm_kernel(xp_ref, xs_ref, w_ref, o_ref, *, n_p):
    i = pl.program_id(0)

    def body(x_ref):
        x = x_ref[...]
        ms = jnp.mean(x * x, axis=-1, keepdims=True)
        o_ref[...] = (x * lax.rsqrt(ms + EPS) * w_ref[...]).astype(o_ref.dtype)

    @pl.when(i < n_p)
    def _():
        body(xp_ref)

    @pl.when(i >= n_p)
    def _():
        body(xs_ref)


def _prenorm(xp, xs, w, tile=512):
    tp, d = xp.shape
    ts = xs.shape[0]
    n_p, n_s = tp // tile, ts // tile
    return pl.pallas_call(
        functools.partial(_prenorm_kernel, n_p=n_p),
        grid=(n_p + n_s,),
        in_specs=[
            pl.BlockSpec((tile, d), lambda i: (jnp.minimum(i, n_p - 1), 0)),
            pl.BlockSpec((tile, d), lambda i: (jnp.maximum(i - n_p, 0), 0)),
            pl.BlockSpec((1, d), lambda i: (0, 0)),
        ],
        out_specs=pl.BlockSpec((tile, d), lambda i: (i, 0)),
        out_shape=jax.ShapeDtypeStruct((tp + ts, d), BF16),
        compiler_params=_cparams(("arbitrary",)),
        name="prenorm",
    )(xp, xs, w.reshape(1, d))


def _head_rms(z, hw, bd_ref):
    ss = z * z
    hi = ss.astype(BF16)
    lo = (ss - hi.astype(F32)).astype(BF16)
    bd = bd_ref[...]
    tot = jnp.dot(hi, bd, preferred_element_type=F32) + jnp.dot(lo, bd, preferred_element_type=F32)
    return z * lax.rsqrt(tot * (1.0 / HEAD_DIM) + EPS) * hw


def _load_w(i, w_ref, wb_ref):
    @pl.when(i == 0)
    def _():
        wb_ref[...] = w_ref[...].astype(BF16)


PROJ_SUB_NORM = 544
PROJ_SUB_ACT = 272


def _row_groups(n_rows, sub):
    sub = sub if n_rows % sub == 0 else n_rows
    return [slice(r0, r0 + sub) for r0 in range(0, n_rows, sub)]


def _proj_q_kernel(x_ref, w_ref, hw_ref, bd_ref, o_ref, wb_ref):
    _load_w(pl.program_id(1), w_ref, wb_ref)
    for rs in _row_groups(x_ref.shape[0], PROJ_SUB_NORM):
        z = jnp.dot(x_ref[rs, :], wb_ref[...], preferred_element_type=F32)
        for c in range(z.shape[1] // MXU_DIM):
            sl = slice(c * MXU_DIM, (c + 1) * MXU_DIM)
            o_ref[rs, sl] = _head_rms(z[:, sl], hw_ref[...], bd_ref).astype(o_ref.dtype)


def _proj_kv_kernel(x_ref, w_ref, hw_ref, bd_ref, k_ref, v_ref, wb_ref):
    _load_w(pl.program_id(1), w_ref, wb_ref)
    kw = k_ref.shape[1]
    for rs in _row_groups(x_ref.shape[0], PROJ_SUB_NORM):
        z = jnp.dot(x_ref[rs, :], wb_ref[...], preferred_element_type=F32)
        k_ref[rs, :] = _head_rms(z[:, :kw], hw_ref[...], bd_ref)
        v_ref[rs, :] = z[:, kw:]


def _proj_glu_kernel(x_ref, wa_ref, wg_ref, o_ref, wab_ref, wgb_ref):
    i = pl.program_id(1)
    _load_w(i, wa_ref, wab_ref)
    _load_w(i, wg_ref, wgb_ref)
    for rs in _row_groups(x_ref.shape[0], PROJ_SUB_ACT):
        x = x_ref[rs, :]
        a = jnp.dot(x, wab_ref[...], preferred_element_type=F32)
        g = jnp.dot(x, wgb_ref[...], preferred_element_type=F32)
        o_ref[rs, :] = a * jax.nn.sigmoid(g)


def _proj_gate_kernel(x_ref, w_ref, b_ref, o_ref, wb_ref):
    _load_w(pl.program_id(1), w_ref, wb_ref)
    for rs in _row_groups(x_ref.shape[0], PROJ_SUB_ACT):
        z = jnp.dot(x_ref[rs, :], wb_ref[...], preferred_element_type=F32)
        o_ref[rs, :] = jax.nn.sigmoid(z + b_ref[...]).astype(o_ref.dtype)


def _row_tile(t, cap=2304):
    best = 16
    for m in range(16, cap + 1, 16):
        if t % m == 0:
            best = m
    return best


def _project(xn, w_in, b_gate, q_norm_w, k_norm_w, d_model, conv_ch):
    t, d = xn.shape
    q_w = d_model
    kv_w = N_KV_HEADS * HEAD_DIM
    tn = 2 * kv_w
    tm = _row_tile(t)
    ni = t // tm
    reps = MXU_DIM // HEAD_DIM
    hq = jnp.tile(q_norm_w.astype(F32) * HEAD_DIM ** -0.5, reps).reshape(1, MXU_DIM)
    hk = jnp.tile(k_norm_w.astype(F32), reps).reshape(1, MXU_DIM)
    gid = jnp.arange(MXU_DIM) // HEAD_DIM
    bd = (gid[:, None] == gid[None, :]).astype(BF16)

    x_spec = pl.BlockSpec((tm, d), lambda s, i: (i, 0))
    small = lambda shape: pl.BlockSpec(shape, lambda s, i: (0, 0))

    def w_spec(c0):
        return pl.BlockSpec((d, tn), lambda s, i: (0, c0 + s))

    wscr = pltpu.VMEM((d, tn), BF16)
    cp = _cparams(("arbitrary", "arbitrary"))

    q = pl.pallas_call(
        _proj_q_kernel, grid=(q_w // tn, ni),
        in_specs=[x_spec, w_spec(0), small((1, MXU_DIM)), small((MXU_DIM, MXU_DIM))],
        out_specs=pl.BlockSpec((tm, tn), lambda s, i: (i, s)),
        out_shape=jax.ShapeDtypeStruct((t, q_w), BF16),
        scratch_shapes=[wscr], compiler_params=cp, name="proj_q",
    )(xn, w_in, hq, bd)

    c_kv = q_w // tn
    k, v = pl.pallas_call(
        _proj_kv_kernel, grid=(1, ni),
        in_specs=[x_spec, w_spec(c_kv), small((1, MXU_DIM)), small((MXU_DIM, MXU_DIM))],
        out_specs=[pl.BlockSpec((tm, kv_w), lambda s, i: (i, 0))] * 2,
        out_shape=[jax.ShapeDtypeStruct((t, kv_w), F32)] * 2,
        scratch_shapes=[wscr], compiler_params=cp, name="proj_kv",
    )(xn, w_in, hk, bd)

    c_a = c_kv + 1
    n_glu = conv_ch // tn
    u = pl.pallas_call(
        _proj_glu_kernel, grid=(n_glu, ni),
        in_specs=[x_spec, w_spec(c_a), w_spec(c_a + n_glu)],
        out_specs=pl.BlockSpec((tm, tn), lambda s, i: (i, s)),
        out_shape=jax.ShapeDtypeStruct((t, conv_ch), F32),
        scratch_shapes=[wscr, wscr], compiler_params=cp, name="proj_glu",
    )(xn, w_in, w_in)

    c_g = c_a + 2 * n_glu
    n_gate = 2 * d_model // tn
    gates = pl.pallas_call(
        _proj_gate_kernel, grid=(n_gate, ni),
        in_specs=[x_spec, w_spec(c_g), pl.BlockSpec((1, tn), lambda s, i: (0, s))],
        out_specs=pl.BlockSpec((tm, tn), lambda s, i: (i, s)),
        out_shape=jax.ShapeDtypeStruct((t, 2 * d_model), BF16),
        scratch_shapes=[wscr], compiler_params=cp, name="proj_gate",
    )(xn, w_in, b_gate.reshape(1, -1))
    return q, k, v, u, gates


ATTN_BLOCKS = 4


def _attn_prompt_kernel(sink_ref, q_ref, kc_ref, kp_ref, vc_ref, vp_ref, o_ref, kbuf_ref, vbuf_ref):
    blk = WINDOW
    n = pl.program_id(1)
    rows = 2 * blk
    kbuf_ref[:blk, :] = kp_ref[...]
    kbuf_ref[blk:, :] = kc_ref[...]
    vbuf_ref[:blk, :] = vp_ref[...]
    vbuf_ref[blk:, :] = vc_ref[...]
    r_iota = lax.broadcasted_iota(jnp.int32, (rows, rows), 0)
    c_iota = lax.broadcasted_iota(jnp.int32, (rows, rows), 1)
    qi = jnp.where(r_iota >= blk, r_iota - blk, r_iota)
    band = (c_iota >= qi) & (c_iota <= qi + WINDOW)
    lo_lane = lax.broadcasted_iota(jnp.int32, (blk, LANES), 1) < HEAD_DIM
    top_row = lax.broadcasted_iota(jnp.int32, (rows, 1), 0) < blk
    zero = jnp.zeros((blk, LANES), q_ref.dtype)
    group = q_ref.shape[1] // (N_KV_HEADS * LANES)

    def query_block(sb, carry):
        r0 = pl.multiple_of(sb * blk, blk)
        q_rows = pl.ds(r0, blk)
        kv_rows = pl.ds(r0, rows)
        mask = band & ((c_iota >= blk) | (n > 0) | (sb > 0))
        for g in range(N_KV_HEADS):
            hs = slice(g * HEAD_DIM, (g + 1) * HEAD_DIM)
            kg = kbuf_ref[kv_rows, hs]
            vg = vbuf_ref[kv_rows, hs]
            kdup = jnp.concatenate([kg, kg], axis=1).astype(BF16)
            vdup = jnp.concatenate([vg, vg], axis=1).astype(BF16)
            for p in range(group):
                pair = g * group + p
                ls = slice(pair * LANES, (pair + 1) * LANES)
                qp = q_ref[q_rows, ls]
                lhs = jnp.concatenate([jnp.where(lo_lane, qp, zero), jnp.where(lo_lane, zero, qp)], axis=0)
                s = lax.dot_general(lhs, kdup, (((1,), (1,)), ((), ())), preferred_element_type=F32)
                s = jnp.where(mask, s, NEG_INF)
                sink = jnp.where(top_row, sink_ref[2 * pair], sink_ref[2 * pair + 1])
                m = jnp.maximum(jnp.max(s, axis=-1, keepdims=True), sink)
                e = jnp.exp(s - m)
                denom = jnp.sum(e, axis=-1, keepdims=True) + jnp.exp(sink - m)
                o2 = jnp.dot(e.astype(BF16), vdup, preferred_element_type=F32) / denom
                o_ref[q_rows, ls] = jnp.where(lo_lane, o2[:blk], o2[blk:]).astype(o_ref.dtype)
        return carry

    lax.fori_loop(0, q_ref.shape[0] // blk, query_block, 0)


def _attn_prompt(q, k, v, sinks, bsz, slen):
    blk = WINDOW
    step = ATTN_BLOCKS * blk
    ns = slen // step
    qw, kw = q.shape[1], k.shape[1]
    cur = lambda b, n: (b * ns + n, 0)
    prev = lambda b, n: (jnp.maximum((b * ns + n) * ATTN_BLOCKS - 1, 0), 0)
    return pl.pallas_call(
        _attn_prompt_kernel,
        grid=(bsz, ns),
        in_specs=[
            pl.BlockSpec(memory_space=pltpu.SMEM),
            pl.BlockSpec((step, qw), cur),
            pl.BlockSpec((step, kw), cur), pl.BlockSpec((blk, kw), prev),
            pl.BlockSpec((step, kw), cur), pl.BlockSpec((blk, kw), prev),
        ],
        out_specs=pl.BlockSpec((step, qw), cur),
        out_shape=jax.ShapeDtypeStruct((bsz * slen, qw), BF16),
        scratch_shapes=[pltpu.VMEM((blk + step, kw), F32), pltpu.VMEM((blk + step, kw), F32)],
        compiler_params=_cparams(("arbitrary", "arbitrary")),
        name="attn_prompt",
    )(sinks.astype(F32), q, k, k, v, v)


def _attn_sample_kernel(sink_ref, q_ref, kt_ref, kn_ref, vt_ref, vn_ref, o_ref, kt_out_ref, vt_out_ref, *, qlen):
    nseq, rows, _ = q_ref.shape
    kvw, ncache = kt_ref.shape[1], kt_ref.shape[2]
    per_kv = rows // N_KV_HEADS
    all_rows = nseq * rows
    r_lane = _div_pow2(_mod_pow2(lax.broadcasted_iota(jnp.int32, (all_rows, kvw), 0), rows), per_kv)
    c_lane = _div_pow2(lax.broadcasted_iota(jnp.int32, (all_rows, kvw), 1), HEAD_DIM)
    own = r_lane == c_lane
    mask_c = (lax.broadcasted_iota(jnp.int32, (all_rows, ncache), 1)
              >= _mod_pow2(lax.broadcasted_iota(jnp.int32, (all_rows, ncache), 0), qlen))
    mask_n = (lax.broadcasted_iota(jnp.int32, (all_rows, qlen), 1)
              <= _mod_pow2(lax.broadcasted_iota(jnp.int32, (all_rows, qlen), 0), qlen))
    keep_old = lax.broadcasted_iota(jnp.int32, (kvw, ncache), 1) < ncache - qlen
    sink = sink_ref[...]
    contract_last = (((1,), (1,)), ((), ()))
    seq_rows = lambda b: slice(b * rows, (b + 1) * rows)
    new_rows = lambda b: slice(b * qlen, (b + 1) * qlen)

    q_all = q_ref[...].reshape(all_rows, HEAD_DIM)
    qe = jnp.where(own, jnp.concatenate([q_all] * N_KV_HEADS, axis=1), jnp.zeros((), q_all.dtype))
    sc = jnp.concatenate([jnp.dot(qe[seq_rows(b)], kt_ref[b].astype(BF16), preferred_element_type=F32)
                          for b in range(nseq)], axis=0)
    sn = jnp.concatenate([lax.dot_general(qe[seq_rows(b)], kn_ref[new_rows(b), :].astype(BF16), contract_last,
                                          preferred_element_type=F32) for b in range(nseq)], axis=0)
    sc = jnp.where(mask_c, sc, NEG_INF)
    sn = jnp.where(mask_n, sn, NEG_INF)
    m = jnp.maximum(jnp.maximum(jnp.max(sc, axis=-1, keepdims=True), jnp.max(sn, axis=-1, keepdims=True)), sink)
    ec = jnp.exp(sc - m)
    en = jnp.exp(sn - m)
    denom = jnp.sum(ec, axis=-1, keepdims=True) + jnp.sum(en, axis=-1, keepdims=True) + jnp.exp(sink - m)
    ec = ec.astype(BF16)
    en = en.astype(BF16)
    o = jnp.concatenate(
        [lax.dot_general(ec[seq_rows(b)], vt_ref[b].astype(BF16), contract_last, preferred_element_type=F32)
         + jnp.dot(en[seq_rows(b)], vn_ref[new_rows(b), :].astype(BF16), preferred_element_type=F32)
         for b in range(nseq)], axis=0)
    o = jnp.where(own, o, 0.0)
    acc = o[:, :HEAD_DIM]
    for g in range(1, N_KV_HEADS):
        acc = acc + o[:, g * HEAD_DIM:(g + 1) * HEAD_DIM]
    o_ref[...] = (acc / denom).astype(o_ref.dtype).reshape(nseq, rows, HEAD_DIM)

    def feature_major(new_ref):
        pad = jnp.zeros((ncache - nseq * qlen, kvw), F32)
        return jnp.concatenate([new_ref[...], pad], axis=0).T

    def shift_in(old_ref, new_ref, out_ref):
        new_t = feature_major(new_ref)
        for b in range(nseq):
            placed = pltpu.roll(new_t, ncache - qlen - b * qlen, axis=1)
            out_ref[b] = jnp.where(keep_old, pltpu.roll(old_ref[b], ncache - qlen, axis=1), placed)

    shift_in(kt_ref, kn_ref, kt_out_ref)
    shift_in(vt_ref, vn_ref, vt_out_ref)


def _attn_sample(q, k_cache, k_new, v_cache, v_new, sinks, seq_per_step=8):
    bsz, ncache = k_cache.shape[:2]
    kvw = k_new.shape[1]
    qlen = k_new.shape[0] // bsz
    assert seq_per_step * qlen <= ncache
    n_heads = q.shape[1] // HEAD_DIM
    group = n_heads // N_KV_HEADS
    rows = n_heads * qlen
    qr = q.reshape(bsz, qlen, N_KV_HEADS, group, HEAD_DIM).transpose(0, 2, 3, 1, 4).reshape(bsz, rows, HEAD_DIM)
    sink_rows = jnp.tile(jnp.repeat(sinks.astype(F32), qlen), seq_per_step).reshape(seq_per_step * rows, 1)
    feature_major = lambda c: c.transpose(0, 2, 3, 1).reshape(bsz, kvw, ncache)
    seq = lambda i: (i, 0, 0)
    cache_spec = pl.BlockSpec((seq_per_step, kvw, ncache), seq)
    new_spec = pl.BlockSpec((seq_per_step * qlen, kvw), lambda i: (i, 0))
    o, kt_next, vt_next = pl.pallas_call(
        functools.partial(_attn_sample_kernel, qlen=qlen),
        grid=(bsz // seq_per_step,),
        in_specs=[
            pl.BlockSpec((seq_per_step * rows, 1), lambda i: (0, 0)),
            pl.BlockSpec((seq_per_step, rows, HEAD_DIM), seq),
            cache_spec, new_spec, cache_spec, new_spec,
        ],
        out_specs=[pl.BlockSpec((seq_per_step, rows, HEAD_DIM), seq), cache_spec, cache_spec],
        out_shape=[jax.ShapeDtypeStruct((bsz, rows, HEAD_DIM), BF16),
                   jax.ShapeDtypeStruct((bsz, kvw, ncache), F32), jax.ShapeDtypeStruct((bsz, kvw, ncache), F32)],
        compiler_params=_cparams(("arbitrary",)),
        name="attn_sample",
    )(sink_rows, qr, feature_major(k_cache), k_new, feature_major(v_cache), v_new)
    position_major = lambda t: t.reshape(bsz, N_KV_HEADS, HEAD_DIM, ncache).transpose(0, 3, 1, 2)
    ao = o.reshape(bsz, N_KV_HEADS, group, qlen, HEAD_DIM).transpose(0, 3, 1, 2, 4).reshape(bsz * qlen, -1)
    return ao, position_major(kt_next), position_major(vt_next)


CONV_HALO = 32
CONV_ROWS = 64
CONV_COLS = 256


def _conv_prompt_kernel(uc_ref, up_ref, w_ref, b_ref, y_ref, sh_ref):
    tile = uc_ref.shape[0]
    n = pl.program_id(1)
    halo = jnp.where(n > 0, up_ref[...], 0.0)
    sh_ref[0, :CONV_HALO, :] = halo
    sh_ref[0, CONV_HALO:, :] = uc_ref[...]
    keep = tile + CONV_HALO - SUBLANES
    for r in range(1, SUBLANES):
        sh_ref[r, :keep, :] = sh_ref[0, r:r + keep, :]
    first = CONV_HALO - (CONV_WIDTH - 1)
    ch = uc_ref.shape[1]

    def rows_step(rc, carry):
        r0 = pl.multiple_of(rc * CONV_ROWS, CONV_ROWS)
        for cc in range(ch // CONV_COLS):
            cs = slice(cc * CONV_COLS, (cc + 1) * CONV_COLS)
            acc = jnp.zeros((CONV_ROWS, CONV_COLS), F32)
            for j in range(CONV_WIDTH):
                off = first + j
                a, r = off // SUBLANES, off % SUBLANES
                acc = acc + sh_ref[r, pl.ds(r0 + a * SUBLANES, CONV_ROWS), cs] * w_ref[j:j + 1, cs]
            y_ref[pl.ds(r0, CONV_ROWS), cs] = acc + b_ref[:, cs]
        return carry

    lax.fori_loop(0, tile // CONV_ROWS, rows_step, 0)


def _conv_prompt(u, w, b, bsz, slen, tile=512):
    ch = u.shape[1]
    nt = slen // tile
    per = tile // CONV_HALO
    return pl.pallas_call(
        _conv_prompt_kernel,
        grid=(bsz, nt),
        in_specs=[
            pl.BlockSpec((tile, ch), lambda bb, n: (bb * nt + n, 0)),
            pl.BlockSpec((CONV_HALO, ch), lambda bb, n: (jnp.maximum((bb * nt + n) * per - 1, 0), 0)),
            pl.BlockSpec((CONV_WIDTH, ch), lambda bb, n: (0, 0)),
            pl.BlockSpec((1, ch), lambda bb, n: (0, 0)),
        ],
        out_specs=pl.BlockSpec((tile, ch), lambda bb, n: (bb * nt + n, 0)),
        out_shape=jax.ShapeDtypeStruct((bsz * slen, ch), F32),
        scratch_shapes=[pltpu.VMEM((SUBLANES, tile + CONV_HALO, ch), F32)],
        compiler_params=_cparams(("arbitrary", "arbitrary")),
        name="conv_prompt",
    )(u, u, w, b.reshape(1, ch))


def _conv_sample_kernel(sc_ref, un_ref, w_ref, b_ref, y_ref, next_ref):
    nctx = sc_ref.shape[0]
    qlen = un_ref.shape[0]
    ch = w_ref.shape[1]

    def row(t):
        return sc_ref[t] if t < nctx else un_ref[t - nctx]

    for i in range(qlen):
        acc = jnp.zeros((sc_ref.shape[1], ch), F32)
        for j in range(CONV_WIDTH):
            acc = acc + row(i + j) * w_ref[j:j + 1, :]
        y_ref[i] = acc + b_ref[...]
    for t in range(nctx):
        next_ref[t] = row(t + qlen)


def _conv_sample(state_t, u_new_t, w, b, seq_per_step=16):
    nctx, bsz, ch = state_t.shape
    qlen = u_new_t.shape[0]
    seq = lambda i: (0, i, 0)
    return pl.pallas_call(
        _conv_sample_kernel,
        grid=(bsz // seq_per_step,),
        in_specs=[
            pl.BlockSpec((nctx, seq_per_step, ch), seq),
            pl.BlockSpec((qlen, seq_per_step, ch), seq),
            pl.BlockSpec((CONV_WIDTH, ch), lambda i: (0, 0)),
            pl.BlockSpec((1, ch), lambda i: (0, 0)),
        ],
        out_specs=[pl.BlockSpec((qlen, seq_per_step, ch), seq), pl.BlockSpec((nctx, seq_per_step, ch), seq)],
        out_shape=[jax.ShapeDtypeStruct((qlen, bsz, ch), F32), jax.ShapeDtypeStruct((nctx, bsz, ch), F32)],
        compiler_params=_cparams(("arbitrary",)),
        name="conv_sample",
    )(state_t, u_new_t, w, b.reshape(1, ch))


def _merge_kernel(y_ref, ao_ref, ga_ref, gc_ref, x_ref, lnw_ref, lnb_ref, wco_ref, wo_ref, fw_ref, wr_ref,
                  h_ref, xn_ref, lg_ref):
    y = y_ref[...]
    mu = jnp.mean(y, axis=-1, keepdims=True)
    yc = y - mu
    var = jnp.mean(yc * yc, axis=-1, keepdims=True)
    yl = yc * lax.rsqrt(var + EPS) * lnw_ref[...] + lnb_ref[...]
    act = yl * jax.nn.sigmoid(yl)
    conv_o = jnp.dot(act.astype(BF16), wco_ref[...], preferred_element_type=F32)
    mix = ga_ref[...].astype(F32) * ao_ref[...].astype(F32) + gc_ref[...].astype(F32) * conv_o
    h = x_ref[...] + jnp.dot(mix.astype(BF16), wo_ref[...], preferred_element_type=F32)
    h_ref[...] = h
    ms = jnp.mean(h * h, axis=-1, keepdims=True)
    xn = (h * lax.rsqrt(ms + EPS) * fw_ref[...]).astype(BF16)
    xn_ref[...] = xn
    lg_ref[...] = jnp.dot(xn, wr_ref[...], preferred_element_type=F32)


def _merge(y, ao, gates, row0, x, ln_w, ln_b, wco, wo, fw, wr, tile=256):
    t, d = x.shape
    ch = y.shape[1]
    off = row0 // tile
    row = lambda i: (i, 0)
    const = lambda shape: pl.BlockSpec(shape, lambda i: (0, 0), pipeline_mode=pl.Buffered(1))
    return pl.pallas_call(
        _merge_kernel,
        grid=(t // tile,),
        in_specs=[
            pl.BlockSpec((tile, ch), row),
            pl.BlockSpec((tile, d), row),
            pl.BlockSpec((tile, d), lambda i: (i + off, 0)),
            pl.BlockSpec((tile, d), lambda i: (i + off, 1)),
            pl.BlockSpec((tile, d), row),
            const((1, ch)), const((1, ch)), const((ch, d)), const((d, d)), const((1, d)),
            const((d, LANES)),
        ],
        out_specs=[pl.BlockSpec((tile, d), row), pl.BlockSpec((tile, d), row), pl.BlockSpec((tile, LANES), row)],
        out_shape=[jax.ShapeDtypeStruct((t, d), F32), jax.ShapeDtypeStruct((t, d), BF16),
                   jax.ShapeDtypeStruct((t, LANES), F32)],
        compiler_params=_cparams(("arbitrary",)),
        name="merge",
    )(y, ao, gates, gates, x, ln_w.reshape(1, ch), ln_b.reshape(1, ch), wco, wo, fw.reshape(1, d), wr)


def _route_kernel(lp_ref, ls_ref, tri_ref, lower_ref, meta_ref, metat_ref, cnt_ref, *, n_p):
    i = pl.program_id(0)
    used = _round_up(N_EXPERTS + N_GROUPS, SUBLANES)

    def body(l_ref):
        lg = l_ref[...].T[:used]
        shape = lg.shape
        row = lax.broadcasted_iota(jnp.int32, shape, 0)
        big = jnp.int32(LANES)
        is_g = (row >= N_EXPERTS) & (row < N_EXPERTS + N_GROUPS)
        gl = jnp.where(is_g, lg, -jnp.inf)
        gmax = jnp.max(gl, axis=0, keepdims=True)
        g_star = jnp.min(jnp.where(gl == gmax, row - N_EXPERTS, big), axis=0, keepdims=True)
        pg_star = 1.0 / jnp.sum(jnp.exp(gl - gmax), axis=0, keepdims=True)
        in_g = (row < N_EXPERTS) & (_div_pow2(row, EXPERTS_PER_GROUP) == g_star)
        el = jnp.where(in_g, lg, -jnp.inf)
        v1 = jnp.max(el, axis=0, keepdims=True)
        i1 = jnp.min(jnp.where(el == v1, row, big), axis=0, keepdims=True)
        el2 = jnp.where(row == i1, -jnp.inf, el)
        v2 = jnp.max(el2, axis=0, keepdims=True)
        i2 = jnp.min(jnp.where(el2 == v2, row, big), axis=0, keepdims=True)
        e2 = jnp.exp(v2 - v1)
        w1 = pg_star * (1.0 / (1.0 + e2))
        w2 = pg_star * (e2 / (1.0 + e2))
        oh1 = row == i1
        oh2 = row == i2
        cnt = jnp.where(oh1 | oh2, 1.0, 0.0)
        before = jnp.dot(cnt.astype(BF16), tri_ref[...], preferred_element_type=F32)
        tot = jnp.sum(cnt, axis=-1, keepdims=True)
        runs = jnp.floor((tot + (CHUNK_ROWS - 1)) * (1.0 / CHUNK_ROWS))
        runs_b = jnp.broadcast_to(runs, (used, LANES)).astype(BF16)
        start = jnp.dot(lower_ref[...], runs_b, preferred_element_type=F32)[:, 0:1] * CHUNK_ROWS
        slot = before + start
        s1 = jnp.sum(jnp.where(oh1, slot, 0.0), axis=0, keepdims=True)
        s2 = jnp.sum(jnp.where(oh2, slot, 0.0), axis=0, keepdims=True)
        fields = [i1.astype(F32), i2.astype(F32), w1, w2, s1, s2]
        fields += [jnp.zeros_like(w1)] * (META_FIELDS - len(fields))
        metat = jnp.concatenate(fields, axis=0)
        metat_ref[...] = metat
        pad = jnp.zeros((LANES - META_FIELDS, shape[1]), F32)
        meta_ref[...] = jnp.concatenate([metat, pad], axis=0).T
        tot_pad = jnp.concatenate([tot, jnp.zeros((LANES - used, 1), F32)], axis=0)
        cnt_ref[...] = jnp.broadcast_to(tot_pad, (LANES, LANES)).T[0:1]

    @pl.when(i < n_p)
    def _():
        body(lp_ref)

    @pl.when(i >= n_p)
    def _():
        body(ls_ref)


def _route(lg_p, lg_s):
    tile = TOKEN_BLOCK
    tp, ts = lg_p.shape[0], lg_s.shape[0]
    n_p, n_s = tp // tile, ts // tile
    nb = n_p + n_s
    used = _round_up(N_EXPERTS + N_GROUPS, SUBLANES)
    tri = (jnp.arange(tile)[:, None] < jnp.arange(tile)[None, :]).astype(BF16)
    lower = (jnp.arange(used)[:, None] > jnp.arange(used)[None, :]).astype(BF16)
    return pl.pallas_call(
        functools.partial(_route_kernel, n_p=n_p),
        grid=(nb,),
        in_specs=[
            pl.BlockSpec((tile, LANES), lambda i: (jnp.minimum(i, n_p - 1), 0)),
            pl.BlockSpec((tile, LANES), lambda i: (jnp.maximum(i - n_p, 0), 0)),
            pl.BlockSpec((tile, tile), lambda i: (0, 0)),
            pl.BlockSpec((used, used), lambda i: (0, 0)),
        ],
        out_specs=[pl.BlockSpec((tile, LANES), lambda i: (i, 0)),
                   pl.BlockSpec((None, META_FIELDS, tile), lambda i: (i, 0, 0)),
                   pl.BlockSpec((None, 1, LANES), lambda i: (i, 0, 0))],
        out_shape=[jax.ShapeDtypeStruct((tp + ts, LANES), F32),
                   jax.ShapeDtypeStruct((nb, META_FIELDS, tile), F32),
                   jax.ShapeDtypeStruct((nb, 1, LANES), F32)],
        compiler_params=_cparams(("arbitrary",)),
        name="route",
    )(lg_p, lg_s, tri, lower)


def _plan_sizes(n_tok):
    nb = n_tok // TOKEN_BLOCK
    pad_per_block = N_EXPERTS * (CHUNK_ROWS - 1)
    max_chunks = _round_up(-(-(TOKEN_BLOCK * TOP_K + pad_per_block) // CHUNK_ROWS), SUBLANES)
    n_rows = _round_up(n_tok * TOP_K + nb * pad_per_block + N_EXPERTS * (ROW_PAD - 1), ROW_PAD)
    n_items = (n_rows + N_EXPERTS * (EXPERT_ROWS - ROW_PAD)) // EXPERT_ROWS
    max_zero = N_EXPERTS * (ROW_PAD // CHUNK_ROWS)
    return nb, max_chunks, n_rows, n_items, max_zero


def _dispatch_plan(counts, max_chunks, n_items, max_zero):
    i32 = jnp.int32
    c = counts[:, 0, :N_EXPERTS].astype(i32)
    run = (c + CHUNK_ROWS - 1) // CHUNK_ROWS
    run_end = jnp.cumsum(run, axis=1)
    src_start = run_end - run
    nchunk = run_end[:, -1]
    seg = jnp.sum(run, axis=0) * CHUNK_ROWS
    padded = (seg + ROW_PAD - 1) // ROW_PAD * ROW_PAD
    pad_end = jnp.cumsum(padded)
    pad_start = pad_end - padded
    dst_start = pad_start[None, :] // CHUNK_ROWS + (jnp.cumsum(run, axis=0) - run)
    jj = jnp.arange(max_chunks, dtype=i32)[None, :, None]
    in_run = (src_start[:, None, :] <= jj) & (jj < run_end[:, None, :])
    chunk_dst = jnp.sum(jnp.where(in_run, (dst_start - src_start)[:, None, :] + jj, 0), axis=-1) * CHUNK_ROWS
    tail = (padded - seg) // CHUNK_ROWS
    tail_end = jnp.cumsum(tail)
    tail_start = tail_end - tail
    z = jnp.arange(max_zero, dtype=i32)[:, None]
    in_tail = (tail_start[None, :] <= z) & (z < tail_end[None, :])
    zero_base = (pad_start + seg) // CHUNK_ROWS - tail_start
    zero_dst = jnp.sum(jnp.where(in_tail, zero_base[None, :] + z, 0), axis=-1) * CHUNK_ROWS
    nzero = tail_end[-1]
    items_e = (padded + EXPERT_ROWS - 1) // EXPERT_ROWS
    item_end = jnp.cumsum(items_e)
    item_start = item_end - items_e
    w = jnp.arange(n_items, dtype=i32)[:, None]
    in_item = (item_start[None, :] <= w) & (w < item_end[None, :])
    k = w - item_start[None, :]
    e_ids = jnp.arange(N_EXPERTS, dtype=i32)
    last_e = jnp.max(jnp.where(items_e > 0, e_ids, 0))
    item_e = jnp.where(w[:, 0] < item_end[-1], jnp.sum(jnp.where(in_item, e_ids[None, :], 0), axis=-1), last_e)
    item_row0 = jnp.sum(jnp.where(in_item, pad_start[None, :] + k * EXPERT_ROWS, 0), axis=-1)
    item_rows = jnp.sum(jnp.where(in_item, jnp.clip(padded[None, :] - k * EXPERT_ROWS, 0, EXPERT_ROWS), 0), axis=-1)
    item_nsub = item_rows // ROW_PAD
    return (chunk_dst.reshape(-1).astype(i32), nchunk.astype(i32), zero_dst.astype(i32),
            nzero.reshape(1).astype(i32), item_e.astype(i32), item_row0.astype(i32), item_nsub.astype(i32))


SCATTER_GROUPS = 2


def _scatter_kernel(cd_ref, nc_ref, zd_ref, nz_ref, mt_ref, xp_ref, xs_ref, out_hbm, loc_ref, zero_ref, sem,
                    zero_sem, *, n_p, max_chunks):
    i = pl.program_id(0)
    buf = i % 2
    n_slots = loc_ref.shape[1]
    d = xp_ref.shape[1]
    mt = mt_ref[...]
    w1, w2 = mt[2:3, :], mt[3:4, :]
    s1, s2 = mt[4:5, :].astype(jnp.int32), mt[5:6, :].astype(jnp.int32)
    half = d // 2
    group = n_slots // SCATTER_GROUPS

    def sort_rows(x_ref):
        for s0 in range(0, n_slots, group):
            slot = lax.broadcasted_iota(jnp.int32, (group, mt.shape[1]), 0) + s0
            p1 = slot == s1
            p2 = slot == s2
            perm = jnp.where(p1 | p2, 1.0, 0.0).astype(BF16)
            wcol = jnp.sum(jnp.where(p1, w1, 0.0) + jnp.where(p2, w2, 0.0), axis=-1, keepdims=True)
            loc_ref[buf, s0:s0 + group, half:] = pltpu.bitcast(jnp.broadcast_to(wcol, (group, LANES)), jnp.uint32)
            rows = jnp.dot(perm, x_ref[...], preferred_element_type=F32)
            loc_ref[buf, s0:s0 + group, :half] = _pack_pairs(rows[:, :half], rows[:, half:])

    @pl.when(i < n_p)
    def _():
        sort_rows(xp_ref)

    @pl.when(i >= n_p)
    def _():
        sort_rows(xs_ref)

    def rows_at(r):
        return pl.ds(pl.multiple_of(r, CHUNK_ROWS), CHUNK_ROWS)

    def chunk_copy(blk, j):
        return pltpu.make_async_copy(loc_ref.at[blk % 2, rows_at(j * CHUNK_ROWS)],
                                     out_hbm.at[rows_at(cd_ref[blk * max_chunks + j])], sem.at[blk % 2])

    def zero_copy(z):
        return pltpu.make_async_copy(zero_ref, out_hbm.at[rows_at(zd_ref[z])], zero_sem)

    @pl.when(i > 0)
    def _():
        _for_each(nc_ref[i - 1], lambda j: chunk_copy(i - 1, j).wait())

    _for_each(nc_ref[i], lambda j: chunk_copy(i, j).start())

    @pl.when(i == pl.num_programs(0) - 1)
    def _():
        zero_ref[...] = jnp.zeros_like(zero_ref)
        _for_each(nz_ref[0], lambda z: zero_copy(z).start())
        _for_each(nc_ref[i], lambda j: chunk_copy(i, j).wait())
        _for_each(nz_ref[0], lambda z: zero_copy(z).wait())


def _scatter_rows(xn_p, xn_s, metat, chunk_dst, nchunk, zero_dst, nzero, max_chunks, n_rows):
    tile = TOKEN_BLOCK
    tp, d = xn_p.shape
    ts = xn_s.shape[0]
    n_p, n_s = tp // tile, ts // tile
    n_slots = max_chunks * CHUNK_ROWS
    width = d // 2 + LANES
    u32 = jnp.uint32
    return pl.pallas_call(
        functools.partial(_scatter_kernel, n_p=n_p, max_chunks=max_chunks),
        grid_spec=pltpu.PrefetchScalarGridSpec(
            num_scalar_prefetch=4,
            grid=(n_p + n_s,),
            in_specs=[
                pl.BlockSpec((None, META_FIELDS, tile), lambda i, *_: (i, 0, 0)),
                pl.BlockSpec((tile, d), lambda i, *_: (jnp.minimum(i, n_p - 1), 0)),
                pl.BlockSpec((tile, d), lambda i, *_: (jnp.maximum(i - n_p, 0), 0)),
            ],
            out_specs=pl.BlockSpec(memory_space=pl.ANY),
            scratch_shapes=[pltpu.VMEM((2, n_slots, width), u32), pltpu.VMEM((CHUNK_ROWS, width), u32),
                            pltpu.SemaphoreType.DMA((2,)), pltpu.SemaphoreType.DMA(())],
        ),
        out_shape=jax.ShapeDtypeStruct((n_rows, width), u32),
        compiler_params=_cparams(("arbitrary",)),
        name="scatter_rows",
    )(chunk_dst, nchunk, zero_dst, nzero, metat, xn_p, xn_s)


def _expert_kernel(e_ref, row0_ref, nsub_ref, xs_hbm, wg_ref, wu_ref, wd_ref, out_hbm,
                   xf_ref, xb_ref, acc_ref, ob_ref, pend_ref, in_sem, out_sem, *, n_chunks):
    w = pl.program_id(0)
    c = pl.program_id(1)
    n_items = pl.num_programs(0)
    last_c = pl.num_programs(1) - 1
    nsub = nsub_ref[w]
    d = acc_ref.shape[1]
    half = d // 2
    slot = w % 2

    def sub_rows(s):
        return pl.ds(pl.multiple_of(s * ROW_PAD, ROW_PAD), ROW_PAD)

    def hbm_rows(item, s):
        return pl.ds(pl.multiple_of(row0_ref[item] + s * ROW_PAD, ROW_PAD), ROW_PAD)

    def in_copy(item, s):
        return pltpu.make_async_copy(xs_hbm.at[hbm_rows(item, s)], xf_ref.at[item % 2, sub_rows(s)],
                                     in_sem.at[item % 2])

    def out_copy(item, s):
        return pltpu.make_async_copy(ob_ref.at[sub_rows(s)], out_hbm.at[hbm_rows(item, s)], out_sem)

    def loop(n, fn):
        def body(s, carry):
            fn(s)
            return carry
        lax.fori_loop(0, n, body, 0)

    @pl.when(c == 0)
    def _():
        @pl.when(w == 0)
        def _():
            pend_ref[0] = 0
            pend_ref[1] = 0
            loop(nsub, lambda s: in_copy(w, s).start())

        loop(nsub, lambda s: in_copy(w, s).wait())
        nxt = jnp.minimum(w + 1, n_items - 1)

        @pl.when(w + 1 < n_items)
        def _():
            loop(nsub_ref[nxt], lambda s: in_copy(nxt, s).start())

    def drain_out():
        prev = pend_ref[1]
        loop(pend_ref[0], lambda s: out_copy(prev, s).wait())
        pend_ref[0] = 0

    def ffn_chunk(x):
        g = jnp.dot(x, wg_ref[...].astype(BF16), preferred_element_type=F32)
        u = jnp.dot(x, wu_ref[...].astype(BF16), preferred_element_type=F32)
        hmid = (g * jax.nn.sigmoid(g) * u).astype(BF16)
        return jnp.dot(hmid, wd_ref[...].astype(BF16), preferred_element_type=F32)

    for n in range(1, EXPERT_ROWS // ROW_PAD + 1):
        m = n * ROW_PAD

        @pl.when((nsub == n) & (c == 0))
        def _(m=m):
            xa, xb = _unpack_pairs(xf_ref[slot, :m, :half])
            x = jnp.concatenate([xa, xb], axis=1).astype(BF16)
            xb_ref[:m, :] = x
            acc_ref[:m, :] = ffn_chunk(x)

        if n_chunks > 2:
            @pl.when((nsub == n) & (c > 0) & (c < last_c))
            def _(m=m):
                acc_ref[:m, :] += ffn_chunk(xb_ref[:m, :])

        @pl.when((nsub == n) & (c == last_c))
        def _(m=m):
            drain_out()
            mix_w = pltpu.bitcast(xf_ref[slot, :m, half:half + 1], F32)
            y = ((acc_ref[:m, :] + ffn_chunk(xb_ref[:m, :])) * mix_w).astype(BF16).astype(F32)
            ob_ref[:m, :] = _pack_pairs(y[:, :half], y[:, half:])
            loop(nsub, lambda s: out_copy(w, s).start())
            pend_ref[0] = nsub
            pend_ref[1] = w

    @pl.when((w == n_items - 1) & (c == last_c))
    def _():
        drain_out()


def _experts(xs, item_e, item_row0, item_nsub, w_gate, w_up, w_down):
    n_rows, width = xs.shape
    d = w_gate.shape[1]
    assert width == d // 2 + LANES
    d_exp = w_gate.shape[2]
    nc = d_exp // D_CHUNK
    assert nc >= 2
    n_items = item_e.shape[0]

    def chunk_of(w, c, nsub_ref):
        return jnp.where(nsub_ref[w] > 0, c, nc - 1)

    return pl.pallas_call(
        functools.partial(_expert_kernel, n_chunks=nc),
        grid_spec=pltpu.PrefetchScalarGridSpec(
            num_scalar_prefetch=3,
            grid=(n_items, nc),
            in_specs=[
                pl.BlockSpec(memory_space=pl.ANY),
                pl.BlockSpec((None, d, D_CHUNK), lambda w, c, e, r, n: (e[w], 0, chunk_of(w, c, n))),
                pl.BlockSpec((None, d, D_CHUNK), lambda w, c, e, r, n: (e[w], 0, chunk_of(w, c, n))),
                pl.BlockSpec((None, D_CHUNK, d), lambda w, c, e, r, n: (e[w], chunk_of(w, c, n), 0)),
            ],
            out_specs=pl.BlockSpec(memory_space=pl.ANY),
            scratch_shapes=[
                pltpu.VMEM((2, EXPERT_ROWS, width), jnp.uint32),
                pltpu.VMEM((EXPERT_ROWS, d), BF16),
                pltpu.VMEM((EXPERT_ROWS, d), F32),
                pltpu.VMEM((EXPERT_ROWS, d // 2), jnp.uint32),
                pltpu.SMEM((2,), jnp.int32),
                pltpu.SemaphoreType.DMA((2,)),
                pltpu.SemaphoreType.DMA(()),
            ],
        ),
        out_shape=jax.ShapeDtypeStruct((n_rows, d // 2), jnp.uint32),
        compiler_params=_cparams(("arbitrary", "arbitrary")),
        name="experts",
    )(item_e, item_row0, item_nsub, xs, w_gate, w_up, w_down)


COMBINE_SPLIT = 2
SLOT_STEP = 256


def _combine_kernel(cd_ref, nc_ref, meta_ref, hp_ref, hs_ref, yb_hbm, yp_ref, ys_ref, loc_ref, sem,
                    *, n_p, max_chunks):
    i = pl.program_id(0)
    j = pl.program_id(1)
    nb = pl.num_programs(0)
    buf = i % 2
    n_slots, half = loc_ref.shape[1], loc_ref.shape[2]
    d = 2 * half

    def rows_at(r):
        return pl.ds(pl.multiple_of(r, CHUNK_ROWS), CHUNK_ROWS)

    def chunk_copy(blk, c):
        return pltpu.make_async_copy(yb_hbm.at[rows_at(cd_ref[blk * max_chunks + c])],
                                     loc_ref.at[blk % 2, rows_at(c * CHUNK_ROWS)], sem.at[blk % 2])

    def fetch(blk):
        n = nc_ref[blk]
        _for_each(n, lambda c: chunk_copy(blk, c).start())

        def clear(c, carry):
            loc_ref[blk % 2, rows_at(c * CHUNK_ROWS), :] = jnp.zeros((CHUNK_ROWS, half), jnp.uint32)
            return carry
        lax.fori_loop(n, max_chunks, clear, 0)

    @pl.when(j == 0)
    def _():
        @pl.when(i == 0)
        def _():
            fetch(i)

        _for_each(nc_ref[i], lambda c: chunk_copy(i, c).wait())

        @pl.when(i + 1 < nb)
        def _():
            fetch(jnp.minimum(i + 1, nb - 1))

    meta = meta_ref[...]
    s1 = meta[:, 4:5].astype(jnp.int32)
    s2 = meta[:, 5:6].astype(jnp.int32)
    rows = meta.shape[0]
    f = jnp.zeros((rows, d), F32)
    for r0 in range(0, n_slots, SLOT_STEP):
        slot = lax.broadcasted_iota(jnp.int32, (rows, SLOT_STEP), 1) + r0
        pt = jnp.where((slot == s1) | (slot == s2), 1.0, 0.0).astype(BF16)
        ya, yb = _unpack_pairs(loc_ref[buf, r0:r0 + SLOT_STEP, :])
        f = f + jnp.dot(pt, jnp.concatenate([ya, yb], axis=1).astype(BF16), preferred_element_type=F32)
    g = i * pl.num_programs(1) + j

    @pl.when(g < n_p)
    def _():
        yp_ref[...] = hp_ref[...] + f

    @pl.when(g >= n_p)
    def _():
        ys_ref[...] = hs_ref[...] + f


def _combine(h_p, h_s, meta, chunk_dst, nchunk, yb, max_chunks):
    tp, d = h_p.shape
    ts = h_s.shape[0]
    tile = TOKEN_BLOCK // COMBINE_SPLIT
    n_p, n_s = tp // tile, ts // tile
    nb = (tp + ts) // TOKEN_BLOCK
    n_slots = max_chunks * CHUNK_ROWS
    assert n_slots % SLOT_STEP == 0
    blk = lambda i, j: i * COMBINE_SPLIT + j
    p_idx = lambda i, j, *_: (jnp.minimum(blk(i, j), n_p - 1), 0)
    s_idx = lambda i, j, *_: (jnp.maximum(blk(i, j) - n_p, 0), 0)
    return pl.pallas_call(
        functools.partial(_combine_kernel, n_p=n_p, max_chunks=max_chunks),
        grid_spec=pltpu.PrefetchScalarGridSpec(
            num_scalar_prefetch=2,
            grid=(nb, COMBINE_SPLIT),
            in_specs=[
                pl.BlockSpec((tile, LANES), lambda i, j, *_: (blk(i, j), 0)),
                pl.BlockSpec((tile, d), p_idx),
                pl.BlockSpec((tile, d), s_idx),
                pl.BlockSpec(memory_space=pl.ANY),
            ],
            out_specs=[pl.BlockSpec((tile, d), p_idx), pl.BlockSpec((tile, d), s_idx)],
            scratch_shapes=[pltpu.VMEM((2, n_slots, d // 2), jnp.uint32), pltpu.SemaphoreType.DMA((2,))],
        ),
        out_shape=[jax.ShapeDtypeStruct((tp, d), F32), jax.ShapeDtypeStruct((ts, d), F32)],
        compiler_params=_cparams(("arbitrary", "arbitrary")),
        name="combine",
    )(chunk_dst, nchunk, meta, h_p, h_s, yb)


def _layer(x_p, x_s, ck, cv, sc, attn_norm_w, w_in, b_gate, q_norm_w, k_norm_w, sinks, conv_dw_w, conv_dw_b,
           conv_ln_w, conv_ln_b, w_conv_out, w_out, ffn_norm_w, router_group, router_expert, w_gate, w_up,
           w_down):
    bsz, slen, d = x_p.shape
    dbs, dlen, _ = x_s.shape
    tp, ts = bsz * slen, dbs * dlen
    conv_ch = conv_dw_w.shape[1]
    kvw = N_KV_HEADS * HEAD_DIM
    xp2, xs2 = x_p.reshape(tp, d), x_s.reshape(ts, d)

    xn = _prenorm(xp2, xs2, attn_norm_w)
    q, k, v, u, gates = _project(xn, w_in, b_gate, q_norm_w, k_norm_w, d, conv_ch)

    ao_p = _attn_prompt(q, k, v, sinks, bsz, slen)
    ao_s, ck_next, cv_next = _attn_sample(q[tp:], ck, k[tp:], cv, v[tp:], sinks)

    y_p = _conv_prompt(u, conv_dw_w, conv_dw_b, bsz, slen)
    sc_t = sc.transpose(1, 0, 2)
    u_s_t = u[tp:].reshape(dbs, dlen, conv_ch).transpose(1, 0, 2)
    y_s_t, sc_next_t = _conv_sample(sc_t, u_s_t, conv_dw_w, conv_dw_b)
    y_s = y_s_t.transpose(1, 0, 2).reshape(ts, conv_ch)

    wco = w_conv_out.astype(BF16)
    wo = w_out.astype(BF16)
    wr = jnp.concatenate([router_expert, router_group,
                          jnp.zeros((d, LANES - N_EXPERTS - N_GROUPS), router_expert.dtype)], axis=1).astype(BF16)
    margs = (conv_ln_w, conv_ln_b, wco, wo, ffn_norm_w, wr)
    h_p, xn_p, lg_p = _merge(y_p, ao_p, gates, 0, xp2, *margs)
    h_s, xn_s, lg_s = _merge(y_s, ao_s, gates, tp, xs2, *margs)

    meta, metat, counts = _route(lg_p, lg_s)
    _, max_chunks, n_rows, n_items, max_zero = _plan_sizes(tp + ts)
    chunk_dst, nchunk, zero_dst, nzero, item_e, item_row0, item_nsub = _dispatch_plan(
        counts, max_chunks, n_items, max_zero)
    xs_rows = _scatter_rows(xn_p, xn_s, metat, chunk_dst, nchunk, zero_dst, nzero, max_chunks, n_rows)
    yb = _experts(xs_rows, item_e, item_row0, item_nsub, w_gate, w_up, w_down)
    out_p, out_s = _combine(h_p, h_s, meta, chunk_dst, nchunk, yb, max_chunks)

    def seq_tails(a, n):
        return jnp.stack([lax.slice_in_dim(a, (b + 1) * slen - n, (b + 1) * slen, axis=0) for b in range(bsz)])

    new_p = (seq_tails(k, WINDOW).reshape(bsz, WINDOW, N_KV_HEADS, HEAD_DIM),
             seq_tails(v, WINDOW).reshape(bsz, WINDOW, N_KV_HEADS, HEAD_DIM),
             seq_tails(u, CONV_WIDTH - 1))
    new_s = (ck_next, cv_next, sc_next_t.transpose(1, 0, 2))
    return out_p.reshape(bsz, slen, d), out_s.reshape(dbs, dlen, d), new_p, new_s


def kernel(x_prompt, x_sample, cache_k, cache_v, state_conv, attn_norm_w, w_in, b_gate, q_norm_w, k_norm_w,
           sinks, conv_dw_w, conv_dw_b, conv_ln_w, conv_ln_b, w_conv_out, w_out, ffn_norm_w, router_group,
           router_expert, w_gate, w_up, w_down):
    x_p, x_s = x_prompt, x_sample
    kp, vp, cp, ks_, vs_, cs_ = [], [], [], [], [], []
    for l in range(cache_k.shape[0]):
        x_p, x_s, new_p, new_s = _layer(
            x_p, x_s, cache_k[l], cache_v[l], state_conv[l], attn_norm_w[l], w_in[l], b_gate[l], q_norm_w[l],
            k_norm_w[l], sinks[l], conv_dw_w[l], conv_dw_b[l], conv_ln_w[l], conv_ln_b[l], w_conv_out[l],
            w_out[l], ffn_norm_w[l], router_group[l], router_expert[l], w_gate[l], w_up[l], w_down[l])
        kp.append(new_p[0]); vp.append(new_p[1]); cp.append(new_p[2])
        ks_.append(new_s[0]); vs_.append(new_s[1]); cs_.append(new_s[2])
    return (x_p, x_s, jnp.stack(kp), jnp.stack(vp), jnp.stack(cp),
            jnp.stack(ks_), jnp.stack(vs_), jnp.stack(cs_))
```

```python
import functools

import jax
import jax.numpy as jnp
from jax import lax
from jax.experimental import pallas as pl
from jax.experimental.pallas import tpu as pltpu

F32 = jnp.float32
BF16 = jnp.bfloat16
EPS = 1e-6
NEG_INF = -1e30

HEAD_DIM = 64
N_KV_HEADS = 4
WINDOW = 128
CONV_WIDTH = 31
N_GROUPS = 4
EXPERTS_PER_GROUP = 8
N_EXPERTS = N_GROUPS * EXPERTS_PER_GROUP
TOP_K = 2

LANES = 128
SUBLANES = 8
MXU_DIM = 256
VMEM_LIMIT = 56 * 1024 * 1024

TOKEN_BLOCK = 512
CHUNK_ROWS = SUBLANES
ROW_PAD = 128
EXPERT_ROWS = 768
D_CHUNK = 512
META_FIELDS = 8


def _cparams(sem):
    return pltpu.CompilerParams(dimension_semantics=sem, vmem_limit_bytes=VMEM_LIMIT)


def _log2(n):
    assert n > 0 and n & (n - 1) == 0, n
    return n.bit_length() - 1


def _div_pow2(x, n):
    return lax.shift_right_logical(x, jnp.int32(_log2(n)))


def _mod_pow2(x, n):
    _log2(n)
    return x & jnp.int32(n - 1)


def _round_up(x, m):
    return (x + m - 1) // m * m


DMA_UNROLL = 8


def _for_each(n, fn):
    _for_each_slot(n, lambda j, slot: fn(j))


def _for_each_slot(n, fn):
    groups = lax.shift_right_logical(n, jnp.int32(_log2(DMA_UNROLL)))

    def group(g, carry):
        for u in range(DMA_UNROLL):
            fn(g * DMA_UNROLL + u, u)
        return carry
    lax.fori_loop(0, groups, group, 0)

    def single(j, carry):
        fn(j, 0)
        return carry
    lax.fori_loop(groups * DMA_UNROLL, n, single, 0)


def _start_each(n, copy):
    _for_each_slot(n, lambda j, slot: copy(j).start(priority=slot % 2))


HIGH_HALF = 0xFFFF0000


def _pack_pairs(a, b):
    ua = pltpu.bitcast(a, jnp.uint32) & jnp.uint32(HIGH_HALF)
    return ua | (pltpu.bitcast(b, jnp.uint32) >> 16)


def _unpack_pairs(w):
    return (pltpu.bitcast(w & jnp.uint32(HIGH_HALF), F32), pltpu.bitcast(w << 16, F32))


def _prenorm_kernel(xp_ref, xs_ref, w_ref, o_ref, *, n_p):
    i = pl.program_id(0)

    def body(x_ref):
        x = x_ref[...]
        ms = jnp.mean(x * x, axis=-1, keepdims=True)
        o_ref[...] = (x * lax.rsqrt(ms + EPS) * w_ref[...]).astype(o_ref.dtype)

    @pl.when(i < n_p)
    def _():
        body(xp_ref)

    @pl.when(i >= n_p)
    def _():
        body(xs_ref)


def _prenorm(xp, xs, w, tile=512):
    tp, d = xp.shape
    ts = xs.shape[0]
    n_p, n_s = tp // tile, ts // tile
    return pl.pallas_call(
        functools.partial(_prenorm_kernel, n_p=n_p),
        grid=(n_p + n_s,),
        in_specs=[
            pl.BlockSpec((tile, d), lambda i: (jnp.minimum(i, n_p - 1), 0)),
            pl.BlockSpec((tile, d), lambda i: (jnp.maximum(i - n_p, 0), 0)),
            pl.BlockSpec((1, d), lambda i: (0, 0)),
        ],
        out_specs=pl.BlockSpec((tile, d), lambda i: (i, 0)),
        out_shape=jax.ShapeDtypeStruct((tp + ts, d), BF16),
        compiler_params=_cparams(("arbitrary",)),
        name="prenorm",
    )(xp, xs, w.reshape(1, d))


def _head_rms(z, hw, bd_ref):
    ss = z * z
    hi = ss.astype(BF16)
    lo = (ss - hi.astype(F32)).astype(BF16)
    bd = bd_ref[...]
    tot = jnp.dot(hi, bd, preferred_element_type=F32) + jnp.dot(lo, bd, preferred_element_type=F32)
    return z * lax.rsqrt(tot * (1.0 / HEAD_DIM) + EPS) * hw


def _load_w(i, w_ref, wb_ref):
    @pl.when(i == 0)
    def _():
        wb_ref[...] = w_ref[...].astype(BF16)


PROJ_SUB_NORM = 544
PROJ_SUB_ACT = 272


def _row_groups(n_rows, sub):
    sub = sub if n_rows % sub == 0 else n_rows
    return [slice(r0, r0 + sub) for r0 in range(0, n_rows, sub)]


def _proj_q_kernel(x_ref, w_ref, hw_ref, bd_ref, o_ref, wb_ref):
    _load_w(pl.program_id(1), w_ref, wb_ref)
    for rs in _row_groups(x_ref.shape[0], PROJ_SUB_NORM):
        z = jnp.dot(x_ref[rs, :], wb_ref[...], preferred_element_type=F32)
        for c in range(z.shape[1] // MXU_DIM):
            sl = slice(c * MXU_DIM, (c + 1) * MXU_DIM)
            o_ref[rs, sl] = _head_rms(z[:, sl], hw_ref[...], bd_ref).astype(o_ref.dtype)


def _proj_kv_kernel(x_ref, w_ref, hw_ref, bd_ref, k_ref, v_ref, wb_ref):
    _load_w(pl.program_id(1), w_ref, wb_ref)
    kw = k_ref.shape[1]
    for rs in _row_groups(x_ref.shape[0], PROJ_SUB_NORM):
        z = jnp.dot(x_ref[rs, :], wb_ref[...], preferred_element_type=F32)
        k_ref[rs, :] = _head_rms(z[:, :kw], hw_ref[...], bd_ref)
        v_ref[rs, :] = z[:, kw:]


def _proj_glu_kernel(x_ref, wa_ref, wg_ref, o_ref, wab_ref, wgb_ref):
    i = pl.program_id(1)
    _load_w(i, wa_ref, wab_ref)
    _load_w(i, wg_ref, wgb_ref)
    for rs in _row_groups(x_ref.shape[0], PROJ_SUB_ACT):
        x = x_ref[rs, :]
        a = jnp.dot(x, wab_ref[...], preferred_element_type=F32)
        g = jnp.dot(x, wgb_ref[...], preferred_element_type=F32)
        o_ref[rs, :] = a * jax.nn.sigmoid(g)


def _proj_gate_kernel(x_ref, w_ref, b_ref, o_ref, wb_ref):
    _load_w(pl.program_id(1), w_ref, wb_ref)
    for rs in _row_groups(x_ref.shape[0], PROJ_SUB_ACT):
        z = jnp.dot(x_ref[rs, :], wb_ref[...], preferred_element_type=F32)
        o_ref[rs, :] = jax.nn.sigmoid(z + b_ref[...]).astype(o_ref.dtype)


def _row_tile(t, cap=2304):
    best = 16
    for m in range(16, cap + 1, 16):
        if t % m == 0:
            best = m
    return best


def _project(xn, w_in, b_gate, q_norm_w, k_norm_w, d_model, conv_ch):
    t, d = xn.shape
    q_w = d_model
    kv_w = N_KV_HEADS * HEAD_DIM
    tn = 2 * kv_w
    tm = _row_tile(t)
    ni = t // tm
    reps = MXU_DIM // HEAD_DIM
    hq = jnp.tile(q_norm_w.astype(F32) * HEAD_DIM ** -0.5, reps).reshape(1, MXU_DIM)
    hk = jnp.tile(k_norm_w.astype(F32), reps).reshape(1, MXU_DIM)
    gid = jnp.arange(MXU_DIM) // HEAD_DIM
    bd = (gid[:, None] == gid[None, :]).astype(BF16)

    x_spec = pl.BlockSpec((tm, d), lambda s, i: (i, 0))
    small = lambda shape: pl.BlockSpec(shape, lambda s, i: (0, 0))

    def w_spec(c0):
        return pl.BlockSpec((d, tn), lambda s, i: (0, c0 + s))

    wscr = pltpu.VMEM((d, tn), BF16)
    cp = _cparams(("arbitrary", "arbitrary"))

    q = pl.pallas_call(
        _proj_q_kernel, grid=(q_w // tn, ni),
        in_specs=[x_spec, w_spec(0), small((1, MXU_DIM)), small((MXU_DIM, MXU_DIM))],
        out_specs=pl.BlockSpec((tm, tn), lambda s, i: (i, s)),
        out_shape=jax.ShapeDtypeStruct((t, q_w), BF16),
        scratch_shapes=[wscr], compiler_params=cp, name="proj_q",
    )(xn, w_in, hq, bd)

    c_kv = q_w // tn
    k, v = pl.pallas_call(
        _proj_kv_kernel, grid=(1, ni),
        in_specs=[x_spec, w_spec(c_kv), small((1, MXU_DIM)), small((MXU_DIM, MXU_DIM))],
        out_specs=[pl.BlockSpec((tm, kv_w), lambda s, i: (i, 0))] * 2,
        out_shape=[jax.ShapeDtypeStruct((t, kv_w), F32)] * 2,
        scratch_shapes=[wscr], compiler_params=cp, name="proj_kv",
    )(xn, w_in, hk, bd)

    c_a = c_kv + 1
    n_glu = conv_ch // tn
    u = pl.pallas_call(
        _proj_glu_kernel, grid=(n_glu, ni),
        in_specs=[x_spec, w_spec(c_a), w_spec(c_a + n_glu)],
        out_specs=pl.BlockSpec((tm, tn), lambda s, i: (i, s)),
        out_shape=jax.ShapeDtypeStruct((t, conv_ch), F32),
        scratch_shapes=[wscr, wscr], compiler_params=cp, name="proj_glu",
    )(xn, w_in, w_in)

    c_g = c_a + 2 * n_glu
    n_gate = 2 * d_model // tn
    gates = pl.pallas_call(
        _proj_gate_kernel, grid=(n_gate, ni),
        in_specs=[x_spec, w_spec(c_g), pl.BlockSpec((1, tn), lambda s, i: (0, s))],
        out_specs=pl.BlockSpec((tm, tn), lambda s, i: (i, s)),
        out_shape=jax.ShapeDtypeStruct((t, 2 * d_model), BF16),
        scratch_shapes=[wscr], compiler_params=cp, name="proj_gate",
    )(xn, w_in, b_gate.reshape(1, -1))
    return q, k, v, u, gates


ATTN_BLOCKS = 4


def _attn_prompt_kernel(sink_ref, q_ref, kc_ref, kp_ref, vc_ref, vp_ref, o_ref, kbuf_ref, vbuf_ref):
    blk = WINDOW
    n = pl.program_id(1)
    rows = 2 * blk
    kbuf_ref[:blk, :] = kp_ref[...]
    kbuf_ref[blk:, :] = kc_ref[...]
    vbuf_ref[:blk, :] = vp_ref[...]
    vbuf_ref[blk:, :] = vc_ref[...]
    r_iota = lax.broadcasted_iota(jnp.int32, (rows, rows), 0)
    c_iota = lax.broadcasted_iota(jnp.int32, (rows, rows), 1)
    qi = jnp.where(r_iota >= blk, r_iota - blk, r_iota)
    band = (c_iota >= qi) & (c_iota <= qi + WINDOW)
    lo_lane = lax.broadcasted_iota(jnp.int32, (blk, LANES), 1) < HEAD_DIM
    top_row = lax.broadcasted_iota(jnp.int32, (rows, 1), 0) < blk
    zero = jnp.zeros((blk, LANES), q_ref.dtype)
    group = q_ref.shape[1] // (N_KV_HEADS * LANES)

    def query_block(sb, carry):
        r0 = pl.multiple_of(sb * blk, blk)
        q_rows = pl.ds(r0, blk)
        kv_rows = pl.ds(r0, rows)
        mask = band & ((c_iota >= blk) | (n > 0) | (sb > 0))
        for g in range(N_KV_HEADS):
            hs = slice(g * HEAD_DIM, (g + 1) * HEAD_DIM)
            kg = kbuf_ref[kv_rows, hs]
            vg = vbuf_ref[kv_rows, hs]
            kdup = jnp.concatenate([kg, kg], axis=1).astype(BF16)
            vdup = jnp.concatenate([vg, vg], axis=1).astype(BF16)
            for p in range(group):
                pair = g * group + p
                ls = slice(pair * LANES, (pair + 1) * LANES)
                qp = q_ref[q_rows, ls]
                lhs = jnp.concatenate([jnp.where(lo_lane, qp, zero), jnp.where(lo_lane, zero, qp)], axis=0)
                s = lax.dot_general(lhs, kdup, (((1,), (1,)), ((), ())), preferred_element_type=F32)
                s = jnp.where(mask, s, NEG_INF)
                sink = jnp.where(top_row, sink_ref[2 * pair], sink_ref[2 * pair + 1])
                m = jnp.maximum(jnp.max(s, axis=-1, keepdims=True), sink)
                e = jnp.exp(s - m)
                denom = jnp.sum(e, axis=-1, keepdims=True) + jnp.exp(sink - m)
                o2 = jnp.dot(e.astype(BF16), vdup, preferred_element_type=F32) / denom
                o_ref[q_rows, ls] = jnp.where(lo_lane, o2[:blk], o2[blk:]).astype(o_ref.dtype)
        return carry

    lax.fori_loop(0, q_ref.shape[0] // blk, query_block, 0)


def _attn_prompt(q, k, v, sinks, bsz, slen):
    blk = WINDOW
    step = ATTN_BLOCKS * blk
    ns = slen // step
    qw, kw = q.shape[1], k.shape[1]
    cur = lambda b, n: (b * ns + n, 0)
    prev = lambda b, n: (jnp.maximum((b * ns + n) * ATTN_BLOCKS - 1, 0), 0)
    return pl.pallas_call(
        _attn_prompt_kernel,
        grid=(bsz, ns),
        in_specs=[
            pl.BlockSpec(memory_space=pltpu.SMEM),
            pl.BlockSpec((step, qw), cur),
            pl.BlockSpec((step, kw), cur), pl.BlockSpec((blk, kw), prev),
            pl.BlockSpec((step, kw), cur), pl.BlockSpec((blk, kw), prev),
        ],
        out_specs=pl.BlockSpec((step, qw), cur),
        out_shape=jax.ShapeDtypeStruct((bsz * slen, qw), BF16),
        scratch_shapes=[pltpu.VMEM((blk + step, kw), F32), pltpu.VMEM((blk + step, kw), F32)],
        compiler_params=_cparams(("arbitrary", "arbitrary")),
        name="attn_prompt",
    )(sinks.astype(F32), q, k, k, v, v)


def _attn_sample_kernel(sink_ref, q_ref, kt_ref, kn_ref, vt_ref, vn_ref, o_ref, kt_out_ref, vt_out_ref, *, qlen):
    nseq, pair_rows, pair_w = q_ref.shape
    rows = 2 * pair_rows
    kvw, ncache = kt_ref.shape[1], kt_ref.shape[2]
    per_kv = pair_rows // N_KV_HEADS
    all_rows = nseq * rows
    r_lane = _div_pow2(_mod_pow2(lax.broadcasted_iota(jnp.int32, (all_rows, kvw), 0), pair_rows), per_kv)
    c_lane = _div_pow2(lax.broadcasted_iota(jnp.int32, (all_rows, kvw), 1), HEAD_DIM)
    own = r_lane == c_lane
    mask_c = (lax.broadcasted_iota(jnp.int32, (all_rows, ncache), 1)
              >= _mod_pow2(lax.broadcasted_iota(jnp.int32, (all_rows, ncache), 0), qlen))
    mask_n = (lax.broadcasted_iota(jnp.int32, (all_rows, qlen), 1)
              <= _mod_pow2(lax.broadcasted_iota(jnp.int32, (all_rows, qlen), 0), qlen))
    keep_old = lax.broadcasted_iota(jnp.int32, (kvw, ncache), 1) < ncache - qlen
    sink = sink_ref[...]
    contract_last = (((1,), (1,)), ((), ()))
    seq_rows = lambda b: slice(b * rows, (b + 1) * rows)
    new_rows = lambda b: slice(b * qlen, (b + 1) * qlen)

    pair_shape = (nseq * pair_rows, kvw)
    pair_kv = _div_pow2(_mod_pow2(lax.broadcasted_iota(jnp.int32, pair_shape, 0), pair_rows), per_kv)
    pair_own = pair_kv == _div_pow2(lax.broadcasted_iota(jnp.int32, pair_shape, 1), HEAD_DIM)
    even_kv = (pair_kv & 1) == 0
    q_pairs = q_ref[...].reshape(nseq * pair_rows, pair_w)
    tiled = jnp.concatenate([q_pairs] * (kvw // pair_w), axis=1)
    swapped = jnp.concatenate([q_pairs[:, HEAD_DIM:], q_pairs[:, :HEAD_DIM]] * (kvw // pair_w), axis=1)
    none = jnp.zeros((), q_pairs.dtype)
    q_first = jnp.where(pair_own & even_kv, tiled, jnp.where(pair_own, swapped, none))
    q_second = jnp.where(pair_own & even_kv, swapped, jnp.where(pair_own, tiled, none))
    pair_seq = lambda b: slice(b * pair_rows, (b + 1) * pair_rows)
    qe = [jnp.concatenate([q_first[pair_seq(b)], q_second[pair_seq(b)]], axis=0) for b in range(nseq)]
    sc = jnp.concatenate([jnp.dot(qe[b], kt_ref[b].astype(BF16), preferred_element_type=F32)
                          for b in range(nseq)], axis=0)
    sn = jnp.concatenate([lax.dot_general(qe[b], kn_ref[new_rows(b), :].astype(BF16), contract_last,
                                          preferred_element_type=F32) for b in range(nseq)], axis=0)
    sc = jnp.where(mask_c, sc, NEG_INF)
    sn = jnp.where(mask_n, sn, NEG_INF)
    m = jnp.maximum(jnp.maximum(jnp.max(sc, axis=-1, keepdims=True), jnp.max(sn, axis=-1, keepdims=True)), sink)
    ec = jnp.exp(sc - m)
    en = jnp.exp(sn - m)
    denom = jnp.sum(ec, axis=-1, keepdims=True) + jnp.sum(en, axis=-1, keepdims=True) + jnp.exp(sink - m)
    ec = ec.astype(BF16)
    en = en.astype(BF16)
    o = jnp.concatenate(
        [lax.dot_general(ec[seq_rows(b)], vt_ref[b].astype(BF16), contract_last, preferred_element_type=F32)
         + jnp.dot(en[seq_rows(b)], vn_ref[new_rows(b), :].astype(BF16), preferred_element_type=F32)
         for b in range(nseq)], axis=0)
    o = jnp.where(own, o, 0.0)
    acc = o[:, :pair_w]
    for g in range(1, kvw // pair_w):
        acc = acc + o[:, g * pair_w:(g + 1) * pair_w]
    out = (acc / denom).reshape(nseq, 2, pair_rows, pair_w)
    first = out[:, 0].reshape(nseq * pair_rows, pair_w)
    second = out[:, 1].reshape(nseq * pair_rows, pair_w)
    even = even_kv[:, :pair_w]
    in_place = jnp.where(even, first, second)
    displaced = jnp.where(even, second, first)
    pairs = in_place + pltpu.roll(displaced, HEAD_DIM, axis=1)
    o_ref[...] = pairs.astype(o_ref.dtype).reshape(nseq, pair_rows, pair_w)

    def feature_major(new_ref):
        pad = jnp.zeros((ncache - nseq * qlen, kvw), F32)
        return jnp.concatenate([new_ref[...], pad], axis=0).T

    def shift_in(old_ref, new_ref, out_ref):
        new_t = feature_major(new_ref)
        for b in range(nseq):
            placed = pltpu.roll(new_t, ncache - qlen - b * qlen, axis=1)
            out_ref[b] = jnp.where(keep_old, pltpu.roll(old_ref[b], ncache - qlen, axis=1), placed)

    shift_in(kt_ref, kn_ref, kt_out_ref)
    shift_in(vt_ref, vn_ref, vt_out_ref)


def _attn_sample(q, k_cache, k_new, v_cache, v_new, sinks, seq_per_step=8):
    bsz, ncache = k_cache.shape[:2]
    kvw = k_new.shape[1]
    qlen = k_new.shape[0] // bsz
    assert seq_per_step * qlen <= ncache
    n_heads = q.shape[1] // HEAD_DIM
    assert n_heads % (2 * N_KV_HEADS) == 0
    rows = n_heads * qlen
    pair_rows = rows // 2
    pair_w = 2 * HEAD_DIM
    qr = q.reshape(bsz, qlen, n_heads // 2, pair_w).transpose(0, 2, 1, 3).reshape(bsz, pair_rows, pair_w)
    by_parity = sinks.astype(F32).reshape(n_heads // 2, 2).T.reshape(n_heads)
    sink_rows = jnp.tile(jnp.repeat(by_parity, qlen), seq_per_step).reshape(seq_per_step * rows, 1)
    feature_major = lambda c: c.transpose(0, 2, 3, 1).reshape(bsz, kvw, ncache)
    seq = lambda i: (i, 0, 0)
    cache_spec = pl.BlockSpec((seq_per_step, kvw, ncache), seq)
    new_spec = pl.BlockSpec((seq_per_step * qlen, kvw), lambda i: (i, 0))
    o, kt_next, vt_next = pl.pallas_call(
        functools.partial(_attn_sample_kernel, qlen=qlen),
        grid=(bsz // seq_per_step,),
        in_specs=[
            pl.BlockSpec((seq_per_step * rows, 1), lambda i: (0, 0)),
            pl.BlockSpec((seq_per_step, pair_rows, pair_w), seq),
            cache_spec, new_spec, cache_spec, new_spec,
        ],
        out_specs=[pl.BlockSpec((seq_per_step, pair_rows, pair_w), seq), cache_spec, cache_spec],
        out_shape=[jax.ShapeDtypeStruct((bsz, pair_rows, pair_w), BF16),
                   jax.ShapeDtypeStruct((bsz, kvw, ncache), F32), jax.ShapeDtypeStruct((bsz, kvw, ncache), F32)],
        compiler_params=_cparams(("arbitrary",)),
        name="attn_sample",
    )(sink_rows, qr, feature_major(k_cache), k_new, feature_major(v_cache), v_new)
    position_major = lambda t: t.reshape(bsz, N_KV_HEADS, HEAD_DIM, ncache).transpose(0, 3, 1, 2)
    ao = o.reshape(bsz, n_heads // 2, qlen, pair_w).transpose(0, 2, 1, 3).reshape(bsz * qlen, -1)
    return ao, position_major(kt_next), position_major(vt_next)


CONV_HALO = 32
CONV_ROWS = 64
CONV_COLS = 256


def _conv_prompt_kernel(uc_ref, up_ref, w_ref, b_ref, y_ref, sh_ref):
    tile = uc_ref.shape[0]
    n = pl.program_id(1)
    halo = jnp.where(n > 0, up_ref[...], 0.0)
    sh_ref[0, :CONV_HALO, :] = halo
    sh_ref[0, CONV_HALO:, :] = uc_ref[...]
    keep = tile + CONV_HALO - SUBLANES
    for r in range(1, SUBLANES):
        sh_ref[r, :keep, :] = sh_ref[0, r:r + keep, :]
    first = CONV_HALO - (CONV_WIDTH - 1)
    ch = uc_ref.shape[1]

    def rows_step(rc, carry):
        r0 = pl.multiple_of(rc * CONV_ROWS, CONV_ROWS)
        for cc in range(ch // CONV_COLS):
            cs = slice(cc * CONV_COLS, (cc + 1) * CONV_COLS)
            acc = jnp.zeros((CONV_ROWS, CONV_COLS), F32)
            for j in range(CONV_WIDTH):
                off = first + j
                a, r = off // SUBLANES, off % SUBLANES
                acc = acc + sh_ref[r, pl.ds(r0 + a * SUBLANES, CONV_ROWS), cs] * w_ref[j:j + 1, cs]
            y_ref[pl.ds(r0, CONV_ROWS), cs] = acc + b_ref[:, cs]
        return carry

    lax.fori_loop(0, tile // CONV_ROWS, rows_step, 0)


def _conv_prompt(u, w, b, bsz, slen, tile=512):
    ch = u.shape[1]
    nt = slen // tile
    per = tile // CONV_HALO
    return pl.pallas_call(
        _conv_prompt_kernel,
        grid=(bsz, nt),
        in_specs=[
            pl.BlockSpec((tile, ch), lambda bb, n: (bb * nt + n, 0)),
            pl.BlockSpec((CONV_HALO, ch), lambda bb, n: (jnp.maximum((bb * nt + n) * per - 1, 0), 0)),
            pl.BlockSpec((CONV_WIDTH, ch), lambda bb, n: (0, 0)),
            pl.BlockSpec((1, ch), lambda bb, n: (0, 0)),
        ],
        out_specs=pl.BlockSpec((tile, ch), lambda bb, n: (bb * nt + n, 0)),
        out_shape=jax.ShapeDtypeStruct((bsz * slen, ch), F32),
        scratch_shapes=[pltpu.VMEM((SUBLANES, tile + CONV_HALO, ch), F32)],
        compiler_params=_cparams(("arbitrary", "arbitrary")),
        name="conv_prompt",
    )(u, u, w, b.reshape(1, ch))


def _conv_sample_kernel(sc_ref, un_ref, w_ref, b_ref, y_ref, next_ref):
    nctx = sc_ref.shape[0]
    qlen = un_ref.shape[0]
    ch = w_ref.shape[1]

    def row(t):
        return sc_ref[t] if t < nctx else un_ref[t - nctx]

    for i in range(qlen):
        acc = jnp.zeros((sc_ref.shape[1], ch), F32)
        for j in range(CONV_WIDTH):
            acc = acc + row(i + j) * w_ref[j:j + 1, :]
        y_ref[i] = acc + b_ref[...]
    for t in range(nctx):
        next_ref[t] = row(t + qlen)


def _conv_sample(state_t, u_new_t, w, b, seq_per_step=16):
    nctx, bsz, ch = state_t.shape
    qlen = u_new_t.shape[0]
    seq = lambda i: (0, i, 0)
    return pl.pallas_call(
        _conv_sample_kernel,
        grid=(bsz // seq_per_step,),
        in_specs=[
            pl.BlockSpec((nctx, seq_per_step, ch), seq),
            pl.BlockSpec((qlen, seq_per_step, ch), seq),
            pl.BlockSpec((CONV_WIDTH, ch), lambda i: (0, 0)),
            pl.BlockSpec((1, ch), lambda i: (0, 0)),
        ],
        out_specs=[pl.BlockSpec((qlen, seq_per_step, ch), seq), pl.BlockSpec((nctx, seq_per_step, ch), seq)],
        out_shape=[jax.ShapeDtypeStruct((qlen, bsz, ch), F32), jax.ShapeDtypeStruct((nctx, bsz, ch), F32)],
        compiler_params=_cparams(("arbitrary",)),
        name="conv_sample",
    )(state_t, u_new_t, w, b.reshape(1, ch))


def _merge_kernel(y_ref, ao_ref, ga_ref, gc_ref, x_ref, lnw_ref, lnb_ref, wco_ref, wo_ref, fw_ref, wr_ref,
                  h_ref, xn_ref, lg_ref):
    y = y_ref[...]
    mu = jnp.mean(y, axis=-1, keepdims=True)
    yc = y - mu
    var = jnp.mean(yc * yc, axis=-1, keepdims=True)
    yl = yc * lax.rsqrt(var + EPS) * lnw_ref[...] + lnb_ref[...]
    act = yl * jax.nn.sigmoid(yl)
    conv_o = jnp.dot(act.astype(BF16), wco_ref[...], preferred_element_type=F32)
    mix = ga_ref[...].astype(F32) * ao_ref[...].astype(F32) + gc_ref[...].astype(F32) * conv_o
    h = x_ref[...] + jnp.dot(mix.astype(BF16), wo_ref[...], preferred_element_type=F32)
    h_ref[...] = h
    ms = jnp.mean(h * h, axis=-1, keepdims=True)
    xn = (h * lax.rsqrt(ms + EPS) * fw_ref[...]).astype(BF16)
    xn_ref[...] = xn
    lg_ref[...] = jnp.dot(xn, wr_ref[...], preferred_element_type=F32)


def _merge(y, ao, gates, row0, x, ln_w, ln_b, wco, wo, fw, wr, tile=256):
    t, d = x.shape
    ch = y.shape[1]
    off = row0 // tile
    row = lambda i: (i, 0)
    const = lambda shape: pl.BlockSpec(shape, lambda i: (0, 0), pipeline_mode=pl.Buffered(1))
    return pl.pallas_call(
        _merge_kernel,
        grid=(t // tile,),
        in_specs=[
            pl.BlockSpec((tile, ch), row),
            pl.BlockSpec((tile, d), row),
            pl.BlockSpec((tile, d), lambda i: (i + off, 0)),
            pl.BlockSpec((tile, d), lambda i: (i + off, 1)),
            pl.BlockSpec((tile, d), row),
            const((1, ch)), const((1, ch)), const((ch, d)), const((d, d)), const((1, d)),
            const((d, LANES)),
        ],
        out_specs=[pl.BlockSpec((tile, d), row), pl.BlockSpec((tile, d), row), pl.BlockSpec((tile, LANES), row)],
        out_shape=[jax.ShapeDtypeStruct((t, d), F32), jax.ShapeDtypeStruct((t, d), BF16),
                   jax.ShapeDtypeStruct((t, LANES), F32)],
        compiler_params=_cparams(("arbitrary",)),
        name="merge",
    )(y, ao, gates, gates, x, ln_w.reshape(1, ch), ln_b.reshape(1, ch), wco, wo, fw.reshape(1, d), wr)


def _route_kernel(lp_ref, ls_ref, tri_ref, lower_ref, meta_ref, metat_ref, cnt_ref, *, n_p):
    i = pl.program_id(0)
    used = _round_up(N_EXPERTS + N_GROUPS, SUBLANES)

    def body(l_ref):
        lg = l_ref[...].T[:used]
        shape = lg.shape
        row = lax.broadcasted_iota(jnp.int32, shape, 0)
        big = jnp.int32(LANES)
        is_g = (row >= N_EXPERTS) & (row < N_EXPERTS + N_GROUPS)
        gl = jnp.where(is_g, lg, -jnp.inf)
        gmax = jnp.max(gl, axis=0, keepdims=True)
        g_star = jnp.min(jnp.where(gl == gmax, row - N_EXPERTS, big), axis=0, keepdims=True)
        pg_star = 1.0 / jnp.sum(jnp.exp(gl - gmax), axis=0, keepdims=True)
        in_g = (row < N_EXPERTS) & (_div_pow2(row, EXPERTS_PER_GROUP) == g_star)
        el = jnp.where(in_g, lg, -jnp.inf)
        v1 = jnp.max(el, axis=0, keepdims=True)
        i1 = jnp.min(jnp.where(el == v1, row, big), axis=0, keepdims=True)
        el2 = jnp.where(row == i1, -jnp.inf, el)
        v2 = jnp.max(el2, axis=0, keepdims=True)
        i2 = jnp.min(jnp.where(el2 == v2, row, big), axis=0, keepdims=True)
        e2 = jnp.exp(v2 - v1)
        w1 = pg_star * (1.0 / (1.0 + e2))
        w2 = pg_star * (e2 / (1.0 + e2))
        oh1 = row == i1
        oh2 = row == i2
        cnt = jnp.where(oh1 | oh2, 1.0, 0.0)
        before = jnp.dot(cnt.astype(BF16), tri_ref[...], preferred_element_type=F32)
        tot = jnp.sum(cnt, axis=-1, keepdims=True)
        runs = jnp.floor((tot + (CHUNK_ROWS - 1)) * (1.0 / CHUNK_ROWS))
        runs_b = jnp.broadcast_to(runs, (used, LANES)).astype(BF16)
        start = jnp.dot(lower_ref[...], runs_b, preferred_element_type=F32)[:, 0:1] * CHUNK_ROWS
        slot = before + start
        s1 = jnp.sum(jnp.where(oh1, slot, 0.0), axis=0, keepdims=True)
        s2 = jnp.sum(jnp.where(oh2, slot, 0.0), axis=0, keepdims=True)
        fields = [i1.astype(F32), i2.astype(F32), w1, w2, s1, s2]
        fields += [jnp.zeros_like(w1)] * (META_FIELDS - len(fields))
        metat = jnp.concatenate(fields, axis=0)
        metat_ref[...] = metat
        pad = jnp.zeros((LANES - META_FIELDS, shape[1]), F32)
        meta_ref[...] = jnp.concatenate([metat, pad], axis=0).T
        tot_pad = jnp.concatenate([tot, jnp.zeros((LANES - used, 1), F32)], axis=0)
        cnt_ref[...] = jnp.broadcast_to(tot_pad, (LANES, LANES)).T[0:1]

    @pl.when(i < n_p)
    def _():
        body(lp_ref)

    @pl.when(i >= n_p)
    def _():
        body(ls_ref)


def _route(lg_p, lg_s):
    tile = TOKEN_BLOCK
    tp, ts = lg_p.shape[0], lg_s.shape[0]
    n_p, n_s = tp // tile, ts // tile
    nb = n_p + n_s
    used = _round_up(N_EXPERTS + N_GROUPS, SUBLANES)
    tri = (jnp.arange(tile)[:, None] < jnp.arange(tile)[None, :]).astype(BF16)
    lower = (jnp.arange(used)[:, None] > jnp.arange(used)[None, :]).astype(BF16)
    return pl.pallas_call(
        functools.partial(_route_kernel, n_p=n_p),
        grid=(nb,),
        in_specs=[
            pl.BlockSpec((tile, LANES), lambda i: (jnp.minimum(i, n_p - 1), 0)),
            pl.BlockSpec((tile, LANES), lambda i: (jnp.maximum(i - n_p, 0), 0)),
            pl.BlockSpec((tile, tile), lambda i: (0, 0)),
            pl.BlockSpec((used, used), lambda i: (0, 0)),
        ],
        out_specs=[pl.BlockSpec((tile, LANES), lambda i: (i, 0)),
                   pl.BlockSpec((None, META_FIELDS, tile), lambda i: (i, 0, 0)),
                   pl.BlockSpec((None, 1, LANES), lambda i: (i, 0, 0))],
        out_shape=[jax.ShapeDtypeStruct((tp + ts, LANES), F32),
                   jax.ShapeDtypeStruct((nb, META_FIELDS, tile), F32),
                   jax.ShapeDtypeStruct((nb, 1, LANES), F32)],
        compiler_params=_cparams(("arbitrary",)),
        name="route",
    )(lg_p, lg_s, tri, lower)


def _plan_sizes(n_tok):
    nb = n_tok // TOKEN_BLOCK
    pad_per_block = N_EXPERTS * (CHUNK_ROWS - 1)
    max_chunks = _round_up(-(-(TOKEN_BLOCK * TOP_K + pad_per_block) // CHUNK_ROWS), SUBLANES)
    n_rows = _round_up(n_tok * TOP_K + nb * pad_per_block + N_EXPERTS * (ROW_PAD - 1), ROW_PAD)
    n_items = (n_rows + N_EXPERTS * (EXPERT_ROWS - ROW_PAD)) // EXPERT_ROWS
    max_zero = N_EXPERTS * (ROW_PAD // CHUNK_ROWS)
    return nb, max_chunks, n_rows, n_items, max_zero


def _dispatch_plan(counts, max_chunks, n_items, max_zero):
    i32 = jnp.int32
    c = counts[:, 0, :N_EXPERTS].astype(i32)
    run = (c + CHUNK_ROWS - 1) // CHUNK_ROWS
    run_end = jnp.cumsum(run, axis=1)
    src_start = run_end - run
    nchunk = run_end[:, -1]
    seg = jnp.sum(run, axis=0) * CHUNK_ROWS
    padded = (seg + ROW_PAD - 1) // ROW_PAD * ROW_PAD
    pad_end = jnp.cumsum(padded)
    pad_start = pad_end - padded
    dst_start = pad_start[None, :] // CHUNK_ROWS + (jnp.cumsum(run, axis=0) - run)
    jj = jnp.arange(max_chunks, dtype=i32)[None, :, None]
    in_run = (src_start[:, None, :] <= jj) & (jj < run_end[:, None, :])
    chunk_dst = jnp.sum(jnp.where(in_run, (dst_start - src_start)[:, None, :] + jj, 0), axis=-1) * CHUNK_ROWS
    tail = (padded - seg) // CHUNK_ROWS
    tail_end = jnp.cumsum(tail)
    tail_start = tail_end - tail
    z = jnp.arange(max_zero, dtype=i32)[:, None]
    in_tail = (tail_start[None, :] <= z) & (z < tail_end[None, :])
    zero_base = (pad_start + seg) // CHUNK_ROWS - tail_start
    zero_dst = jnp.sum(jnp.where(in_tail, zero_base[None, :] + z, 0), axis=-1) * CHUNK_ROWS
    nzero = tail_end[-1]
    items_e = (padded + EXPERT_ROWS - 1) // EXPERT_ROWS
    item_end = jnp.cumsum(items_e)
    item_start = item_end - items_e
    w = jnp.arange(n_items, dtype=i32)[:, None]
    in_item = (item_start[None, :] <= w) & (w < item_end[None, :])
    k = w - item_start[None, :]
    e_ids = jnp.arange(N_EXPERTS, dtype=i32)
    last_e = jnp.max(jnp.where(items_e > 0, e_ids, 0))
    item_e = jnp.where(w[:, 0] < item_end[-1], jnp.sum(jnp.where(in_item, e_ids[None, :], 0), axis=-1), last_e)
    item_row0 = jnp.sum(jnp.where(in_item, pad_start[None, :] + k * EXPERT_ROWS, 0), axis=-1)
    item_rows = jnp.sum(jnp.where(in_item, jnp.clip(padded[None, :] - k * EXPERT_ROWS, 0, EXPERT_ROWS), 0), axis=-1)
    item_nsub = item_rows // ROW_PAD
    return (chunk_dst.reshape(-1).astype(i32), nchunk.astype(i32), zero_dst.astype(i32),
            nzero.reshape(1).astype(i32), item_e.astype(i32), item_row0.astype(i32), item_nsub.astype(i32))


SCATTER_GROUPS = 2


def _scatter_kernel(cd_ref, nc_ref, zd_ref, nz_ref, mt_ref, xp_ref, xs_ref, out_hbm, loc_ref, zero_ref, sem,
                    zero_sem, *, n_p, max_chunks):
    i = pl.program_id(0)
    buf = i % 2
    n_slots = loc_ref.shape[1]
    d = xp_ref.shape[1]
    mt = mt_ref[...]
    w1, w2 = mt[2:3, :], mt[3:4, :]
    s1, s2 = mt[4:5, :].astype(jnp.int32), mt[5:6, :].astype(jnp.int32)
    half = d // 2
    group = n_slots // SCATTER_GROUPS

    def sort_rows(x_ref):
        for s0 in range(0, n_slots, group):
            slot = lax.broadcasted_iota(jnp.int32, (group, mt.shape[1]), 0) + s0
            p1 = slot == s1
            p2 = slot == s2
            perm = jnp.where(p1 | p2, 1.0, 0.0).astype(BF16)
            wcol = jnp.sum(jnp.where(p1, w1, 0.0) + jnp.where(p2, w2, 0.0), axis=-1, keepdims=True)
            loc_ref[buf, s0:s0 + group, half:] = pltpu.bitcast(jnp.broadcast_to(wcol, (group, LANES)), jnp.uint32)
            rows = jnp.dot(perm, x_ref[...], preferred_element_type=F32)
            loc_ref[buf, s0:s0 + group, :half] = _pack_pairs(rows[:, :half], rows[:, half:])

    @pl.when(i < n_p)
    def _():
        sort_rows(xp_ref)

    @pl.when(i >= n_p)
    def _():
        sort_rows(xs_ref)

    def rows_at(r):
        return pl.ds(pl.multiple_of(r, CHUNK_ROWS), CHUNK_ROWS)

    def chunk_copy(blk, j):
        return pltpu.make_async_copy(loc_ref.at[blk % 2, rows_at(j * CHUNK_ROWS)],
                                     out_hbm.at[rows_at(cd_ref[blk * max_chunks + j])], sem.at[blk % 2])

    def zero_copy(z):
        return pltpu.make_async_copy(zero_ref, out_hbm.at[rows_at(zd_ref[z])], zero_sem)

    @pl.when(i > 0)
    def _():
        _for_each(nc_ref[i - 1], lambda j: chunk_copy(i - 1, j).wait())

    _start_each(nc_ref[i], lambda j: chunk_copy(i, j))

    @pl.when(i == pl.num_programs(0) - 1)
    def _():
        zero_ref[...] = jnp.zeros_like(zero_ref)
        _start_each(nz_ref[0], zero_copy)
        _for_each(nc_ref[i], lambda j: chunk_copy(i, j).wait())
        _for_each(nz_ref[0], lambda z: zero_copy(z).wait())


def _scatter_rows(xn_p, xn_s, metat, chunk_dst, nchunk, zero_dst, nzero, max_chunks, n_rows):
    tile = TOKEN_BLOCK
    tp, d = xn_p.shape
    ts = xn_s.shape[0]
    n_p, n_s = tp // tile, ts // tile
    n_slots = max_chunks * CHUNK_ROWS
    width = d // 2 + LANES
    u32 = jnp.uint32
    return pl.pallas_call(
        functools.partial(_scatter_kernel, n_p=n_p, max_chunks=max_chunks),
        grid_spec=pltpu.PrefetchScalarGridSpec(
            num_scalar_prefetch=4,
            grid=(n_p + n_s,),
            in_specs=[
                pl.BlockSpec((None, META_FIELDS, tile), lambda i, *_: (i, 0, 0)),
                pl.BlockSpec((tile, d), lambda i, *_: (jnp.minimum(i, n_p - 1), 0)),
                pl.BlockSpec((tile, d), lambda i, *_: (jnp.maximum(i - n_p, 0), 0)),
            ],
            out_specs=pl.BlockSpec(memory_space=pl.ANY),
            scratch_shapes=[pltpu.VMEM((2, n_slots, width), u32), pltpu.VMEM((CHUNK_ROWS, width), u32),
                            pltpu.SemaphoreType.DMA((2,)), pltpu.SemaphoreType.DMA(())],
        ),
        out_shape=jax.ShapeDtypeStruct((n_rows, width), u32),
        compiler_params=_cparams(("arbitrary",)),
        name="scatter_rows",
    )(chunk_dst, nchunk, zero_dst, nzero, metat, xn_p, xn_s)


def _expert_kernel(e_ref, row0_ref, nsub_ref, xs_hbm, wg_ref, wu_ref, wd_ref, out_hbm,
                   xf_ref, xb_ref, acc_ref, ob_ref, pend_ref, in_sem, out_sem, *, n_chunks):
    w = pl.program_id(0)
    c = pl.program_id(1)
    n_items = pl.num_programs(0)
    last_c = pl.num_programs(1) - 1
    nsub = nsub_ref[w]
    d = acc_ref.shape[1]
    half = d // 2
    slot = w % 2

    def sub_rows(s):
        return pl.ds(pl.multiple_of(s * ROW_PAD, ROW_PAD), ROW_PAD)

    def hbm_rows(item, s):
        return pl.ds(pl.multiple_of(row0_ref[item] + s * ROW_PAD, ROW_PAD), ROW_PAD)

    def in_copy(item, s):
        return pltpu.make_async_copy(xs_hbm.at[hbm_rows(item, s)], xf_ref.at[item % 2, sub_rows(s)],
                                     in_sem.at[item % 2])

    def out_copy(item, s):
        return pltpu.make_async_copy(ob_ref.at[sub_rows(s)], out_hbm.at[hbm_rows(item, s)], out_sem)

    def loop(n, fn):
        def body(s, carry):
            fn(s)
            return carry
        lax.fori_loop(0, n, body, 0)

    @pl.when(c == 0)
    def _():
        @pl.when(w == 0)
        def _():
            pend_ref[0] = 0
            pend_ref[1] = 0
            loop(nsub, lambda s: in_copy(w, s).start())

        loop(nsub, lambda s: in_copy(w, s).wait())
        nxt = jnp.minimum(w + 1, n_items - 1)

        @pl.when(w + 1 < n_items)
        def _():
            loop(nsub_ref[nxt], lambda s: in_copy(nxt, s).start())

    def drain_out():
        prev = pend_ref[1]
        loop(pend_ref[0], lambda s: out_copy(prev, s).wait())
        pend_ref[0] = 0

    def ffn_chunk(x):
        g = jnp.dot(x, wg_ref[...].astype(BF16), preferred_element_type=F32)
        u = jnp.dot(x, wu_ref[...].astype(BF16), preferred_element_type=F32)
        hmid = (g * jax.nn.sigmoid(g) * u).astype(BF16)
        return jnp.dot(hmid, wd_ref[...].astype(BF16), preferred_element_type=F32)

    for n in range(1, EXPERT_ROWS // ROW_PAD + 1):
        m = n * ROW_PAD

        @pl.when((nsub == n) & (c == 0))
        def _(m=m):
            xa, xb = _unpack_pairs(xf_ref[slot, :m, :half])
            x = jnp.concatenate([xa, xb], axis=1).astype(BF16)
            xb_ref[:m, :] = x
            acc_ref[:m, :] = ffn_chunk(x)

        if n_chunks > 2:
            @pl.when((nsub == n) & (c > 0) & (c < last_c))
            def _(m=m):
                acc_ref[:m, :] += ffn_chunk(xb_ref[:m, :])

        @pl.when((nsub == n) & (c == last_c))
        def _(m=m):
            drain_out()
            mix_w = pltpu.bitcast(xf_ref[slot, :m, half:half + 1], F32)
            y = ((acc_ref[:m, :] + ffn_chunk(xb_ref[:m, :])) * mix_w).astype(BF16).astype(F32)
            ob_ref[:m, :] = _pack_pairs(y[:, :half], y[:, half:])
            loop(nsub, lambda s: out_copy(w, s).start())
            pend_ref[0] = nsub
            pend_ref[1] = w

    @pl.when((w == n_items - 1) & (c == last_c))
    def _():
        drain_out()


def _experts(xs, item_e, item_row0, item_nsub, w_gate, w_up, w_down):
    n_rows, width = xs.shape
    d = w_gate.shape[1]
    assert width == d // 2 + LANES
    d_exp = w_gate.shape[2]
    nc = d_exp // D_CHUNK
    assert nc >= 2
    n_items = item_e.shape[0]

    def chunk_of(w, c, nsub_ref):
        return jnp.where(nsub_ref[w] > 0, c, nc - 1)

    return pl.pallas_call(
        functools.partial(_expert_kernel, n_chunks=nc),
        grid_spec=pltpu.PrefetchScalarGridSpec(
            num_scalar_prefetch=3,
            grid=(n_items, nc),
            in_specs=[
                pl.BlockSpec(memory_space=pl.ANY),
                pl.BlockSpec((None, d, D_CHUNK), lambda w, c, e, r, n: (e[w], 0, chunk_of(w, c, n))),
                pl.BlockSpec((None, d, D_CHUNK), lambda w, c, e, r, n: (e[w], 0, chunk_of(w, c, n))),
                pl.BlockSpec((None, D_CHUNK, d), lambda w, c, e, r, n: (e[w], chunk_of(w, c, n), 0)),
            ],
            out_specs=pl.BlockSpec(memory_space=pl.ANY),
            scratch_shapes=[
                pltpu.VMEM((2, EXPERT_ROWS, width), jnp.uint32),
                pltpu.VMEM((EXPERT_ROWS, d), BF16),
                pltpu.VMEM((EXPERT_ROWS, d), F32),
                pltpu.VMEM((EXPERT_ROWS, d // 2), jnp.uint32),
                pltpu.SMEM((2,), jnp.int32),
                pltpu.SemaphoreType.DMA((2,)),
                pltpu.SemaphoreType.DMA(()),
            ],
        ),
        out_shape=jax.ShapeDtypeStruct((n_rows, d // 2), jnp.uint32),
        compiler_params=_cparams(("arbitrary", "arbitrary")),
        name="experts",
    )(item_e, item_row0, item_nsub, xs, w_gate, w_up, w_down)


COMBINE_SPLIT = 2
SLOT_STEP = 256


def _combine_kernel(cd_ref, nc_ref, meta_ref, hp_ref, hs_ref, yb_hbm, yp_ref, ys_ref, loc_ref, sem,
                    *, n_p, max_chunks):
    i = pl.program_id(0)
    j = pl.program_id(1)
    nb = pl.num_programs(0)
    buf = i % 2
    n_slots, half = loc_ref.shape[1], loc_ref.shape[2]
    d = 2 * half

    def rows_at(r):
        return pl.ds(pl.multiple_of(r, CHUNK_ROWS), CHUNK_ROWS)

    def chunk_copy(blk, c):
        return pltpu.make_async_copy(yb_hbm.at[rows_at(cd_ref[blk * max_chunks + c])],
                                     loc_ref.at[blk % 2, rows_at(c * CHUNK_ROWS)], sem.at[blk % 2])

    def fetch(blk):
        n = nc_ref[blk]
        _start_each(n, lambda c: chunk_copy(blk, c))

        def clear(c, carry):
            loc_ref[blk % 2, rows_at(c * CHUNK_ROWS), :] = jnp.zeros((CHUNK_ROWS, half), jnp.uint32)
            return carry
        lax.fori_loop(n, max_chunks, clear, 0)

    @pl.when(j == 0)
    def _():
        @pl.when(i == 0)
        def _():
            fetch(i)

        _for_each(nc_ref[i], lambda c: chunk_copy(i, c).wait())

        @pl.when(i + 1 < nb)
        def _():
            fetch(jnp.minimum(i + 1, nb - 1))

    meta = meta_ref[...]
    s1 = meta[:, 4:5].astype(jnp.int32)
    s2 = meta[:, 5:6].astype(jnp.int32)
    rows = meta.shape[0]
    f = jnp.zeros((rows, d), F32)
    for r0 in range(0, n_slots, SLOT_STEP):
        slot = lax.broadcasted_iota(jnp.int32, (rows, SLOT_STEP), 1) + r0
        pt = jnp.where((slot == s1) | (slot == s2), 1.0, 0.0).astype(BF16)
        ya, yb = _unpack_pairs(loc_ref[buf, r0:r0 + SLOT_STEP, :])
        f = f + jnp.dot(pt, jnp.concatenate([ya, yb], axis=1).astype(BF16), preferred_element_type=F32)
    g = i * pl.num_programs(1) + j

    @pl.when(g < n_p)
    def _():
        yp_ref[...] = hp_ref[...] + f

    @pl.when(g >= n_p)
    def _():
        ys_ref[...] = hs_ref[...] + f


def _combine(h_p, h_s, meta, chunk_dst, nchunk, yb, max_chunks):
    tp, d = h_p.shape
    ts = h_s.shape[0]
    tile = TOKEN_BLOCK // COMBINE_SPLIT
    n_p, n_s = tp // tile, ts // tile
    nb = (tp + ts) // TOKEN_BLOCK
    n_slots = max_chunks * CHUNK_ROWS
    assert n_slots % SLOT_STEP == 0
    blk = lambda i, j: i * COMBINE_SPLIT + j
    p_idx = lambda i, j, *_: (jnp.minimum(blk(i, j), n_p - 1), 0)
    s_idx = lambda i, j, *_: (jnp.maximum(blk(i, j) - n_p, 0), 0)
    return pl.pallas_call(
        functools.partial(_combine_kernel, n_p=n_p, max_chunks=max_chunks),
        grid_spec=pltpu.PrefetchScalarGridSpec(
            num_scalar_prefetch=2,
            grid=(nb, COMBINE_SPLIT),
            in_specs=[
                pl.BlockSpec((tile, LANES), lambda i, j, *_: (blk(i, j), 0)),
                pl.BlockSpec((tile, d), p_idx),
                pl.BlockSpec((tile, d), s_idx),
                pl.BlockSpec(memory_space=pl.ANY),
            ],
            out_specs=[pl.BlockSpec((tile, d), p_idx), pl.BlockSpec((tile, d), s_idx)],
            scratch_shapes=[pltpu.VMEM((2, n_slots, d // 2), jnp.uint32), pltpu.SemaphoreType.DMA((2,))],
        ),
        out_shape=[jax.ShapeDtypeStruct((tp, d), F32), jax.ShapeDtypeStruct((ts, d), F32)],
        compiler_params=_cparams(("arbitrary", "arbitrary")),
        name="combine",
    )(chunk_dst, nchunk, meta, h_p, h_s, yb)


def _layer(x_p, x_s, ck, cv, sc, attn_norm_w, w_in, b_gate, q_norm_w, k_norm_w, sinks, conv_dw_w, conv_dw_b,
           conv_ln_w, conv_ln_b, w_conv_out, w_out, ffn_norm_w, router_group, router_expert, w_gate, w_up,
           w_down):
    bsz, slen, d = x_p.shape
    dbs, dlen, _ = x_s.shape
    tp, ts = bsz * slen, dbs * dlen
    conv_ch = conv_dw_w.shape[1]
    kvw = N_KV_HEADS * HEAD_DIM
    xp2, xs2 = x_p.reshape(tp, d), x_s.reshape(ts, d)

    xn = _prenorm(xp2, xs2, attn_norm_w)
    q, k, v, u, gates = _project(xn, w_in, b_gate, q_norm_w, k_norm_w, d, conv_ch)

    ao_p = _attn_prompt(q, k, v, sinks, bsz, slen)
    ao_s, ck_next, cv_next = _attn_sample(q[tp:], ck, k[tp:], cv, v[tp:], sinks)

    y_p = _conv_prompt(u, conv_dw_w, conv_dw_b, bsz, slen)
    sc_t = sc.transpose(1, 0, 2)
    u_s_t = u[tp:].reshape(dbs, dlen, conv_ch).transpose(1, 0, 2)
    y_s_t, sc_next_t = _conv_sample(sc_t, u_s_t, conv_dw_w, conv_dw_b)
    y_s = y_s_t.transpose(1, 0, 2).reshape(ts, conv_ch)

    wco = w_conv_out.astype(BF16)
    wo = w_out.astype(BF16)
    wr = jnp.concatenate([router_expert, router_group,
                          jnp.zeros((d, LANES - N_EXPERTS - N_GROUPS), router_expert.dtype)], axis=1).astype(BF16)
    margs = (conv_ln_w, conv_ln_b, wco, wo, ffn_norm_w, wr)
    h_p, xn_p, lg_p = _merge(y_p, ao_p, gates, 0, xp2, *margs)
    h_s, xn_s, lg_s = _merge(y_s, ao_s, gates, tp, xs2, *margs)

    meta, metat, counts = _route(lg_p, lg_s)
    _, max_chunks, n_rows, n_items, max_zero = _plan_sizes(tp + ts)
    chunk_dst, nchunk, zero_dst, nzero, item_e, item_row0, item_nsub = _dispatch_plan(
        counts, max_chunks, n_items, max_zero)
    xs_rows = _scatter_rows(xn_p, xn_s, metat, chunk_dst, nchunk, zero_dst, nzero, max_chunks, n_rows)
    yb = _experts(xs_rows, item_e, item_row0, item_nsub, w_gate, w_up, w_down)
    out_p, out_s = _combine(h_p, h_s, meta, chunk_dst, nchunk, yb, max_chunks)

    def seq_tails(a, n):
        return jnp.stack([lax.slice_in_dim(a, (b + 1) * slen - n, (b + 1) * slen, axis=0) for b in range(bsz)])

    new_p = (seq_tails(k, WINDOW).reshape(bsz, WINDOW, N_KV_HEADS, HEAD_DIM),
             seq_tails(v, WINDOW).reshape(bsz, WINDOW, N_KV_HEADS, HEAD_DIM),
             seq_tails(u, CONV_WIDTH - 1))
    new_s = (ck_next, cv_next, sc_next_t.transpose(1, 0, 2))
    return out_p.reshape(bsz, slen, d), out_s.reshape(dbs, dlen, d), new_p, new_s


def kernel(x_prompt, x_sample, cache_k, cache_v, state_conv, attn_norm_w, w_in, b_gate, q_norm_w, k_norm_w,
           sinks, conv_dw_w, conv_dw_b, conv_ln_w, conv_ln_b, w_conv_out, w_out, ffn_norm_w, router_group,
           router_expert, w_gate, w_up, w_down):
    x_p, x_s = x_prompt, x_sample
    kp, vp, cp, ks_, vs_, cs_ = [], [], [], [], [], []
    for l in range(cache_k.shape[0]):
        x_p, x_s, new_p, new_s = _layer(
            x_p, x_s, cache_k[l], cache_v[l], state_conv[l], attn_norm_w[l], w_in[l], b_gate[l], q_norm_w[l],
            k_norm_w[l], sinks[l], conv_dw_w[l], conv_dw_b[l], conv_ln_w[l], conv_ln_b[l], w_conv_out[l],
            w_out[l], ffn_norm_w[l], router_group[l], router_expert[l], w_gate[l], w_up[l], w_down[l])
        kp.append(new_p[0]); vp.append(new_p[1]); cp.append(new_p[2])
        ks_.append(new_s[0]); vs_.append(new_s[1]); cs_.append(new_s[2])
    return (x_p, x_s, jnp.stack(kp), jnp.stack(vp), jnp.stack(cp),
            jnp.stack(ks_), jnp.stack(vs_), jnp.stack(cs_))
```

```python
import functools

import jax
import jax.numpy as jnp
from jax import lax
from jax.experimental import pallas as pl
from jax.experimental.pallas import tpu as pltpu

F32 = jnp.float32
BF16 = jnp.bfloat16
EPS = 1e-6
NEG_INF = -1e30

HEAD_DIM = 64
N_KV_HEADS = 4
WINDOW = 128
CONV_WIDTH = 31
N_GROUPS = 4
EXPERTS_PER_GROUP = 8
N_EXPERTS = N_GROUPS * EXPERTS_PER_GROUP
TOP_K = 2

LANES = 128
SUBLANES = 8
MXU_DIM = 256
VMEM_LIMIT = 56 * 1024 * 1024

TOKEN_BLOCK = 512
CHUNK_ROWS = SUBLANES
ROW_PAD = 128
EXPERT_ROWS = 768
D_CHUNK = 512
META_FIELDS = 8


def _cparams(sem):
    return pltpu.CompilerParams(dimension_semantics=sem, vmem_limit_bytes=VMEM_LIMIT)


def _log2(n):
    assert n > 0 and n & (n - 1) == 0, n
    return n.bit_length() - 1


def _div_pow2(x, n):
    return lax.shift_right_logical(x, jnp.int32(_log2(n)))


def _mod_pow2(x, n):
    _log2(n)
    return x & jnp.int32(n - 1)


def _round_up(x, m):
    return (x + m - 1) // m * m


DMA_UNROLL = 8


def _for_each(n, fn):
    groups = lax.shift_right_logical(n, jnp.int32(_log2(DMA_UNROLL)))

    def group(g, carry):
        for u in range(DMA_UNROLL):
            fn(g * DMA_UNROLL + u)
        return carry
    lax.fori_loop(0, groups, group, 0)

    def single(j, carry):
        fn(j)
        return carry
    lax.fori_loop(groups * DMA_UNROLL, n, single, 0)


HIGH_HALF = 0xFFFF0000


def _pack_pairs(a, b):
    ua = pltpu.bitcast(a, jnp.uint32) & jnp.uint32(HIGH_HALF)
    return ua | (pltpu.bitcast(b, jnp.uint32) >> 16)


def _unpack_pairs(w):
    return (pltpu.bitcast(w & jnp.uint32(HIGH_HALF), F32), pltpu.bitcast(w << 16, F32))


def _prenorm_kernel(xp_ref, xs_ref, w_ref, o_ref, *, n_p):
    i = pl.program_id(0)

    def body(x_ref):
        x = x_ref[...]
        ms = jnp.mean(x * x, axis=-1, keepdims=True)
        o_ref[...] = (x * lax.rsqrt(ms + EPS) * w_ref[...]).astype(o_ref.dtype)

    @pl.when(i < n_p)
    def _():
        body(xp_ref)

    @pl.when(i >= n_p)
    def _():
        body(xs_ref)


def _prenorm(xp, xs, w, tile=512):
    tp, d = xp.shape
    ts = xs.shape[0]
    n_p, n_s = tp // tile, ts // tile
    return pl.pallas_call(
        functools.partial(_prenorm_kernel, n_p=n_p),
        grid=(n_p + n_s,),
        in_specs=[
            pl.BlockSpec((tile, d), lambda i: (jnp.minimum(i, n_p - 1), 0)),
            pl.BlockSpec((tile, d), lambda i: (jnp.maximum(i - n_p, 0), 0)),
            pl.BlockSpec((1, d), lambda i: (0, 0)),
        ],
        out_specs=pl.BlockSpec((tile, d), lambda i: (i, 0)),
        out_shape=jax.ShapeDtypeStruct((tp + ts, d), BF16),
        compiler_params=_cparams(("arbitrary",)),
        name="prenorm",
    )(xp, xs, w.reshape(1, d))


def _head_rms(z, hw, bd_ref):
    ss = z * z
    hi = ss.astype(BF16)
    lo = (ss - hi.astype(F32)).astype(BF16)
    bd = bd_ref[...]
    tot = jnp.dot(hi, bd, preferred_element_type=F32) + jnp.dot(lo, bd, preferred_element_type=F32)
    return z * lax.rsqrt(tot * (1.0 / HEAD_DIM) + EPS) * hw


def _load_w(i, w_ref, wb_ref):
    @pl.when(i == 0)
    def _():
        wb_ref[...] = w_ref[...].astype(BF16)


PROJ_SUB_NORM = 544
PROJ_SUB_ACT = 272


def _row_groups(n_rows, sub):
    sub = sub if n_rows % sub == 0 else n_rows
    return [slice(r0, r0 + sub) for r0 in range(0, n_rows, sub)]


def _proj_q_kernel(x_ref, w_ref, hw_ref, bd_ref, o_ref, wb_ref):
    _load_w(pl.program_id(1), w_ref, wb_ref)
    for rs in _row_groups(x_ref.shape[0], PROJ_SUB_NORM):
        z = jnp.dot(x_ref[rs, :], wb_ref[...], preferred_element_type=F32)
        for c in range(z.shape[1] // MXU_DIM):
            sl = slice(c * MXU_DIM, (c + 1) * MXU_DIM)
            o_ref[rs, sl] = _head_rms(z[:, sl], hw_ref[...], bd_ref).astype(o_ref.dtype)


def _proj_kv_kernel(x_ref, w_ref, hw_ref, bd_ref, k_ref, v_ref, wb_ref):
    _load_w(pl.program_id(1), w_ref, wb_ref)
    kw = k_ref.shape[1]
    for rs in _row_groups(x_ref.shape[0], PROJ_SUB_NORM):
        z = jnp.dot(x_ref[rs, :], wb_ref[...], preferred_element_type=F32)
        k_ref[rs, :] = _head_rms(z[:, :kw], hw_ref[...], bd_ref)
        v_ref[rs, :] = z[:, kw:]


def _proj_glu_kernel(x_ref, wa_ref, wg_ref, o_ref, wab_ref, wgb_ref):
    i = pl.program_id(1)
    _load_w(i, wa_ref, wab_ref)
    _load_w(i, wg_ref, wgb_ref)
    for rs in _row_groups(x_ref.shape[0], PROJ_SUB_ACT):
        x = x_ref[rs, :]
        a = jnp.dot(x, wab_ref[...], preferred_element_type=F32)
        g = jnp.dot(x, wgb_ref[...], preferred_element_type=F32)
        o_ref[rs, :] = a * jax.nn.sigmoid(g)


def _proj_gate_kernel(x_ref, w_ref, b_ref, o_ref, wb_ref):
    _load_w(pl.program_id(1), w_ref, wb_ref)
    for rs in _row_groups(x_ref.shape[0], PROJ_SUB_ACT):
        z = jnp.dot(x_ref[rs, :], wb_ref[...], preferred_element_type=F32)
        o_ref[rs, :] = jax.nn.sigmoid(z + b_ref[...]).astype(o_ref.dtype)


def _row_tile(t, cap=2304):
    best = 16
    for m in range(16, cap + 1, 16):
        if t % m == 0:
            best = m
    return best


def _project(xn, w_in, b_gate, q_norm_w, k_norm_w, d_model, conv_ch):
    t, d = xn.shape
    q_w = d_model
    kv_w = N_KV_HEADS * HEAD_DIM
    tn = 2 * kv_w
    tm = _row_tile(t)
    ni = t // tm
    reps = MXU_DIM // HEAD_DIM
    hq = jnp.tile(q_norm_w.astype(F32) * HEAD_DIM ** -0.5, reps).reshape(1, MXU_DIM)
    hk = jnp.tile(k_norm_w.astype(F32), reps).reshape(1, MXU_DIM)
    gid = jnp.arange(MXU_DIM) // HEAD_DIM
    bd = (gid[:, None] == gid[None, :]).astype(BF16)

    x_spec = pl.BlockSpec((tm, d), lambda s, i: (i, 0))
    small = lambda shape: pl.BlockSpec(shape, lambda s, i: (0, 0))

    def w_spec(c0):
        return pl.BlockSpec((d, tn), lambda s, i: (0, c0 + s))

    wscr = pltpu.VMEM((d, tn), BF16)
    cp = _cparams(("arbitrary", "arbitrary"))

    q = pl.pallas_call(
        _proj_q_kernel, grid=(q_w // tn, ni),
        in_specs=[x_spec, w_spec(0), small((1, MXU_DIM)), small((MXU_DIM, MXU_DIM))],
        out_specs=pl.BlockSpec((tm, tn), lambda s, i: (i, s)),
        out_shape=jax.ShapeDtypeStruct((t, q_w), BF16),
        scratch_shapes=[wscr], compiler_params=cp, name="proj_q",
    )(xn, w_in, hq, bd)

    c_kv = q_w // tn
    k, v = pl.pallas_call(
        _proj_kv_kernel, grid=(1, ni),
        in_specs=[x_spec, w_spec(c_kv), small((1, MXU_DIM)), small((MXU_DIM, MXU_DIM))],
        out_specs=[pl.BlockSpec((tm, kv_w), lambda s, i: (i, 0))] * 2,
        out_shape=[jax.ShapeDtypeStruct((t, kv_w), F32)] * 2,
        scratch_shapes=[wscr], compiler_params=cp, name="proj_kv",
    )(xn, w_in, hk, bd)

    c_a = c_kv + 1
    n_glu = conv_ch // tn
    u = pl.pallas_call(
        _proj_glu_kernel, grid=(n_glu, ni),
        in_specs=[x_spec, w_spec(c_a), w_spec(c_a + n_glu)],
        out_specs=pl.BlockSpec((tm, tn), lambda s, i: (i, s)),
        out_shape=jax.ShapeDtypeStruct((t, conv_ch), F32),
        scratch_shapes=[wscr, wscr], compiler_params=cp, name="proj_glu",
    )(xn, w_in, w_in)

    c_g = c_a + 2 * n_glu
    n_gate = 2 * d_model // tn
    gates = pl.pallas_call(
        _proj_gate_kernel, grid=(n_gate, ni),
        in_specs=[x_spec, w_spec(c_g), pl.BlockSpec((1, tn), lambda s, i: (0, s))],
        out_specs=pl.BlockSpec((tm, tn), lambda s, i: (i, s)),
        out_shape=jax.ShapeDtypeStruct((t, 2 * d_model), BF16),
        scratch_shapes=[wscr], compiler_params=cp, name="proj_gate",
    )(xn, w_in, b_gate.reshape(1, -1))
    return q, k, v, u, gates


ATTN_BLOCKS = 4


def _attn_prompt_kernel(sink_ref, q_ref, kc_ref, kp_ref, vc_ref, vp_ref, o_ref, kbuf_ref, vbuf_ref):
    blk = WINDOW
    n = pl.program_id(1)
    rows = 2 * blk
    kbuf_ref[:blk, :] = kp_ref[...]
    kbuf_ref[blk:, :] = kc_ref[...]
    vbuf_ref[:blk, :] = vp_ref[...]
    vbuf_ref[blk:, :] = vc_ref[...]
    r_iota = lax.broadcasted_iota(jnp.int32, (rows, rows), 0)
    c_iota = lax.broadcasted_iota(jnp.int32, (rows, rows), 1)
    qi = jnp.where(r_iota >= blk, r_iota - blk, r_iota)
    band = (c_iota >= qi) & (c_iota <= qi + WINDOW)
    lo_lane = lax.broadcasted_iota(jnp.int32, (blk, LANES), 1) < HEAD_DIM
    top_row = lax.broadcasted_iota(jnp.int32, (rows, 1), 0) < blk
    zero = jnp.zeros((blk, LANES), q_ref.dtype)
    group = q_ref.shape[1] // (N_KV_HEADS * LANES)

    def query_block(sb, carry):
        r0 = pl.multiple_of(sb * blk, blk)
        q_rows = pl.ds(r0, blk)
        kv_rows = pl.ds(r0, rows)
        mask = band & ((c_iota >= blk) | (n > 0) | (sb > 0))
        for g in range(N_KV_HEADS):
            hs = slice(g * HEAD_DIM, (g + 1) * HEAD_DIM)
            kg = kbuf_ref[kv_rows, hs]
            vg = vbuf_ref[kv_rows, hs]
            kdup = jnp.concatenate([kg, kg], axis=1).astype(BF16)
            vdup = jnp.concatenate([vg, vg], axis=1).astype(BF16)
            for p in range(group):
                pair = g * group + p
                ls = slice(pair * LANES, (pair + 1) * LANES)
                qp = q_ref[q_rows, ls]
                lhs = jnp.concatenate([jnp.where(lo_lane, qp, zero), jnp.where(lo_lane, zero, qp)], axis=0)
                s = lax.dot_general(lhs, kdup, (((1,), (1,)), ((), ())), preferred_element_type=F32)
                s = jnp.where(mask, s, NEG_INF)
                sink = jnp.where(top_row, sink_ref[2 * pair], sink_ref[2 * pair + 1])
                m = jnp.maximum(jnp.max(s, axis=-1, keepdims=True), sink)
                e = jnp.exp(s - m)
                denom = jnp.sum(e, axis=-1, keepdims=True) + jnp.exp(sink - m)
                o2 = jnp.dot(e.astype(BF16), vdup, preferred_element_type=F32) / denom
                o_ref[q_rows, ls] = jnp.where(lo_lane, o2[:blk], o2[blk:]).astype(o_ref.dtype)
        return carry

    lax.fori_loop(0, q_ref.shape[0] // blk, query_block, 0)


def _attn_prompt(q, k, v, sinks, bsz, slen):
    blk = WINDOW
    step = ATTN_BLOCKS * blk
    ns = slen // step
    qw, kw = q.shape[1], k.shape[1]
    cur = lambda b, n: (b * ns + n, 0)
    prev = lambda b, n: (jnp.maximum((b * ns + n) * ATTN_BLOCKS - 1, 0), 0)
    return pl.pallas_call(
        _attn_prompt_kernel,
        grid=(bsz, ns),
        in_specs=[
            pl.BlockSpec(memory_space=pltpu.SMEM),
            pl.BlockSpec((step, qw), cur),
            pl.BlockSpec((step, kw), cur), pl.BlockSpec((blk, kw), prev),
            pl.BlockSpec((step, kw), cur), pl.BlockSpec((blk, kw), prev),
        ],
        out_specs=pl.BlockSpec((step, qw), cur),
        out_shape=jax.ShapeDtypeStruct((bsz * slen, qw), BF16),
        scratch_shapes=[pltpu.VMEM((blk + step, kw), F32), pltpu.VMEM((blk + step, kw), F32)],
        compiler_params=_cparams(("arbitrary", "arbitrary")),
        name="attn_prompt",
    )(sinks.astype(F32), q, k, k, v, v)


def _attn_sample_kernel(sink_ref, q_ref, kt_ref, kn_ref, vt_ref, vn_ref, o_ref, kt_out_ref, vt_out_ref, *, qlen):
    nseq, pair_rows, pair_w = q_ref.shape
    rows = 2 * pair_rows
    kvw, ncache = kt_ref.shape[1], kt_ref.shape[2]
    per_kv = pair_rows // N_KV_HEADS
    all_rows = nseq * rows
    r_lane = _div_pow2(_mod_pow2(lax.broadcasted_iota(jnp.int32, (all_rows, kvw), 0), pair_rows), per_kv)
    c_lane = _div_pow2(lax.broadcasted_iota(jnp.int32, (all_rows, kvw), 1), HEAD_DIM)
    own = r_lane == c_lane
    mask_c = (lax.broadcasted_iota(jnp.int32, (all_rows, ncache), 1)
              >= _mod_pow2(lax.broadcasted_iota(jnp.int32, (all_rows, ncache), 0), qlen))
    mask_n = (lax.broadcasted_iota(jnp.int32, (all_rows, qlen), 1)
              <= _mod_pow2(lax.broadcasted_iota(jnp.int32, (all_rows, qlen), 0), qlen))
    keep_old = lax.broadcasted_iota(jnp.int32, (kvw, ncache), 1) < ncache - qlen
    sink = sink_ref[...]
    contract_last = (((1,), (1,)), ((), ()))
    seq_rows = lambda b: slice(b * rows, (b + 1) * rows)
    new_rows = lambda b: slice(b * qlen, (b + 1) * qlen)

    pair_shape = (nseq * pair_rows, kvw)
    pair_kv = _div_pow2(_mod_pow2(lax.broadcasted_iota(jnp.int32, pair_shape, 0), pair_rows), per_kv)
    pair_own = pair_kv == _div_pow2(lax.broadcasted_iota(jnp.int32, pair_shape, 1), HEAD_DIM)
    even_kv = (pair_kv & 1) == 0
    q_pairs = q_ref[...].reshape(nseq * pair_rows, pair_w)
    tiled = jnp.concatenate([q_pairs] * (kvw // pair_w), axis=1)
    swapped = jnp.concatenate([q_pairs[:, HEAD_DIM:], q_pairs[:, :HEAD_DIM]] * (kvw // pair_w), axis=1)
    none = jnp.zeros((), q_pairs.dtype)
    q_first = jnp.where(pair_own & even_kv, tiled, jnp.where(pair_own, swapped, none))
    q_second = jnp.where(pair_own & even_kv, swapped, jnp.where(pair_own, tiled, none))
    pair_seq = lambda b: slice(b * pair_rows, (b + 1) * pair_rows)
    qe = [jnp.concatenate([q_first[pair_seq(b)], q_second[pair_seq(b)]], axis=0) for b in range(nseq)]
    sc = jnp.concatenate([jnp.dot(qe[b], kt_ref[b].astype(BF16), preferred_element_type=F32)
                          for b in range(nseq)], axis=0)
    sn = jnp.concatenate([lax.dot_general(qe[b], kn_ref[new_rows(b), :].astype(BF16), contract_last,
                                          preferred_element_type=F32) for b in range(nseq)], axis=0)
    sc = jnp.where(mask_c, sc, NEG_INF)
    sn = jnp.where(mask_n, sn, NEG_INF)
    m = jnp.maximum(jnp.maximum(jnp.max(sc, axis=-1, keepdims=True), jnp.max(sn, axis=-1, keepdims=True)), sink)
    ec = jnp.exp(sc - m)
    en = jnp.exp(sn - m)
    denom = jnp.sum(ec, axis=-1, keepdims=True) + jnp.sum(en, axis=-1, keepdims=True) + jnp.exp(sink - m)
    ec = ec.astype(BF16)
    en = en.astype(BF16)
    o = jnp.concatenate(
        [lax.dot_general(ec[seq_rows(b)], vt_ref[b].astype(BF16), contract_last, preferred_element_type=F32)
         + jnp.dot(en[seq_rows(b)], vn_ref[new_rows(b), :].astype(BF16), preferred_element_type=F32)
         for b in range(nseq)], axis=0)
    o = jnp.where(own, o, 0.0)
    acc = o[:, :pair_w]
    for g in range(1, kvw // pair_w):
        acc = acc + o[:, g * pair_w:(g + 1) * pair_w]
    out = (acc / denom).reshape(nseq, 2, pair_rows, pair_w)
    first = out[:, 0].reshape(nseq * pair_rows, pair_w)
    second = out[:, 1].reshape(nseq * pair_rows, pair_w)
    even = even_kv[:, :pair_w]
    in_place = jnp.where(even, first, second)
    displaced = jnp.where(even, second, first)
    pairs = in_place + pltpu.roll(displaced, HEAD_DIM, axis=1)
    o_ref[...] = pairs.astype(o_ref.dtype).reshape(nseq, pair_rows, pair_w)

    def feature_major(new_ref):
        pad = jnp.zeros((ncache - nseq * qlen, kvw), F32)
        return jnp.concatenate([new_ref[...], pad], axis=0).T

    def shift_in(old_ref, new_ref, out_ref):
        new_t = feature_major(new_ref)
        for b in range(nseq):
            placed = pltpu.roll(new_t, ncache - qlen - b * qlen, axis=1)
            out_ref[b] = jnp.where(keep_old, pltpu.roll(old_ref[b], ncache - qlen, axis=1), placed)

    shift_in(kt_ref, kn_ref, kt_out_ref)
    shift_in(vt_ref, vn_ref, vt_out_ref)


def _attn_sample(q, k_cache, k_new, v_cache, v_new, sinks, seq_per_step=16):
    bsz, ncache = k_cache.shape[:2]
    kvw = k_new.shape[1]
    qlen = q.shape[0] // bsz
    step_rows = seq_per_step * qlen
    first_step, rem = divmod(k_new.shape[0] - bsz * qlen, step_rows)
    assert rem == 0 and step_rows <= ncache
    n_heads = q.shape[1] // HEAD_DIM
    assert n_heads % (2 * N_KV_HEADS) == 0
    rows = n_heads * qlen
    pair_rows = rows // 2
    pair_w = 2 * HEAD_DIM
    qr = q.reshape(bsz, qlen, n_heads // 2, pair_w).transpose(0, 2, 1, 3).reshape(bsz, pair_rows, pair_w)
    by_parity = sinks.astype(F32).reshape(n_heads // 2, 2).T.reshape(n_heads)
    sink_rows = jnp.tile(jnp.repeat(by_parity, qlen), seq_per_step).reshape(seq_per_step * rows, 1)
    feature_major = lambda c: c.transpose(0, 2, 3, 1).reshape(bsz, kvw, ncache)
    seq = lambda i: (i, 0, 0)
    cache_spec = pl.BlockSpec((seq_per_step, kvw, ncache), seq)
    new_spec = pl.BlockSpec((step_rows, kvw), lambda i: (i + first_step, 0))
    o, kt_next, vt_next = pl.pallas_call(
        functools.partial(_attn_sample_kernel, qlen=qlen),
        grid=(bsz // seq_per_step,),
        in_specs=[
            pl.BlockSpec((seq_per_step * rows, 1), lambda i: (0, 0)),
            pl.BlockSpec((seq_per_step, pair_rows, pair_w), seq),
            cache_spec, new_spec, cache_spec, new_spec,
        ],
        out_specs=[pl.BlockSpec((seq_per_step, pair_rows, pair_w), seq), cache_spec, cache_spec],
        out_shape=[jax.ShapeDtypeStruct((bsz, pair_rows, pair_w), BF16),
                   jax.ShapeDtypeStruct((bsz, kvw, ncache), F32), jax.ShapeDtypeStruct((bsz, kvw, ncache), F32)],
        compiler_params=_cparams(("arbitrary",)),
        name="attn_sample",
    )(sink_rows, qr, feature_major(k_cache), k_new, feature_major(v_cache), v_new)
    position_major = lambda t: t.reshape(bsz, N_KV_HEADS, HEAD_DIM, ncache).transpose(0, 3, 1, 2)
    ao = o.reshape(bsz, n_heads // 2, qlen, pair_w).transpose(0, 2, 1, 3).reshape(bsz * qlen, -1)
    return ao, position_major(kt_next), position_major(vt_next)


CONV_HALO = 32
CONV_ROWS = 64
CONV_COLS = 256


def _conv_prompt_kernel(uc_ref, up_ref, w_ref, b_ref, y_ref, sh_ref):
    tile = uc_ref.shape[0]
    n = pl.program_id(1)
    halo = jnp.where(n > 0, up_ref[...], 0.0)
    sh_ref[0, :CONV_HALO, :] = halo
    sh_ref[0, CONV_HALO:, :] = uc_ref[...]
    keep = tile + CONV_HALO - SUBLANES
    for r in range(1, SUBLANES):
        sh_ref[r, :keep, :] = sh_ref[0, r:r + keep, :]
    first = CONV_HALO - (CONV_WIDTH - 1)
    ch = uc_ref.shape[1]

    def rows_step(rc, carry):
        r0 = pl.multiple_of(rc * CONV_ROWS, CONV_ROWS)
        for cc in range(ch // CONV_COLS):
            cs = slice(cc * CONV_COLS, (cc + 1) * CONV_COLS)
            acc = jnp.zeros((CONV_ROWS, CONV_COLS), F32)
            for j in range(CONV_WIDTH):
                off = first + j
                a, r = off // SUBLANES, off % SUBLANES
                acc = acc + sh_ref[r, pl.ds(r0 + a * SUBLANES, CONV_ROWS), cs] * w_ref[j:j + 1, cs]
            y_ref[pl.ds(r0, CONV_ROWS), cs] = acc + b_ref[:, cs]
        return carry

    lax.fori_loop(0, tile // CONV_ROWS, rows_step, 0)


def _conv_prompt(u, w, b, bsz, slen, tile=512):
    ch = u.shape[1]
    nt = slen // tile
    per = tile // CONV_HALO
    return pl.pallas_call(
        _conv_prompt_kernel,
        grid=(bsz, nt),
        in_specs=[
            pl.BlockSpec((tile, ch), lambda bb, n: (bb * nt + n, 0)),
            pl.BlockSpec((CONV_HALO, ch), lambda bb, n: (jnp.maximum((bb * nt + n) * per - 1, 0), 0)),
            pl.BlockSpec((CONV_WIDTH, ch), lambda bb, n: (0, 0)),
            pl.BlockSpec((1, ch), lambda bb, n: (0, 0)),
        ],
        out_specs=pl.BlockSpec((tile, ch), lambda bb, n: (bb * nt + n, 0)),
        out_shape=jax.ShapeDtypeStruct((bsz * slen, ch), F32),
        scratch_shapes=[pltpu.VMEM((SUBLANES, tile + CONV_HALO, ch), F32)],
        compiler_params=_cparams(("arbitrary", "arbitrary")),
        name="conv_prompt",
    )(u, u, w, b.reshape(1, ch))


def _conv_sample_kernel(sc_ref, un_ref, w_ref, b_ref, y_ref, next_ref):
    nctx = sc_ref.shape[0]
    qlen = un_ref.shape[0]
    ch = w_ref.shape[1]

    def row(t):
        return sc_ref[t] if t < nctx else un_ref[t - nctx]

    for i in range(qlen):
        acc = jnp.zeros((sc_ref.shape[1], ch), F32)
        for j in range(CONV_WIDTH):
            acc = acc + row(i + j) * w_ref[j:j + 1, :]
        y_ref[i] = acc + b_ref[...]
    for t in range(nctx):
        next_ref[t] = row(t + qlen)


def _conv_sample(state_t, u_new_t, w, b, seq_per_step=16):
    nctx, bsz, ch = state_t.shape
    qlen = u_new_t.shape[0]
    seq = lambda i: (0, i, 0)
    return pl.pallas_call(
        _conv_sample_kernel,
        grid=(bsz // seq_per_step,),
        in_specs=[
            pl.BlockSpec((nctx, seq_per_step, ch), seq),
            pl.BlockSpec((qlen, seq_per_step, ch), seq),
            pl.BlockSpec((CONV_WIDTH, ch), lambda i: (0, 0)),
            pl.BlockSpec((1, ch), lambda i: (0, 0)),
        ],
        out_specs=[pl.BlockSpec((qlen, seq_per_step, ch), seq), pl.BlockSpec((nctx, seq_per_step, ch), seq)],
        out_shape=[jax.ShapeDtypeStruct((qlen, bsz, ch), F32), jax.ShapeDtypeStruct((nctx, bsz, ch), F32)],
        compiler_params=_cparams(("arbitrary",)),
        name="conv_sample",
    )(state_t, u_new_t, w, b.reshape(1, ch))


def _merge_kernel(y_ref, ao_ref, ga_ref, gc_ref, x_ref, lnw_ref, lnb_ref, wco_ref, wo_ref, fw_ref, wr_ref,
                  h_ref, xn_ref, lg_ref):
    y = y_ref[...]
    mu = jnp.mean(y, axis=-1, keepdims=True)
    yc = y - mu
    var = jnp.mean(yc * yc, axis=-1, keepdims=True)
    yl = yc * lax.rsqrt(var + EPS) * lnw_ref[...] + lnb_ref[...]
    act = yl * jax.nn.sigmoid(yl)
    conv_o = jnp.dot(act.astype(BF16), wco_ref[...], preferred_element_type=F32)
    mix = ga_ref[...].astype(F32) * ao_ref[...].astype(F32) + gc_ref[...].astype(F32) * conv_o
    h = x_ref[...] + jnp.dot(mix.astype(BF16), wo_ref[...], preferred_element_type=F32)
    h_ref[...] = h
    ms = jnp.mean(h * h, axis=-1, keepdims=True)
    xn = (h * lax.rsqrt(ms + EPS) * fw_ref[...]).astype(BF16)
    xn_ref[...] = xn
    lg_ref[...] = jnp.dot(xn, wr_ref[...], preferred_element_type=F32)


def _merge(y, ao, gates, row0, x, ln_w, ln_b, wco, wo, fw, wr, tile=256):
    t, d = x.shape
    ch = y.shape[1]
    off = row0 // tile
    row = lambda i: (i, 0)
    const = lambda shape: pl.BlockSpec(shape, lambda i: (0, 0), pipeline_mode=pl.Buffered(1))
    return pl.pallas_call(
        _merge_kernel,
        grid=(t // tile,),
        in_specs=[
            pl.BlockSpec((tile, ch), row),
            pl.BlockSpec((tile, d), row),
            pl.BlockSpec((tile, d), lambda i: (i + off, 0)),
            pl.BlockSpec((tile, d), lambda i: (i + off, 1)),
            pl.BlockSpec((tile, d), row),
            const((1, ch)), const((1, ch)), const((ch, d)), const((d, d)), const((1, d)),
            const((d, LANES)),
        ],
        out_specs=[pl.BlockSpec((tile, d), row), pl.BlockSpec((tile, d), row), pl.BlockSpec((tile, LANES), row)],
        out_shape=[jax.ShapeDtypeStruct((t, d), F32), jax.ShapeDtypeStruct((t, d), BF16),
                   jax.ShapeDtypeStruct((t, LANES), F32)],
        compiler_params=_cparams(("arbitrary",)),
        name="merge",
    )(y, ao, gates, gates, x, ln_w.reshape(1, ch), ln_b.reshape(1, ch), wco, wo, fw.reshape(1, d), wr)


def _route_kernel(lp_ref, ls_ref, tri_ref, lower_ref, meta_ref, metat_ref, cnt_ref, *, n_p):
    i = pl.program_id(0)
    used = _round_up(N_EXPERTS + N_GROUPS, SUBLANES)

    def body(l_ref):
        lg = l_ref[...].T[:used]
        shape = lg.shape
        row = lax.broadcasted_iota(jnp.int32, shape, 0)
        big = jnp.int32(LANES)
        is_g = (row >= N_EXPERTS) & (row < N_EXPERTS + N_GROUPS)
        gl = jnp.where(is_g, lg, -jnp.inf)
        gmax = jnp.max(gl, axis=0, keepdims=True)
        g_star = jnp.min(jnp.where(gl == gmax, row - N_EXPERTS, big), axis=0, keepdims=True)
        pg_star = 1.0 / jnp.sum(jnp.exp(gl - gmax), axis=0, keepdims=True)
        in_g = (row < N_EXPERTS) & (_div_pow2(row, EXPERTS_PER_GROUP) == g_star)
        el = jnp.where(in_g, lg, -jnp.inf)
        v1 = jnp.max(el, axis=0, keepdims=True)
        i1 = jnp.min(jnp.where(el == v1, row, big), axis=0, keepdims=True)
        el2 = jnp.where(row == i1, -jnp.inf, el)
        v2 = jnp.max(el2, axis=0, keepdims=True)
        i2 = jnp.min(jnp.where(el2 == v2, row, big), axis=0, keepdims=True)
        e2 = jnp.exp(v2 - v1)
        w1 = pg_star * (1.0 / (1.0 + e2))
        w2 = pg_star * (e2 / (1.0 + e2))
        oh1 = row == i1
        oh2 = row == i2
        cnt = jnp.where(oh1 | oh2, 1.0, 0.0)
        before = jnp.dot(cnt.astype(BF16), tri_ref[...], preferred_element_type=F32)
        tot = jnp.sum(cnt, axis=-1, keepdims=True)
        runs = jnp.floor((tot + (CHUNK_ROWS - 1)) * (1.0 / CHUNK_ROWS))
        runs_b = jnp.broadcast_to(runs, (used, LANES)).astype(BF16)
        start = jnp.dot(lower_ref[...], runs_b, preferred_element_type=F32)[:, 0:1] * CHUNK_ROWS
        slot = before + start
        s1 = jnp.sum(jnp.where(oh1, slot, 0.0), axis=0, keepdims=True)
        s2 = jnp.sum(jnp.where(oh2, slot, 0.0), axis=0, keepdims=True)
        fields = [i1.astype(F32), i2.astype(F32), w1, w2, s1, s2]
        fields += [jnp.zeros_like(w1)] * (META_FIELDS - len(fields))
        metat = jnp.concatenate(fields, axis=0)
        metat_ref[...] = metat
        pad = jnp.zeros((LANES - META_FIELDS, shape[1]), F32)
        meta_ref[...] = jnp.concatenate([metat, pad], axis=0).T
        tot_pad = jnp.concatenate([tot, jnp.zeros((LANES - used, 1), F32)], axis=0)
        cnt_ref[...] = jnp.broadcast_to(tot_pad, (LANES, LANES)).T[0:1]

    @pl.when(i < n_p)
    def _():
        body(lp_ref)

    @pl.when(i >= n_p)
    def _():
        body(ls_ref)


def _route(lg_p, lg_s):
    tile = TOKEN_BLOCK
    tp, ts = lg_p.shape[0], lg_s.shape[0]
    n_p, n_s = tp // tile, ts // tile
    nb = n_p + n_s
    used = _round_up(N_EXPERTS + N_GROUPS, SUBLANES)
    tri = (jnp.arange(tile)[:, None] < jnp.arange(tile)[None, :]).astype(BF16)
    lower = (jnp.arange(used)[:, None] > jnp.arange(used)[None, :]).astype(BF16)
    return pl.pallas_call(
        functools.partial(_route_kernel, n_p=n_p),
        grid=(nb,),
        in_specs=[
            pl.BlockSpec((tile, LANES), lambda i: (jnp.minimum(i, n_p - 1), 0)),
            pl.BlockSpec((tile, LANES), lambda i: (jnp.maximum(i - n_p, 0), 0)),
            pl.BlockSpec((tile, tile), lambda i: (0, 0)),
            pl.BlockSpec((used, used), lambda i: (0, 0)),
        ],
        out_specs=[pl.BlockSpec((tile, LANES), lambda i: (i, 0)),
                   pl.BlockSpec((None, META_FIELDS, tile), lambda i: (i, 0, 0)),
                   pl.BlockSpec((None, 1, LANES), lambda i: (i, 0, 0))],
        out_shape=[jax.ShapeDtypeStruct((tp + ts, LANES), F32),
                   jax.ShapeDtypeStruct((nb, META_FIELDS, tile), F32),
                   jax.ShapeDtypeStruct((nb, 1, LANES), F32)],
        compiler_params=_cparams(("arbitrary",)),
        name="route",
    )(lg_p, lg_s, tri, lower)


def _plan_sizes(n_tok):
    nb = n_tok // TOKEN_BLOCK
    pad_per_block = N_EXPERTS * (CHUNK_ROWS - 1)
    max_chunks = _round_up(-(-(TOKEN_BLOCK * TOP_K + pad_per_block) // CHUNK_ROWS), SUBLANES)
    n_rows = _round_up(n_tok * TOP_K + nb * pad_per_block + N_EXPERTS * (ROW_PAD - 1), ROW_PAD)
    n_items = (n_rows + N_EXPERTS * (EXPERT_ROWS - ROW_PAD)) // EXPERT_ROWS
    max_zero = N_EXPERTS * (ROW_PAD // CHUNK_ROWS)
    return nb, max_chunks, n_rows, n_items, max_zero


def _dispatch_plan(counts, max_chunks, n_items, max_zero):
    i32 = jnp.int32
    c = counts[:, 0, :N_EXPERTS].astype(i32)
    run = (c + CHUNK_ROWS - 1) // CHUNK_ROWS
    run_end = jnp.cumsum(run, axis=1)
    src_start = run_end - run
    nchunk = run_end[:, -1]
    seg = jnp.sum(run, axis=0) * CHUNK_ROWS
    padded = (seg + ROW_PAD - 1) // ROW_PAD * ROW_PAD
    pad_end = jnp.cumsum(padded)
    pad_start = pad_end - padded
    dst_start = pad_start[None, :] // CHUNK_ROWS + (jnp.cumsum(run, axis=0) - run)
    jj = jnp.arange(max_chunks, dtype=i32)[None, :, None]
    in_run = (src_start[:, None, :] <= jj) & (jj < run_end[:, None, :])
    chunk_dst = jnp.sum(jnp.where(in_run, (dst_start - src_start)[:, None, :] + jj, 0), axis=-1) * CHUNK_ROWS
    tail = (padded - seg) // CHUNK_ROWS
    tail_end = jnp.cumsum(tail)
    tail_start = tail_end - tail
    z = jnp.arange(max_zero, dtype=i32)[:, None]
    in_tail = (tail_start[None, :] <= z) & (z < tail_end[None, :])
    zero_base = (pad_start + seg) // CHUNK_ROWS - tail_start
    zero_dst = jnp.sum(jnp.where(in_tail, zero_base[None, :] + z, 0), axis=-1) * CHUNK_ROWS
    nzero = tail_end[-1]
    items_e = (padded + EXPERT_ROWS - 1) // EXPERT_ROWS
    item_end = jnp.cumsum(items_e)
    item_start = item_end - items_e
    w = jnp.arange(n_items, dtype=i32)[:, None]
    in_item = (item_start[None, :] <= w) & (w < item_end[None, :])
    k = w - item_start[None, :]
    e_ids = jnp.arange(N_EXPERTS, dtype=i32)
    last_e = jnp.max(jnp.where(items_e > 0, e_ids, 0))
    item_e = jnp.where(w[:, 0] < item_end[-1], jnp.sum(jnp.where(in_item, e_ids[None, :], 0), axis=-1), last_e)
    item_row0 = jnp.sum(jnp.where(in_item, pad_start[None, :] + k * EXPERT_ROWS, 0), axis=-1)
    item_rows = jnp.sum(jnp.where(in_item, jnp.clip(padded[None, :] - k * EXPERT_ROWS, 0, EXPERT_ROWS), 0), axis=-1)
    item_nsub = item_rows // ROW_PAD
    return (chunk_dst.reshape(-1).astype(i32), nchunk.astype(i32), zero_dst.astype(i32),
            nzero.reshape(1).astype(i32), item_e.astype(i32), item_row0.astype(i32), item_nsub.astype(i32))


SCATTER_GROUPS = 2


def _scatter_kernel(cd_ref, nc_ref, zd_ref, nz_ref, mt_ref, xp_ref, xs_ref, out_hbm, loc_ref, zero_ref, sem,
                    zero_sem, *, n_p, max_chunks):
    i = pl.program_id(0)
    buf = i % 2
    n_slots = loc_ref.shape[1]
    d = xp_ref.shape[1]
    mt = mt_ref[...]
    w1, w2 = mt[2:3, :], mt[3:4, :]
    s1, s2 = mt[4:5, :].astype(jnp.int32), mt[5:6, :].astype(jnp.int32)
    half = d // 2
    group = n_slots // SCATTER_GROUPS

    def sort_rows(x_ref):
        for s0 in range(0, n_slots, group):
            slot = lax.broadcasted_iota(jnp.int32, (group, mt.shape[1]), 0) + s0
            p1 = slot == s1
            p2 = slot == s2
            perm = jnp.where(p1 | p2, 1.0, 0.0).astype(BF16)
            wcol = jnp.sum(jnp.where(p1, w1, 0.0) + jnp.where(p2, w2, 0.0), axis=-1, keepdims=True)
            loc_ref[buf, s0:s0 + group, half:] = pltpu.bitcast(jnp.broadcast_to(wcol, (group, LANES)), jnp.uint32)
            rows = jnp.dot(perm, x_ref[...], preferred_element_type=F32)
            loc_ref[buf, s0:s0 + group, :half] = _pack_pairs(rows[:, :half], rows[:, half:])

    @pl.when(i < n_p)
    def _():
        sort_rows(xp_ref)

    @pl.when(i >= n_p)
    def _():
        sort_rows(xs_ref)

    def rows_at(r):
        return pl.ds(pl.multiple_of(r, CHUNK_ROWS), CHUNK_ROWS)

    def chunk_copy(blk, j):
        return pltpu.make_async_copy(loc_ref.at[blk % 2, rows_at(j * CHUNK_ROWS)],
                                     out_hbm.at[rows_at(cd_ref[blk * max_chunks + j])], sem.at[blk % 2])

    def zero_copy(z):
        return pltpu.make_async_copy(zero_ref, out_hbm.at[rows_at(zd_ref[z])], zero_sem)

    @pl.when(i > 0)
    def _():
        _for_each(nc_ref[i - 1], lambda j: chunk_copy(i - 1, j).wait())

    _for_each(nc_ref[i], lambda j: chunk_copy(i, j).start())

    @pl.when(i == pl.num_programs(0) - 1)
    def _():
        zero_ref[...] = jnp.zeros_like(zero_ref)
        _for_each(nz_ref[0], lambda z: zero_copy(z).start())
        _for_each(nc_ref[i], lambda j: chunk_copy(i, j).wait())
        _for_each(nz_ref[0], lambda z: zero_copy(z).wait())


def _scatter_rows(xn_p, xn_s, metat, chunk_dst, nchunk, zero_dst, nzero, max_chunks, n_rows):
    tile = TOKEN_BLOCK
    tp, d = xn_p.shape
    ts = xn_s.shape[0]
    n_p, n_s = tp // tile, ts // tile
    n_slots = max_chunks * CHUNK_ROWS
    width = d // 2 + LANES
    u32 = jnp.uint32
    return pl.pallas_call(
        functools.partial(_scatter_kernel, n_p=n_p, max_chunks=max_chunks),
        grid_spec=pltpu.PrefetchScalarGridSpec(
            num_scalar_prefetch=4,
            grid=(n_p + n_s,),
            in_specs=[
                pl.BlockSpec((None, META_FIELDS, tile), lambda i, *_: (i, 0, 0)),
                pl.BlockSpec((tile, d), lambda i, *_: (jnp.minimum(i, n_p - 1), 0)),
                pl.BlockSpec((tile, d), lambda i, *_: (jnp.maximum(i - n_p, 0), 0)),
            ],
            out_specs=pl.BlockSpec(memory_space=pl.ANY),
            scratch_shapes=[pltpu.VMEM((2, n_slots, width), u32), pltpu.VMEM((CHUNK_ROWS, width), u32),
                            pltpu.SemaphoreType.DMA((2,)), pltpu.SemaphoreType.DMA(())],
        ),
        out_shape=jax.ShapeDtypeStruct((n_rows, width), u32),
        compiler_params=_cparams(("arbitrary",)),
        name="scatter_rows",
    )(chunk_dst, nchunk, zero_dst, nzero, metat, xn_p, xn_s)


def _expert_kernel(e_ref, row0_ref, nsub_ref, xs_hbm, wg_ref, wu_ref, wd_ref, out_hbm,
                   xf_ref, xb_ref, acc_ref, ob_ref, pend_ref, in_sem, out_sem, *, n_chunks):
    w = pl.program_id(0)
    c = pl.program_id(1)
    n_items = pl.num_programs(0)
    last_c = pl.num_programs(1) - 1
    nsub = nsub_ref[w]
    d = acc_ref.shape[1]
    half = d // 2
    slot = w % 2

    def sub_rows(s):
        return pl.ds(pl.multiple_of(s * ROW_PAD, ROW_PAD), ROW_PAD)

    def hbm_rows(item, s):
        return pl.ds(pl.multiple_of(row0_ref[item] + s * ROW_PAD, ROW_PAD), ROW_PAD)

    def in_copy(item, s):
        return pltpu.make_async_copy(xs_hbm.at[hbm_rows(item, s)], xf_ref.at[item % 2, sub_rows(s)],
                                     in_sem.at[item % 2])

    def out_copy(item, s):
        return pltpu.make_async_copy(ob_ref.at[sub_rows(s)], out_hbm.at[hbm_rows(item, s)], out_sem)

    def loop(n, fn):
        def body(s, carry):
            fn(s)
            return carry
        lax.fori_loop(0, n, body, 0)

    @pl.when(c == 0)
    def _():
        @pl.when(w == 0)
        def _():
            pend_ref[0] = 0
            pend_ref[1] = 0
            loop(nsub, lambda s: in_copy(w, s).start())

        loop(nsub, lambda s: in_copy(w, s).wait())
        nxt = jnp.minimum(w + 1, n_items - 1)

        @pl.when(w + 1 < n_items)
        def _():
            loop(nsub_ref[nxt], lambda s: in_copy(nxt, s).start())

    def drain_out():
        prev = pend_ref[1]
        loop(pend_ref[0], lambda s: out_copy(prev, s).wait())
        pend_ref[0] = 0

    def ffn_chunk(x):
        g = jnp.dot(x, wg_ref[...].astype(BF16), preferred_element_type=F32)
        u = jnp.dot(x, wu_ref[...].astype(BF16), preferred_element_type=F32)
        hmid = (g * jax.nn.sigmoid(g) * u).astype(BF16)
        return jnp.dot(hmid, wd_ref[...].astype(BF16), preferred_element_type=F32)

    for n in range(1, EXPERT_ROWS // ROW_PAD + 1):
        m = n * ROW_PAD

        @pl.when((nsub == n) & (c == 0))
        def _(m=m):
            xa, xb = _unpack_pairs(xf_ref[slot, :m, :half])
            x = jnp.concatenate([xa, xb], axis=1).astype(BF16)
            xb_ref[:m, :] = x
            acc_ref[:m, :] = ffn_chunk(x)

        if n_chunks > 2:
            @pl.when((nsub == n) & (c > 0) & (c < last_c))
            def _(m=m):
                acc_ref[:m, :] += ffn_chunk(xb_ref[:m, :])

        @pl.when((nsub == n) & (c == last_c))
        def _(m=m):
            drain_out()
            mix_w = pltpu.bitcast(xf_ref[slot, :m, half:half + 1], F32)
            y = ((acc_ref[:m, :] + ffn_chunk(xb_ref[:m, :])) * mix_w).astype(BF16).astype(F32)
            ob_ref[:m, :] = _pack_pairs(y[:, :half], y[:, half:])
            loop(nsub, lambda s: out_copy(w, s).start())
            pend_ref[0] = nsub
            pend_ref[1] = w

    @pl.when((w == n_items - 1) & (c == last_c))
    def _():
        drain_out()


def _experts(xs, item_e, item_row0, item_nsub, w_gate, w_up, w_down):
    n_rows, width = xs.shape
    d = w_gate.shape[1]
    assert width == d // 2 + LANES
    d_exp = w_gate.shape[2]
    nc = d_exp // D_CHUNK
    assert nc >= 2
    n_items = item_e.shape[0]

    def chunk_of(w, c, nsub_ref):
        return jnp.where(nsub_ref[w] > 0, c, nc - 1)

    return pl.pallas_call(
        functools.partial(_expert_kernel, n_chunks=nc),
        grid_spec=pltpu.PrefetchScalarGridSpec(
            num_scalar_prefetch=3,
            grid=(n_items, nc),
            in_specs=[
                pl.BlockSpec(memory_space=pl.ANY),
                pl.BlockSpec((None, d, D_CHUNK), lambda w, c, e, r, n: (e[w], 0, chunk_of(w, c, n))),
                pl.BlockSpec((None, d, D_CHUNK), lambda w, c, e, r, n: (e[w], 0, chunk_of(w, c, n))),
                pl.BlockSpec((None, D_CHUNK, d), lambda w, c, e, r, n: (e[w], chunk_of(w, c, n), 0)),
            ],
            out_specs=pl.BlockSpec(memory_space=pl.ANY),
            scratch_shapes=[
                pltpu.VMEM((2, EXPERT_ROWS, width), jnp.uint32),
                pltpu.VMEM((EXPERT_ROWS, d), BF16),
                pltpu.VMEM((EXPERT_ROWS, d), F32),
                pltpu.VMEM((EXPERT_ROWS, d // 2), jnp.uint32),
                pltpu.SMEM((2,), jnp.int32),
                pltpu.SemaphoreType.DMA((2,)),
                pltpu.SemaphoreType.DMA(()),
            ],
        ),
        out_shape=jax.ShapeDtypeStruct((n_rows, d // 2), jnp.uint32),
        compiler_params=_cparams(("arbitrary", "arbitrary")),
        name="experts",
    )(item_e, item_row0, item_nsub, xs, w_gate, w_up, w_down)


COMBINE_SPLIT = 2
SLOT_STEP = 256


def _combine_kernel(cd_ref, nc_ref, meta_ref, hp_ref, hs_ref, yb_hbm, yp_ref, ys_ref, loc_ref, sem,
                    *, n_p, max_chunks):
    i = pl.program_id(0)
    j = pl.program_id(1)
    nb = pl.num_programs(0)
    buf = i % 2
    n_slots, half = loc_ref.shape[1], loc_ref.shape[2]
    d = 2 * half

    def rows_at(r):
        return pl.ds(pl.multiple_of(r, CHUNK_ROWS), CHUNK_ROWS)

    def chunk_copy(blk, c):
        return pltpu.make_async_copy(yb_hbm.at[rows_at(cd_ref[blk * max_chunks + c])],
                                     loc_ref.at[blk % 2, rows_at(c * CHUNK_ROWS)], sem.at[blk % 2])

    def fetch(blk):
        n = nc_ref[blk]
        _for_each(n, lambda c: chunk_copy(blk, c).start())

        def clear(c, carry):
            loc_ref[blk % 2, rows_at(c * CHUNK_ROWS), :] = jnp.zeros((CHUNK_ROWS, half), jnp.uint32)
            return carry
        lax.fori_loop(n, max_chunks, clear, 0)

    @pl.when(j == 0)
    def _():
        @pl.when(i == 0)
        def _():
            fetch(i)

        _for_each(nc_ref[i], lambda c: chunk_copy(i, c).wait())

        @pl.when(i + 1 < nb)
        def _():
            fetch(jnp.minimum(i + 1, nb - 1))

    meta = meta_ref[...]
    s1 = meta[:, 4:5].astype(jnp.int32)
    s2 = meta[:, 5:6].astype(jnp.int32)
    rows = meta.shape[0]
    f = jnp.zeros((rows, d), F32)
    for r0 in range(0, n_slots, SLOT_STEP):
        slot = lax.broadcasted_iota(jnp.int32, (rows, SLOT_STEP), 1) + r0
        pt = jnp.where((slot == s1) | (slot == s2), 1.0, 0.0).astype(BF16)
        ya, yb = _unpack_pairs(loc_ref[buf, r0:r0 + SLOT_STEP, :])
        f = f + jnp.dot(pt, jnp.concatenate([ya, yb], axis=1).astype(BF16), preferred_element_type=F32)
    g = i * pl.num_programs(1) + j

    @pl.when(g < n_p)
    def _():
        yp_ref[...] = hp_ref[...] + f

    @pl.when(g >= n_p)
    def _():
        ys_ref[...] = hs_ref[...] + f


def _combine(h_p, h_s, meta, chunk_dst, nchunk, yb, max_chunks):
    tp, d = h_p.shape
    ts = h_s.shape[0]
    tile = TOKEN_BLOCK // COMBINE_SPLIT
    n_p, n_s = tp // tile, ts // tile
    nb = (tp + ts) // TOKEN_BLOCK
    n_slots = max_chunks * CHUNK_ROWS
    assert n_slots % SLOT_STEP == 0
    blk = lambda i, j: i * COMBINE_SPLIT + j
    p_idx = lambda i, j, *_: (jnp.minimum(blk(i, j), n_p - 1), 0)
    s_idx = lambda i, j, *_: (jnp.maximum(blk(i, j) - n_p, 0), 0)
    return pl.pallas_call(
        functools.partial(_combine_kernel, n_p=n_p, max_chunks=max_chunks),
        grid_spec=pltpu.PrefetchScalarGridSpec(
            num_scalar_prefetch=2,
            grid=(nb, COMBINE_SPLIT),
            in_specs=[
                pl.BlockSpec((tile, LANES), lambda i, j, *_: (blk(i, j), 0)),
                pl.BlockSpec((tile, d), p_idx),
                pl.BlockSpec((tile, d), s_idx),
                pl.BlockSpec(memory_space=pl.ANY),
            ],
            out_specs=[pl.BlockSpec((tile, d), p_idx), pl.BlockSpec((tile, d), s_idx)],
            scratch_shapes=[pltpu.VMEM((2, n_slots, d // 2), jnp.uint32), pltpu.SemaphoreType.DMA((2,))],
        ),
        out_shape=[jax.ShapeDtypeStruct((tp, d), F32), jax.ShapeDtypeStruct((ts, d), F32)],
        compiler_params=_cparams(("arbitrary", "arbitrary")),
        name="combine",
    )(chunk_dst, nchunk, meta, h_p, h_s, yb)


def _layer(x_p, x_s, ck, cv, sc, attn_norm_w, w_in, b_gate, q_norm_w, k_norm_w, sinks, conv_dw_w, conv_dw_b,
           conv_ln_w, conv_ln_b, w_conv_out, w_out, ffn_norm_w, router_group, router_expert, w_gate, w_up,
           w_down):
    bsz, slen, d = x_p.shape
    dbs, dlen, _ = x_s.shape
    tp, ts = bsz * slen, dbs * dlen
    conv_ch = conv_dw_w.shape[1]
    kvw = N_KV_HEADS * HEAD_DIM
    xp2, xs2 = x_p.reshape(tp, d), x_s.reshape(ts, d)

    xn = _prenorm(xp2, xs2, attn_norm_w)
    q, k, v, u, gates = _project(xn, w_in, b_gate, q_norm_w, k_norm_w, d, conv_ch)

    ao_p = _attn_prompt(q, k, v, sinks, bsz, slen)
    ao_s, ck_next, cv_next = _attn_sample(q[tp:], ck, k, cv, v, sinks)

    y_p = _conv_prompt(u, conv_dw_w, conv_dw_b, bsz, slen)
    sc_t = sc.transpose(1, 0, 2)
    u_s_t = u[tp:].reshape(dbs, dlen, conv_ch).transpose(1, 0, 2)
    y_s_t, sc_next_t = _conv_sample(sc_t, u_s_t, conv_dw_w, conv_dw_b)
    y_s = y_s_t.transpose(1, 0, 2).reshape(ts, conv_ch)

    wco = w_conv_out.astype(BF16)
    wo = w_out.astype(BF16)
    wr = jnp.concatenate([router_expert, router_group,
                          jnp.zeros((d, LANES - N_EXPERTS - N_GROUPS), router_expert.dtype)], axis=1).astype(BF16)
    margs = (conv_ln_w, conv_ln_b, wco, wo, ffn_norm_w, wr)
    h_p, xn_p, lg_p = _merge(y_p, ao_p, gates, 0, xp2, *margs)
    h_s, xn_s, lg_s = _merge(y_s, ao_s, gates, tp, xs2, *margs)

    meta, metat, counts = _route(lg_p, lg_s)
    _, max_chunks, n_rows, n_items, max_zero = _plan_sizes(tp + ts)
    chunk_dst, nchunk, zero_dst, nzero, item_e, item_row0, item_nsub = _dispatch_plan(
        counts, max_chunks, n_items, max_zero)
    xs_rows = _scatter_rows(xn_p, xn_s, metat, chunk_dst, nchunk, zero_dst, nzero, max_chunks, n_rows)
    yb = _experts(xs_rows, item_e, item_row0, item_nsub, w_gate, w_up, w_down)
    out_p, out_s = _combine(h_p, h_s, meta, chunk_dst, nchunk, yb, max_chunks)

    def seq_tails(a, n):
        return jnp.stack([lax.slice_in_dim(a, (b + 1) * slen - n, (b + 1) * slen, axis=0) for b in range(bsz)])

    new_p = (seq_tails(k, WINDOW).reshape(bsz, WINDOW, N_KV_HEADS, HEAD_DIM),
             seq_tails(v, WINDOW).reshape(bsz, WINDOW, N_KV_HEADS, HEAD_DIM),
             seq_tails(u, CONV_WIDTH - 1))
    new_s = (ck_next, cv_next, sc_next_t.transpose(1, 0, 2))
    return out_p.reshape(bsz, slen, d), out_s.reshape(dbs, dlen, d), new_p, new_s


def kernel(x_prompt, x_sample, cache_k, cache_v, state_conv, attn_norm_w, w_in, b_gate, q_norm_w, k_norm_w,
           sinks, conv_dw_w, conv_dw_b, conv_ln_w, conv_ln_b, w_conv_out, w_out, ffn_norm_w, router_group,
           router_expert, w_gate, w_up, w_down):
    x_p, x_s = x_prompt, x_sample
    kp, vp, cp, ks_, vs_, cs_ = [], [], [], [], [], []
    for l in range(cache_k.shape[0]):
        x_p, x_s, new_p, new_s = _layer(
            x_p, x_s, cache_k[l], cache_v[l], state_conv[l], attn_norm_w[l], w_in[l], b_gate[l], q_norm_w[l],
            k_norm_w[l], sinks[l], conv_dw_w[l], conv_dw_b[l], conv_ln_w[l], conv_ln_b[l], w_conv_out[l],
            w_out[l], ffn_norm_w[l], router_group[l], router_expert[l], w_gate[l], w_up[l], w_down[l])
        kp.append(new_p[0]); vp.append(new_p[1]); cp.append(new_p[2])
        ks_.append(new_s[0]); vs_.append(new_s[1]); cs_.append(new_s[2])
    return (x_p, x_s, jnp.stack(kp), jnp.stack(vp), jnp.stack(cp),
            jnp.stack(ks_), jnp.stack(vs_), jnp.stack(cs_))
```

```python
import functools

import jax
import jax.numpy as jnp
from jax import lax
from jax.experimental import pallas as pl
from jax.experimental.pallas import tpu as pltpu

F32 = jnp.float32
BF16 = jnp.bfloat16
EPS = 1e-6
NEG_INF = -1e30

HEAD_DIM = 64
N_KV_HEADS = 4
WINDOW = 128
CONV_WIDTH = 31
N_GROUPS = 4
EXPERTS_PER_GROUP = 8
N_EXPERTS = N_GROUPS * EXPERTS_PER_GROUP
TOP_K = 2

LANES = 128
SUBLANES = 8
MXU_DIM = 256
VMEM_LIMIT = 56 * 1024 * 1024

TOKEN_BLOCK = 512
CHUNK_ROWS = SUBLANES
ROW_PAD = 128
EXPERT_ROWS = 768
D_CHUNK = 512
META_FIELDS = 8


def _cparams(sem):
    return pltpu.CompilerParams(dimension_semantics=sem, vmem_limit_bytes=VMEM_LIMIT)


def _log2(n):
    assert n > 0 and n & (n - 1) == 0, n
    return n.bit_length() - 1


def _div_pow2(x, n):
    return lax.shift_right_logical(x, jnp.int32(_log2(n)))


def _mod_pow2(x, n):
    _log2(n)
    return x & jnp.int32(n - 1)


def _round_up(x, m):
    return (x + m - 1) // m * m


DMA_UNROLL = 8


def _for_each(n, fn):
    groups = lax.shift_right_logical(n, jnp.int32(_log2(DMA_UNROLL)))

    def group(g, carry):
        for u in range(DMA_UNROLL):
            fn(g * DMA_UNROLL + u)
        return carry
    lax.fori_loop(0, groups, group, 0)

    def single(j, carry):
        fn(j)
        return carry
    lax.fori_loop(groups * DMA_UNROLL, n, single, 0)


HIGH_HALF = 0xFFFF0000


def _pack_pairs(a, b):
    ua = pltpu.bitcast(a, jnp.uint32) & jnp.uint32(HIGH_HALF)
    return ua | (pltpu.bitcast(b, jnp.uint32) >> 16)


def _unpack_pairs(w):
    return (pltpu.bitcast(w & jnp.uint32(HIGH_HALF), F32), pltpu.bitcast(w << 16, F32))


def _prenorm_kernel(xp_ref, xs_ref, w_ref, o_ref, *, n_p):
    i = pl.program_id(0)

    def body(x_ref):
        x = x_ref[...]
        ms = jnp.mean(x * x, axis=-1, keepdims=True)
        o_ref[...] = (x * lax.rsqrt(ms + EPS) * w_ref[...]).astype(o_ref.dtype)

    @pl.when(i < n_p)
    def _():
        body(xp_ref)

    @pl.when(i >= n_p)
    def _():
        body(xs_ref)


def _prenorm(xp, xs, w, tile=512):
    tp, d = xp.shape
    ts = xs.shape[0]
    n_p, n_s = tp // tile, ts // tile
    return pl.pallas_call(
        functools.partial(_prenorm_kernel, n_p=n_p),
        grid=(n_p + n_s,),
        in_specs=[
            pl.BlockSpec((tile, d), lambda i: (jnp.minimum(i, n_p - 1), 0)),
            pl.BlockSpec((tile, d), lambda i: (jnp.maximum(i - n_p, 0), 0)),
            pl.BlockSpec((1, d), lambda i: (0, 0)),
        ],
        out_specs=pl.BlockSpec((tile, d), lambda i: (i, 0)),
        out_shape=jax.ShapeDtypeStruct((tp + ts, d), BF16),
        compiler_params=_cparams(("arbitrary",)),
        name="prenorm",
    )(xp, xs, w.reshape(1, d))


def _head_rms(z, hw, bd_ref):
    ss = z * z
    hi = ss.astype(BF16)
    lo = (ss - hi.astype(F32)).astype(BF16)
    bd = bd_ref[...]
    tot = jnp.dot(hi, bd, preferred_element_type=F32) + jnp.dot(lo, bd, preferred_element_type=F32)
    return z * lax.rsqrt(tot * (1.0 / HEAD_DIM) + EPS) * hw


def _load_w(i, w_ref, wb_ref):
    @pl.when(i == 0)
    def _():
        wb_ref[...] = w_ref[...].astype(BF16)


PROJ_SUB_NORM = 544
PROJ_SUB_ACT = 272


def _row_groups(n_rows, sub):
    sub = sub if n_rows % sub == 0 else n_rows
    return [slice(r0, r0 + sub) for r0 in range(0, n_rows, sub)]


def _proj_q_kernel(x_ref, w_ref, hw_ref, bd_ref, o_ref, wb_ref):
    _load_w(pl.program_id(1), w_ref, wb_ref)
    for rs in _row_groups(x_ref.shape[0], PROJ_SUB_NORM):
        z = jnp.dot(x_ref[rs, :], wb_ref[...], preferred_element_type=F32)
        for c in range(z.shape[1] // MXU_DIM):
            sl = slice(c * MXU_DIM, (c + 1) * MXU_DIM)
            o_ref[rs, sl] = _head_rms(z[:, sl], hw_ref[...], bd_ref).astype(o_ref.dtype)


def _proj_kv_kernel(x_ref, w_ref, hw_ref, bd_ref, k_ref, v_ref, wb_ref):
    _load_w(pl.program_id(1), w_ref, wb_ref)
    kw = k_ref.shape[1]
    for rs in _row_groups(x_ref.shape[0], PROJ_SUB_NORM):
        z = jnp.dot(x_ref[rs, :], wb_ref[...], preferred_element_type=F32)
        k_ref[rs, :] = _head_rms(z[:, :kw], hw_ref[...], bd_ref)
        v_ref[rs, :] = z[:, kw:]


def _proj_glu_kernel(x_ref, wa_ref, wg_ref, o_ref, wab_ref, wgb_ref):
    i = pl.program_id(1)
    _load_w(i, wa_ref, wab_ref)
    _load_w(i, wg_ref, wgb_ref)
    for rs in _row_groups(x_ref.shape[0], PROJ_SUB_ACT):
        x = x_ref[rs, :]
        a = jnp.dot(x, wab_ref[...], preferred_element_type=F32)
        g = jnp.dot(x, wgb_ref[...], preferred_element_type=F32)
        o_ref[rs, :] = a * jax.nn.sigmoid(g)


def _proj_gate_kernel(x_ref, w_ref, b_ref, o_ref, wb_ref):
    _load_w(pl.program_id(1), w_ref, wb_ref)
    for rs in _row_groups(x_ref.shape[0], PROJ_SUB_ACT):
        z = jnp.dot(x_ref[rs, :], wb_ref[...], preferred_element_type=F32)
        o_ref[rs, :] = jax.nn.sigmoid(z + b_ref[...]).astype(o_ref.dtype)


def _row_tile(t, cap=2304):
    best = 16
    for m in range(16, cap + 1, 16):
        if t % m == 0:
            best = m
    return best


def _project(xn, w_in, b_gate, q_norm_w, k_norm_w, d_model, conv_ch):
    t, d = xn.shape
    q_w = d_model
    kv_w = N_KV_HEADS * HEAD_DIM
    tn = 2 * kv_w
    tm = _row_tile(t)
    ni = t // tm
    reps = MXU_DIM // HEAD_DIM
    hq = jnp.tile(q_norm_w.astype(F32) * HEAD_DIM ** -0.5, reps).reshape(1, MXU_DIM)
    hk = jnp.tile(k_norm_w.astype(F32), reps).reshape(1, MXU_DIM)
    gid = jnp.arange(MXU_DIM) // HEAD_DIM
    bd = (gid[:, None] == gid[None, :]).astype(BF16)

    x_spec = pl.BlockSpec((tm, d), lambda s, i: (i, 0))
    small = lambda shape: pl.BlockSpec(shape, lambda s, i: (0, 0))

    def w_spec(c0):
        return pl.BlockSpec((d, tn), lambda s, i: (0, c0 + s))

    wscr = pltpu.VMEM((d, tn), BF16)
    cp = _cparams(("arbitrary", "arbitrary"))

    q = pl.pallas_call(
        _proj_q_kernel, grid=(q_w // tn, ni),
        in_specs=[x_spec, w_spec(0), small((1, MXU_DIM)), small((MXU_DIM, MXU_DIM))],
        out_specs=pl.BlockSpec((tm, tn), lambda s, i: (i, s)),
        out_shape=jax.ShapeDtypeStruct((t, q_w), BF16),
        scratch_shapes=[wscr], compiler_params=cp, name="proj_q",
    )(xn, w_in, hq, bd)

    c_kv = q_w // tn
    k, v = pl.pallas_call(
        _proj_kv_kernel, grid=(1, ni),
        in_specs=[x_spec, w_spec(c_kv), small((1, MXU_DIM)), small((MXU_DIM, MXU_DIM))],
        out_specs=[pl.BlockSpec((tm, kv_w), lambda s, i: (i, 0))] * 2,
        out_shape=[jax.ShapeDtypeStruct((t, kv_w), F32)] * 2,
        scratch_shapes=[wscr], compiler_params=cp, name="proj_kv",
    )(xn, w_in, hk, bd)

    c_a = c_kv + 1
    n_glu = conv_ch // tn
    u = pl.pallas_call(
        _proj_glu_kernel, grid=(n_glu, ni),
        in_specs=[x_spec, w_spec(c_a), w_spec(c_a + n_glu)],
        out_specs=pl.BlockSpec((tm, tn), lambda s, i: (i, s)),
        out_shape=jax.ShapeDtypeStruct((t, conv_ch), F32),
        scratch_shapes=[wscr, wscr], compiler_params=cp, name="proj_glu",
    )(xn, w_in, w_in)

    c_g = c_a + 2 * n_glu
    n_gate = 2 * d_model // tn
    gates = pl.pallas_call(
        _proj_gate_kernel, grid=(n_gate, ni),
        in_specs=[x_spec, w_spec(c_g), pl.BlockSpec((1, tn), lambda s, i: (0, s))],
        out_specs=pl.BlockSpec((tm, tn), lambda s, i: (i, s)),
        out_shape=jax.ShapeDtypeStruct((t, 2 * d_model), BF16),
        scratch_shapes=[wscr], compiler_params=cp, name="proj_gate",
    )(xn, w_in, b_gate.reshape(1, -1))
    return q, k, v, u, gates


ATTN_BLOCKS = 4


def _attn_prompt_kernel(sink_ref, q_ref, kc_ref, kp_ref, vc_ref, vp_ref, o_ref, kbuf_ref, vbuf_ref):
    blk = WINDOW
    n = pl.program_id(1)
    rows = 2 * blk
    kbuf_ref[:blk, :] = kp_ref[...]
    kbuf_ref[blk:, :] = kc_ref[...]
    vbuf_ref[:blk, :] = vp_ref[...]
    vbuf_ref[blk:, :] = vc_ref[...]
    r_iota = lax.broadcasted_iota(jnp.int32, (rows, rows), 0)
    c_iota = lax.broadcasted_iota(jnp.int32, (rows, rows), 1)
    qi = jnp.where(r_iota >= blk, r_iota - blk, r_iota)
    band = (c_iota >= qi) & (c_iota <= qi + WINDOW)
    lo_lane = lax.broadcasted_iota(jnp.int32, (blk, LANES), 1) < HEAD_DIM
    top_row = lax.broadcasted_iota(jnp.int32, (rows, 1), 0) < blk
    zero = jnp.zeros((blk, LANES), q_ref.dtype)
    group = q_ref.shape[1] // (N_KV_HEADS * LANES)

    def query_block(sb, carry):
        r0 = pl.multiple_of(sb * blk, blk)
        q_rows = pl.ds(r0, blk)
        kv_rows = pl.ds(r0, rows)
        mask = band & ((c_iota >= blk) | (n > 0) | (sb > 0))
        for g in range(N_KV_HEADS):
            hs = slice(g * HEAD_DIM, (g + 1) * HEAD_DIM)
            kg = kbuf_ref[kv_rows, hs]
            vg = vbuf_ref[kv_rows, hs]
            kdup = jnp.concatenate([kg, kg], axis=1).astype(BF16)
            vdup = jnp.concatenate([vg, vg], axis=1).astype(BF16)
            for p in range(group):
                pair = g * group + p
                ls = slice(pair * LANES, (pair + 1) * LANES)
                qp = q_ref[q_rows, ls]
                lhs = jnp.concatenate([jnp.where(lo_lane, qp, zero), jnp.where(lo_lane, zero, qp)], axis=0)
                s = lax.dot_general(lhs, kdup, (((1,), (1,)), ((), ())), preferred_element_type=F32)
                s = jnp.where(mask, s, NEG_INF)
                sink = jnp.where(top_row, sink_ref[2 * pair], sink_ref[2 * pair + 1])
                m = jnp.maximum(jnp.max(s, axis=-1, keepdims=True), sink)
                e = jnp.exp(s - m)
                denom = jnp.sum(e, axis=-1, keepdims=True) + jnp.exp(sink - m)
                o2 = jnp.dot(e.astype(BF16), vdup, preferred_element_type=F32) / denom
                o_ref[q_rows, ls] = jnp.where(lo_lane, o2[:blk], o2[blk:]).astype(o_ref.dtype)
        return carry

    lax.fori_loop(0, q_ref.shape[0] // blk, query_block, 0)


def _attn_prompt(q, k, v, sinks, bsz, slen):
    blk = WINDOW
    step = ATTN_BLOCKS * blk
    ns = slen // step
    qw, kw = q.shape[1], k.shape[1]
    cur = lambda b, n: (b * ns + n, 0)
    prev = lambda b, n: (jnp.maximum((b * ns + n) * ATTN_BLOCKS - 1, 0), 0)
    return pl.pallas_call(
        _attn_prompt_kernel,
        grid=(bsz, ns),
        in_specs=[
            pl.BlockSpec(memory_space=pltpu.SMEM),
            pl.BlockSpec((step, qw), cur),
            pl.BlockSpec((step, kw), cur), pl.BlockSpec((blk, kw), prev),
            pl.BlockSpec((step, kw), cur), pl.BlockSpec((blk, kw), prev),
        ],
        out_specs=pl.BlockSpec((step, qw), cur),
        out_shape=jax.ShapeDtypeStruct((bsz * slen, qw), BF16),
        scratch_shapes=[pltpu.VMEM((blk + step, kw), F32), pltpu.VMEM((blk + step, kw), F32)],
        compiler_params=_cparams(("arbitrary", "arbitrary")),
        name="attn_prompt",
    )(sinks.astype(F32), q, k, k, v, v)


def _attn_sample_kernel(sink_ref, q_ref, kt_ref, kn_ref, vt_ref, vn_ref, o_ref, kt_out_ref, vt_out_ref, *, qlen):
    nseq, pair_rows, pair_w = q_ref.shape
    rows = 2 * pair_rows
    kvw, ncache = kt_ref.shape[1], kt_ref.shape[2]
    per_kv = pair_rows // N_KV_HEADS
    all_rows = nseq * rows
    r_lane = _div_pow2(_mod_pow2(lax.broadcasted_iota(jnp.int32, (all_rows, kvw), 0), pair_rows), per_kv)
    c_lane = _div_pow2(lax.broadcasted_iota(jnp.int32, (all_rows, kvw), 1), HEAD_DIM)
    own = r_lane == c_lane
    mask_c = (lax.broadcasted_iota(jnp.int32, (all_rows, ncache), 1)
              >= _mod_pow2(lax.broadcasted_iota(jnp.int32, (all_rows, ncache), 0), qlen))
    mask_n = (lax.broadcasted_iota(jnp.int32, (all_rows, qlen), 1)
              <= _mod_pow2(lax.broadcasted_iota(jnp.int32, (all_rows, qlen), 0), qlen))
    keep_old = lax.broadcasted_iota(jnp.int32, (kvw, ncache), 1) < ncache - qlen
    sink = sink_ref[...]
    contract_last = (((1,), (1,)), ((), ()))
    seq_rows = lambda b: slice(b * rows, (b + 1) * rows)
    new_rows = lambda b: slice(b * qlen, (b + 1) * qlen)

    pair_shape = (nseq * pair_rows, kvw)
    pair_kv = _div_pow2(_mod_pow2(lax.broadcasted_iota(jnp.int32, pair_shape, 0), pair_rows), per_kv)
    pair_own = pair_kv == _div_pow2(lax.broadcasted_iota(jnp.int32, pair_shape, 1), HEAD_DIM)
    even_kv = (pair_kv & 1) == 0
    q_pairs = q_ref[...].reshape(nseq * pair_rows, pair_w)
    tiled = jnp.concatenate([q_pairs] * (kvw // pair_w), axis=1)
    swapped = jnp.concatenate([q_pairs[:, HEAD_DIM:], q_pairs[:, :HEAD_DIM]] * (kvw // pair_w), axis=1)
    none = jnp.zeros((), q_pairs.dtype)
    q_first = jnp.where(pair_own & even_kv, tiled, jnp.where(pair_own, swapped, none))
    q_second = jnp.where(pair_own & even_kv, swapped, jnp.where(pair_own, tiled, none))
    pair_seq = lambda b: slice(b * pair_rows, (b + 1) * pair_rows)
    qe = [jnp.concatenate([q_first[pair_seq(b)], q_second[pair_seq(b)]], axis=0) for b in range(nseq)]
    sc = jnp.concatenate([jnp.dot(qe[b], kt_ref[b].astype(BF16), preferred_element_type=F32)
                          for b in range(nseq)], axis=0)
    sn = jnp.concatenate([lax.dot_general(qe[b], kn_ref[new_rows(b), :].astype(BF16), contract_last,
                                          preferred_element_type=F32) for b in range(nseq)], axis=0)
    sc = jnp.where(mask_c, sc, NEG_INF)
    sn = jnp.where(mask_n, sn, NEG_INF)
    m = jnp.maximum(jnp.maximum(jnp.max(sc, axis=-1, keepdims=True), jnp.max(sn, axis=-1, keepdims=True)), sink)
    ec = jnp.exp(sc - m)
    en = jnp.exp(sn - m)
    denom = jnp.sum(ec, axis=-1, keepdims=True) + jnp.sum(en, axis=-1, keepdims=True) + jnp.exp(sink - m)
    ec = ec.astype(BF16)
    en = en.astype(BF16)
    o = jnp.concatenate(
        [lax.dot_general(ec[seq_rows(b)], vt_ref[b].astype(BF16), contract_last, preferred_element_type=F32)
         + jnp.dot(en[seq_rows(b)], vn_ref[new_rows(b), :].astype(BF16), preferred_element_type=F32)
         for b in range(nseq)], axis=0)
    o = jnp.where(own, o, 0.0)
    acc = o[:, :pair_w]
    for g in range(1, kvw // pair_w):
        acc = acc + o[:, g * pair_w:(g + 1) * pair_w]
    out = (acc / denom).reshape(nseq, 2, pair_rows, pair_w)
    first = out[:, 0].reshape(nseq * pair_rows, pair_w)
    second = out[:, 1].reshape(nseq * pair_rows, pair_w)
    even = even_kv[:, :pair_w]
    in_place = jnp.where(even, first, second)
    displaced = jnp.where(even, second, first)
    pairs = in_place + pltpu.roll(displaced, HEAD_DIM, axis=1)
    o_ref[...] = pairs.astype(o_ref.dtype).reshape(nseq, pair_rows, pair_w)

    def feature_major(new_ref):
        pad = jnp.zeros((ncache - nseq * qlen, kvw), F32)
        return jnp.concatenate([new_ref[...], pad], axis=0).T

    def shift_in(old_ref, new_ref, out_ref):
        new_t = feature_major(new_ref)
        for b in range(nseq):
            placed = pltpu.roll(new_t, ncache - qlen - b * qlen, axis=1)
            out_ref[b] = jnp.where(keep_old, pltpu.roll(old_ref[b], ncache - qlen, axis=1), placed)

    shift_in(kt_ref, kn_ref, kt_out_ref)
    shift_in(vt_ref, vn_ref, vt_out_ref)


def _attn_sample(q, k_cache, k_new, v_cache, v_new, sinks, seq_per_step=16):
    bsz, ncache = k_cache.shape[:2]
    kvw = k_new.shape[1]
    qlen = q.shape[0] // bsz
    step_rows = seq_per_step * qlen
    first_step, rem = divmod(k_new.shape[0] - bsz * qlen, step_rows)
    assert rem == 0 and step_rows <= ncache
    n_heads = q.shape[1] // HEAD_DIM
    assert n_heads % (2 * N_KV_HEADS) == 0
    rows = n_heads * qlen
    pair_rows = rows // 2
    pair_w = 2 * HEAD_DIM
    qr = q.reshape(bsz, qlen, n_heads // 2, pair_w).transpose(0, 2, 1, 3).reshape(bsz, pair_rows, pair_w)
    by_parity = sinks.astype(F32).reshape(n_heads // 2, 2).T.reshape(n_heads)
    sink_rows = jnp.tile(jnp.repeat(by_parity, qlen), seq_per_step).reshape(seq_per_step * rows, 1)
    feature_major = lambda c: c.transpose(0, 2, 3, 1).reshape(bsz, kvw, ncache)
    seq = lambda i: (i, 0, 0)
    cache_spec = pl.BlockSpec((seq_per_step, kvw, ncache), seq)
    new_spec = pl.BlockSpec((step_rows, kvw), lambda i: (i + first_step, 0))
    o, kt_next, vt_next = pl.pallas_call(
        functools.partial(_attn_sample_kernel, qlen=qlen),
        grid=(bsz // seq_per_step,),
        in_specs=[
            pl.BlockSpec((seq_per_step * rows, 1), lambda i: (0, 0)),
            pl.BlockSpec((seq_per_step, pair_rows, pair_w), seq),
            cache_spec, new_spec, cache_spec, new_spec,
        ],
        out_specs=[pl.BlockSpec((seq_per_step, pair_rows, pair_w), seq), cache_spec, cache_spec],
        out_shape=[jax.ShapeDtypeStruct((bsz, pair_rows, pair_w), BF16),
                   jax.ShapeDtypeStruct((bsz, kvw, ncache), F32), jax.ShapeDtypeStruct((bsz, kvw, ncache), F32)],
        compiler_params=_cparams(("arbitrary",)),
        name="attn_sample",
    )(sink_rows, qr, feature_major(k_cache), k_new, feature_major(v_cache), v_new)
    position_major = lambda t: t.reshape(bsz, N_KV_HEADS, HEAD_DIM, ncache).transpose(0, 3, 1, 2)
    ao = o.reshape(bsz, n_heads // 2, qlen, pair_w).transpose(0, 2, 1, 3).reshape(bsz * qlen, -1)
    return ao, position_major(kt_next), position_major(vt_next)


CONV_HALO = 32
CONV_ROWS = 64
CONV_COLS = 256


def _conv_prompt_kernel(uc_ref, up_ref, w_ref, b_ref, y_ref, sh_ref):
    tile = uc_ref.shape[0]
    n = pl.program_id(1)
    halo = jnp.where(n > 0, up_ref[...], 0.0)
    sh_ref[0, :CONV_HALO, :] = halo
    sh_ref[0, CONV_HALO:, :] = uc_ref[...]
    keep = tile + CONV_HALO - SUBLANES
    for r in range(1, SUBLANES):
        sh_ref[r, :keep, :] = sh_ref[0, r:r + keep, :]
    first = CONV_HALO - (CONV_WIDTH - 1)
    ch = uc_ref.shape[1]

    def rows_step(rc, carry):
        r0 = pl.multiple_of(rc * CONV_ROWS, CONV_ROWS)
        for cc in range(ch // CONV_COLS):
            cs = slice(cc * CONV_COLS, (cc + 1) * CONV_COLS)
            acc = jnp.zeros((CONV_ROWS, CONV_COLS), F32)
            for j in range(CONV_WIDTH):
                off = first + j
                a, r = off // SUBLANES, off % SUBLANES
                acc = acc + sh_ref[r, pl.ds(r0 + a * SUBLANES, CONV_ROWS), cs] * w_ref[j:j + 1, cs]
            y_ref[pl.ds(r0, CONV_ROWS), cs] = acc + b_ref[:, cs]
        return carry

    lax.fori_loop(0, tile // CONV_ROWS, rows_step, 0)


def _conv_prompt(u, w, b, bsz, slen, tile=512):
    ch = u.shape[1]
    nt = slen // tile
    per = tile // CONV_HALO
    return pl.pallas_call(
        _conv_prompt_kernel,
        grid=(bsz, nt),
        in_specs=[
            pl.BlockSpec((tile, ch), lambda bb, n: (bb * nt + n, 0)),
            pl.BlockSpec((CONV_HALO, ch), lambda bb, n: (jnp.maximum((bb * nt + n) * per - 1, 0), 0)),
            pl.BlockSpec((CONV_WIDTH, ch), lambda bb, n: (0, 0)),
            pl.BlockSpec((1, ch), lambda bb, n: (0, 0)),
        ],
        out_specs=pl.BlockSpec((tile, ch), lambda bb, n: (bb * nt + n, 0)),
        out_shape=jax.ShapeDtypeStruct((bsz * slen, ch), F32),
        scratch_shapes=[pltpu.VMEM((SUBLANES, tile + CONV_HALO, ch), F32)],
        compiler_params=_cparams(("arbitrary", "arbitrary")),
        name="conv_prompt",
    )(u, u, w, b.reshape(1, ch))


def _conv_sample_kernel(sc_ref, un_ref, w_ref, b_ref, y_ref, next_ref):
    nctx = sc_ref.shape[0]
    qlen = un_ref.shape[0]
    ch = w_ref.shape[1]

    def row(t):
        return sc_ref[t] if t < nctx else un_ref[t - nctx]

    for i in range(qlen):
        acc = jnp.zeros((sc_ref.shape[1], ch), F32)
        for j in range(CONV_WIDTH):
            acc = acc + row(i + j) * w_ref[j:j + 1, :]
        y_ref[i] = acc + b_ref[...]
    for t in range(nctx):
        next_ref[t] = row(t + qlen)


def _conv_sample(state_t, u_new_t, w, b, seq_per_step=16):
    nctx, bsz, ch = state_t.shape
    qlen = u_new_t.shape[0]
    seq = lambda i: (0, i, 0)
    return pl.pallas_call(
        _conv_sample_kernel,
        grid=(bsz // seq_per_step,),
        in_specs=[
            pl.BlockSpec((nctx, seq_per_step, ch), seq),
            pl.BlockSpec((qlen, seq_per_step, ch), seq),
            pl.BlockSpec((CONV_WIDTH, ch), lambda i: (0, 0)),
            pl.BlockSpec((1, ch), lambda i: (0, 0)),
        ],
        out_specs=[pl.BlockSpec((qlen, seq_per_step, ch), seq), pl.BlockSpec((nctx, seq_per_step, ch), seq)],
        out_shape=[jax.ShapeDtypeStruct((qlen, bsz, ch), F32), jax.ShapeDtypeStruct((nctx, bsz, ch), F32)],
        compiler_params=_cparams(("arbitrary",)),
        name="conv_sample",
    )(state_t, u_new_t, w, b.reshape(1, ch))


def _merge_kernel(y_ref, ao_ref, ga_ref, gc_ref, x_ref, lnw_ref, lnb_ref, wco_ref, wo_ref, fw_ref, wr_ref,
                  h_ref, xn_ref, lg_ref):
    y = y_ref[...]
    mu = jnp.mean(y, axis=-1, keepdims=True)
    yc = y - mu
    var = jnp.mean(yc * yc, axis=-1, keepdims=True)
    yl = yc * lax.rsqrt(var + EPS) * lnw_ref[...] + lnb_ref[...]
    act = yl * jax.nn.sigmoid(yl)
    conv_o = jnp.dot(act.astype(BF16), wco_ref[...], preferred_element_type=F32)
    mix = ga_ref[...].astype(F32) * ao_ref[...].astype(F32) + gc_ref[...].astype(F32) * conv_o
    h = x_ref[...] + jnp.dot(mix.astype(BF16), wo_ref[...], preferred_element_type=F32)
    h_ref[...] = h
    ms = jnp.mean(h * h, axis=-1, keepdims=True)
    xn = (h * lax.rsqrt(ms + EPS) * fw_ref[...]).astype(BF16)
    xn_ref[...] = xn
    lg_ref[...] = jnp.dot(xn, wr_ref[...], preferred_element_type=F32)


def _merge(y, ao, gates, row0, x, ln_w, ln_b, wco, wo, fw, wr, tile=256):
    t, d = x.shape
    ch = y.shape[1]
    off = row0 // tile
    row = lambda i: (i, 0)
    const = lambda shape: pl.BlockSpec(shape, lambda i: (0, 0), pipeline_mode=pl.Buffered(1))
    return pl.pallas_call(
        _merge_kernel,
        grid=(t // tile,),
        in_specs=[
            pl.BlockSpec((tile, ch), row),
            pl.BlockSpec((tile, d), row),
            pl.BlockSpec((tile, d), lambda i: (i + off, 0)),
            pl.BlockSpec((tile, d), lambda i: (i + off, 1)),
            pl.BlockSpec((tile, d), row),
            const((1, ch)), const((1, ch)), const((ch, d)), const((d, d)), const((1, d)),
            const((d, LANES)),
        ],
        out_specs=[pl.BlockSpec((tile, d), row), pl.BlockSpec((tile, d), row), pl.BlockSpec((tile, LANES), row)],
        out_shape=[jax.ShapeDtypeStruct((t, d), F32), jax.ShapeDtypeStruct((t, d), BF16),
                   jax.ShapeDtypeStruct((t, LANES), F32)],
        compiler_params=_cparams(("arbitrary",)),
        name="merge",
    )(y, ao, gates, gates, x, ln_w.reshape(1, ch), ln_b.reshape(1, ch), wco, wo, fw.reshape(1, d), wr)


def _route_kernel(lp_ref, ls_ref, tri_ref, lower_ref, meta_ref, metat_ref, cnt_ref, *, n_p):
    i = pl.program_id(0)
    used = _round_up(N_EXPERTS + N_GROUPS, SUBLANES)

    def body(l_ref):
        lg = l_ref[...].T[:used]
        shape = lg.shape
        row = lax.broadcasted_iota(jnp.int32, shape, 0)
        big = jnp.int32(LANES)
        is_g = (row >= N_EXPERTS) & (row < N_EXPERTS + N_GROUPS)
        gl = jnp.where(is_g, lg, -jnp.inf)
        gmax = jnp.max(gl, axis=0, keepdims=True)
        g_star = jnp.min(jnp.where(gl == gmax, row - N_EXPERTS, big), axis=0, keepdims=True)
        pg_star = 1.0 / jnp.sum(jnp.exp(gl - gmax), axis=0, keepdims=True)
        in_g = (row < N_EXPERTS) & (_div_pow2(row, EXPERTS_PER_GROUP) == g_star)
        el = jnp.where(in_g, lg, -jnp.inf)
        v1 = jnp.max(el, axis=0, keepdims=True)
        i1 = jnp.min(jnp.where(el == v1, row, big), axis=0, keepdims=True)
        el2 = jnp.where(row == i1, -jnp.inf, el)
        v2 = jnp.max(el2, axis=0, keepdims=True)
        i2 = jnp.min(jnp.where(el2 == v2, row, big), axis=0, keepdims=True)
        e2 = jnp.exp(v2 - v1)
        w1 = pg_star * (1.0 / (1.0 + e2))
        w2 = pg_star * (e2 / (1.0 + e2))
        oh1 = row == i1
        oh2 = row == i2
        cnt = jnp.where(oh1 | oh2, 1.0, 0.0)
        before = jnp.dot(cnt.astype(BF16), tri_ref[...], preferred_element_type=F32)
        tot = jnp.sum(cnt, axis=-1, keepdims=True)
        runs = jnp.floor((tot + (CHUNK_ROWS - 1)) * (1.0 / CHUNK_ROWS))
        runs_b = jnp.broadcast_to(runs, (used, LANES)).astype(BF16)
        start = jnp.dot(lower_ref[...], runs_b, preferred_element_type=F32)[:, 0:1] * CHUNK_ROWS
        slot = before + start
        s1 = jnp.sum(jnp.where(oh1, slot, 0.0), axis=0, keepdims=True)
        s2 = jnp.sum(jnp.where(oh2, slot, 0.0), axis=0, keepdims=True)
        fields = [i1.astype(F32), i2.astype(F32), w1, w2, s1, s2]
        fields += [jnp.zeros_like(w1)] * (META_FIELDS - len(fields))
        metat = jnp.concatenate(fields, axis=0)
        metat_ref[...] = metat
        pad = jnp.zeros((LANES - META_FIELDS, shape[1]), F32)
        meta_ref[...] = jnp.concatenate([metat, pad], axis=0).T
        tot_pad = jnp.concatenate([tot, jnp.zeros((LANES - used, 1), F32)], axis=0)
        cnt_ref[...] = jnp.broadcast_to(tot_pad, (LANES, LANES)).T[0:1]

    @pl.when(i < n_p)
    def _():
        body(lp_ref)

    @pl.when(i >= n_p)
    def _():
        body(ls_ref)


def _route(lg_p, lg_s):
    tile = TOKEN_BLOCK
    tp, ts = lg_p.shape[0], lg_s.shape[0]
    n_p, n_s = tp // tile, ts // tile
    nb = n_p + n_s
    used = _round_up(N_EXPERTS + N_GROUPS, SUBLANES)
    tri = (jnp.arange(tile)[:, None] < jnp.arange(tile)[None, :]).astype(BF16)
    lower = (jnp.arange(used)[:, None] > jnp.arange(used)[None, :]).astype(BF16)
    return pl.pallas_call(
        functools.partial(_route_kernel, n_p=n_p),
        grid=(nb,),
        in_specs=[
            pl.BlockSpec((tile, LANES), lambda i: (jnp.minimum(i, n_p - 1), 0)),
            pl.BlockSpec((tile, LANES), lambda i: (jnp.maximum(i - n_p, 0), 0)),
            pl.BlockSpec((tile, tile), lambda i: (0, 0)),
            pl.BlockSpec((used, used), lambda i: (0, 0)),
        ],
        out_specs=[pl.BlockSpec((tile, LANES), lambda i: (i, 0)),
                   pl.BlockSpec((None, META_FIELDS, tile), lambda i: (i, 0, 0)),
                   pl.BlockSpec((None, 1, LANES), lambda i: (i, 0, 0))],
        out_shape=[jax.ShapeDtypeStruct((tp + ts, LANES), F32),
                   jax.ShapeDtypeStruct((nb, META_FIELDS, tile), F32),
                   jax.ShapeDtypeStruct((nb, 1, LANES), F32)],
        compiler_params=_cparams(("arbitrary",)),
        name="route",
    )(lg_p, lg_s, tri, lower)


def _plan_sizes(n_tok):
    nb = n_tok // TOKEN_BLOCK
    pad_per_block = N_EXPERTS * (CHUNK_ROWS - 1)
    max_chunks = _round_up(-(-(TOKEN_BLOCK * TOP_K + pad_per_block) // CHUNK_ROWS), SUBLANES)
    n_rows = _round_up(n_tok * TOP_K + nb * pad_per_block + N_EXPERTS * (ROW_PAD - 1), ROW_PAD)
    n_items = (n_rows + N_EXPERTS * (EXPERT_ROWS - ROW_PAD)) // EXPERT_ROWS
    max_zero = N_EXPERTS * (ROW_PAD // CHUNK_ROWS)
    return nb, max_chunks, n_rows, n_items, max_zero


def _dispatch_plan(counts, max_chunks, n_items, max_zero):
    i32 = jnp.int32
    c = counts[:, 0, :N_EXPERTS].astype(i32)
    run = (c + CHUNK_ROWS - 1) // CHUNK_ROWS
    run_end = jnp.cumsum(run, axis=1)
    src_start = run_end - run
    nchunk = run_end[:, -1]
    seg = jnp.sum(run, axis=0) * CHUNK_ROWS
    padded = (seg + ROW_PAD - 1) // ROW_PAD * ROW_PAD
    pad_end = jnp.cumsum(padded)
    pad_start = pad_end - padded
    dst_start = pad_start[None, :] // CHUNK_ROWS + (jnp.cumsum(run, axis=0) - run)
    jj = jnp.arange(max_chunks, dtype=i32)[None, :, None]
    in_run = (src_start[:, None, :] <= jj) & (jj < run_end[:, None, :])
    chunk_dst = jnp.sum(jnp.where(in_run, (dst_start - src_start)[:, None, :] + jj, 0), axis=-1) * CHUNK_ROWS
    tail = (padded - seg) // CHUNK_ROWS
    tail_end = jnp.cumsum(tail)
    tail_start = tail_end - tail
    z = jnp.arange(max_zero, dtype=i32)[:, None]
    in_tail = (tail_start[None, :] <= z) & (z < tail_end[None, :])
    zero_base = (pad_start + seg) // CHUNK_ROWS - tail_start
    zero_dst = jnp.sum(jnp.where(in_tail, zero_base[None, :] + z, 0), axis=-1) * CHUNK_ROWS
    nzero = tail_end[-1]
    items_e = (padded + EXPERT_ROWS - 1) // EXPERT_ROWS
    item_end = jnp.cumsum(items_e)
    item_start = item_end - items_e
    w = jnp.arange(n_items, dtype=i32)[:, None]
    in_item = (item_start[None, :] <= w) & (w < item_end[None, :])
    k = w - item_start[None, :]
    e_ids = jnp.arange(N_EXPERTS, dtype=i32)
    last_e = jnp.max(jnp.where(items_e > 0, e_ids, 0))
    item_e = jnp.where(w[:, 0] < item_end[-1], jnp.sum(jnp.where(in_item, e_ids[None, :], 0), axis=-1), last_e)
    item_row0 = jnp.sum(jnp.where(in_item, pad_start[None, :] + k * EXPERT_ROWS, 0), axis=-1)
    item_rows = jnp.sum(jnp.where(in_item, jnp.clip(padded[None, :] - k * EXPERT_ROWS, 0, EXPERT_ROWS), 0), axis=-1)
    item_nsub = item_rows // ROW_PAD
    return (chunk_dst.reshape(-1).astype(i32), nchunk.astype(i32), zero_dst.astype(i32),
            nzero.reshape(1).astype(i32), item_e.astype(i32), item_row0.astype(i32), item_nsub.astype(i32))


SCATTER_GROUPS = 2


def _scatter_kernel(cd_ref, nc_ref, zd_ref, nz_ref, mt_ref, xp_ref, xs_ref, out_hbm, loc_ref, zero_ref, sem,
                    zero_sem, *, n_p, max_chunks):
    i = pl.program_id(0)
    buf = i % 2
    n_slots = loc_ref.shape[1]
    d = xp_ref.shape[1]
    mt = mt_ref[...]
    w1, w2 = mt[2:3, :], mt[3:4, :]
    s1, s2 = mt[4:5, :].astype(jnp.int32), mt[5:6, :].astype(jnp.int32)
    half = d // 2
    group = n_slots // SCATTER_GROUPS

    def sort_rows(x_ref):
        for s0 in range(0, n_slots, group):
            slot = lax.broadcasted_iota(jnp.int32, (group, mt.shape[1]), 0) + s0
            p1 = slot == s1
            p2 = slot == s2
            perm = jnp.where(p1 | p2, 1.0, 0.0).astype(BF16)
            wcol = jnp.sum(jnp.where(p1, w1, 0.0) + jnp.where(p2, w2, 0.0), axis=-1, keepdims=True)
            loc_ref[buf, s0:s0 + group, half:] = pltpu.bitcast(jnp.broadcast_to(wcol, (group, LANES)), jnp.uint32)
            rows = jnp.dot(perm, x_ref[...], preferred_element_type=F32)
            loc_ref[buf, s0:s0 + group, :half] = _pack_pairs(rows[:, :half], rows[:, half:])

    @pl.when(i < n_p)
    def _():
        sort_rows(xp_ref)

    @pl.when(i >= n_p)
    def _():
        sort_rows(xs_ref)

    def rows_at(r):
        return pl.ds(pl.multiple_of(r, CHUNK_ROWS), CHUNK_ROWS)

    def chunk_copy(blk, j):
        return pltpu.make_async_copy(loc_ref.at[blk % 2, rows_at(j * CHUNK_ROWS)],
                                     out_hbm.at[rows_at(cd_ref[blk * max_chunks + j])], sem.at[blk % 2])

    def zero_copy(z):
        return pltpu.make_async_copy(zero_ref, out_hbm.at[rows_at(zd_ref[z])], zero_sem)

    @pl.when(i > 0)
    def _():
        _for_each(nc_ref[i - 1], lambda j: chunk_copy(i - 1, j).wait())

    _for_each(nc_ref[i], lambda j: chunk_copy(i, j).start())

    @pl.when(i == pl.num_programs(0) - 1)
    def _():
        zero_ref[...] = jnp.zeros_like(zero_ref)
        _for_each(nz_ref[0], lambda z: zero_copy(z).start())
        _for_each(nc_ref[i], lambda j: chunk_copy(i, j).wait())
        _for_each(nz_ref[0], lambda z: zero_copy(z).wait())


def _scatter_rows(xn_p, xn_s, metat, chunk_dst, nchunk, zero_dst, nzero, max_chunks, n_rows):
    tile = TOKEN_BLOCK
    tp, d = xn_p.shape
    ts = xn_s.shape[0]
    n_p, n_s = tp // tile, ts // tile
    n_slots = max_chunks * CHUNK_ROWS
    width = d // 2 + LANES
    u32 = jnp.uint32
    return pl.pallas_call(
        functools.partial(_scatter_kernel, n_p=n_p, max_chunks=max_chunks),
        grid_spec=pltpu.PrefetchScalarGridSpec(
            num_scalar_prefetch=4,
            grid=(n_p + n_s,),
            in_specs=[
                pl.BlockSpec((None, META_FIELDS, tile), lambda i, *_: (i, 0, 0)),
                pl.BlockSpec((tile, d), lambda i, *_: (jnp.minimum(i, n_p - 1), 0)),
                pl.BlockSpec((tile, d), lambda i, *_: (jnp.maximum(i - n_p, 0), 0)),
            ],
            out_specs=pl.BlockSpec(memory_space=pl.ANY),
            scratch_shapes=[pltpu.VMEM((2, n_slots, width), u32), pltpu.VMEM((CHUNK_ROWS, width), u32),
                            pltpu.SemaphoreType.DMA((2,)), pltpu.SemaphoreType.DMA(())],
        ),
        out_shape=jax.ShapeDtypeStruct((n_rows, width), u32),
        compiler_params=_cparams(("arbitrary",)),
        name="scatter_rows",
    )(chunk_dst, nchunk, zero_dst, nzero, metat, xn_p, xn_s)


def _expert_kernel(e_ref, row0_ref, nsub_ref, xs_hbm, wg_ref, wu_ref, wd_ref, out_hbm,
                   xf_ref, xb_ref, acc_ref, ob_ref, pend_ref, in_sem, out_sem, *, n_chunks):
    w = pl.program_id(0)
    c = pl.program_id(1)
    n_items = pl.num_programs(0)
    last_c = pl.num_programs(1) - 1
    nsub = nsub_ref[w]
    d = acc_ref.shape[1]
    half = d // 2
    slot = w % 2

    def sub_rows(s):
        return pl.ds(pl.multiple_of(s * ROW_PAD, ROW_PAD), ROW_PAD)

    def hbm_rows(item, s):
        return pl.ds(pl.multiple_of(row0_ref[item] + s * ROW_PAD, ROW_PAD), ROW_PAD)

    def in_copy(item, s):
        return pltpu.make_async_copy(xs_hbm.at[hbm_rows(item, s)], xf_ref.at[item % 2, sub_rows(s)],
                                     in_sem.at[item % 2])

    def out_copy(item, s):
        return pltpu.make_async_copy(ob_ref.at[sub_rows(s)], out_hbm.at[hbm_rows(item, s)], out_sem)

    def loop(n, fn):
        def body(s, carry):
            fn(s)
            return carry
        lax.fori_loop(0, n, body, 0)

    @pl.when(c == 0)
    def _():
        @pl.when(w == 0)
        def _():
            pend_ref[0] = 0
            pend_ref[1] = 0
            loop(nsub, lambda s: in_copy(w, s).start())

        loop(nsub, lambda s: in_copy(w, s).wait())
        nxt = jnp.minimum(w + 1, n_items - 1)

        @pl.when(w + 1 < n_items)
        def _():
            loop(nsub_ref[nxt], lambda s: in_copy(nxt, s).start())

    def drain_out():
        prev = pend_ref[1]
        loop(pend_ref[0], lambda s: out_copy(prev, s).wait())
        pend_ref[0] = 0

    def ffn_chunk(x):
        g = jnp.dot(x, wg_ref[...].astype(BF16), preferred_element_type=F32)
        u = jnp.dot(x, wu_ref[...].astype(BF16), preferred_element_type=F32)
        hmid = (g * jax.nn.sigmoid(g) * u).astype(BF16)
        return jnp.dot(hmid, wd_ref[...].astype(BF16), preferred_element_type=F32)

    for n in range(1, EXPERT_ROWS // ROW_PAD + 1):
        m = n * ROW_PAD

        @pl.when((nsub == n) & (c == 0))
        def _(m=m):
            xa, xb = _unpack_pairs(xf_ref[slot, :m, :half])
            x = jnp.concatenate([xa, xb], axis=1).astype(BF16)
            xb_ref[:m, :] = x
            acc_ref[:m, :] = ffn_chunk(x)

        if n_chunks > 2:
            @pl.when((nsub == n) & (c > 0) & (c < last_c))
            def _(m=m):
                acc_ref[:m, :] += ffn_chunk(xb_ref[:m, :])

        @pl.when((nsub == n) & (c == last_c))
        def _(m=m):
            drain_out()
            mix_w = pltpu.bitcast(xf_ref[slot, :m, half:half + 1], F32)
            y = ((acc_ref[:m, :] + ffn_chunk(xb_ref[:m, :])) * mix_w).astype(BF16).astype(F32)
            ob_ref[:m, :] = _pack_pairs(y[:, :half], y[:, half:])
            loop(nsub, lambda s: out_copy(w, s).start())
            pend_ref[0] = nsub
            pend_ref[1] = w

    @pl.when((w == n_items - 1) & (c == last_c))
    def _():
        drain_out()


def _experts(xs, item_e, item_row0, item_nsub, w_gate, w_up, w_down):
    n_rows, width = xs.shape
    d = w_gate.shape[1]
    assert width == d // 2 + LANES
    d_exp = w_gate.shape[2]
    nc = d_exp // D_CHUNK
    assert nc >= 2
    n_items = item_e.shape[0]

    def chunk_of(w, c, nsub_ref):
        return jnp.where(nsub_ref[w] > 0, c, nc - 1)

    return pl.pallas_call(
        functools.partial(_expert_kernel, n_chunks=nc),
        grid_spec=pltpu.PrefetchScalarGridSpec(
            num_scalar_prefetch=3,
            grid=(n_items, nc),
            in_specs=[
                pl.BlockSpec(memory_space=pl.ANY),
                pl.BlockSpec((None, d, D_CHUNK), lambda w, c, e, r, n: (e[w], 0, chunk_of(w, c, n))),
                pl.BlockSpec((None, d, D_CHUNK), lambda w, c, e, r, n: (e[w], 0, chunk_of(w, c, n))),
                pl.BlockSpec((None, D_CHUNK, d), lambda w, c, e, r, n: (e[w], chunk_of(w, c, n), 0)),
            ],
            out_specs=pl.BlockSpec(memory_space=pl.ANY),
            scratch_shapes=[
                pltpu.VMEM((2, EXPERT_ROWS, width), jnp.uint32),
                pltpu.VMEM((EXPERT_ROWS, d), BF16),
                pltpu.VMEM((EXPERT_ROWS, d), F32),
                pltpu.VMEM((EXPERT_ROWS, d // 2), jnp.uint32),
                pltpu.SMEM((2,), jnp.int32),
                pltpu.SemaphoreType.DMA((2,)),
                pltpu.SemaphoreType.DMA(()),
            ],
        ),
        out_shape=jax.ShapeDtypeStruct((n_rows, d // 2), jnp.uint32),
        compiler_params=_cparams(("arbitrary", "arbitrary")),
        name="experts",
    )(item_e, item_row0, item_nsub, xs, w_gate, w_up, w_down)


COMBINE_SPLIT = 2
SLOT_STEP = 256


def _combine_kernel(cd_ref, nc_ref, meta_ref, hp_ref, hs_ref, yb_hbm, yp_ref, ys_ref, loc_ref, sem,
                    *, n_p, max_chunks):
    i = pl.program_id(0)
    j = pl.program_id(1)
    nb = pl.num_programs(0)
    buf = i % 2
    n_slots, half = loc_ref.shape[1], loc_ref.shape[2]
    d = 2 * half

    def rows_at(r):
        return pl.ds(pl.multiple_of(r, CHUNK_ROWS), CHUNK_ROWS)

    def chunk_copy(blk, c):
        return pltpu.make_async_copy(yb_hbm.at[rows_at(cd_ref[blk * max_chunks + c])],
                                     loc_ref.at[blk % 2, rows_at(c * CHUNK_ROWS)], sem.at[blk % 2])

    def fetch(blk):
        n = nc_ref[blk]
        _for_each(n, lambda c: chunk_copy(blk, c).start())

        def clear(c, carry):
            loc_ref[blk % 2, rows_at(c * CHUNK_ROWS), :] = jnp.zeros((CHUNK_ROWS, half), jnp.uint32)
            return carry
        lax.fori_loop(n, max_chunks, clear, 0)

    @pl.when(j == 0)
    def _():
        @pl.when(i == 0)
        def _():
            fetch(i)

        _for_each(nc_ref[i], lambda c: chunk_copy(i, c).wait())

        @pl.when(i + 1 < nb)
        def _():
            fetch(jnp.minimum(i + 1, nb - 1))

    meta = meta_ref[...]
    s1 = meta[:, 4:5].astype(jnp.int32)
    s2 = meta[:, 5:6].astype(jnp.int32)
    rows = meta.shape[0]
    f = jnp.zeros((rows, d), F32)
    for r0 in range(0, n_slots, SLOT_STEP):
        slot = lax.broadcasted_iota(jnp.int32, (rows, SLOT_STEP), 1) + r0
        pt = jnp.where((slot == s1) | (slot == s2), 1.0, 0.0).astype(BF16)
        ya, yb = _unpack_pairs(loc_ref[buf, r0:r0 + SLOT_STEP, :])
        f = f + jnp.dot(pt, jnp.concatenate([ya, yb], axis=1).astype(BF16), preferred_element_type=F32)
    g = i * pl.num_programs(1) + j

    @pl.when(g < n_p)
    def _():
        yp_ref[...] = hp_ref[...] + f

    @pl.when(g >= n_p)
    def _():
        ys_ref[...] = (hs_ref[...] + f).reshape(ys_ref.shape)


def _combine(h_p, h_s, meta, chunk_dst, nchunk, yb, max_chunks, sample_len):
    tp, d = h_p.shape
    ts = h_s.shape[0]
    tile = TOKEN_BLOCK // COMBINE_SPLIT
    n_p, n_s = tp // tile, ts // tile
    nb = (tp + ts) // TOKEN_BLOCK
    n_slots = max_chunks * CHUNK_ROWS
    assert n_slots % SLOT_STEP == 0
    blk = lambda i, j: i * COMBINE_SPLIT + j
    p_idx = lambda i, j, *_: (jnp.minimum(blk(i, j), n_p - 1), 0)
    s_idx = lambda i, j, *_: (jnp.maximum(blk(i, j) - n_p, 0), 0)
    return pl.pallas_call(
        functools.partial(_combine_kernel, n_p=n_p, max_chunks=max_chunks),
        grid_spec=pltpu.PrefetchScalarGridSpec(
            num_scalar_prefetch=2,
            grid=(nb, COMBINE_SPLIT),
            in_specs=[
                pl.BlockSpec((tile, LANES), lambda i, j, *_: (blk(i, j), 0)),
                pl.BlockSpec((tile, d), p_idx),
                pl.BlockSpec((tile, d), s_idx),
                pl.BlockSpec(memory_space=pl.ANY),
            ],
            out_specs=[pl.BlockSpec((tile, d), p_idx),
                       pl.BlockSpec((tile // sample_len, sample_len, d), lambda i, j, *_: s_idx(i, j) + (0,))],
            scratch_shapes=[pltpu.VMEM((2, n_slots, d // 2), jnp.uint32), pltpu.SemaphoreType.DMA((2,))],
        ),
        out_shape=[jax.ShapeDtypeStruct((tp, d), F32), jax.ShapeDtypeStruct((ts // sample_len, sample_len, d), F32)],
        compiler_params=_cparams(("arbitrary", "arbitrary")),
        name="combine",
    )(chunk_dst, nchunk, meta, h_p, h_s, yb)


def _layer(x_p, x_s, ck, cv, sc, attn_norm_w, w_in, b_gate, q_norm_w, k_norm_w, sinks, conv_dw_w, conv_dw_b,
           conv_ln_w, conv_ln_b, w_conv_out, w_out, ffn_norm_w, router_group, router_expert, w_gate, w_up,
           w_down):
    bsz, slen, d = x_p.shape
    dbs, dlen, _ = x_s.shape
    tp, ts = bsz * slen, dbs * dlen
    conv_ch = conv_dw_w.shape[1]
    kvw = N_KV_HEADS * HEAD_DIM
    xp2, xs2 = x_p.reshape(tp, d), x_s.reshape(ts, d)

    xn = _prenorm(xp2, xs2, attn_norm_w)
    q, k, v, u, gates = _project(xn, w_in, b_gate, q_norm_w, k_norm_w, d, conv_ch)

    ao_p = _attn_prompt(q, k, v, sinks, bsz, slen)
    ao_s, ck_next, cv_next = _attn_sample(q[tp:], ck, k, cv, v, sinks)

    y_p = _conv_prompt(u, conv_dw_w, conv_dw_b, bsz, slen)
    sc_t = sc.transpose(1, 0, 2)
    u_s_t = u[tp:].reshape(dbs, dlen, conv_ch).transpose(1, 0, 2)
    y_s_t, sc_next_t = _conv_sample(sc_t, u_s_t, conv_dw_w, conv_dw_b)
    y_s = y_s_t.transpose(1, 0, 2).reshape(ts, conv_ch)

    wco = w_conv_out.astype(BF16)
    wo = w_out.astype(BF16)
    wr = jnp.concatenate([router_expert, router_group,
                          jnp.zeros((d, LANES - N_EXPERTS - N_GROUPS), router_expert.dtype)], axis=1).astype(BF16)
    margs = (conv_ln_w, conv_ln_b, wco, wo, ffn_norm_w, wr)
    h_p, xn_p, lg_p = _merge(y_p, ao_p, gates, 0, xp2, *margs)
    h_s, xn_s, lg_s = _merge(y_s, ao_s, gates, tp, xs2, *margs)

    meta, metat, counts = _route(lg_p, lg_s)
    _, max_chunks, n_rows, n_items, max_zero = _plan_sizes(tp + ts)
    chunk_dst, nchunk, zero_dst, nzero, item_e, item_row0, item_nsub = _dispatch_plan(
        counts, max_chunks, n_items, max_zero)
    xs_rows = _scatter_rows(xn_p, xn_s, metat, chunk_dst, nchunk, zero_dst, nzero, max_chunks, n_rows)
    yb = _experts(xs_rows, item_e, item_row0, item_nsub, w_gate, w_up, w_down)
    out_p, out_s = _combine(h_p, h_s, meta, chunk_dst, nchunk, yb, max_chunks, dlen)

    def seq_tails(a, n):
        return jnp.stack([lax.slice_in_dim(a, (b + 1) * slen - n, (b + 1) * slen, axis=0) for b in range(bsz)])

    new_p = (seq_tails(k, WINDOW).reshape(bsz, WINDOW, N_KV_HEADS, HEAD_DIM),
             seq_tails(v, WINDOW).reshape(bsz, WINDOW, N_KV_HEADS, HEAD_DIM),
             seq_tails(u, CONV_WIDTH - 1))
    new_s = (ck_next, cv_next, sc_next_t.transpose(1, 0, 2))
    return out_p.reshape(bsz, slen, d), out_s.reshape(dbs, dlen, d), new_p, new_s


def kernel(x_prompt, x_sample, cache_k, cache_v, state_conv, attn_norm_w, w_in, b_gate, q_norm_w, k_norm_w,
           sinks, conv_dw_w, conv_dw_b, conv_ln_w, conv_ln_b, w_conv_out, w_out, ffn_norm_w, router_group,
           router_expert, w_gate, w_up, w_down):
    x_p, x_s = x_prompt, x_sample
    kp, vp, cp, ks_, vs_, cs_ = [], [], [], [], [], []
    for l in range(cache_k.shape[0]):
        x_p, x_s, new_p, new_s = _layer(
            x_p, x_s, cache_k[l], cache_v[l], state_conv[l], attn_norm_w[l], w_in[l], b_gate[l], q_norm_w[l],
            k_norm_w[l], sinks[l], conv_dw_w[l], conv_dw_b[l], conv_ln_w[l], conv_ln_b[l], w_conv_out[l],
            w_out[l], ffn_norm_w[l], router_group[l], router_expert[l], w_gate[l], w_up[l], w_down[l])
        kp.append(new_p[0]); vp.append(new_p[1]); cp.append(new_p[2])
        ks_.append(new_s[0]); vs_.append(new_s[1]); cs_.append(new_s[2])
    return (x_p, x_s, jnp.stack(kp), jnp.stack(vp), jnp.stack(cp),
            jnp.stack(ks_), jnp.stack(vs_), jnp.stack(cs_))
```
